```python
import math
import jax, jax.numpy as jnp
from jax import lax
import numpy as np

D_MODEL = 1024
BATCH = 8
SEQ = 2048
DEPTH = 2
DEC_BATCH = 128
DEC_SEQ = 1
PAST_LEN = 2048
PAGE_SIZE = 128

HEAD_DIM = 64
GROUP_W = D_MODEL // 2
H_A = GROUP_W // HEAD_DIM
H_B = GROUP_W // HEAD_DIM
H_C = GROUP_W // (2 * HEAD_DIM)
DV_C = 2 * HEAD_DIM
N_BLK_D = GROUP_W // HEAD_DIM
D_FF = 4 * D_MODEL
P_DIM = 256
RET_CHUNK = 64
Q_BLOCK = 128
CONV_W = 4
LRU_C = 8.0
RWKV_LORA_W = 64
RWKV_LORA_A = 64
RWKV_LORA_G = 128
N_NORMS = 8
ROPE_BASE = 10000.0
EPS = 1e-6
RWKV_GN_EPS = 64e-5

kernel_name = 'hybrid_retention_rwkv7_diffattn_rglru_step'

F32 = jnp.float32


def rmsnorm(x, g):
    x32 = x.astype(F32)
    y = x32 * lax.rsqrt(jnp.mean(x32 * x32, axis=-1, keepdims=True) + EPS)
    return (y * g.astype(F32)).astype(x.dtype)


def head_layernorm(x, eps):
    x = x.astype(F32)
    xc = x - jnp.mean(x, axis=-1, keepdims=True)
    return xc * lax.rsqrt(jnp.mean(xc * xc, axis=-1, keepdims=True) + eps)


def swiglu(x, w_in, w_out):
    gate, up = jnp.split(x @ w_in, 2, axis=-1)
    return (jax.nn.silu(gate) * up) @ w_out


def rotary(x, pos):
    half = x.shape[-1] // 2
    freq = 1.0 / (ROPE_BASE ** jnp.linspace(0.0, 1.0, half, dtype=F32))
    ang = pos[:, None] * freq[None, :]
    cos = jnp.cos(ang)[None, :, None, :]
    sin = jnp.sin(ang)[None, :, None, :]
    x32 = x.astype(F32)
    x1, x2 = x32[..., :half], x32[..., half:]
    return jnp.concatenate([x1 * cos - x2 * sin, x2 * cos + x1 * sin], axis=-1)


def retention_chunkwise(q, k, v, s0):
    B, L, H, _ = q.shape
    dv = v.shape[-1]
    c = math.gcd(L, RET_CHUNK)
    n = L // c
    lg = jnp.log1p(-jnp.exp2(-5.0 - jnp.arange(H, dtype=F32)))
    idx = jnp.arange(c, dtype=F32)
    rel = idx[:, None] - idx[None, :]
    dmask = jnp.where(rel[None] >= 0, jnp.exp(jnp.maximum(rel, 0.0)[None] * lg[:, None, None]), 0.0)
    q_dec = jnp.exp((idx[None, :] + 1.0) * lg[:, None])
    k_dec = jnp.exp((c - 1.0 - idx[None, :]) * lg[:, None])
    c_dec = jnp.exp(c * lg)

    def chunks(t):
        return t.reshape(B, n, c, H, t.shape[-1]).transpose(1, 0, 3, 2, 4)

    def step(s, inp):
        qc, kc, vc = inp
        att = jnp.einsum('bhid,bhjd->bhij', qc, kc) * dmask
        o = jnp.einsum('bhij,bhjv->bhiv', att, vc) + jnp.einsum('bhid,bhdv->bhiv', qc, s) * q_dec[..., None]
        s = s * c_dec[:, None, None] + jnp.einsum('bhjd,bhjv->bhdv', kc * k_dec[..., None], vc)
        return s, o

    s, o = lax.scan(step, s0.astype(F32), (chunks(q), chunks(k), chunks(v)))
    return o.transpose(1, 0, 3, 2, 4).reshape(B, L, H, dv), s


def rwkv7_scan(r, w, k, v, kk, a, s0):
    def step(s, inp):
        rt, wt, kt, vt, kkt, at = inp
        sa = jnp.einsum('bhvk,bhk->bhv', s, -kkt)
        s = s * wt[:, :, None, :] + sa[..., None] * (kkt * at)[:, :, None, :] + vt[..., None] * kt[:, :, None, :]
        return s, jnp.einsum('bhvk,bhk->bhv', s, rt)

    seq = tuple(t.transpose(1, 0, 2, 3) for t in (r, w, k, v, kk, a))
    s, o = lax.scan(step, s0.astype(F32), seq)
    return o.transpose(1, 0, 2, 3), s


def token_shift(u, buf):
    full = jnp.concatenate([buf.astype(u.dtype), u], axis=1)
    return full[:, :-1], full[:, -1:]


def mix_even(xn, pos, s_ret, s_rwkv, buf, prm, j):
    B, L, _ = xn.shape
    dt = xn.dtype
    u = xn @ prm['ab_w_in'][j]
    qa, ka, va, ga, rr, kr, vr, zr = jnp.split(u, 8, axis=-1)
    heads = lambda t, H: t.reshape(B, L, H, -1)
    q = rotary(heads(qa, H_A), pos)
    k = rotary(heads(ka, H_A), pos) * (HEAD_DIM ** -0.5)
    o_a, s_ret_new = retention_chunkwise(q, k, heads(va, H_A).astype(F32), s_ret)
    o_a = head_layernorm(o_a, EPS).reshape(B, L, GROUP_W).astype(dt) * jax.nn.silu(ga)
    cur = jnp.concatenate([rr, kr, vr, zr], axis=-1)
    prev, buf_new = token_shift(cur, buf)
    pr, pk, pv, pz = jnp.split(prev, 4, axis=-1)
    mu = prm['rwkv_mu'][j]
    lerp = lambda x, xp, m: x + (xp - x) * m
    r = lerp(rr, pr, mu[0])
    kx = lerp(kr, pk, mu[1])
    vx = lerp(vr, pv, mu[2])
    zw = lerp(zr, pz, mu[3])
    za = lerp(zr, pz, mu[4])
    zg = lerp(zr, pz, mu[5])
    wpre = (prm['rwkv_w0'][j] + jnp.tanh(zw @ prm['rwkv_w1'][j]) @ prm['rwkv_w2'][j]).astype(F32)
    decay = jnp.exp(-jnp.exp(-jax.nn.softplus(-wpre) - 0.5))
    a = jax.nn.sigmoid((prm['rwkv_a0'][j] + (za @ prm['rwkv_a1'][j]) @ prm['rwkv_a2'][j]).astype(F32))
    g = jax.nn.sigmoid(zg @ prm['rwkv_g1'][j]) @ prm['rwkv_g2'][j]
    k32 = kx.astype(F32)
    kk = heads(k32 * prm['rwkv_kk'][j].astype(F32), H_B)
    kk = kk / jnp.maximum(jnp.sqrt(jnp.sum(kk * kk, axis=-1, keepdims=True)), 1e-12)
    k32 = k32 * (1.0 + (a - 1.0) * prm['rwkv_ka'][j].astype(F32))
    r_h, k_h, v_h, a_h = heads(r.astype(F32), H_B), heads(k32, H_B), heads(vx.astype(F32), H_B), heads(a, H_B)
    o_b, s_rwkv_new = rwkv7_scan(r_h, heads(decay, H_B), k_h, v_h, kk, a_h, s_rwkv)
    ln = prm['rwkv_ln'][j].astype(F32).reshape(2, H_B, HEAD_DIM)
    o_b = head_layernorm(o_b, RWKV_GN_EPS) * ln[0] + ln[1]
    o_b = o_b + jnp.sum(r_h * k_h * prm['rwkv_rk'][j].astype(F32), axis=-1, keepdims=True) * v_h
    o_b = o_b.reshape(B, L, GROUP_W).astype(dt) * g
    out = jnp.concatenate([o_a, o_b], axis=-1) @ prm['ab_w_out'][j]
    return out, s_ret_new.astype(dt), s_rwkv_new.astype(dt), buf_new


def diff_attn_prompt(q, k, v, lam):
    B, L, H, _, dh = q.shape
    blk = math.gcd(L, Q_BLOCK)
    nb = L // blk
    scale = dh ** -0.5
    qb = q.reshape(B, nb, blk, H, 2, dh).transpose(1, 0, 2, 3, 4, 5)
    kpos = jnp.arange(L)

    def one(args):
        qi, q0 = args
        s = jnp.einsum('bqhmd,bkhmd->bhmqk', qi, k).astype(F32) * scale
        mask = kpos[None, :] <= (q0 + jnp.arange(blk))[:, None]
        pr = jax.nn.softmax(jnp.where(mask, s, -jnp.inf), axis=-1)
        wgt = pr[:, :, 0] - lam * pr[:, :, 1]
        return jnp.einsum('bhqk,bkhv->bqhv', wgt.astype(v.dtype), v)

    o = lax.map(one, (qb, jnp.arange(nb) * blk))
    return o.transpose(1, 0, 2, 3, 4).reshape(B, L, H, v.shape[-1])


def diff_attn_decode(q, k, v, kp, vp, lam):
    Ld, dh = q.shape[1], q.shape[-1]
    P = kp.shape[1]
    scale = dh ** -0.5
    s_past = jnp.einsum('bqhmd,bkhmd->bhmqk', q, kp).astype(F32) * scale
    s_new = jnp.einsum('bqhmd,bkhmd->bhmqk', q, k).astype(F32) * scale
    causal = jnp.arange(Ld)[None, :] <= jnp.arange(Ld)[:, None]
    s_new = jnp.where(causal, s_new, -jnp.inf)
    pr = jax.nn.softmax(jnp.concatenate([s_past, s_new], axis=-1), axis=-1)
    wgt = (pr[:, :, 0] - lam * pr[:, :, 1]).astype(v.dtype)
    return (jnp.einsum('bhqk,bkhv->bqhv', wgt[..., :P], vp)
            + jnp.einsum('bhqk,bkhv->bqhv', wgt[..., P:], v))


def causal_conv(x, buf, w, b):
    L = x.shape[1]
    full = jnp.concatenate([buf.astype(x.dtype), x], axis=1)
    y = sum(full[:, t:t + L] * w[t] for t in range(CONV_W)) + b
    return y, full[:, L:]


def rglru(x, h0, wa, ba, wi, bi, lam):
    B, L, C = x.shape
    x32 = x.astype(F32)
    xb = x32.reshape(B, L, N_BLK_D, -1)
    r = jax.nn.sigmoid(jnp.einsum('blnd,nde->blne', xb, wa.astype(F32)).reshape(B, L, C) + ba.astype(F32))
    ig = jax.nn.sigmoid(jnp.einsum('blnd,nde->blne', xb, wi.astype(F32)).reshape(B, L, C) + bi.astype(F32))
    log_a = -LRU_C * r * jax.nn.softplus(-lam.astype(F32))
    a = jnp.exp(log_a)
    bt = jnp.sqrt(-jnp.expm1(2.0 * log_a)) * (ig * x32)
    bt = bt.at[:, 0].add(a[:, 0] * h0.astype(F32))

    def comb(e1, e2):
        a1, b1 = e1
        a2, b2 = e2
        return a1 * a2, a2 * b1 + b2

    _, h = lax.associative_scan(comb, (a, bt), axis=1)
    return h, h[:, -1]


def mix_odd(xn, lru_h, lru_buf, past, prm, i, j):
    B, L, _ = xn.shape
    dt = xn.dtype
    u = xn @ prm['cd_w_in'][j]
    qc, kc, vc, xr, gr = jnp.split(u, 5, axis=-1)
    q = qc.reshape(B, L, H_C, 2, HEAD_DIM)
    k = kc.reshape(B, L, H_C, 2, HEAD_DIM)
    v = vc.reshape(B, L, H_C, DV_C)
    lp = prm['diff_lam'][j].astype(F32)
    lam_init = 0.8 - 0.6 * math.exp(-0.3 * i)
    lam = jnp.exp(jnp.sum(lp[0] * lp[1])) - jnp.exp(jnp.sum(lp[2] * lp[3])) + lam_init
    if past is None:
        o_c = diff_attn_prompt(q, k, v, lam)
    else:
        o_c = diff_attn_decode(q, k, v, past[0], past[1], lam)
    o_c = (rmsnorm(o_c, prm['diff_subln'][j]) * (1.0 - lam_init)).reshape(B, L, GROUP_W)
    xconv, buf_new = causal_conv(xr, lru_buf, prm['lru_conv_w'][j], prm['lru_conv_b'][j])
    hseq, h_last = rglru(xconv, lru_h, prm['lru_wa'][j], prm['lru_ba'][j], prm['lru_wi'][j],
                         prm['lru_bi'][j], prm['lru_lambda'][j])
    o_d = hseq.astype(dt) * jax.nn.gelu(gr)
    out = jnp.concatenate([o_c, o_d], axis=-1) @ prm['cd_w_out'][j]
    return out, k.reshape(B, L, H_C, 2 * HEAD_DIM), v, h_last.astype(dt), buf_new


def trunk(x, p, pos0, s_ret, s_rwkv, s_shift, s_lru_h, s_lru_conv, pages, prm):
    B, L, _ = x.shape
    pos = pos0 + jnp.arange(L, dtype=F32)
    ret_l, rwkv_l, shift_l, k_l, v_l, lh_l, lc_l = [], [], [], [], [], [], []
    h = x
    for i in range(DEPTH):
        j = i // 2
        g = prm['norm_g'][i]
        h = h + 0.5 * rmsnorm(swiglu(rmsnorm(h, g[0]), prm['ffn_w_in'][i, 0], prm['ffn_w_out'][i, 0]), g[1])
        xn = rmsnorm(h, g[2])
        if i % 2 == 0:
            mo, sr, sw, sb = mix_even(xn, pos, s_ret[j], s_rwkv[j], s_shift[j], prm, j)
            ret_l.append(sr)
            rwkv_l.append(sw)
            shift_l.append(sb)
        else:
            past = None
            if pages is not None:
                ck, cv, pt = pages
                past = (ck[j][pt].reshape(B, -1, H_C, 2, HEAD_DIM),
                        cv[j][pt].reshape(B, -1, H_C, DV_C))
            mo, kn, vn, lh, lc = mix_odd(xn, s_lru_h[j], s_lru_conv[j], past, prm, i, j)
            k_l.append(kn)
            v_l.append(vn)
            lh_l.append(lh)
            lc_l.append(lc)
        h = h + rmsnorm(mo, g[3])
        h = h + 0.5 * rmsnorm(swiglu(rmsnorm(h, g[4]), prm['ffn_w_in'][i, 1], prm['ffn_w_out'][i, 1]), g[5])
        gate = jax.nn.sigmoid(rmsnorm(h, g[6]) @ prm['ple_gate_w'][i])
        h = h + rmsnorm(gate * (p[i] @ prm['ple_w'][i]), g[7])
    st = lambda lst: jnp.stack(lst, axis=0)
    return h, st(k_l), st(v_l), st(ret_l), st(rwkv_l), st(shift_l), st(lh_l), st(lc_l)


def setup_inputs(seed: int = 0) -> dict:
    key = jax.random.key(seed)
    ks = iter(jax.random.split(key, 64))

    def nrm(shape, scale):
        return jax.random.normal(next(ks), shape, F32) * scale

    def unif(shape, lo, hi):
        return jax.random.uniform(next(ks), shape, F32, lo, hi)

    n_a = (DEPTH + 1) // 2
    n_c = DEPTH // 2
    n_pages = PAST_LEN // PAGE_SIZE
    n_pool = (5 * DEC_BATCH * n_pages + 3) // 4
    W = GROUP_W
    page_table = jax.random.permutation(next(ks), n_pool)[:DEC_BATCH * n_pages]
    page_table = page_table.reshape(DEC_BATCH, n_pages).astype(jnp.int32)
    s_lru = unif((n_c, W), 0.9, 0.999) ** (1.0 / LRU_C)
    return {
        'x_prompt': nrm((BATCH, SEQ, D_MODEL), 1.0),
        'x_sample': nrm((DEC_BATCH, DEC_SEQ, D_MODEL), 1.0),
        'cache_k': nrm((n_c, n_pool, PAGE_SIZE, H_C, 2 * HEAD_DIM), 1.0),
        'cache_v': nrm((n_c, n_pool, PAGE_SIZE, H_C, DV_C), 1.0),
        'state_ret': nrm((n_a, DEC_BATCH, H_A, HEAD_DIM, HEAD_DIM), 0.5),
        'state_rwkv': nrm((n_a, DEC_BATCH, H_B, HEAD_DIM, HEAD_DIM), 0.5),
        'state_rwkv_shift': nrm((n_a, DEC_BATCH, 1, 4 * W), 1.0),
        'state_lru_h': nrm((n_c, DEC_BATCH, W), 0.5),
        'state_lru_conv': nrm((n_c, DEC_BATCH, CONV_W - 1, W), 1.0),
        'page_table': page_table,
        'p_prompt': nrm((DEPTH, BATCH, SEQ, P_DIM), 1.0),
        'p_sample': nrm((DEPTH, DEC_BATCH, DEC_SEQ, P_DIM), 1.0),
        'norm_g': 1.0 + nrm((DEPTH, N_NORMS, D_MODEL), 0.05),
        'ffn_w_in': nrm((DEPTH, 2, D_MODEL, 2 * D_FF), D_MODEL ** -0.5),
        'ffn_w_out': nrm((DEPTH, 2, D_FF, D_MODEL), D_FF ** -0.5),
        'ple_w': nrm((DEPTH, P_DIM, D_MODEL), P_DIM ** -0.5),
        'ple_gate_w': nrm((DEPTH, D_MODEL, D_MODEL), D_MODEL ** -0.5),
        'ab_w_in': nrm((n_a, D_MODEL, 8 * W), D_MODEL ** -0.5),
        'ab_w_out': nrm((n_a, 2 * W, D_MODEL), (2 * W) ** -0.5),
        'rwkv_mu': unif((n_a, 6, W), 0.0, 1.0),
        'rwkv_w0': nrm((n_a, W), 0.5) - 0.5,
        'rwkv_w1': nrm((n_a, W, RWKV_LORA_W), W ** -0.5),
        'rwkv_w2': nrm((n_a, RWKV_LORA_W, W), 0.1 * RWKV_LORA_W ** -0.5),
        'rwkv_a0': nrm((n_a, W), 0.1),
        'rwkv_a1': nrm((n_a, W, RWKV_LORA_A), W ** -0.5),
        'rwkv_a2': nrm((n_a, RWKV_LORA_A, W), 0.1 * RWKV_LORA_A ** -0.5),
        'rwkv_g1': nrm((n_a, W, RWKV_LORA_G), W ** -0.5),
        'rwkv_g2': nrm((n_a, RWKV_LORA_G, W), RWKV_LORA_G ** -0.5),
        'rwkv_kk': 0.85 + nrm((n_a, W), 0.05),
        'rwkv_ka': 1.0 + nrm((n_a, W), 0.05),
        'rwkv_rk': nrm((n_a, H_B, HEAD_DIM), 0.1),
        'rwkv_ln': jnp.stack([1.0 + nrm((n_a, W), 0.05), nrm((n_a, W), 0.02)], axis=1),
        'cd_w_in': nrm((n_c, D_MODEL, 5 * W), D_MODEL ** -0.5),
        'cd_w_out': nrm((n_c, 2 * W, D_MODEL), (2 * W) ** -0.5),
        'diff_lam': nrm((n_c, 4, HEAD_DIM), 0.1),
        'diff_subln': 1.0 + nrm((n_c, DV_C), 0.05),
        'lru_conv_w': nrm((n_c, CONV_W, W), CONV_W ** -0.5),
        'lru_conv_b': nrm((n_c, W), 0.02),
        'lru_wa': nrm((n_c, N_BLK_D, HEAD_DIM, HEAD_DIM), HEAD_DIM ** -0.5),
        'lru_ba': nrm((n_c, W), 0.02),
        'lru_wi': nrm((n_c, N_BLK_D, HEAD_DIM, HEAD_DIM), HEAD_DIM ** -0.5),
        'lru_bi': nrm((n_c, W), 0.02),
        'lru_lambda': jnp.log(s_lru) - jnp.log1p(-s_lru),
    }


def reference(x_prompt, x_sample, cache_k, cache_v, state_ret, state_rwkv, state_rwkv_shift,
              state_lru_h, state_lru_conv, page_table, p_prompt, p_sample, norm_g, ffn_w_in,
              ffn_w_out, ple_w, ple_gate_w, ab_w_in, ab_w_out, rwkv_mu, rwkv_w0, rwkv_w1, rwkv_w2,
              rwkv_a0, rwkv_a1, rwkv_a2, rwkv_g1, rwkv_g2, rwkv_kk, rwkv_ka, rwkv_rk, rwkv_ln,
              cd_w_in, cd_w_out, diff_lam, diff_subln, lru_conv_w, lru_conv_b, lru_wa, lru_ba,
              lru_wi, lru_bi, lru_lambda):
    prm = dict(norm_g=norm_g, ffn_w_in=ffn_w_in, ffn_w_out=ffn_w_out, ple_w=ple_w,
               ple_gate_w=ple_gate_w, ab_w_in=ab_w_in, ab_w_out=ab_w_out, rwkv_mu=rwkv_mu,
               rwkv_w0=rwkv_w0, rwkv_w1=rwkv_w1, rwkv_w2=rwkv_w2, rwkv_a0=rwkv_a0,
               rwkv_a1=rwkv_a1, rwkv_a2=rwkv_a2, rwkv_g1=rwkv_g1, rwkv_g2=rwkv_g2,
               rwkv_kk=rwkv_kk, rwkv_ka=rwkv_ka, rwkv_rk=rwkv_rk, rwkv_ln=rwkv_ln,
               cd_w_in=cd_w_in, cd_w_out=cd_w_out, diff_lam=diff_lam, diff_subln=diff_subln,
               lru_conv_w=lru_conv_w, lru_conv_b=lru_conv_b, lru_wa=lru_wa, lru_ba=lru_ba,
               lru_wi=lru_wi, lru_bi=lru_bi, lru_lambda=lru_lambda)
    dt = x_prompt.dtype
    bp = x_prompt.shape[0]
    n_a = state_ret.shape[0]
    n_c = state_lru_h.shape[0]
    yp, kp, vp, rp, wp, sp, hp, cp = trunk(
        x_prompt, p_prompt, 0,
        jnp.zeros((n_a, bp, H_A, HEAD_DIM, HEAD_DIM), dt),
        jnp.zeros((n_a, bp, H_B, HEAD_DIM, HEAD_DIM), dt),
        jnp.zeros((n_a, bp, 1, 4 * GROUP_W), dt),
        jnp.zeros((n_c, bp, GROUP_W), dt),
        jnp.zeros((n_c, bp, CONV_W - 1, GROUP_W), dt),
        None, prm)
    ys, ks_, vs, rs, ws, ss, hs, cs = trunk(
        x_sample, p_sample, PAST_LEN, state_ret, state_rwkv, state_rwkv_shift,
        state_lru_h, state_lru_conv, (cache_k, cache_v, page_table), prm)
    return (yp, ys, kp, vp, rp, wp, sp, hp, cp, ks_, vs, rs, ws, ss, hs, cs)
```

```python
import functools
import math

import jax
import jax.numpy as jnp
from jax import lax
from jax.experimental import pallas as pl
from jax.experimental.pallas import tpu as pltpu

F32 = jnp.float32
BF16 = jnp.bfloat16

HEAD_DIM = 64
CONV_W = 4
LRU_C = 8.0
ROPE_BASE = 10000.0
EPS = 1e-6
RWKV_GN_EPS = 64e-5
RET_CHUNK = 256
ATTN_BLOCK = 512
SCAN_CHUNK = 128
LANES = 128
SUBLANES = 8
VMEM_LIMIT = 48 * 1024 * 1024


def _cparams(sem):
    return pltpu.CompilerParams(dimension_semantics=sem, vmem_limit_bytes=VMEM_LIMIT)


def _tile(n, pref):
    t = min(n, pref)
    while n % t:
        t //= 2
    return t


def _rms(x, g):
    return x * lax.rsqrt(jnp.mean(x * x, axis=-1, keepdims=True) + EPS) * g


def _dot(a, b):
    return jnp.dot(a.astype(BF16), b.astype(BF16), preferred_element_type=F32)


def _seg_sum(x, ones_bd):
    hi = x.astype(BF16)
    lo = (x - hi.astype(F32)).astype(BF16)
    return (jnp.dot(hi, ones_bd, preferred_element_type=F32)
            + jnp.dot(lo, ones_bd, preferred_element_type=F32))


def _ffn_kernel(h_ref, gpre_ref, wg_ref, wu_ref, wo_ref, gpost_ref, o_ref, xn_ref, acc_ref):
    j = pl.program_id(1)

    @pl.when(j == 0)
    def _():
        xn_ref[...] = _rms(h_ref[...], gpre_ref[...]).astype(BF16)
        acc_ref[...] = jnp.zeros_like(acc_ref)

    xn = xn_ref[...]
    gate = jnp.dot(xn, wg_ref[...], preferred_element_type=F32)
    up = jnp.dot(xn, wu_ref[...], preferred_element_type=F32)
    act = (gate * jax.nn.sigmoid(gate) * up).astype(BF16)
    acc_ref[...] += jnp.dot(act, wo_ref[...], preferred_element_type=F32)

    @pl.when(j == pl.num_programs(1) - 1)
    def _():
        o_ref[...] = h_ref[...] + 0.5 * _rms(acc_ref[...], gpost_ref[...])


def ffn_block(h, g_pre, w_in, w_out, g_post):
    m, d = h.shape
    f = w_out.shape[0]
    tm, tf = _tile(m, 1024), _tile(f, 512)
    nf = f // tf
    return pl.pallas_call(
        _ffn_kernel,
        grid=(m // tm, nf),
        in_specs=[
            pl.BlockSpec((tm, d), lambda i, j: (i, 0)),
            pl.BlockSpec((1, d), lambda i, j: (0, 0)),
            pl.BlockSpec((d, tf), lambda i, j: (0, j)),
            pl.BlockSpec((d, tf), lambda i, j: (0, j + nf)),
            pl.BlockSpec((tf, d), lambda i, j: (j, 0)),
            pl.BlockSpec((1, d), lambda i, j: (0, 0)),
        ],
        out_specs=pl.BlockSpec((tm, d), lambda i, j: (i, 0)),
        out_shape=jax.ShapeDtypeStruct((m, d), F32),
        scratch_shapes=[pltpu.VMEM((tm, d), BF16), pltpu.VMEM((tm, d), F32)],
        compiler_params=_cparams(("parallel", "arbitrary")),
        name="ffn_block",
    )(h, g_pre, w_in, w_in, w_out, g_post)


def _norm_matmul_kernel(h_ref, g_ref, w_ref, o_ref, xn_ref):
    @pl.when(pl.program_id(1) == 0)
    def _():
        xn_ref[...] = _rms(h_ref[...], g_ref[...]).astype(BF16)

    o_ref[0] = jnp.dot(xn_ref[...], w_ref[...], preferred_element_type=F32)


def norm_matmul(h, g, w):
    m, d = h.shape
    n = w.shape[1]
    tn = 512
    tm = _tile(m, 512)
    return pl.pallas_call(
        _norm_matmul_kernel,
        grid=(m // tm, n // tn),
        in_specs=[
            pl.BlockSpec((tm, d), lambda i, j: (i, 0)),
            pl.BlockSpec((1, d), lambda i, j: (0, 0)),
            pl.BlockSpec((d, tn), lambda i, j: (0, j)),
        ],
        out_specs=pl.BlockSpec((1, tm, tn), lambda i, j: (j, i, 0)),
        out_shape=jax.ShapeDtypeStruct((n // tn, m, tn), F32),
        scratch_shapes=[pltpu.VMEM((tm, d), BF16)],
        compiler_params=_cparams(("parallel", "arbitrary")),
        name="norm_matmul",
    )(h, g, w)


def _out_proj_kernel(oa_ref, ob_ref, wa_ref, wb_ref, h_ref, g_ref, o_ref):
    y = _dot(oa_ref[...], wa_ref[...]) + _dot(ob_ref[...], wb_ref[...])
    o_ref[...] = h_ref[...] + _rms(y, g_ref[...])


def out_proj(oa, ob, w, h, g):
    m, d = h.shape
    gw = oa.shape[1]
    tm = _tile(m, 512)
    return pl.pallas_call(
        _out_proj_kernel,
        grid=(m // tm,),
        in_specs=[
            pl.BlockSpec((tm, gw), lambda i: (i, 0)),
            pl.BlockSpec((tm, gw), lambda i: (i, 0)),
            pl.BlockSpec((gw, d), lambda i: (0, 0)),
            pl.BlockSpec((gw, d), lambda i: (1, 0)),
            pl.BlockSpec((tm, d), lambda i: (i, 0)),
            pl.BlockSpec((1, d), lambda i: (0, 0)),
        ],
        out_specs=pl.BlockSpec((tm, d), lambda i: (i, 0)),
        out_shape=jax.ShapeDtypeStruct((m, d), F32),
        compiler_params=_cparams(("parallel",)),
        name="out_proj",
    )(oa, ob, w, w, h, g)


def _ple_kernel(h_ref, g6_ref, wg_ref, p_ref, wp_ref, g7_ref, o_ref):
    h = h_ref[...]
    gate = jax.nn.sigmoid(_dot(_rms(h, g6_ref[...]), wg_ref[...]))
    y = gate * _dot(p_ref[...], wp_ref[...])
    o_ref[...] = h + _rms(y, g7_ref[...])


def ple_block(h, g6, wg, p, wp, g7):
    m, d = h.shape
    pd = p.shape[1]
    tm = _tile(m, 512)
    return pl.pallas_call(
        _ple_kernel,
        grid=(m // tm,),
        in_specs=[
            pl.BlockSpec((tm, d), lambda i: (i, 0)),
            pl.BlockSpec((1, d), lambda i: (0, 0)),
            pl.BlockSpec((d, d), lambda i: (0, 0)),
            pl.BlockSpec((tm, pd), lambda i: (i, 0)),
            pl.BlockSpec((pd, d), lambda i: (0, 0)),
            pl.BlockSpec((1, d), lambda i: (0, 0)),
        ],
        out_specs=pl.BlockSpec((tm, d), lambda i: (i, 0)),
        out_shape=jax.ShapeDtypeStruct((m, d), F32),
        compiler_params=_cparams(("parallel",)),
        name="ple_block",
    )(h, g6, wg, p, wp, g7)


def _retention_tables(n_heads, c, pos):
    lg = jnp.log1p(-jnp.exp2(-5.0 - jnp.arange(n_heads, dtype=F32)))
    idx = jnp.arange(c, dtype=F32)
    rel = idx[:, None] - idx[None, :]
    dmask = jnp.where(rel[None] >= 0, jnp.exp(jnp.maximum(rel, 0.0)[None] * lg[:, None, None]), 0.0)
    rep = lambda t: jnp.repeat(t, HEAD_DIM, axis=-1)
    q_dec = rep(jnp.exp((idx[:, None] + 1.0) * lg[None, :]))
    k_dec = rep(jnp.exp((c - 1.0 - idx[:, None]) * lg[None, :]))
    c_dec = rep(jnp.exp(c * lg)[None, :])
    half = HEAD_DIM // 2
    freq = 1.0 / (ROPE_BASE ** jnp.linspace(0.0, 1.0, half, dtype=F32))
    ang = pos[:, None] * freq[None, :]
    cos, sin = jnp.cos(ang), jnp.sin(ang)
    cos_t = jnp.tile(jnp.concatenate([cos, cos], axis=-1), (1, n_heads))
    sin_t = jnp.tile(jnp.concatenate([-sin, sin], axis=-1), (1, n_heads))
    return dmask, q_dec, k_dec, c_dec, cos_t, sin_t


def _retention_kernel(q_ref, k_ref, v_ref, g_ref, cos_ref, sin_ref, dmask_ref, qdec_ref, kdec_ref,
                      cdec_ref, o_ref, s_ref, s_scr, *, n_heads):
    c = pl.program_id(1)

    @pl.when(c == 0)
    def _():
        s_scr[...] = jnp.zeros_like(s_scr)

    q, k, v, g = q_ref[0], k_ref[0], v_ref[0], g_ref[0]
    cos, sin = cos_ref[...], sin_ref[...]
    width = q.shape[-1]
    lane = lax.broadcasted_iota(jnp.int32, q.shape, 1)
    first_half = (lane % HEAD_DIM) < (HEAD_DIM // 2)

    def rot(x):
        swapped = jnp.where(first_half, pltpu.roll(x, width - HEAD_DIM // 2, 1),
                            pltpu.roll(x, HEAD_DIM // 2, 1))
        return x * cos + swapped * sin

    qr = rot(q)
    kr = rot(k) * (HEAD_DIM ** -0.5)
    kd = kr * kdec_ref[...]
    qdec = qdec_ref[...]
    cdec = cdec_ref[...]
    gate = g * jax.nn.sigmoid(g)
    for h in range(n_heads):
        sl = slice(h * HEAD_DIM, (h + 1) * HEAD_DIM)
        qh = qr[:, sl].astype(BF16)
        kh = kr[:, sl].astype(BF16)
        vh = v[:, sl].astype(BF16)
        att = lax.dot_general(qh, kh, (((1,), (1,)), ((), ())), preferred_element_type=F32) * dmask_ref[h]
        s_old = s_scr[h]
        o = (jnp.dot(att.astype(BF16), vh, preferred_element_type=F32)
             + jnp.dot(qh, s_old.astype(BF16), preferred_element_type=F32) * qdec[:, sl])
        s_scr[h] = s_old * cdec[:, sl] + lax.dot_general(
            kd[:, sl].astype(BF16), vh, (((0,), (0,)), ((), ())), preferred_element_type=F32)
        oc = o - jnp.mean(o, axis=-1, keepdims=True)
        on = oc * lax.rsqrt(jnp.mean(oc * oc, axis=-1, keepdims=True) + EPS)
        o_ref[0, :, sl] = on * gate[:, sl]

    @pl.when(c == pl.num_programs(1) - 1)
    def _():
        s_ref[0] = s_scr[...]


def retention_prompt(q, k, v, g, pos):
    b, l, width = q.shape
    n_heads = width // HEAD_DIM
    c = _tile(l, RET_CHUNK)
    dmask, q_dec, k_dec, c_dec, cos_t, sin_t = _retention_tables(n_heads, c, pos)
    seq = pl.BlockSpec((1, c, width), lambda i, j: (i, j, 0))
    tab = pl.BlockSpec((c, width), lambda i, j: (j, 0))
    fixed = lambda shape: pl.BlockSpec(shape, lambda i, j: (0,) * len(shape))
    return pl.pallas_call(
        functools.partial(_retention_kernel, n_heads=n_heads),
        grid=(b, l // c),
        in_specs=[seq, seq, seq, seq, tab, tab, fixed((n_heads, c, c)), fixed((c, width)),
                  fixed((c, width)), fixed((1, width))],
        out_specs=[seq, pl.BlockSpec((1, n_heads, HEAD_DIM, HEAD_DIM), lambda i, j: (i, 0, 0, 0))],
        out_shape=[jax.ShapeDtypeStruct((b, l, width), F32),
                   jax.ShapeDtypeStruct((b, n_heads, HEAD_DIM, HEAD_DIM), F32)],
        scratch_shapes=[pltpu.VMEM((n_heads, HEAD_DIM, HEAD_DIM), F32)],
        compiler_params=_cparams(("parallel", "arbitrary")),
        name="retention_prompt",
    )(q, k, v, g, cos_t, sin_t, dmask, q_dec, k_dec, c_dec)


def _retention_step_kernel(q_ref, k_ref, v_ref, g_ref, cos_ref, sin_ref, gam_ref, s0_ref, o_ref, s_ref):
    cos, sin = cos_ref[...], sin_ref[...]

    def rot(x):
        half = HEAD_DIM // 2
        return x * cos + jnp.concatenate([x[:, :, half:], x[:, :, :half]], axis=2) * sin

    q = rot(q_ref[...])
    k = rot(k_ref[...]) * (HEAD_DIM ** -0.5)
    v, g = v_ref[...], g_ref[...]
    gam = gam_ref[...]
    s0 = s0_ref[...]
    att = jnp.sum(q * k, axis=2, keepdims=True)
    o = att * v + jnp.sum(q * s0, axis=2, keepdims=True) * gam
    s_ref[...] = s0 * gam + k * v
    oc = o - jnp.mean(o, axis=-1, keepdims=True)
    on = oc * lax.rsqrt(jnp.mean(oc * oc, axis=-1, keepdims=True) + EPS)
    o_ref[...] = on * (g * jax.nn.sigmoid(g))


def retention_step(q, k, v, g, s0, pos):
    b, width = q.shape
    n_heads = width // HEAD_DIM
    bb = _tile(b, 8)
    half = HEAD_DIM // 2
    freq = 1.0 / (ROPE_BASE ** jnp.linspace(0.0, 1.0, half, dtype=F32))
    ang = pos * freq
    cos_c = jnp.concatenate([jnp.cos(ang), jnp.cos(ang)])[:, None]
    sin_c = jnp.concatenate([-jnp.sin(ang), jnp.sin(ang)])[:, None]
    gam = jnp.exp(jnp.log1p(-jnp.exp2(-5.0 - jnp.arange(n_heads, dtype=F32)))).reshape(n_heads, 1, 1)
    col = lambda t: t.reshape(b, n_heads, HEAD_DIM, 1)
    row = lambda t: t.reshape(b, n_heads, 1, HEAD_DIM)
    cspec = pl.BlockSpec((bb, n_heads, HEAD_DIM, 1), lambda i: (i, 0, 0, 0))
    rspec = pl.BlockSpec((bb, n_heads, 1, HEAD_DIM), lambda i: (i, 0, 0, 0))
    sspec = pl.BlockSpec((bb, n_heads, HEAD_DIM, HEAD_DIM), lambda i: (i, 0, 0, 0))
    o, s = pl.pallas_call(
        _retention_step_kernel,
        grid=(b // bb,),
        in_specs=[cspec, cspec, rspec, rspec,
                  pl.BlockSpec((HEAD_DIM, 1), lambda i: (0, 0)), pl.BlockSpec((HEAD_DIM, 1), lambda i: (0, 0)),
                  pl.BlockSpec((n_heads, 1, 1), lambda i: (0, 0, 0)), sspec],
        out_specs=[rspec, sspec],
        out_shape=[jax.ShapeDtypeStruct((b, n_heads, 1, HEAD_DIM), F32),
                   jax.ShapeDtypeStruct((b, n_heads, HEAD_DIM, HEAD_DIM), F32)],
        compiler_params=_cparams(("parallel",)),
        name="retention_step",
    )(col(q), col(k), row(v), row(g), cos_c, sin_c, gam, s0)
    return o.reshape(b, width), s


def _softplus(x):
    return jnp.maximum(x, 0.0) + jnp.log1p(jnp.exp(-jnp.abs(x)))


def _rwkv_prep_kernel(rr_ref, kr_ref, vr_ref, zr_ref, pr_ref, pk_ref, pv_ref, pz_ref, mu_ref,
                      w0_ref, w1_ref, w2_ref, a0_ref, a1_ref, a2_ref, g1_ref, g2_ref, kkp_ref,
                      kap_ref, rk_ref, ones_ref,
                      r_out, w_out, k_out, v_out, kk_out, kka_out, g_out, bonus_out):
    mu = mu_ref[...]
    lerp = lambda x, xp, i: x + (xp - x) * mu[i:i + 1]
    zr, pz = zr_ref[...], pz_ref[...]
    r = lerp(rr_ref[...], pr_ref[...], 0)
    kx = lerp(kr_ref[...], pk_ref[...], 1)
    vx = lerp(vr_ref[...], pv_ref[...], 2)
    zw, za, zg = lerp(zr, pz, 3), lerp(zr, pz, 4), lerp(zr, pz, 5)
    wpre = w0_ref[...] + _dot(jnp.tanh(_dot(zw, w1_ref[...])), w2_ref[...])
    decay = jnp.exp(-jnp.exp(-_softplus(-wpre) - 0.5))
    a = jax.nn.sigmoid(a0_ref[...] + _dot(_dot(za, a1_ref[...]), a2_ref[...]))
    g = _dot(jax.nn.sigmoid(_dot(zg, g1_ref[...])), g2_ref[...])
    ones_bd = ones_ref[...]
    kk = kx * kkp_ref[...]
    kk = kk / jnp.maximum(jnp.sqrt(_seg_sum(kk * kk, ones_bd)), 1e-12)
    k32 = kx * (1.0 + (a - 1.0) * kap_ref[...])
    r_out[...] = r
    w_out[...] = decay
    k_out[...] = k32
    v_out[...] = vx
    kk_out[...] = kk
    kka_out[...] = kk * a
    g_out[...] = g
    bonus_out[...] = _seg_sum(r * k32 * rk_ref[...], ones_bd) * vx


def rwkv_prep(cur, prev, prm):
    m, w = cur[0].shape
    tm = _tile(m, 256)
    rows = pl.BlockSpec((tm, w), lambda i: (i, 0))
    full = lambda a: pl.BlockSpec(a.shape, lambda i: (0,) * a.ndim)
    params = [prm[n] for n in ("mu", "w0", "w1", "w2", "a0", "a1", "a2", "g1", "g2", "kk", "ka", "rk", "ones")]
    return pl.pallas_call(
        _rwkv_prep_kernel,
        grid=(m // tm,),
        in_specs=[rows] * 8 + [full(a) for a in params],
        out_specs=[rows] * 8,
        out_shape=[jax.ShapeDtypeStruct((m, w), F32)] * 8,
        compiler_params=_cparams(("parallel",)),
        name="rwkv_prep",
    )(*cur, *prev, *params)


def _rwkv_post_kernel(o_ref, bonus_ref, g_ref, ln_ref, ones_ref, out_ref):
    ones_bd = ones_ref[...]
    o = o_ref[...]
    inv = 1.0 / HEAD_DIM
    oc = o - _seg_sum(o, ones_bd) * inv
    on = oc * lax.rsqrt(_seg_sum(oc * oc, ones_bd) * inv + RWKV_GN_EPS)
    ln = ln_ref[...]
    out_ref[...] = (on * ln[0:1] + ln[1:2] + bonus_ref[...]) * g_ref[...]


def rwkv_post(o, bonus, g, ln, ones_bd):
    m, w = o.shape
    tm = _tile(m, 512)
    rows = pl.BlockSpec((tm, w), lambda i: (i, 0))
    return pl.pallas_call(
        _rwkv_post_kernel,
        grid=(m // tm,),
        in_specs=[rows, rows, rows, pl.BlockSpec(ln.shape, lambda i: (0, 0)),
                  pl.BlockSpec(ones_bd.shape, lambda i: (0, 0))],
        out_specs=rows,
        out_shape=jax.ShapeDtypeStruct((m, w), F32),
        compiler_params=_cparams(("parallel",)),
        name="rwkv_post",
    )(o, bonus, g, ln, ones_bd)


def _rwkv_scan_kernel(r_ref, w_ref, k_ref, kk_ref, kka_ref, vt_ref, ot_ref, s_ref, s_scr, *, n_pairs):
    c = pl.program_id(1)
    tc = r_ref.shape[1]

    @pl.when(c == 0)
    def _():
        s_scr[...] = jnp.zeros_like(s_scr)

    ot_ref[...] = jnp.zeros_like(ot_ref)
    lo = lax.broadcasted_iota(jnp.int32, (HEAD_DIM, LANES), 1) < HEAD_DIM
    t_lane = lax.broadcasted_iota(jnp.int32, (HEAD_DIM, tc), 1)

    def pair_sum(x):
        a = jnp.sum(jnp.where(lo, x, 0.0), axis=-1, keepdims=True)
        b = jnp.sum(jnp.where(lo, 0.0, x), axis=-1, keepdims=True)
        return a, b

    def block(tb, carry):
        base = pl.multiple_of(tb * SUBLANES, SUBLANES)
        for p in range(n_pairs):
            ln = slice(p * LANES, (p + 1) * LANES)
            ra = slice(p * LANES, p * LANES + HEAD_DIM)
            rb = slice(p * LANES + HEAD_DIM, (p + 1) * LANES)
            rows = lambda ref: ref[0, pl.ds(base, SUBLANES), ln]
            kk8, w8, kka8, k8, r8 = rows(kk_ref), rows(w_ref), rows(kka_ref), rows(k_ref), rows(r_ref)
            s = s_scr[p]
            acc_a = jnp.zeros((HEAD_DIM, tc), F32)
            acc_b = jnp.zeros((HEAD_DIM, tc), F32)
            for i in range(SUBLANES):
                at_t = t_lane == base + i
                row = lambda x: x[i:i + 1, :]
                sa_a, sa_b = pair_sum(s * row(kk8))
                sa = -jnp.where(lo, sa_a, sa_b)
                v_a = jnp.sum(jnp.where(at_t, vt_ref[0, ra, :], 0.0), axis=-1, keepdims=True)
                v_b = jnp.sum(jnp.where(at_t, vt_ref[0, rb, :], 0.0), axis=-1, keepdims=True)
                s = s * row(w8) + sa * row(kka8) + jnp.where(lo, v_a, v_b) * row(k8)
                o_a, o_b = pair_sum(s * row(r8))
                acc_a = jnp.where(at_t, o_a, acc_a)
                acc_b = jnp.where(at_t, o_b, acc_b)
            s_scr[p] = s
            ot_ref[0, ra, :] += acc_a
            ot_ref[0, rb, :] += acc_b
        return carry

    lax.fori_loop(0, tc // SUBLANES, block, 0)

    @pl.when(c == pl.num_programs(1) - 1)
    def _():
        s_ref[0] = s_scr[...]


def rwkv_scan_prompt(r, w, k, kk, kka, v):
    b, l, width = r.shape
    n_pairs = width // LANES
    tc = _tile(l, SCAN_CHUNK)
    vt = jnp.swapaxes(v, 1, 2)
    seq = pl.BlockSpec((1, tc, width), lambda i, j: (i, j, 0))
    tseq = pl.BlockSpec((1, width, tc), lambda i, j: (i, 0, j))
    ot, s = pl.pallas_call(
        functools.partial(_rwkv_scan_kernel, n_pairs=n_pairs),
        grid=(b, l // tc),
        in_specs=[seq, seq, seq, seq, seq, tseq],
        out_specs=[tseq, pl.BlockSpec((1, n_pairs, HEAD_DIM, LANES), lambda i, j: (i, 0, 0, 0))],
        out_shape=[jax.ShapeDtypeStruct((b, width, l), F32),
                   jax.ShapeDtypeStruct((b, n_pairs, HEAD_DIM, LANES), F32)],
        scratch_shapes=[pltpu.VMEM((n_pairs, HEAD_DIM, LANES), F32)],
        compiler_params=_cparams(("parallel", "arbitrary")),
        name="rwkv_scan_prompt",
    )(r, w, k, kk, kka, vt)
    s = s.reshape(b, n_pairs, HEAD_DIM, 2, HEAD_DIM).transpose(0, 1, 3, 2, 4)
    return jnp.swapaxes(ot, 1, 2), s.reshape(b, 2 * n_pairs, HEAD_DIM, HEAD_DIM)


def _rwkv_step_kernel(r_ref, w_ref, k_ref, kk_ref, kka_ref, v_ref, s0_ref, o_ref, s_ref):
    s0 = s0_ref[...]
    sa = -jnp.sum(s0 * kk_ref[...], axis=-1, keepdims=True)
    s = s0 * w_ref[...] + sa * kka_ref[...] + v_ref[...] * k_ref[...]
    s_ref[...] = s
    o_ref[...] = jnp.sum(s * r_ref[...], axis=-1, keepdims=True)


def rwkv_step(r, w, k, kk, kka, v, s0):
    b, width = r.shape
    n_heads = width // HEAD_DIM
    bb = _tile(b, 8)
    row = lambda t: t.reshape(b, n_heads, 1, HEAD_DIM)
    cspec = pl.BlockSpec((bb, n_heads, HEAD_DIM, 1), lambda i: (i, 0, 0, 0))
    rspec = pl.BlockSpec((bb, n_heads, 1, HEAD_DIM), lambda i: (i, 0, 0, 0))
    sspec = pl.BlockSpec((bb, n_heads, HEAD_DIM, HEAD_DIM), lambda i: (i, 0, 0, 0))
    o, s = pl.pallas_call(
        _rwkv_step_kernel,
        grid=(b // bb,),
        in_specs=[rspec] * 5 + [cspec, sspec],
        out_specs=[cspec, sspec],
        out_shape=[jax.ShapeDtypeStruct((b, n_heads, HEAD_DIM, 1), F32),
                   jax.ShapeDtypeStruct((b, n_heads, HEAD_DIM, HEAD_DIM), F32)],
        compiler_params=_cparams(("parallel",)),
        name="rwkv_step",
    )(row(r), row(w), row(k), row(kk), row(kka), v.reshape(b, n_heads, HEAD_DIM, 1), s0)
    return o.reshape(b, width), s


def _diff_lambda(lp, lam_init):
    e1 = jnp.exp(jnp.sum(lp[0:1] * lp[1:2], axis=-1, keepdims=True))
    e2 = jnp.exp(jnp.sum(lp[2:3] * lp[3:4], axis=-1, keepdims=True))
    return e1 - e2 + lam_init


def _diff_attn_kernel(q_ref, k_ref, v_ref, lam_ref, subln_ref, o_ref, m_scr, l_scr, acc_scr, *, lam_init):
    i, j = pl.program_id(2), pl.program_id(3)
    tq, tk = q_ref.shape[1], k_ref.shape[1]
    scale = HEAD_DIM ** -0.5

    @pl.when(j == 0)
    def _():
        m_scr[...] = jnp.full_like(m_scr, -jnp.inf)
        l_scr[...] = jnp.zeros_like(l_scr)
        acc_scr[...] = jnp.zeros_like(acc_scr)

    @pl.when(j <= i)
    def _():
        q, k = q_ref[0].astype(BF16), k_ref[0].astype(BF16)
        v = v_ref[0].astype(BF16)
        qpos = i * tq + lax.broadcasted_iota(jnp.int32, (tq, tk), 0)
        kpos = j * tk + lax.broadcasted_iota(jnp.int32, (tq, tk), 1)
        visible = kpos <= qpos
        for mi in range(2):
            sl = slice(mi * HEAD_DIM, (mi + 1) * HEAD_DIM)
            s = lax.dot_general(q[:, sl], k[:, sl], (((1,), (1,)), ((), ())),
                                preferred_element_type=F32) * scale
            s = jnp.where(visible, s, -jnp.inf)
            m_old = m_scr[mi]
            m_new = jnp.maximum(m_old, jnp.max(s, axis=-1, keepdims=True))
            alpha = jnp.exp(m_old - m_new)
            p = jnp.exp(s - m_new)
            l_scr[mi] = alpha * l_scr[mi] + jnp.sum(p, axis=-1, keepdims=True)
            acc_scr[mi] = alpha * acc_scr[mi] + jnp.dot(p.astype(BF16), v, preferred_element_type=F32)
            m_scr[mi] = m_new

    @pl.when(j == i)
    def _():
        lam = _diff_lambda(lam_ref[...], lam_init)
        o = acc_scr[0] / l_scr[0] - lam * (acc_scr[1] / l_scr[1])
        o_ref[0] = _rms(o, subln_ref[...]) * (1.0 - lam_init)


def diff_attn_prompt(q, k, v, lam_p, subln, lam_init):
    b, l, width = q.shape
    dv = 2 * HEAD_DIM
    n_heads = width // dv
    t = _tile(l, ATTN_BLOCK)
    n = l // t
    qspec = pl.BlockSpec((1, t, dv), lambda bi, h, i, j: (bi, i, h))
    kspec = pl.BlockSpec((1, t, dv), lambda bi, h, i, j: (bi, jnp.minimum(i, j), h))
    return pl.pallas_call(
        functools.partial(_diff_attn_kernel, lam_init=lam_init),
        grid=(b, n_heads, n, n),
        in_specs=[qspec, kspec, kspec,
                  pl.BlockSpec(lam_p.shape, lambda bi, h, i, j: (0, 0)),
                  pl.BlockSpec(subln.shape, lambda bi, h, i, j: (0, 0))],
        out_specs=qspec,
        out_shape=jax.ShapeDtypeStruct((b, l, width), F32),
        scratch_shapes=[pltpu.VMEM((2, t, 1), F32), pltpu.VMEM((2, t, 1), F32), pltpu.VMEM((2, t, dv), F32)],
        compiler_params=_cparams(("parallel", "parallel", "parallel", "arbitrary")),
        name="diff_attn_prompt",
    )(q, k, v, lam_p, subln)


def _diff_attn_decode_kernel(pt_ref, q_ref, kn_ref, vn_ref, kc_ref, vc_ref, lam_ref, subln_ref, o_ref,
                             m_scr, l_scr, acc_scr, *, lam_init, n_heads):
    p = pl.program_id(1)
    n_rows = 2 * n_heads
    width = q_ref.shape[-1]
    scale = HEAD_DIM ** -0.5
    dv = 2 * HEAD_DIM
    row = lax.broadcasted_iota(jnp.int32, (n_rows, width), 0)
    lane = lax.broadcasted_iota(jnp.int32, (n_rows, width), 1)
    qmat = jnp.where(lane // HEAD_DIM == row, q_ref[0], 0.0)

    @pl.when(p == 0)
    def _():
        m_scr[...] = jnp.full_like(m_scr, -jnp.inf)
        l_scr[...] = jnp.zeros_like(l_scr)
        acc_scr[...] = jnp.zeros_like(acc_scr)

    s = lax.dot_general(qmat.astype(BF16), kc_ref[0].astype(BF16), (((1,), (1,)), ((), ())),
                        preferred_element_type=F32) * scale
    m_old = m_scr[...]
    m_new = jnp.maximum(m_old, jnp.max(s, axis=-1, keepdims=True))
    alpha = jnp.exp(m_old - m_new)
    pr = jnp.exp(s - m_new)
    l_scr[...] = alpha * l_scr[...] + jnp.sum(pr, axis=-1, keepdims=True)
    acc_scr[...] = alpha * acc_scr[...] + jnp.dot(pr.astype(BF16), vc_ref[0].astype(BF16),
                                                  preferred_element_type=F32)
    m_scr[...] = m_new

    @pl.when(p == pl.num_programs(1) - 1)
    def _():
        s_new = jnp.sum(qmat * kn_ref[0], axis=-1, keepdims=True) * scale
        m_old = m_scr[...]
        m_fin = jnp.maximum(m_old, s_new)
        alpha = jnp.exp(m_old - m_fin)
        p_new = jnp.exp(s_new - m_fin)
        l_fin = alpha * l_scr[...] + p_new
        acc = (alpha * acc_scr[...] + p_new * vn_ref[0]) / l_fin
        lam = _diff_lambda(lam_ref[...], lam_init)
        coef = jnp.where(row % 2 == 0, 1.0, -lam)
        own = lane // dv == row // 2
        o = jnp.sum(jnp.where(own, acc * coef, 0.0), axis=0, keepdims=True)
        subln = subln_ref[...]
        for h in range(n_heads):
            sl = slice(h * dv, (h + 1) * dv)
            o_ref[0, :, sl] = _rms(o[:, sl], subln) * (1.0 - lam_init)


def diff_attn_decode(q, k_new, v_new, cache_k, cache_v, page_table, lam_p, subln, lam_init):
    b, width = q.shape
    n_heads = width // (2 * HEAD_DIM)
    n_pages = page_table.shape[1]
    page = cache_k.shape[1]
    r3 = lambda t: t.reshape(b, 1, width)
    vec = pl.BlockSpec((1, 1, width), lambda bi, p, pt: (bi, 0, 0))
    cache = pl.BlockSpec((1, page, width), lambda bi, p, pt: (pt[bi * n_pages + p], 0, 0))
    out = pl.pallas_call(
        functools.partial(_diff_attn_decode_kernel, lam_init=lam_init, n_heads=n_heads),
        grid_spec=pltpu.PrefetchScalarGridSpec(
            num_scalar_prefetch=1,
            grid=(b, n_pages),
            in_specs=[vec, vec, vec, cache, cache,
                      pl.BlockSpec(lam_p.shape, lambda bi, p, pt: (0, 0)),
                      pl.BlockSpec(subln.shape, lambda bi, p, pt: (0, 0))],
            out_specs=vec,
            scratch_shapes=[pltpu.VMEM((2 * n_heads, 1), F32), pltpu.VMEM((2 * n_heads, 1), F32),
                            pltpu.VMEM((2 * n_heads, width), F32)],
        ),
        out_shape=jax.ShapeDtypeStruct((b, 1, width), F32),
        compiler_params=_cparams(("parallel", "arbitrary")),
        name="diff_attn_decode",
    )(page_table.reshape(-1), r3(q), r3(k_new), r3(v_new), cache_k, cache_v, lam_p, subln)
    return out.reshape(b, width)


def _lru_gates_kernel(x_ref, x1_ref, x2_ref, x3_ref, gr_ref, cw_ref, cb_ref, wa_ref, ba_ref, wi_ref,
                      bi_ref, lam_ref, a_out, b_out, gel_out):
    cw = cw_ref[...]
    xc = x3_ref[...] * cw[0:1] + x2_ref[...] * cw[1:2] + x1_ref[...] * cw[2:3] + x_ref[...] * cw[3:4]
    xc = xc + cb_ref[...]
    r = jax.nn.sigmoid(_dot(xc, wa_ref[...]) + ba_ref[...])
    ig = jax.nn.sigmoid(_dot(xc, wi_ref[...]) + bi_ref[...])
    log_a = -LRU_C * r * _softplus(-lam_ref[...])
    a = jnp.exp(log_a)
    a_out[...] = a
    b_out[...] = jnp.sqrt(-jnp.tanh(log_a) * (a * a + 1.0)) * (ig * xc)
    gel_out[...] = jax.nn.gelu(gr_ref[...])


def lru_gates(x, x1, x2, x3, gr, prm):
    m, w = x.shape
    tm = _tile(m, 512)
    rows = pl.BlockSpec((tm, w), lambda i: (i, 0))
    full = lambda a: pl.BlockSpec(a.shape, lambda i: (0,) * a.ndim)
    params = [prm[n] for n in ("conv_w", "conv_b", "wa", "ba", "wi", "bi", "lam")]
    return pl.pallas_call(
        _lru_gates_kernel,
        grid=(m // tm,),
        in_specs=[rows] * 5 + [full(a) for a in params],
        out_specs=[rows] * 3,
        out_shape=[jax.ShapeDtypeStruct((m, w), F32)] * 3,
        compiler_params=_cparams(("parallel",)),
        name="lru_gates",
    )(x, x1, x2, x3, gr, *params)


def _lru_scan_kernel(a_ref, b_ref, gel_ref, h0_ref, o_ref, hl_ref, h_scr):
    c = pl.program_id(1)
    tl = a_ref.shape[1]

    @pl.when(c == 0)
    def _():
        h_scr[...] = h0_ref[0]

    rows = min(tl, SUBLANES)
    row_id = lax.broadcasted_iota(jnp.int32, (rows, a_ref.shape[2]), 0)

    def block(tb, h):
        base = pl.multiple_of(tb * rows, rows)
        a8, b8 = a_ref[0, pl.ds(base, rows), :], b_ref[0, pl.ds(base, rows), :]
        hs = jnp.zeros_like(a8)
        for i in range(rows):
            h = a8[i:i + 1, :] * h + b8[i:i + 1, :]
            hs = jnp.where(row_id == i, h, hs)
        o_ref[0, pl.ds(base, rows), :] = hs * gel_ref[0, pl.ds(base, rows), :]
        return h

    h = lax.fori_loop(0, tl // rows, block, h_scr[...])
    h_scr[...] = h
    hl_ref[0] = h


def lru_scan(a, b, gel, h0):
    bsz, l, w = a.shape
    tl = _tile(l, 512)
    seq = pl.BlockSpec((1, tl, w), lambda i, j: (i, j, 0))
    vec = pl.BlockSpec((1, 1, w), lambda i, j: (i, 0, 0))
    o, hl = pl.pallas_call(
        _lru_scan_kernel,
        grid=(bsz, l // tl),
        in_specs=[seq, seq, seq, vec],
        out_specs=[seq, vec],
        out_shape=[jax.ShapeDtypeStruct((bsz, l, w), F32), jax.ShapeDtypeStruct((bsz, 1, w), F32)],
        scratch_shapes=[pltpu.VMEM((1, w), F32)],
        compiler_params=_cparams(("parallel", "arbitrary")),
        name="lru_scan",
    )(a, b, gel, h0.reshape(bsz, 1, w))
    return o, hl.reshape(bsz, w)


def _block_diag(w):
    n, d, e = w.shape
    eye = jnp.eye(n, dtype=w.dtype)
    return (eye[:, None, :, None] * w[:, :, None, :]).reshape(n * d, n * e)


def _delayed(x, buf, d):
    nb, l = buf.shape[1], x.shape[1]
    return jnp.concatenate([buf[:, nb - d:], x[:, :max(l - d, 0)]], axis=1)[:, :l]


def _mix_even(h, g_norm, pos0, s_ret, s_rwkv, buf, wts, is_prompt):
    b, l, d = h.shape
    m = b * l
    gw = d // 2
    u = norm_matmul(h.reshape(m, d), g_norm, wts["ab_w_in"])
    seq = lambda t: t.reshape(b, l, gw)
    qa, ka, va, ga = u[0], u[1], u[2], u[3]
    cur = [u[4], u[5], u[6], u[7]]
    prev = [_delayed(seq(t), buf[..., i * gw:(i + 1) * gw], 1).reshape(m, gw) for i, t in enumerate(cur)]
    buf_new = jnp.concatenate([seq(t)[:, l - 1:] for t in cur], axis=-1)
    r, w, k, v, kk, kka, g, bonus = rwkv_prep(cur, prev, wts["rwkv"])
    if is_prompt:
        pos = pos0 + jnp.arange(l, dtype=F32)
        o_a, s_ret_new = retention_prompt(seq(qa), seq(ka), seq(va), seq(ga), pos)
        o_a = o_a.reshape(m, gw)
        o_b, s_rwkv_new = rwkv_scan_prompt(seq(r), seq(w), seq(k), seq(kk), seq(kka), seq(v))
        o_b = o_b.reshape(m, gw)
    else:
        o_a, s_ret_new = retention_step(qa, ka, va, ga, s_ret, jnp.float32(pos0))
        o_b, s_rwkv_new = rwkv_step(r, w, k, kk, kka, v, s_rwkv)
    o_b = rwkv_post(o_b, bonus, g, wts["rwkv_ln"], wts["rwkv"]["ones"])
    return o_a, o_b, s_ret_new, s_rwkv_new, buf_new


def _mix_odd(h, g_norm, lru_h, lru_buf, pages, wts, layer, is_prompt):
    b, l, d = h.shape
    m = b * l
    gw = d // 2
    u = norm_matmul(h.reshape(m, d), g_norm, wts["cd_w_in"])
    seq = lambda t: t.reshape(b, l, gw)
    lam_init = 0.8 - 0.6 * math.exp(-0.3 * layer)
    if is_prompt:
        o_c = diff_attn_prompt(seq(u[0]), seq(u[1]), seq(u[2]), wts["diff_lam"], wts["diff_subln"], lam_init)
        o_c = o_c.reshape(m, gw)
    else:
        cache_k, cache_v, page_table = pages
        o_c = diff_attn_decode(u[0], u[1], u[2], cache_k, cache_v, page_table, wts["diff_lam"],
                               wts["diff_subln"], lam_init)
    xr = seq(u[3])
    x1, x2, x3 = (_delayed(xr, lru_buf, t).reshape(m, gw) for t in (1, 2, 3))
    a, bt, gel = lru_gates(u[3], x1, x2, x3, u[4], wts["lru"])
    o_d, h_last = lru_scan(seq(a), seq(bt), seq(gel), lru_h)
    buf_new = jnp.concatenate([lru_buf, xr], axis=1)[:, l:]
    n_heads = gw // (2 * HEAD_DIM)
    k_new = u[1].reshape(b, l, n_heads, 2 * HEAD_DIM)
    v_new = u[2].reshape(b, l, n_heads, 2 * HEAD_DIM)
    return o_c, o_d.reshape(m, gw), k_new, v_new, h_last, buf_new


def _trunk(x, p, pos0, s_ret, s_rwkv, s_shift, s_lru_h, s_lru_conv, pages, wts, is_prompt):
    b, l, d = x.shape
    m = b * l
    depth = wts["norm_g"].shape[0]
    h = x.reshape(m, d)
    ret_l, rwkv_l, shift_l, k_l, v_l, lh_l, lc_l = [], [], [], [], [], [], []
    for i in range(depth):
        j = i // 2
        g = wts["norm_g"][i]
        gn = lambda n: g[n:n + 1]
        h = ffn_block(h, gn(0), wts["ffn_in"][i][0], wts["ffn_out"][i][0], gn(1))
        if i % 2 == 0:
            o1, o2, sr, sw, sb = _mix_even(h.reshape(b, l, d), gn(2), pos0, s_ret[j], s_rwkv[j], s_shift[j],
                                           wts["even"][j], is_prompt)
            ret_l.append(sr)
            rwkv_l.append(sw)
            shift_l.append(sb)
            w_out = wts["even"][j]["w_out"]
        else:
            pg = None if pages is None else (pages[0][j], pages[1][j], pages[2])
            o1, o2, kn, vn, lh, lc = _mix_odd(h.reshape(b, l, d), gn(2), s_lru_h[j], s_lru_conv[j], pg,
                                              wts["odd"][j], i, is_prompt)
            k_l.append(kn)
            v_l.append(vn)
            lh_l.append(lh)
            lc_l.append(lc)
            w_out = wts["odd"][j]["w_out"]
        h = out_proj(o1, o2, w_out, h, gn(3))
        h = ffn_block(h, gn(4), wts["ffn_in"][i][1], wts["ffn_out"][i][1], gn(5))
        h = ple_block(h, gn(6), wts["ple_gate"][i], p[i].reshape(m, -1), wts["ple"][i], gn(7))
    st = lambda lst: jnp.stack(lst, axis=0)
    return (h.reshape(b, l, d), st(k_l), st(v_l), st(ret_l), st(rwkv_l), st(shift_l), st(lh_l), st(lc_l))


def kernel(x_prompt, x_sample, cache_k, cache_v, state_ret, state_rwkv, state_rwkv_shift, state_lru_h, state_lru_conv, page_table, p_prompt, p_sample, norm_g, ffn_w_in, ffn_w_out, ple_w, ple_gate_w, ab_w_in, ab_w_out, rwkv_mu, rwkv_w0, rwkv_w1, rwkv_w2, rwkv_a0, rwkv_a1, rwkv_a2, rwkv_g1, rwkv_g2, rwkv_kk, rwkv_ka, rwkv_rk, rwkv_ln, cd_w_in, cd_w_out, diff_lam, diff_subln, lru_conv_w, lru_conv_b, lru_wa, lru_ba, lru_wi, lru_bi, lru_lambda):
    depth = norm_g.shape[0]
    n_a, n_c = state_ret.shape[0], state_lru_h.shape[0]
    bp = x_prompt.shape[0]
    gw = ab_w_out.shape[1] // 2
    bf = lambda t: t.astype(BF16)
    row = lambda t: t.reshape(1, -1)
    ones_bd = _block_diag(jnp.ones((gw // HEAD_DIM, HEAD_DIM, HEAD_DIM), BF16))
    wts = {
        "norm_g": norm_g,
        "ffn_in": [[bf(ffn_w_in[i, s]) for s in range(2)] for i in range(depth)],
        "ffn_out": [[bf(ffn_w_out[i, s]) for s in range(2)] for i in range(depth)],
        "ple": [bf(ple_w[i]) for i in range(depth)],
        "ple_gate": [bf(ple_gate_w[i]) for i in range(depth)],
        "even": [{
            "ab_w_in": bf(ab_w_in[j]), "w_out": bf(ab_w_out[j]), "rwkv_ln": rwkv_ln[j],
            "rwkv": {"mu": rwkv_mu[j], "w0": row(rwkv_w0[j]), "w1": bf(rwkv_w1[j]), "w2": bf(rwkv_w2[j]),
                     "a0": row(rwkv_a0[j]), "a1": bf(rwkv_a1[j]), "a2": bf(rwkv_a2[j]),
                     "g1": bf(rwkv_g1[j]), "g2": bf(rwkv_g2[j]), "kk": row(rwkv_kk[j]),
                     "ka": row(rwkv_ka[j]), "rk": row(rwkv_rk[j]), "ones": ones_bd},
        } for j in range(n_a)],
        "odd": [{
            "cd_w_in": bf(cd_w_in[j]), "w_out": bf(cd_w_out[j]), "diff_lam": diff_lam[j],
            "diff_subln": row(diff_subln[j]),
            "lru": {"conv_w": lru_conv_w[j], "conv_b": row(lru_conv_b[j]), "wa": bf(_block_diag(lru_wa[j])),
                    "ba": row(lru_ba[j]), "wi": bf(_block_diag(lru_wi[j])), "bi": row(lru_bi[j]),
                    "lam": row(lru_lambda[j])},
        } for j in range(n_c)],
    }
    zeros = lambda *shape: jnp.zeros(shape, F32)
    yp, kp, vp, rp, wp, sp, hp, cp = _trunk(
        x_prompt, p_prompt, 0.0, [None] * n_a, [None] * n_a,
        zeros(n_a, bp, 1, 4 * gw), zeros(n_c, bp, gw), zeros(n_c, bp, CONV_W - 1, gw),
        None, wts, True)
    past_len = page_table.shape[1] * cache_k.shape[2]
    n_pool, page = cache_k.shape[1], cache_k.shape[2]
    pages = (cache_k.reshape(n_c, n_pool, page, gw), cache_v.reshape(n_c, n_pool, page, gw), page_table)
    ys, ks_, vs, rs, ws, ss, hs, cs = _trunk(
        x_sample, p_sample, float(past_len), state_ret, state_rwkv, state_rwkv_shift,
        state_lru_h, state_lru_conv, pages, wts, False)
    return (yp, ys, kp, vp, rp, wp, sp, hp, cp, ks_, vs, rs, ws, ss, hs, cs)
```

```python
import functools
import math

import jax
import jax.numpy as jnp
from jax import lax
from jax.experimental import pallas as pl
from jax.experimental.pallas import tpu as pltpu

F32 = jnp.float32
BF16 = jnp.bfloat16

HEAD_DIM = 64
CONV_W = 4
LRU_C = 8.0
ROPE_BASE = 10000.0
EPS = 1e-6
RWKV_GN_EPS = 64e-5
RET_CHUNK = 256
ATTN_BLOCK = 512
SCAN_CHUNK = 128
DECODE_PAGES = 4
SCAN_GROUP = 2
SEL_STEPS = 32
LANES = 128
SUBLANES = 8
VMEM_LIMIT = 48 * 1024 * 1024


def _cparams(sem):
    return pltpu.CompilerParams(dimension_semantics=sem, vmem_limit_bytes=VMEM_LIMIT)


def _tile(n, pref):
    t = min(n, pref)
    while n % t:
        t //= 2
    return t


def _rms(x, g):
    return x * lax.rsqrt(jnp.mean(x * x, axis=-1, keepdims=True) + EPS) * g


def _dot(a, b):
    return jnp.dot(a.astype(BF16), b.astype(BF16), preferred_element_type=F32)


def _seg_sum(x, ones_bd):
    hi = x.astype(BF16)
    lo = (x - hi.astype(F32)).astype(BF16)
    return (jnp.dot(hi, ones_bd, preferred_element_type=F32)
            + jnp.dot(lo, ones_bd, preferred_element_type=F32))


def _ffn_kernel(h_ref, gpre_ref, wg_ref, wu_ref, wo_ref, gpost_ref, o_ref, xn_ref, acc_ref):
    j = pl.program_id(1)

    @pl.when(j == 0)
    def _():
        xn_ref[...] = _rms(h_ref[...], gpre_ref[...]).astype(BF16)
        acc_ref[...] = jnp.zeros_like(acc_ref)

    xn = xn_ref[...]
    gate = jnp.dot(xn, wg_ref[...], preferred_element_type=F32)
    up = jnp.dot(xn, wu_ref[...], preferred_element_type=F32)
    act = (gate * jax.nn.sigmoid(gate) * up).astype(BF16)
    acc_ref[...] += jnp.dot(act, wo_ref[...], preferred_element_type=F32)

    @pl.when(j == pl.num_programs(1) - 1)
    def _():
        o_ref[...] = h_ref[...] + 0.5 * _rms(acc_ref[...], gpost_ref[...])


def ffn_block(h, g_pre, w_in, w_out, g_post):
    m, d = h.shape
    f = w_out.shape[0]
    tm, tf = _tile(m, 1024), _tile(f, 512)
    nf = f // tf
    return pl.pallas_call(
        _ffn_kernel,
        grid=(m // tm, nf),
        in_specs=[
            pl.BlockSpec((tm, d), lambda i, j: (i, 0)),
            pl.BlockSpec((1, d), lambda i, j: (0, 0)),
            pl.BlockSpec((d, tf), lambda i, j: (0, j)),
            pl.BlockSpec((d, tf), lambda i, j: (0, j + nf)),
            pl.BlockSpec((tf, d), lambda i, j: (j, 0)),
            pl.BlockSpec((1, d), lambda i, j: (0, 0)),
        ],
        out_specs=pl.BlockSpec((tm, d), lambda i, j: (i, 0)),
        out_shape=jax.ShapeDtypeStruct((m, d), F32),
        scratch_shapes=[pltpu.VMEM((tm, d), BF16), pltpu.VMEM((tm, d), F32)],
        compiler_params=_cparams(("parallel", "arbitrary")),
        name="ffn_block",
    )(h, g_pre, w_in, w_in, w_out, g_post)


def _norm_matmul_kernel(h_ref, g_ref, w_ref, o_ref, xn_ref):
    @pl.when(pl.program_id(1) == 0)
    def _():
        xn_ref[...] = _rms(h_ref[...], g_ref[...]).astype(BF16)

    o_ref[0] = jnp.dot(xn_ref[...], w_ref[...], preferred_element_type=F32)


def norm_matmul(h, g, w):
    m, d = h.shape
    n = w.shape[1]
    tn = 512
    tm = _tile(m, 512)
    return pl.pallas_call(
        _norm_matmul_kernel,
        grid=(m // tm, n // tn),
        in_specs=[
            pl.BlockSpec((tm, d), lambda i, j: (i, 0)),
            pl.BlockSpec((1, d), lambda i, j: (0, 0)),
            pl.BlockSpec((d, tn), lambda i, j: (0, j)),
        ],
        out_specs=pl.BlockSpec((1, tm, tn), lambda i, j: (j, i, 0)),
        out_shape=jax.ShapeDtypeStruct((n // tn, m, tn), F32),
        scratch_shapes=[pltpu.VMEM((tm, d), BF16)],
        compiler_params=_cparams(("parallel", "arbitrary")),
        name="norm_matmul",
    )(h, g, w)


def _out_proj_kernel(oa_ref, ob_ref, wa_ref, wb_ref, h_ref, g_ref, o_ref):
    y = _dot(oa_ref[...], wa_ref[...]) + _dot(ob_ref[...], wb_ref[...])
    o_ref[...] = h_ref[...] + _rms(y, g_ref[...])


def out_proj(oa, ob, w, h, g):
    m, d = h.shape
    gw = oa.shape[1]
    tm = _tile(m, 512)
    return pl.pallas_call(
        _out_proj_kernel,
        grid=(m // tm,),
        in_specs=[
            pl.BlockSpec((tm, gw), lambda i: (i, 0)),
            pl.BlockSpec((tm, gw), lambda i: (i, 0)),
            pl.BlockSpec((gw, d), lambda i: (0, 0)),
            pl.BlockSpec((gw, d), lambda i: (1, 0)),
            pl.BlockSpec((tm, d), lambda i: (i, 0)),
            pl.BlockSpec((1, d), lambda i: (0, 0)),
        ],
        out_specs=pl.BlockSpec((tm, d), lambda i: (i, 0)),
        out_shape=jax.ShapeDtypeStruct((m, d), F32),
        compiler_params=_cparams(("parallel",)),
        name="out_proj",
    )(oa, ob, w, w, h, g)


def _ple_kernel(h_ref, g6_ref, wg_ref, p_ref, wp_ref, g7_ref, o_ref):
    h = h_ref[...]
    gate = jax.nn.sigmoid(_dot(_rms(h, g6_ref[...]), wg_ref[...]))
    y = gate * _dot(p_ref[...], wp_ref[...])
    o_ref[...] = h + _rms(y, g7_ref[...])


def ple_block(h, g6, wg, p, wp, g7):
    m, d = h.shape
    pd = p.shape[1]
    tm = _tile(m, 512)
    return pl.pallas_call(
        _ple_kernel,
        grid=(m // tm,),
        in_specs=[
            pl.BlockSpec((tm, d), lambda i: (i, 0)),
            pl.BlockSpec((1, d), lambda i: (0, 0)),
            pl.BlockSpec((d, d), lambda i: (0, 0)),
            pl.BlockSpec((tm, pd), lambda i: (i, 0)),
            pl.BlockSpec((pd, d), lambda i: (0, 0)),
            pl.BlockSpec((1, d), lambda i: (0, 0)),
        ],
        out_specs=pl.BlockSpec((tm, d), lambda i: (i, 0)),
        out_shape=jax.ShapeDtypeStruct((m, d), F32),
        compiler_params=_cparams(("parallel",)),
        name="ple_block",
    )(h, g6, wg, p, wp, g7)


def _retention_tables(n_heads, c, pos):
    lg = jnp.log1p(-jnp.exp2(-5.0 - jnp.arange(n_heads, dtype=F32)))
    idx = jnp.arange(c, dtype=F32)
    rel = idx[:, None] - idx[None, :]
    dmask = jnp.where(rel[None] >= 0, jnp.exp(jnp.maximum(rel, 0.0)[None] * lg[:, None, None]), 0.0)
    rep = lambda t: jnp.repeat(t, HEAD_DIM, axis=-1)
    q_dec = rep(jnp.exp((idx[:, None] + 1.0) * lg[None, :]))
    k_dec = rep(jnp.exp((c - 1.0 - idx[:, None]) * lg[None, :]))
    c_dec = rep(jnp.exp(c * lg)[None, :])
    half = HEAD_DIM // 2
    freq = 1.0 / (ROPE_BASE ** jnp.linspace(0.0, 1.0, half, dtype=F32))
    ang = pos[:, None] * freq[None, :]
    cos, sin = jnp.cos(ang), jnp.sin(ang)
    cos_t = jnp.tile(jnp.concatenate([cos, cos], axis=-1), (1, n_heads))
    sin_t = jnp.tile(jnp.concatenate([-sin, sin], axis=-1), (1, n_heads))
    return dmask, q_dec, k_dec, c_dec, cos_t, sin_t


def _retention_kernel(q_ref, k_ref, v_ref, g_ref, cos_ref, sin_ref, dmask_ref, qdec_ref, kdec_ref,
                      cdec_ref, o_ref, s_ref, s_scr, *, n_heads):
    c = pl.program_id(1)

    @pl.when(c == 0)
    def _():
        s_scr[...] = jnp.zeros_like(s_scr)

    q, k, v, g = q_ref[0], k_ref[0], v_ref[0], g_ref[0]
    cos, sin = cos_ref[...], sin_ref[...]
    width = q.shape[-1]
    lane = lax.broadcasted_iota(jnp.int32, q.shape, 1)
    first_half = (lane % HEAD_DIM) < (HEAD_DIM // 2)

    def rot(x):
        swapped = jnp.where(first_half, pltpu.roll(x, width - HEAD_DIM // 2, 1),
                            pltpu.roll(x, HEAD_DIM // 2, 1))
        return x * cos + swapped * sin

    qr = rot(q)
    kr = rot(k) * (HEAD_DIM ** -0.5)
    kd = kr * kdec_ref[...]
    qdec = qdec_ref[...]
    cdec = cdec_ref[...]
    gate = g * jax.nn.sigmoid(g)
    for h in range(n_heads):
        sl = slice(h * HEAD_DIM, (h + 1) * HEAD_DIM)
        qh = qr[:, sl].astype(BF16)
        kh = kr[:, sl].astype(BF16)
        vh = v[:, sl].astype(BF16)
        att = lax.dot_general(qh, kh, (((1,), (1,)), ((), ())), preferred_element_type=F32) * dmask_ref[h]
        s_old = s_scr[h]
        o = (jnp.dot(att.astype(BF16), vh, preferred_element_type=F32)
             + jnp.dot(qh, s_old.astype(BF16), preferred_element_type=F32) * qdec[:, sl])
        s_scr[h] = s_old * cdec[:, sl] + lax.dot_general(
            kd[:, sl].astype(BF16), vh, (((0,), (0,)), ((), ())), preferred_element_type=F32)
        oc = o - jnp.mean(o, axis=-1, keepdims=True)
        on = oc * lax.rsqrt(jnp.mean(oc * oc, axis=-1, keepdims=True) + EPS)
        o_ref[0, :, sl] = on * gate[:, sl]

    @pl.when(c == pl.num_programs(1) - 1)
    def _():
        s_ref[0] = s_scr[...]


def retention_prompt(q, k, v, g, pos):
    b, l, width = q.shape
    n_heads = width // HEAD_DIM
    c = _tile(l, RET_CHUNK)
    dmask, q_dec, k_dec, c_dec, cos_t, sin_t = _retention_tables(n_heads, c, pos)
    seq = pl.BlockSpec((1, c, width), lambda i, j: (i, j, 0))
    tab = pl.BlockSpec((c, width), lambda i, j: (j, 0))
    fixed = lambda shape: pl.BlockSpec(shape, lambda i, j: (0,) * len(shape))
    return pl.pallas_call(
        functools.partial(_retention_kernel, n_heads=n_heads),
        grid=(b, l // c),
        in_specs=[seq, seq, seq, seq, tab, tab, fixed((n_heads, c, c)), fixed((c, width)),
                  fixed((c, width)), fixed((1, width))],
        out_specs=[seq, pl.BlockSpec((1, n_heads, HEAD_DIM, HEAD_DIM), lambda i, j: (i, 0, 0, 0))],
        out_shape=[jax.ShapeDtypeStruct((b, l, width), F32),
                   jax.ShapeDtypeStruct((b, n_heads, HEAD_DIM, HEAD_DIM), F32)],
        scratch_shapes=[pltpu.VMEM((n_heads, HEAD_DIM, HEAD_DIM), F32)],
        compiler_params=_cparams(("parallel", "arbitrary")),
        name="retention_prompt",
    )(q, k, v, g, cos_t, sin_t, dmask, q_dec, k_dec, c_dec)


def _retention_step_kernel(q_ref, k_ref, v_ref, g_ref, cos_ref, sin_ref, gam_ref, s0_ref, o_ref, s_ref):
    cos, sin = cos_ref[...], sin_ref[...]

    def rot(x):
        half = HEAD_DIM // 2
        return x * cos + jnp.concatenate([x[:, :, half:], x[:, :, :half]], axis=2) * sin

    q = rot(q_ref[...])
    k = rot(k_ref[...]) * (HEAD_DIM ** -0.5)
    v, g = v_ref[...], g_ref[...]
    gam = gam_ref[...]
    s0 = s0_ref[...]
    att = jnp.sum(q * k, axis=2, keepdims=True)
    o = att * v + jnp.sum(q * s0, axis=2, keepdims=True) * gam
    s_ref[...] = s0 * gam + k * v
    oc = o - jnp.mean(o, axis=-1, keepdims=True)
    on = oc * lax.rsqrt(jnp.mean(oc * oc, axis=-1, keepdims=True) + EPS)
    o_ref[...] = on * (g * jax.nn.sigmoid(g))


def retention_step(q, k, v, g, s0, pos):
    b, width = q.shape
    n_heads = width // HEAD_DIM
    bb = _tile(b, 8)
    half = HEAD_DIM // 2
    freq = 1.0 / (ROPE_BASE ** jnp.linspace(0.0, 1.0, half, dtype=F32))
    ang = pos * freq
    cos_c = jnp.concatenate([jnp.cos(ang), jnp.cos(ang)])[:, None]
    sin_c = jnp.concatenate([-jnp.sin(ang), jnp.sin(ang)])[:, None]
    gam = jnp.exp(jnp.log1p(-jnp.exp2(-5.0 - jnp.arange(n_heads, dtype=F32)))).reshape(n_heads, 1, 1)
    col = lambda t: t.reshape(b, n_heads, HEAD_DIM, 1)
    row = lambda t: t.reshape(b, n_heads, 1, HEAD_DIM)
    cspec = pl.BlockSpec((bb, n_heads, HEAD_DIM, 1), lambda i: (i, 0, 0, 0))
    rspec = pl.BlockSpec((bb, n_heads, 1, HEAD_DIM), lambda i: (i, 0, 0, 0))
    sspec = pl.BlockSpec((bb, n_heads, HEAD_DIM, HEAD_DIM), lambda i: (i, 0, 0, 0))
    o, s = pl.pallas_call(
        _retention_step_kernel,
        grid=(b // bb,),
        in_specs=[cspec, cspec, rspec, rspec,
                  pl.BlockSpec((HEAD_DIM, 1), lambda i: (0, 0)), pl.BlockSpec((HEAD_DIM, 1), lambda i: (0, 0)),
                  pl.BlockSpec((n_heads, 1, 1), lambda i: (0, 0, 0)), sspec],
        out_specs=[rspec, sspec],
        out_shape=[jax.ShapeDtypeStruct((b, n_heads, 1, HEAD_DIM), F32),
                   jax.ShapeDtypeStruct((b, n_heads, HEAD_DIM, HEAD_DIM), F32)],
        compiler_params=_cparams(("parallel",)),
        name="retention_step",
    )(col(q), col(k), row(v), row(g), cos_c, sin_c, gam, s0)
    return o.reshape(b, width), s


def _softplus(x):
    return jnp.maximum(x, 0.0) + jnp.log1p(jnp.exp(-jnp.abs(x)))


def _rwkv_prep_kernel(rr_ref, kr_ref, vr_ref, zr_ref, pr_ref, pk_ref, pv_ref, pz_ref, mu_ref,
                      w0_ref, w1_ref, w2_ref, a0_ref, a1_ref, a2_ref, g1_ref, g2_ref, kkp_ref,
                      kap_ref, rk_ref, ones_ref,
                      r_out, w_out, k_out, v_out, kk_out, kka_out, g_out, bonus_out, vhi_out, vlo_out):
    mu = mu_ref[...]
    lerp = lambda x, xp, i: x + (xp - x) * mu[i:i + 1]
    zr, pz = zr_ref[...], pz_ref[...]
    r = lerp(rr_ref[...], pr_ref[...], 0)
    kx = lerp(kr_ref[...], pk_ref[...], 1)
    vx = lerp(vr_ref[...], pv_ref[...], 2)
    zw, za, zg = lerp(zr, pz, 3), lerp(zr, pz, 4), lerp(zr, pz, 5)
    wpre = w0_ref[...] + _dot(jnp.tanh(_dot(zw, w1_ref[...])), w2_ref[...])
    decay = jnp.exp(-jnp.exp(-_softplus(-wpre) - 0.5))
    a = jax.nn.sigmoid(a0_ref[...] + _dot(_dot(za, a1_ref[...]), a2_ref[...]))
    g = _dot(jax.nn.sigmoid(_dot(zg, g1_ref[...])), g2_ref[...])
    ones_bd = ones_ref[...]
    kk = kx * kkp_ref[...]
    kk = kk / jnp.maximum(jnp.sqrt(_seg_sum(kk * kk, ones_bd)), 1e-12)
    k32 = kx * (1.0 + (a - 1.0) * kap_ref[...])
    r_out[...] = r
    w_out[...] = decay
    k_out[...] = k32
    v_out[...] = vx
    v_hi = vx.astype(BF16)
    vhi_out[...] = v_hi
    vlo_out[...] = (vx - v_hi.astype(F32)).astype(BF16)
    kk_out[...] = kk
    kka_out[...] = kk * a
    g_out[...] = g
    bonus_out[...] = _seg_sum(r * k32 * rk_ref[...], ones_bd) * vx


def rwkv_prep(cur, prev, prm):
    m, w = cur[0].shape
    tm = _tile(m, 256)
    rows = pl.BlockSpec((tm, w), lambda i: (i, 0))
    full = lambda a: pl.BlockSpec(a.shape, lambda i: (0,) * a.ndim)
    params = [prm[n] for n in ("mu", "w0", "w1", "w2", "a0", "a1", "a2", "g1", "g2", "kk", "ka", "rk", "ones")]
    return pl.pallas_call(
        _rwkv_prep_kernel,
        grid=(m // tm,),
        in_specs=[rows] * 8 + [full(a) for a in params],
        out_specs=[rows] * 10,
        out_shape=[jax.ShapeDtypeStruct((m, w), F32)] * 8 + [jax.ShapeDtypeStruct((m, w), BF16)] * 2,
        compiler_params=_cparams(("parallel",)),
        name="rwkv_prep",
    )(*cur, *prev, *params)


def _rwkv_post_kernel(o_ref, bonus_ref, g_ref, ln_ref, ones_ref, out_ref):
    ones_bd = ones_ref[...]
    o = o_ref[...]
    inv = 1.0 / HEAD_DIM
    oc = o - _seg_sum(o, ones_bd) * inv
    on = oc * lax.rsqrt(_seg_sum(oc * oc, ones_bd) * inv + RWKV_GN_EPS)
    ln = ln_ref[...]
    out_ref[...] = (on * ln[0:1] + ln[1:2] + bonus_ref[...]) * g_ref[...]


def rwkv_post(o, bonus, g, ln, ones_bd):
    m, w = o.shape
    tm = _tile(m, 512)
    rows = pl.BlockSpec((tm, w), lambda i: (i, 0))
    return pl.pallas_call(
        _rwkv_post_kernel,
        grid=(m // tm,),
        in_specs=[rows, rows, rows, pl.BlockSpec(ln.shape, lambda i: (0, 0)),
                  pl.BlockSpec(ones_bd.shape, lambda i: (0, 0))],
        out_specs=rows,
        out_shape=jax.ShapeDtypeStruct((m, w), F32),
        compiler_params=_cparams(("parallel",)),
        name="rwkv_post",
    )(o, bonus, g, ln, ones_bd)


def _hi_lo(x):
    hi = x.astype(BF16)
    lo = (x - hi.astype(F32)).astype(BF16)
    return jnp.concatenate([hi, lo], axis=1)


def _rwkv_scan_kernel(r_ref, w_ref, k_ref, kk_ref, kka_ref, vp_ref, sel_ref, ones_ref, o_ref, s_ref, s_scr,
                      *, n_pairs):
    c = pl.program_id(1)
    n_grp, tc = r_ref.shape[0], r_ref.shape[1]
    chains = [(g, p) for g in range(n_grp) for p in range(n_pairs)]

    @pl.when(c == 0)
    def _():
        s_scr[...] = jnp.zeros_like(s_scr)

    o_ref[...] = jnp.zeros_like(o_ref)
    ones2 = ones_ref[...]
    t_lane = lax.broadcasted_iota(jnp.int32, (len(chains) * HEAD_DIM, LANES), 1) % HEAD_DIM

    def block(tb, carry):
        base = pl.multiple_of(tb * SUBLANES, SUBLANES)
        tiles = {name: [ref[g, pl.ds(base, SUBLANES), p * LANES:(p + 1) * LANES] for g, p in chains]
                 for name, ref in (("kk", kk_ref), ("w", w_ref), ("kka", kka_ref), ("k", k_ref), ("r", r_ref))}
        vp = jnp.concatenate([vp_ref[g, tb // (SEL_STEPS // SUBLANES), p] for g, p in chains], axis=0)
        s = s_scr[...]
        acc = jnp.zeros_like(s)
        for i in range(SUBLANES):
            rows = lambda name: jnp.concatenate(
                [jnp.broadcast_to(t[i:i + 1, :], (HEAD_DIM, LANES)) for t in tiles[name]], axis=0)
            sa = jnp.dot(_hi_lo(s * rows("kk")), ones2, preferred_element_type=F32)
            vcol = jnp.dot(vp, sel_ref[(tb % (SEL_STEPS // SUBLANES)) * SUBLANES + i], preferred_element_type=F32)
            s = s * rows("w") - sa * rows("kka") + vcol * rows("k")
            o = jnp.dot(_hi_lo(s * rows("r")), ones2, preferred_element_type=F32)
            acc = jnp.where(t_lane == (base + i) % HEAD_DIM, o, acc)
        s_scr[...] = s
        tile = base // HEAD_DIM
        for ci, (g, p) in enumerate(chains):
            o_ref[g, tile, p] += acc[ci * HEAD_DIM:(ci + 1) * HEAD_DIM]
        return carry

    lax.fori_loop(0, tc // SUBLANES, block, 0)

    @pl.when(c == pl.num_programs(1) - 1)
    def _():
        for ci, (g, p) in enumerate(chains):
            s_ref[g, p] = s_scr[ci * HEAD_DIM:(ci + 1) * HEAD_DIM, :]


def rwkv_scan_prompt(r, w, k, kk, kka, v_hi, v_lo):
    b, l, width = r.shape
    n_pairs = width // LANES
    tc = _tile(l, SCAN_CHUNK)
    grp = _tile(b, SCAN_GROUP)
    nsel = l // SEL_STEPS
    vp = jnp.stack([v_hi, v_lo], axis=2).reshape(b, nsel, SEL_STEPS, 2, n_pairs, 2, HEAD_DIM)
    vp = vp.transpose(0, 1, 4, 6, 5, 3, 2).reshape(b, nsel, n_pairs, HEAD_DIM, LANES)
    kl = jnp.arange(LANES)
    sel = ((kl[None, :, None] % SEL_STEPS == jnp.arange(SEL_STEPS)[:, None, None])
           & (kl[None, :, None] // HEAD_DIM == kl[None, None, :] // HEAD_DIM)).astype(BF16)
    k2 = jnp.arange(2 * LANES)
    ones2 = ((k2[:, None] % LANES) // HEAD_DIM == kl[None, :] // HEAD_DIM).astype(BF16)
    seq = pl.BlockSpec((grp, tc, width), lambda i, j: (i, j, 0))
    o, s = pl.pallas_call(
        functools.partial(_rwkv_scan_kernel, n_pairs=n_pairs),
        grid=(b // grp, l // tc),
        in_specs=[seq, seq, seq, seq, seq,
                  pl.BlockSpec((grp, tc // SEL_STEPS, n_pairs, HEAD_DIM, LANES), lambda i, j: (i, j, 0, 0, 0)),
                  pl.BlockSpec(sel.shape, lambda i, j: (0, 0, 0)),
                  pl.BlockSpec(ones2.shape, lambda i, j: (0, 0))],
        out_specs=[pl.BlockSpec((grp, tc // HEAD_DIM, n_pairs, HEAD_DIM, LANES), lambda i, j: (i, j, 0, 0, 0)),
                   pl.BlockSpec((grp, n_pairs, HEAD_DIM, LANES), lambda i, j: (i, 0, 0, 0))],
        out_shape=[jax.ShapeDtypeStruct((b, l // HEAD_DIM, n_pairs, HEAD_DIM, LANES), F32),
                   jax.ShapeDtypeStruct((b, n_pairs, HEAD_DIM, LANES), F32)],
        scratch_shapes=[pltpu.VMEM((grp * n_pairs * HEAD_DIM, LANES), F32)],
        compiler_params=_cparams(("parallel", "arbitrary")),
        name="rwkv_scan_prompt",
    )(r, w, k, kk, kka, vp, sel, ones2)
    o = o.reshape(b, l // HEAD_DIM, n_pairs, HEAD_DIM, 2, HEAD_DIM).transpose(0, 1, 5, 2, 4, 3).reshape(b, l, width)
    s = s.reshape(b, n_pairs, HEAD_DIM, 2, HEAD_DIM).transpose(0, 1, 3, 2, 4)
    return o, s.reshape(b, 2 * n_pairs, HEAD_DIM, HEAD_DIM)


def _rwkv_step_kernel(r_ref, w_ref, k_ref, kk_ref, kka_ref, v_ref, s0_ref, o_ref, s_ref):
    s0 = s0_ref[...]
    sa = -jnp.sum(s0 * kk_ref[...], axis=-1, keepdims=True)
    s = s0 * w_ref[...] + sa * kka_ref[...] + v_ref[...] * k_ref[...]
    s_ref[...] = s
    o_ref[...] = jnp.sum(s * r_ref[...], axis=-1, keepdims=True)


def rwkv_step(r, w, k, kk, kka, v, s0):
    b, width = r.shape
    n_heads = width // HEAD_DIM
    bb = _tile(b, 8)
    row = lambda t: t.reshape(b, n_heads, 1, HEAD_DIM)
    cspec = pl.BlockSpec((bb, n_heads, HEAD_DIM, 1), lambda i: (i, 0, 0, 0))
    rspec = pl.BlockSpec((bb, n_heads, 1, HEAD_DIM), lambda i: (i, 0, 0, 0))
    sspec = pl.BlockSpec((bb, n_heads, HEAD_DIM, HEAD_DIM), lambda i: (i, 0, 0, 0))
    o, s = pl.pallas_call(
        _rwkv_step_kernel,
        grid=(b // bb,),
        in_specs=[rspec] * 5 + [cspec, sspec],
        out_specs=[cspec, sspec],
        out_shape=[jax.ShapeDtypeStruct((b, n_heads, HEAD_DIM, 1), F32),
                   jax.ShapeDtypeStruct((b, n_heads, HEAD_DIM, HEAD_DIM), F32)],
        compiler_params=_cparams(("parallel",)),
        name="rwkv_step",
    )(row(r), row(w), row(k), row(kk), row(kka), v.reshape(b, n_heads, HEAD_DIM, 1), s0)
    return o.reshape(b, width), s


def _diff_lambda(lp, lam_init):
    e1 = jnp.exp(jnp.sum(lp[0:1] * lp[1:2], axis=-1, keepdims=True))
    e2 = jnp.exp(jnp.sum(lp[2:3] * lp[3:4], axis=-1, keepdims=True))
    return e1 - e2 + lam_init


def _diff_attn_kernel(q_ref, k_ref, v_ref, lam_ref, subln_ref, o_ref, m_scr, l_scr, acc_scr, *, lam_init):
    i, j = pl.program_id(2), pl.program_id(3)
    tq, tk = q_ref.shape[1], k_ref.shape[1]
    scale = HEAD_DIM ** -0.5

    @pl.when(j == 0)
    def _():
        m_scr[...] = jnp.full_like(m_scr, -jnp.inf)
        l_scr[...] = jnp.zeros_like(l_scr)
        acc_scr[...] = jnp.zeros_like(acc_scr)

    @pl.when(j <= i)
    def _():
        q, k = q_ref[0].astype(BF16), k_ref[0].astype(BF16)
        v = v_ref[0].astype(BF16)
        qpos = i * tq + lax.broadcasted_iota(jnp.int32, (tq, tk), 0)
        kpos = j * tk + lax.broadcasted_iota(jnp.int32, (tq, tk), 1)
        visible = kpos <= qpos
        for mi in range(2):
            sl = slice(mi * HEAD_DIM, (mi + 1) * HEAD_DIM)
            s = lax.dot_general(q[:, sl], k[:, sl], (((1,), (1,)), ((), ())),
                                preferred_element_type=F32) * scale
            s = jnp.where(visible, s, -jnp.inf)
            m_old = m_scr[mi]
            m_new = jnp.maximum(m_old, jnp.max(s, axis=-1, keepdims=True))
            alpha = jnp.exp(m_old - m_new)
            p = jnp.exp(s - m_new)
            l_scr[mi] = alpha * l_scr[mi] + jnp.sum(p, axis=-1, keepdims=True)
            acc_scr[mi] = alpha * acc_scr[mi] + jnp.dot(p.astype(BF16), v, preferred_element_type=F32)
            m_scr[mi] = m_new

    @pl.when(j == i)
    def _():
        lam = _diff_lambda(lam_ref[...], lam_init)
        o = acc_scr[0] / l_scr[0] - lam * (acc_scr[1] / l_scr[1])
        o_ref[0] = _rms(o, subln_ref[...]) * (1.0 - lam_init)


def diff_attn_prompt(q, k, v, lam_p, subln, lam_init):
    b, l, width = q.shape
    dv = 2 * HEAD_DIM
    n_heads = width // dv
    t = _tile(l, ATTN_BLOCK)
    n = l // t
    qspec = pl.BlockSpec((1, t, dv), lambda bi, h, i, j: (bi, i, h))
    kspec = pl.BlockSpec((1, t, dv), lambda bi, h, i, j: (bi, jnp.minimum(i, j), h))
    return pl.pallas_call(
        functools.partial(_diff_attn_kernel, lam_init=lam_init),
        grid=(b, n_heads, n, n),
        in_specs=[qspec, kspec, kspec,
                  pl.BlockSpec(lam_p.shape, lambda bi, h, i, j: (0, 0)),
                  pl.BlockSpec(subln.shape, lambda bi, h, i, j: (0, 0))],
        out_specs=qspec,
        out_shape=jax.ShapeDtypeStruct((b, l, width), F32),
        scratch_shapes=[pltpu.VMEM((2, t, 1), F32), pltpu.VMEM((2, t, 1), F32), pltpu.VMEM((2, t, dv), F32)],
        compiler_params=_cparams(("parallel", "parallel", "parallel", "arbitrary")),
        name="diff_attn_prompt",
    )(q, k, v, lam_p, subln)


def _diff_attn_decode_kernel(pt_ref, q_ref, kn_ref, vn_ref, *rest, lam_init, n_heads, n_slots):
    kc_refs, vc_refs = rest[:n_slots], rest[n_slots:2 * n_slots]
    lam_ref, subln_ref, o_ref, m_scr, l_scr, acc_scr = rest[2 * n_slots:]
    p = pl.program_id(1)
    n_rows = 2 * n_heads
    dv = 2 * HEAD_DIM
    scale = HEAD_DIM ** -0.5
    row = lax.broadcasted_iota(jnp.int32, (n_rows, dv), 0)
    lane = lax.broadcasted_iota(jnp.int32, (n_rows, dv), 1)
    qmat = jnp.where(lane // HEAD_DIM == row % 2, q_ref[0], 0.0)
    cols = kc_refs[0].shape[1]
    col_head = lax.broadcasted_iota(jnp.int32, (n_rows, cols), 1) % n_heads
    own = col_head == lax.broadcasted_iota(jnp.int32, (n_rows, cols), 0) // 2

    @pl.when(p == 0)
    def _():
        m_scr[...] = jnp.full_like(m_scr, -jnp.inf)
        l_scr[...] = jnp.zeros_like(l_scr)
        acc_scr[...] = jnp.zeros_like(acc_scr)

    qb = qmat.astype(BF16)
    for kc_ref, vc_ref in zip(kc_refs, vc_refs):
        s = lax.dot_general(qb, kc_ref[0].astype(BF16), (((1,), (1,)), ((), ())),
                            preferred_element_type=F32) * scale
        s = jnp.where(own, s, -jnp.inf)
        m_old = m_scr[...]
        m_new = jnp.maximum(m_old, jnp.max(s, axis=-1, keepdims=True))
        alpha = jnp.exp(m_old - m_new)
        pr = jnp.exp(s - m_new)
        l_scr[...] = alpha * l_scr[...] + jnp.sum(pr, axis=-1, keepdims=True)
        acc_scr[...] = alpha * acc_scr[...] + jnp.dot(pr.astype(BF16), vc_ref[0].astype(BF16),
                                                      preferred_element_type=F32)
        m_scr[...] = m_new

    @pl.when(p == pl.num_programs(1) - 1)
    def _():
        s_new = jnp.sum(qmat * kn_ref[0], axis=-1, keepdims=True) * scale
        m_old = m_scr[...]
        m_fin = jnp.maximum(m_old, s_new)
        alpha = jnp.exp(m_old - m_fin)
        p_new = jnp.exp(s_new - m_fin)
        l_fin = alpha * l_scr[...] + p_new
        acc = (alpha * acc_scr[...] + p_new * vn_ref[0]) / l_fin
        lam = _diff_lambda(lam_ref[...], lam_init)
        acc = acc * jnp.where(row % 2 == 0, 1.0, -lam)
        subln = subln_ref[...]
        for h in range(n_heads):
            o = acc[2 * h:2 * h + 1] + acc[2 * h + 1:2 * h + 2]
            o_ref[0, h:h + 1, :] = _rms(o, subln) * (1.0 - lam_init)


def diff_attn_decode(q, k_new, v_new, cache_k, cache_v, page_table, lam_p, subln, lam_init):
    b, width = q.shape
    dv = 2 * HEAD_DIM
    n_heads = width // dv
    n_pages = page_table.shape[1]
    rows = cache_k.shape[1]
    n_slots = _tile(n_pages, DECODE_PAGES)
    per_map = lambda t: jnp.repeat(t.reshape(b, n_heads, dv), 2, axis=1)
    vec = pl.BlockSpec((1, 2 * n_heads, dv), lambda bi, p, pt: (bi, 0, 0))
    cache = [pl.BlockSpec((1, rows, dv), functools.partial(
        lambda bi, p, pt, slot: (pt[bi * n_pages + p * n_slots + slot], 0, 0), slot=slot))
        for slot in range(n_slots)]
    out = pl.pallas_call(
        functools.partial(_diff_attn_decode_kernel, lam_init=lam_init, n_heads=n_heads, n_slots=n_slots),
        grid_spec=pltpu.PrefetchScalarGridSpec(
            num_scalar_prefetch=1,
            grid=(b, n_pages // n_slots),
            in_specs=[vec, vec, vec] + cache + cache + [
                pl.BlockSpec(lam_p.shape, lambda bi, p, pt: (0, 0)),
                pl.BlockSpec(subln.shape, lambda bi, p, pt: (0, 0))],
            out_specs=pl.BlockSpec((1, n_heads, dv), lambda bi, p, pt: (bi, 0, 0)),
            scratch_shapes=[pltpu.VMEM((2 * n_heads, 1), F32), pltpu.VMEM((2 * n_heads, 1), F32),
                            pltpu.VMEM((2 * n_heads, dv), F32)],
        ),
        out_shape=jax.ShapeDtypeStruct((b, n_heads, dv), F32),
        compiler_params=_cparams(("parallel", "arbitrary")),
        name="diff_attn_decode",
    )(page_table.reshape(-1), per_map(q), per_map(k_new), per_map(v_new),
      *([cache_k] * n_slots), *([cache_v] * n_slots), lam_p, subln)
    return out.reshape(b, width)


def _lru_gates_kernel(x_ref, x1_ref, x2_ref, x3_ref, gr_ref, cw_ref, cb_ref, wa_ref, ba_ref, wi_ref,
                      bi_ref, lam_ref, a_out, b_out, gel_out):
    cw = cw_ref[...]
    xc = x3_ref[...] * cw[0:1] + x2_ref[...] * cw[1:2] + x1_ref[...] * cw[2:3] + x_ref[...] * cw[3:4]
    xc = xc + cb_ref[...]
    r = jax.nn.sigmoid(_dot(xc, wa_ref[...]) + ba_ref[...])
    ig = jax.nn.sigmoid(_dot(xc, wi_ref[...]) + bi_ref[...])
    log_a = -LRU_C * r * _softplus(-lam_ref[...])
    a = jnp.exp(log_a)
    a_out[...] = a
    b_out[...] = jnp.sqrt(-jnp.tanh(log_a) * (a * a + 1.0)) * (ig * xc)
    gel_out[...] = jax.nn.gelu(gr_ref[...])


def lru_gates(x, x1, x2, x3, gr, prm):
    m, w = x.shape
    tm = _tile(m, 512)
    rows = pl.BlockSpec((tm, w), lambda i: (i, 0))
    full = lambda a: pl.BlockSpec(a.shape, lambda i: (0,) * a.ndim)
    params = [prm[n] for n in ("conv_w", "conv_b", "wa", "ba", "wi", "bi", "lam")]
    return pl.pallas_call(
        _lru_gates_kernel,
        grid=(m // tm,),
        in_specs=[rows] * 5 + [full(a) for a in params],
        out_specs=[rows] * 3,
        out_shape=[jax.ShapeDtypeStruct((m, w), F32)] * 3,
        compiler_params=_cparams(("parallel",)),
        name="lru_gates",
    )(x, x1, x2, x3, gr, *params)


def _lru_scan_kernel(a_ref, b_ref, gel_ref, h0_ref, o_ref, hl_ref, h_scr):
    c = pl.program_id(1)
    tl = a_ref.shape[1]

    @pl.when(c == 0)
    def _():
        h_scr[...] = h0_ref[0]

    rows = min(tl, SUBLANES)
    row_id = lax.broadcasted_iota(jnp.int32, (rows, a_ref.shape[2]), 0)

    def block(tb, h):
        base = pl.multiple_of(tb * rows, rows)
        a8, b8 = a_ref[0, pl.ds(base, rows), :], b_ref[0, pl.ds(base, rows), :]
        hs = jnp.zeros_like(a8)
        for i in range(rows):
            h = a8[i:i + 1, :] * h + b8[i:i + 1, :]
            hs = jnp.where(row_id == i, h, hs)
        o_ref[0, pl.ds(base, rows), :] = hs * gel_ref[0, pl.ds(base, rows), :]
        return h

    h = lax.fori_loop(0, tl // rows, block, h_scr[...])
    h_scr[...] = h
    hl_ref[0] = h


def lru_scan(a, b, gel, h0):
    bsz, l, w = a.shape
    tl = _tile(l, 512)
    seq = pl.BlockSpec((1, tl, w), lambda i, j: (i, j, 0))
    vec = pl.BlockSpec((1, 1, w), lambda i, j: (i, 0, 0))
    o, hl = pl.pallas_call(
        _lru_scan_kernel,
        grid=(bsz, l // tl),
        in_specs=[seq, seq, seq, vec],
        out_specs=[seq, vec],
        out_shape=[jax.ShapeDtypeStruct((bsz, l, w), F32), jax.ShapeDtypeStruct((bsz, 1, w), F32)],
        scratch_shapes=[pltpu.VMEM((1, w), F32)],
        compiler_params=_cparams(("parallel", "arbitrary")),
        name="lru_scan",
    )(a, b, gel, h0.reshape(bsz, 1, w))
    return o, hl.reshape(bsz, w)


def _block_diag(w):
    n, d, e = w.shape
    eye = jnp.eye(n, dtype=w.dtype)
    return (eye[:, None, :, None] * w[:, :, None, :]).reshape(n * d, n * e)


def _delayed(x, buf, d):
    nb, l = buf.shape[1], x.shape[1]
    return jnp.concatenate([buf[:, nb - d:], x[:, :max(l - d, 0)]], axis=1)[:, :l]


def _mix_even(h, g_norm, pos0, s_ret, s_rwkv, buf, wts, is_prompt):
    b, l, d = h.shape
    m = b * l
    gw = d // 2
    u = norm_matmul(h.reshape(m, d), g_norm, wts["ab_w_in"])
    seq = lambda t: t.reshape(b, l, gw)
    qa, ka, va, ga = u[0], u[1], u[2], u[3]
    cur = [u[4], u[5], u[6], u[7]]
    prev = [_delayed(seq(t), buf[..., i * gw:(i + 1) * gw], 1).reshape(m, gw) for i, t in enumerate(cur)]
    buf_new = jnp.concatenate([seq(t)[:, l - 1:] for t in cur], axis=-1)
    r, w, k, v, kk, kka, g, bonus, v_hi, v_lo = rwkv_prep(cur, prev, wts["rwkv"])
    if is_prompt:
        pos = pos0 + jnp.arange(l, dtype=F32)
        o_a, s_ret_new = retention_prompt(seq(qa), seq(ka), seq(va), seq(ga), pos)
        o_a = o_a.reshape(m, gw)
        o_b, s_rwkv_new = rwkv_scan_prompt(seq(r), seq(w), seq(k), seq(kk), seq(kka), seq(v_hi), seq(v_lo))
        o_b = o_b.reshape(m, gw)
    else:
        o_a, s_ret_new = retention_step(qa, ka, va, ga, s_ret, jnp.float32(pos0))
        o_b, s_rwkv_new = rwkv_step(r, w, k, kk, kka, v, s_rwkv)
    o_b = rwkv_post(o_b, bonus, g, wts["rwkv_ln"], wts["rwkv"]["ones"])
    return o_a, o_b, s_ret_new, s_rwkv_new, buf_new


def _mix_odd(h, g_norm, lru_h, lru_buf, pages, wts, layer, is_prompt):
    b, l, d = h.shape
    m = b * l
    gw = d // 2
    u = norm_matmul(h.reshape(m, d), g_norm, wts["cd_w_in"])
    seq = lambda t: t.reshape(b, l, gw)
    lam_init = 0.8 - 0.6 * math.exp(-0.3 * layer)
    if is_prompt:
        o_c = diff_attn_prompt(seq(u[0]), seq(u[1]), seq(u[2]), wts["diff_lam"], wts["diff_subln"], lam_init)
        o_c = o_c.reshape(m, gw)
    else:
        cache_k, cache_v, page_table = pages
        o_c = diff_attn_decode(u[0], u[1], u[2], cache_k, cache_v, page_table, wts["diff_lam"],
                               wts["diff_subln"], lam_init)
    xr = seq(u[3])
    x1, x2, x3 = (_delayed(xr, lru_buf, t).reshape(m, gw) for t in (1, 2, 3))
    a, bt, gel = lru_gates(u[3], x1, x2, x3, u[4], wts["lru"])
    o_d, h_last = lru_scan(seq(a), seq(bt), seq(gel), lru_h)
    buf_new = jnp.concatenate([lru_buf, xr], axis=1)[:, l:]
    n_heads = gw // (2 * HEAD_DIM)
    k_new = u[1].reshape(b, l, n_heads, 2 * HEAD_DIM)
    v_new = u[2].reshape(b, l, n_heads, 2 * HEAD_DIM)
    return o_c, o_d.reshape(m, gw), k_new, v_new, h_last, buf_new


def _trunk(x, p, pos0, s_ret, s_rwkv, s_shift, s_lru_h, s_lru_conv, pages, wts, is_prompt):
    b, l, d = x.shape
    m = b * l
    depth = wts["norm_g"].shape[0]
    h = x.reshape(m, d)
    ret_l, rwkv_l, shift_l, k_l, v_l, lh_l, lc_l = [], [], [], [], [], [], []
    for i in range(depth):
        j = i // 2
        g = wts["norm_g"][i]
        gn = lambda n: g[n:n + 1]
        h = ffn_block(h, gn(0), wts["ffn_in"][i][0], wts["ffn_out"][i][0], gn(1))
        if i % 2 == 0:
            o1, o2, sr, sw, sb = _mix_even(h.reshape(b, l, d), gn(2), pos0, s_ret[j], s_rwkv[j], s_shift[j],
                                           wts["even"][j], is_prompt)
            ret_l.append(sr)
            rwkv_l.append(sw)
            shift_l.append(sb)
            w_out = wts["even"][j]["w_out"]
        else:
            pg = None if pages is None else (pages[0][j], pages[1][j], pages[2])
            o1, o2, kn, vn, lh, lc = _mix_odd(h.reshape(b, l, d), gn(2), s_lru_h[j], s_lru_conv[j], pg,
                                              wts["odd"][j], i, is_prompt)
            k_l.append(kn)
            v_l.append(vn)
            lh_l.append(lh)
            lc_l.append(lc)
            w_out = wts["odd"][j]["w_out"]
        h = out_proj(o1, o2, w_out, h, gn(3))
        h = ffn_block(h, gn(4), wts["ffn_in"][i][1], wts["ffn_out"][i][1], gn(5))
        h = ple_block(h, gn(6), wts["ple_gate"][i], p[i].reshape(m, -1), wts["ple"][i], gn(7))
    st = lambda lst: jnp.stack(lst, axis=0)
    return (h.reshape(b, l, d), st(k_l), st(v_l), st(ret_l), st(rwkv_l), st(shift_l), st(lh_l), st(lc_l))


def kernel(x_prompt, x_sample, cache_k, cache_v, state_ret, state_rwkv, state_rwkv_shift, state_lru_h, state_lru_conv, page_table, p_prompt, p_sample, norm_g, ffn_w_in, ffn_w_out, ple_w, ple_gate_w, ab_w_in, ab_w_out, rwkv_mu, rwkv_w0, rwkv_w1, rwkv_w2, rwkv_a0, rwkv_a1, rwkv_a2, rwkv_g1, rwkv_g2, rwkv_kk, rwkv_ka, rwkv_rk, rwkv_ln, cd_w_in, cd_w_out, diff_lam, diff_subln, lru_conv_w, lru_conv_b, lru_wa, lru_ba, lru_wi, lru_bi, lru_lambda):
    depth = norm_g.shape[0]
    n_a, n_c = state_ret.shape[0], state_lru_h.shape[0]
    bp = x_prompt.shape[0]
    gw = ab_w_out.shape[1] // 2
    bf = lambda t: t.astype(BF16)
    row = lambda t: t.reshape(1, -1)
    ones_bd = _block_diag(jnp.ones((gw // HEAD_DIM, HEAD_DIM, HEAD_DIM), BF16))
    wts = {
        "norm_g": norm_g,
        "ffn_in": [[bf(ffn_w_in[i, s]) for s in range(2)] for i in range(depth)],
        "ffn_out": [[bf(ffn_w_out[i, s]) for s in range(2)] for i in range(depth)],
        "ple": [bf(ple_w[i]) for i in range(depth)],
        "ple_gate": [bf(ple_gate_w[i]) for i in range(depth)],
        "even": [{
            "ab_w_in": bf(ab_w_in[j]), "w_out": bf(ab_w_out[j]), "rwkv_ln": rwkv_ln[j],
            "rwkv": {"mu": rwkv_mu[j], "w0": row(rwkv_w0[j]), "w1": bf(rwkv_w1[j]), "w2": bf(rwkv_w2[j]),
                     "a0": row(rwkv_a0[j]), "a1": bf(rwkv_a1[j]), "a2": bf(rwkv_a2[j]),
                     "g1": bf(rwkv_g1[j]), "g2": bf(rwkv_g2[j]), "kk": row(rwkv_kk[j]),
                     "ka": row(rwkv_ka[j]), "rk": row(rwkv_rk[j]), "ones": ones_bd},
        } for j in range(n_a)],
        "odd": [{
            "cd_w_in": bf(cd_w_in[j]), "w_out": bf(cd_w_out[j]), "diff_lam": diff_lam[j],
            "diff_subln": row(diff_subln[j]),
            "lru": {"conv_w": lru_conv_w[j], "conv_b": row(lru_conv_b[j]), "wa": bf(_block_diag(lru_wa[j])),
                    "ba": row(lru_ba[j]), "wi": bf(_block_diag(lru_wi[j])), "bi": row(lru_bi[j]),
                    "lam": row(lru_lambda[j])},
        } for j in range(n_c)],
    }
    zeros = lambda *shape: jnp.zeros(shape, F32)
    yp, kp, vp, rp, wp, sp, hp, cp = _trunk(
        x_prompt, p_prompt, 0.0, [None] * n_a, [None] * n_a,
        zeros(n_a, bp, 1, 4 * gw), zeros(n_c, bp, gw), zeros(n_c, bp, CONV_W - 1, gw),
        None, wts, True)
    past_len = page_table.shape[1] * cache_k.shape[2]
    n_pool, page = cache_k.shape[1], cache_k.shape[2]
    as_rows = lambda c: c.reshape(n_c, n_pool, page * c.shape[3], c.shape[4])
    pages = (as_rows(cache_k), as_rows(cache_v), page_table)
    ys, ks_, vs, rs, ws, ss, hs, cs = _trunk(
        x_sample, p_sample, float(past_len), state_ret, state_rwkv, state_rwkv_shift,
        state_lru_h, state_lru_conv, pages, wts, False)
    return (yp, ys, kp, vp, rp, wp, sp, hp, cp, ks_, vs, rs, ws, ss, hs, cs)
```

```python
import functools
import math

import jax
import jax.numpy as jnp
from jax import lax
from jax.experimental import pallas as pl
from jax.experimental.pallas import tpu as pltpu

F32 = jnp.float32
BF16 = jnp.bfloat16

HEAD_DIM = 64
CONV_W = 4
LRU_C = 8.0
ROPE_BASE = 10000.0
EPS = 1e-6
RWKV_GN_EPS = 64e-5
RET_CHUNK = 256
ATTN_BLOCK = 512
SCAN_CHUNK = 128
DECODE_PAGES = 4
SCAN_GROUP = 2
SEL_STEPS = 32
LANES = 128
SUBLANES = 8
VMEM_LIMIT = 48 * 1024 * 1024


def _cparams(sem):
    return pltpu.CompilerParams(dimension_semantics=sem, vmem_limit_bytes=VMEM_LIMIT)


def _tile(n, pref):
    t = min(n, pref)
    while n % t:
        t //= 2
    return t


def _rms(x, g):
    return x * lax.rsqrt(jnp.mean(x * x, axis=-1, keepdims=True) + EPS) * g


def _dot(a, b):
    return jnp.dot(a.astype(BF16), b.astype(BF16), preferred_element_type=F32)


def _seg_sum(x, ones_bd):
    hi = x.astype(BF16)
    lo = (x - hi.astype(F32)).astype(BF16)
    return (jnp.dot(hi, ones_bd, preferred_element_type=F32)
            + jnp.dot(lo, ones_bd, preferred_element_type=F32))


def _ffn_kernel(h_ref, gpre_ref, wg_ref, wu_ref, wo_ref, gpost_ref, o_ref, xn_ref, acc_ref):
    j = pl.program_id(1)

    @pl.when(j == 0)
    def _():
        xn_ref[...] = _rms(h_ref[...], gpre_ref[...]).astype(BF16)
        acc_ref[...] = jnp.zeros_like(acc_ref)

    xn = xn_ref[...]
    gate = jnp.dot(xn, wg_ref[...], preferred_element_type=F32)
    up = jnp.dot(xn, wu_ref[...], preferred_element_type=F32)
    act = (gate * jax.nn.sigmoid(gate) * up).astype(BF16)
    acc_ref[...] += jnp.dot(act, wo_ref[...], preferred_element_type=F32)

    @pl.when(j == pl.num_programs(1) - 1)
    def _():
        o_ref[...] = h_ref[...] + 0.5 * _rms(acc_ref[...], gpost_ref[...])


def ffn_block(h, g_pre, w_in, w_out, g_post):
    m, d = h.shape
    f = w_out.shape[0]
    tm, tf = _tile(m, 1024), _tile(f, 512)
    nf = f // tf
    return pl.pallas_call(
        _ffn_kernel,
        grid=(m // tm, nf),
        in_specs=[
            pl.BlockSpec((tm, d), lambda i, j: (i, 0)),
            pl.BlockSpec((1, d), lambda i, j: (0, 0)),
            pl.BlockSpec((d, tf), lambda i, j: (0, j)),
            pl.BlockSpec((d, tf), lambda i, j: (0, j + nf)),
            pl.BlockSpec((tf, d), lambda i, j: (j, 0)),
            pl.BlockSpec((1, d), lambda i, j: (0, 0)),
        ],
        out_specs=pl.BlockSpec((tm, d), lambda i, j: (i, 0)),
        out_shape=jax.ShapeDtypeStruct((m, d), F32),
        scratch_shapes=[pltpu.VMEM((tm, d), BF16), pltpu.VMEM((tm, d), F32)],
        compiler_params=_cparams(("parallel", "arbitrary")),
        name="ffn_block",
    )(h, g_pre, w_in, w_in, w_out, g_post)


def _norm_matmul_kernel(h_ref, g_ref, w_ref, *o_refs):
    xn = _rms(h_ref[...], g_ref[...]).astype(BF16)
    tn = o_refs[0].shape[1]
    for gi, o_ref in enumerate(o_refs):
        o_ref[...] = jnp.dot(xn, w_ref[:, gi * tn:(gi + 1) * tn], preferred_element_type=F32)


def norm_matmul(h, g, w, tn):
    m, d = h.shape
    n = w.shape[1]
    tm = _tile(m, 512)
    rows = pl.BlockSpec((tm, tn), lambda i: (i, 0))
    return pl.pallas_call(
        _norm_matmul_kernel,
        grid=(m // tm,),
        in_specs=[
            pl.BlockSpec((tm, d), lambda i: (i, 0)),
            pl.BlockSpec((1, d), lambda i: (0, 0)),
            pl.BlockSpec((d, n), lambda i: (0, 0)),
        ],
        out_specs=[rows] * (n // tn),
        out_shape=[jax.ShapeDtypeStruct((m, tn), F32)] * (n // tn),
        compiler_params=_cparams(("parallel",)),
        name="norm_matmul",
    )(h, g, w)


def _out_proj_kernel(oa_ref, ob_ref, wa_ref, wb_ref, h_ref, g_ref, o_ref):
    y = _dot(oa_ref[...], wa_ref[...]) + _dot(ob_ref[...], wb_ref[...])
    o_ref[...] = h_ref[...] + _rms(y, g_ref[...])


def out_proj(oa, ob, w, h, g):
    m, d = h.shape
    gw = oa.shape[1]
    tm = _tile(m, 512)
    return pl.pallas_call(
        _out_proj_kernel,
        grid=(m // tm,),
        in_specs=[
            pl.BlockSpec((tm, gw), lambda i: (i, 0)),
            pl.BlockSpec((tm, gw), lambda i: (i, 0)),
            pl.BlockSpec((gw, d), lambda i: (0, 0)),
            pl.BlockSpec((gw, d), lambda i: (1, 0)),
            pl.BlockSpec((tm, d), lambda i: (i, 0)),
            pl.BlockSpec((1, d), lambda i: (0, 0)),
        ],
        out_specs=pl.BlockSpec((tm, d), lambda i: (i, 0)),
        out_shape=jax.ShapeDtypeStruct((m, d), F32),
        compiler_params=_cparams(("parallel",)),
        name="out_proj",
    )(oa, ob, w, w, h, g)


def _ple_kernel(h_ref, g6_ref, wg_ref, p_ref, wp_ref, g7_ref, o_ref):
    h = h_ref[...]
    gate = jax.nn.sigmoid(_dot(_rms(h, g6_ref[...]), wg_ref[...]))
    y = gate * _dot(p_ref[...], wp_ref[...])
    o_ref[...] = h + _rms(y, g7_ref[...])


def ple_block(h, g6, wg, p, wp, g7):
    m, d = h.shape
    pd = p.shape[1]
    tm = _tile(m, 512)
    return pl.pallas_call(
        _ple_kernel,
        grid=(m // tm,),
        in_specs=[
            pl.BlockSpec((tm, d), lambda i: (i, 0)),
            pl.BlockSpec((1, d), lambda i: (0, 0)),
            pl.BlockSpec((d, d), lambda i: (0, 0)),
            pl.BlockSpec((tm, pd), lambda i: (i, 0)),
            pl.BlockSpec((pd, d), lambda i: (0, 0)),
            pl.BlockSpec((1, d), lambda i: (0, 0)),
        ],
        out_specs=pl.BlockSpec((tm, d), lambda i: (i, 0)),
        out_shape=jax.ShapeDtypeStruct((m, d), F32),
        compiler_params=_cparams(("parallel",)),
        name="ple_block",
    )(h, g6, wg, p, wp, g7)


def _retention_tables(n_heads, c, pos):
    lg = jnp.log1p(-jnp.exp2(-5.0 - jnp.arange(n_heads, dtype=F32)))
    idx = jnp.arange(c, dtype=F32)
    rel = idx[:, None] - idx[None, :]
    dmask = jnp.where(rel[None] >= 0, jnp.exp(jnp.maximum(rel, 0.0)[None] * lg[:, None, None]), 0.0)
    rep = lambda t: jnp.repeat(t, HEAD_DIM, axis=-1)
    q_dec = rep(jnp.exp((idx[:, None] + 1.0) * lg[None, :]))
    k_dec = rep(jnp.exp((c - 1.0 - idx[:, None]) * lg[None, :]))
    c_dec = rep(jnp.exp(c * lg)[None, :])
    half = HEAD_DIM // 2
    freq = 1.0 / (ROPE_BASE ** jnp.linspace(0.0, 1.0, half, dtype=F32))
    ang = pos[:, None] * freq[None, :]
    cos, sin = jnp.cos(ang), jnp.sin(ang)
    cos_t = jnp.tile(jnp.concatenate([cos, cos], axis=-1), (1, n_heads))
    sin_t = jnp.tile(jnp.concatenate([-sin, sin], axis=-1), (1, n_heads))
    return dmask, q_dec, k_dec, c_dec, cos_t, sin_t


def _retention_kernel(q_ref, k_ref, v_ref, g_ref, cos_ref, sin_ref, dmask_ref, qdec_ref, kdec_ref,
                      cdec_ref, o_ref, s_ref, s_scr, *, n_heads):
    c = pl.program_id(1)

    @pl.when(c == 0)
    def _():
        s_scr[...] = jnp.zeros_like(s_scr)

    q, k, v, g = q_ref[0], k_ref[0], v_ref[0], g_ref[0]
    cos, sin = cos_ref[...], sin_ref[...]
    width = q.shape[-1]
    lane = lax.broadcasted_iota(jnp.int32, q.shape, 1)
    first_half = (lane % HEAD_DIM) < (HEAD_DIM // 2)

    def rot(x):
        swapped = jnp.where(first_half, pltpu.roll(x, width - HEAD_DIM // 2, 1),
                            pltpu.roll(x, HEAD_DIM // 2, 1))
        return x * cos + swapped * sin

    qr = rot(q)
    kr = rot(k) * (HEAD_DIM ** -0.5)
    kd = kr * kdec_ref[...]
    qdec = qdec_ref[...]
    cdec = cdec_ref[...]
    gate = g * jax.nn.sigmoid(g)
    for h in range(n_heads):
        sl = slice(h * HEAD_DIM, (h + 1) * HEAD_DIM)
        qh = qr[:, sl].astype(BF16)
        kh = kr[:, sl].astype(BF16)
        vh = v[:, sl].astype(BF16)
        att = lax.dot_general(qh, kh, (((1,), (1,)), ((), ())), preferred_element_type=F32) * dmask_ref[h]
        s_old = s_scr[h]
        o = (jnp.dot(att.astype(BF16), vh, preferred_element_type=F32)
             + jnp.dot(qh, s_old.astype(BF16), preferred_element_type=F32) * qdec[:, sl])
        s_scr[h] = s_old * cdec[:, sl] + lax.dot_general(
            kd[:, sl].astype(BF16), vh, (((0,), (0,)), ((), ())), preferred_element_type=F32)
        oc = o - jnp.mean(o, axis=-1, keepdims=True)
        on = oc * lax.rsqrt(jnp.mean(oc * oc, axis=-1, keepdims=True) + EPS)
        o_ref[0, :, sl] = on * gate[:, sl]

    @pl.when(c == pl.num_programs(1) - 1)
    def _():
        s_ref[0] = s_scr[...]


def retention_prompt(q, k, v, g, pos):
    b, l, width = q.shape
    n_heads = width // HEAD_DIM
    c = _tile(l, RET_CHUNK)
    dmask, q_dec, k_dec, c_dec, cos_t, sin_t = _retention_tables(n_heads, c, pos)
    seq = pl.BlockSpec((1, c, width), lambda i, j: (i, j, 0))
    tab = pl.BlockSpec((c, width), lambda i, j: (j, 0))
    fixed = lambda shape: pl.BlockSpec(shape, lambda i, j: (0,) * len(shape))
    return pl.pallas_call(
        functools.partial(_retention_kernel, n_heads=n_heads),
        grid=(b, l // c),
        in_specs=[seq, seq, seq, seq, tab, tab, fixed((n_heads, c, c)), fixed((c, width)),
                  fixed((c, width)), fixed((1, width))],
        out_specs=[seq, pl.BlockSpec((1, n_heads, HEAD_DIM, HEAD_DIM), lambda i, j: (i, 0, 0, 0))],
        out_shape=[jax.ShapeDtypeStruct((b, l, width), F32),
                   jax.ShapeDtypeStruct((b, n_heads, HEAD_DIM, HEAD_DIM), F32)],
        scratch_shapes=[pltpu.VMEM((n_heads, HEAD_DIM, HEAD_DIM), F32)],
        compiler_params=_cparams(("parallel", "arbitrary")),
        name="retention_prompt",
    )(q, k, v, g, cos_t, sin_t, dmask, q_dec, k_dec, c_dec)


def _retention_step_kernel(q_ref, k_ref, v_ref, g_ref, cos_ref, sin_ref, gam_ref, s0_ref, o_ref, s_ref):
    cos, sin = cos_ref[...], sin_ref[...]

    def rot(x):
        half = HEAD_DIM // 2
        return x * cos + jnp.concatenate([x[:, :, half:], x[:, :, :half]], axis=2) * sin

    q = rot(q_ref[...])
    k = rot(k_ref[...]) * (HEAD_DIM ** -0.5)
    v, g = v_ref[...], g_ref[...]
    gam = gam_ref[...]
    s0 = s0_ref[...]
    att = jnp.sum(q * k, axis=2, keepdims=True)
    o = att * v + jnp.sum(q * s0, axis=2, keepdims=True) * gam
    s_ref[...] = s0 * gam + k * v
    oc = o - jnp.mean(o, axis=-1, keepdims=True)
    on = oc * lax.rsqrt(jnp.mean(oc * oc, axis=-1, keepdims=True) + EPS)
    o_ref[...] = on * (g * jax.nn.sigmoid(g))


def retention_step(q, k, v, g, s0, pos):
    b, width = q.shape
    n_heads = width // HEAD_DIM
    bb = _tile(b, 8)
    half = HEAD_DIM // 2
    freq = 1.0 / (ROPE_BASE ** jnp.linspace(0.0, 1.0, half, dtype=F32))
    ang = pos * freq
    cos_c = jnp.concatenate([jnp.cos(ang), jnp.cos(ang)])[:, None]
    sin_c = jnp.concatenate([-jnp.sin(ang), jnp.sin(ang)])[:, None]
    gam = jnp.exp(jnp.log1p(-jnp.exp2(-5.0 - jnp.arange(n_heads, dtype=F32)))).reshape(n_heads, 1, 1)
    col = lambda t: t.reshape(b, n_heads, HEAD_DIM, 1)
    row = lambda t: t.reshape(b, n_heads, 1, HEAD_DIM)
    cspec = pl.BlockSpec((bb, n_heads, HEAD_DIM, 1), lambda i: (i, 0, 0, 0))
    rspec = pl.BlockSpec((bb, n_heads, 1, HEAD_DIM), lambda i: (i, 0, 0, 0))
    sspec = pl.BlockSpec((bb, n_heads, HEAD_DIM, HEAD_DIM), lambda i: (i, 0, 0, 0))
    o, s = pl.pallas_call(
        _retention_step_kernel,
        grid=(b // bb,),
        in_specs=[cspec, cspec, rspec, rspec,
                  pl.BlockSpec((HEAD_DIM, 1), lambda i: (0, 0)), pl.BlockSpec((HEAD_DIM, 1), lambda i: (0, 0)),
                  pl.BlockSpec((n_heads, 1, 1), lambda i: (0, 0, 0)), sspec],
        out_specs=[rspec, sspec],
        out_shape=[jax.ShapeDtypeStruct((b, n_heads, 1, HEAD_DIM), F32),
                   jax.ShapeDtypeStruct((b, n_heads, HEAD_DIM, HEAD_DIM), F32)],
        compiler_params=_cparams(("parallel",)),
        name="retention_step",
    )(col(q), col(k), row(v), row(g), cos_c, sin_c, gam, s0)
    return o.reshape(b, width), s


def _softplus(x):
    return jnp.maximum(x, 0.0) + jnp.log1p(jnp.exp(-jnp.abs(x)))


def _shifted(cur, carry_row):
    if cur.shape[0] == 1:
        return carry_row
    first = lax.broadcasted_iota(jnp.int32, cur.shape, 0) == 0
    return jnp.where(first, carry_row, pltpu.roll(cur, 1, 0))


def _rwkv_prep_kernel(*refs, shift_in_kernel):
    cur_refs = refs[:4]
    if shift_in_kernel:
        buf_ref = refs[4]
        n_in = 5
    else:
        prev_refs = refs[4:8]
        n_in = 8
    (mu_ref, w0_ref, w1_ref, w2_ref, a0_ref, a1_ref, a2_ref, g1_ref, g2_ref, kkp_ref, kap_ref, rk_ref,
     ones_ref) = refs[n_in:n_in + 13]
    (r_out, w_out, k_out, v_out, kk_out, kka_out, g_out, bonus_out, vhi_out,
     vlo_out) = refs[n_in + 13:n_in + 23]
    cur = [ref[0] for ref in cur_refs]
    width = cur[0].shape[1]
    if shift_in_kernel:
        carry = refs[n_in + 23]

        @pl.when(pl.program_id(1) == 0)
        def _():
            for gi in range(4):
                carry[gi] = buf_ref[0, :, gi * width:(gi + 1) * width]

        prev = [_shifted(x, carry[gi]) for gi, x in enumerate(cur)]
        for gi, x in enumerate(cur):
            carry[gi] = x[x.shape[0] - 1:, :]
    else:
        prev = [ref[0] for ref in prev_refs]
    mu = mu_ref[...]
    lerp = lambda x, xp, i: x + (xp - x) * mu[i:i + 1]
    zr, pz = cur[3], prev[3]
    r = lerp(cur[0], prev[0], 0)
    kx = lerp(cur[1], prev[1], 1)
    vx = lerp(cur[2], prev[2], 2)
    zw, za, zg = lerp(zr, pz, 3), lerp(zr, pz, 4), lerp(zr, pz, 5)
    wpre = w0_ref[...] + _dot(jnp.tanh(_dot(zw, w1_ref[...])), w2_ref[...])
    decay = jnp.exp(-jnp.exp(-_softplus(-wpre) - 0.5))
    a = jax.nn.sigmoid(a0_ref[...] + _dot(_dot(za, a1_ref[...]), a2_ref[...]))
    g = _dot(jax.nn.sigmoid(_dot(zg, g1_ref[...])), g2_ref[...])
    ones_bd = ones_ref[...]
    kk = kx * kkp_ref[...]
    kk = kk / jnp.maximum(jnp.sqrt(_seg_sum(kk * kk, ones_bd)), 1e-12)
    k32 = kx * (1.0 + (a - 1.0) * kap_ref[...])
    r_out[0] = r
    w_out[0] = decay
    k_out[0] = k32
    v_out[0] = vx
    v_hi = vx.astype(BF16)
    vhi_out[0] = v_hi
    vlo_out[0] = (vx - v_hi.astype(F32)).astype(BF16)
    kk_out[0] = kk
    kka_out[0] = kk * a
    g_out[0] = g
    bonus_out[0] = _seg_sum(r * k32 * rk_ref[...], ones_bd) * vx


def rwkv_prep(cur, prev, buf, prm):
    b, l, w = cur[0].shape
    tl = _tile(l, 256)
    seq = pl.BlockSpec((1, tl, w), lambda i, j: (i, j, 0))
    full = lambda a: pl.BlockSpec(a.shape, lambda i, j: (0,) * a.ndim)
    params = [prm[n] for n in ("mu", "w0", "w1", "w2", "a0", "a1", "a2", "g1", "g2", "kk", "ka", "rk", "ones")]
    shift = prev is None
    if shift:
        extra, extra_specs = [buf], [pl.BlockSpec((1, 1, 4 * w), lambda i, j: (i, 0, 0))]
        scratch = [pltpu.VMEM((4, 1, w), F32)]
    else:
        extra, extra_specs, scratch = list(prev), [seq] * 4, []
    return pl.pallas_call(
        functools.partial(_rwkv_prep_kernel, shift_in_kernel=shift),
        grid=(b, l // tl),
        in_specs=[seq] * 4 + extra_specs + [full(a) for a in params],
        out_specs=[seq] * 10,
        out_shape=[jax.ShapeDtypeStruct((b, l, w), F32)] * 8 + [jax.ShapeDtypeStruct((b, l, w), BF16)] * 2,
        scratch_shapes=scratch,
        compiler_params=_cparams(("parallel", "arbitrary")),
        name="rwkv_prep",
    )(*cur, *extra, *params)


def _rwkv_post_kernel(o_ref, bonus_ref, g_ref, ln_ref, ones_ref, out_ref):
    ones_bd = ones_ref[...]
    o = o_ref[...]
    inv = 1.0 / HEAD_DIM
    oc = o - _seg_sum(o, ones_bd) * inv
    on = oc * lax.rsqrt(_seg_sum(oc * oc, ones_bd) * inv + RWKV_GN_EPS)
    ln = ln_ref[...]
    out_ref[...] = (on * ln[0:1] + ln[1:2] + bonus_ref[...]) * g_ref[...]


def rwkv_post(o, bonus, g, ln, ones_bd):
    m, w = o.shape
    tm = _tile(m, 512)
    rows = pl.BlockSpec((tm, w), lambda i: (i, 0))
    return pl.pallas_call(
        _rwkv_post_kernel,
        grid=(m // tm,),
        in_specs=[rows, rows, rows, pl.BlockSpec(ln.shape, lambda i: (0, 0)),
                  pl.BlockSpec(ones_bd.shape, lambda i: (0, 0))],
        out_specs=rows,
        out_shape=jax.ShapeDtypeStruct((m, w), F32),
        compiler_params=_cparams(("parallel",)),
        name="rwkv_post",
    )(o, bonus, g, ln, ones_bd)


def _hi_lo(x):
    hi = x.astype(BF16)
    lo = (x - hi.astype(F32)).astype(BF16)
    return jnp.concatenate([hi, lo], axis=1)


def _rwkv_scan_kernel(r_ref, w_ref, k_ref, kk_ref, kka_ref, vp_ref, sel_ref, ones_ref, o_ref, s_ref, s_scr,
                      ot_scr, *, n_pairs):
    c = pl.program_id(1)
    n_grp, tc = r_ref.shape[0], r_ref.shape[1]
    chains = [(g, p) for g in range(n_grp) for p in range(n_pairs)]

    @pl.when(c == 0)
    def _():
        s_scr[...] = jnp.zeros_like(s_scr)

    ot_scr[...] = jnp.zeros_like(ot_scr)
    ones2 = ones_ref[...]
    t_lane = lax.broadcasted_iota(jnp.int32, (len(chains) * HEAD_DIM, LANES), 1) % HEAD_DIM

    def block(tb, carry):
        base = pl.multiple_of(tb * SUBLANES, SUBLANES)
        tiles = {name: [ref[g, pl.ds(base, SUBLANES), p * LANES:(p + 1) * LANES] for g, p in chains]
                 for name, ref in (("kk", kk_ref), ("w", w_ref), ("kka", kka_ref), ("k", k_ref), ("r", r_ref))}
        vp = jnp.concatenate([vp_ref[g, tb // (SEL_STEPS // SUBLANES), p] for g, p in chains], axis=0)
        s = s_scr[...]
        acc = jnp.zeros_like(s)
        for i in range(SUBLANES):
            rows = lambda name: jnp.concatenate(
                [jnp.broadcast_to(t[i:i + 1, :], (HEAD_DIM, LANES)) for t in tiles[name]], axis=0)
            sa = jnp.dot(_hi_lo(s * rows("kk")), ones2, preferred_element_type=F32)
            vcol = jnp.dot(vp, sel_ref[(tb % (SEL_STEPS // SUBLANES)) * SUBLANES + i], preferred_element_type=F32)
            s = s * rows("w") - sa * rows("kka") + vcol * rows("k")
            o = jnp.dot(_hi_lo(s * rows("r")), ones2, preferred_element_type=F32)
            acc = jnp.where(t_lane == (base + i) % HEAD_DIM, o, acc)
        s_scr[...] = s
        tile = base // HEAD_DIM
        ot_scr[tile] += acc
        return carry

    lax.fori_loop(0, tc // SUBLANES, block, 0)

    for tile in range(tc // HEAD_DIM):
        for ci, (g, p) in enumerate(chains):
            o_t = ot_scr[tile, ci * HEAD_DIM:(ci + 1) * HEAD_DIM, :].T
            for ab in range(2):
                lanes = slice(p * LANES + ab * HEAD_DIM, p * LANES + (ab + 1) * HEAD_DIM)
                o_ref[g, tile * HEAD_DIM:(tile + 1) * HEAD_DIM, lanes] = o_t[ab * HEAD_DIM:(ab + 1) * HEAD_DIM, :]

    @pl.when(c == pl.num_programs(1) - 1)
    def _():
        for ci, (g, p) in enumerate(chains):
            s_ref[g, p] = s_scr[ci * HEAD_DIM:(ci + 1) * HEAD_DIM, :]


def rwkv_scan_prompt(r, w, k, kk, kka, v_hi, v_lo):
    b, l, width = r.shape
    n_pairs = width // LANES
    tc = _tile(l, SCAN_CHUNK)
    grp = _tile(b, SCAN_GROUP)
    nsel = l // SEL_STEPS
    vp = jnp.stack([v_hi, v_lo], axis=2).reshape(b, nsel, SEL_STEPS, 2, n_pairs, 2, HEAD_DIM)
    vp = vp.transpose(0, 1, 4, 6, 5, 3, 2).reshape(b, nsel, n_pairs, HEAD_DIM, LANES)
    kl = jnp.arange(LANES)
    sel = ((kl[None, :, None] % SEL_STEPS == jnp.arange(SEL_STEPS)[:, None, None])
           & (kl[None, :, None] // HEAD_DIM == kl[None, None, :] // HEAD_DIM)).astype(BF16)
    k2 = jnp.arange(2 * LANES)
    ones2 = ((k2[:, None] % LANES) // HEAD_DIM == kl[None, :] // HEAD_DIM).astype(BF16)
    seq = pl.BlockSpec((grp, tc, width), lambda i, j: (i, j, 0))
    o, s = pl.pallas_call(
        functools.partial(_rwkv_scan_kernel, n_pairs=n_pairs),
        grid=(b // grp, l // tc),
        in_specs=[seq, seq, seq, seq, seq,
                  pl.BlockSpec((grp, tc // SEL_STEPS, n_pairs, HEAD_DIM, LANES), lambda i, j: (i, j, 0, 0, 0)),
                  pl.BlockSpec(sel.shape, lambda i, j: (0, 0, 0)),
                  pl.BlockSpec(ones2.shape, lambda i, j: (0, 0))],
        out_specs=[seq, pl.BlockSpec((grp, n_pairs, HEAD_DIM, LANES), lambda i, j: (i, 0, 0, 0))],
        out_shape=[jax.ShapeDtypeStruct((b, l, width), F32),
                   jax.ShapeDtypeStruct((b, n_pairs, HEAD_DIM, LANES), F32)],
        scratch_shapes=[pltpu.VMEM((grp * n_pairs * HEAD_DIM, LANES), F32),
                        pltpu.VMEM((tc // HEAD_DIM, grp * n_pairs * HEAD_DIM, LANES), F32)],
        compiler_params=_cparams(("parallel", "arbitrary")),
        name="rwkv_scan_prompt",
    )(r, w, k, kk, kka, vp, sel, ones2)
    s = s.reshape(b, n_pairs, HEAD_DIM, 2, HEAD_DIM).transpose(0, 1, 3, 2, 4)
    return o, s.reshape(b, 2 * n_pairs, HEAD_DIM, HEAD_DIM)


def _rwkv_step_kernel(r_ref, w_ref, k_ref, kk_ref, kka_ref, v_ref, s0_ref, o_ref, s_ref):
    s0 = s0_ref[...]
    sa = -jnp.sum(s0 * kk_ref[...], axis=-1, keepdims=True)
    s = s0 * w_ref[...] + sa * kka_ref[...] + v_ref[...] * k_ref[...]
    s_ref[...] = s
    o_ref[...] = jnp.sum(s * r_ref[...], axis=-1, keepdims=True)


def rwkv_step(r, w, k, kk, kka, v, s0):
    b, width = r.shape
    n_heads = width // HEAD_DIM
    bb = _tile(b, 8)
    row = lambda t: t.reshape(b, n_heads, 1, HEAD_DIM)
    cspec = pl.BlockSpec((bb, n_heads, HEAD_DIM, 1), lambda i: (i, 0, 0, 0))
    rspec = pl.BlockSpec((bb, n_heads, 1, HEAD_DIM), lambda i: (i, 0, 0, 0))
    sspec = pl.BlockSpec((bb, n_heads, HEAD_DIM, HEAD_DIM), lambda i: (i, 0, 0, 0))
    o, s = pl.pallas_call(
        _rwkv_step_kernel,
        grid=(b // bb,),
        in_specs=[rspec] * 5 + [cspec, sspec],
        out_specs=[cspec, sspec],
        out_shape=[jax.ShapeDtypeStruct((b, n_heads, HEAD_DIM, 1), F32),
                   jax.ShapeDtypeStruct((b, n_heads, HEAD_DIM, HEAD_DIM), F32)],
        compiler_params=_cparams(("parallel",)),
        name="rwkv_step",
    )(row(r), row(w), row(k), row(kk), row(kka), v.reshape(b, n_heads, HEAD_DIM, 1), s0)
    return o.reshape(b, width), s


def _diff_lambda(lp, lam_init):
    e1 = jnp.exp(jnp.sum(lp[0:1] * lp[1:2], axis=-1, keepdims=True))
    e2 = jnp.exp(jnp.sum(lp[2:3] * lp[3:4], axis=-1, keepdims=True))
    return e1 - e2 + lam_init


def _diff_attn_kernel(q_ref, k_ref, v_ref, lam_ref, subln_ref, o_ref, m_scr, l_scr, acc_scr, *, lam_init):
    i, j = pl.program_id(2), pl.program_id(3)
    tq, tk = q_ref.shape[1], k_ref.shape[1]
    scale = HEAD_DIM ** -0.5

    @pl.when(j == 0)
    def _():
        m_scr[...] = jnp.full_like(m_scr, -jnp.inf)
        l_scr[...] = jnp.zeros_like(l_scr)
        acc_scr[...] = jnp.zeros_like(acc_scr)

    def update(on_diagonal):
        q = (q_ref[0] * scale).astype(BF16)
        k, v = k_ref[0].astype(BF16), v_ref[0].astype(BF16)
        if on_diagonal:
            visible = (lax.broadcasted_iota(jnp.int32, (tq, tk), 1)
                       <= lax.broadcasted_iota(jnp.int32, (tq, tk), 0))
        for mi in range(2):
            sl = slice(mi * HEAD_DIM, (mi + 1) * HEAD_DIM)
            s = lax.dot_general(q[:, sl], k[:, sl], (((1,), (1,)), ((), ())), preferred_element_type=F32)
            if on_diagonal:
                s = jnp.where(visible, s, -jnp.inf)
            m_old = m_scr[mi]
            m_new = jnp.maximum(m_old, jnp.max(s, axis=-1, keepdims=True))
            alpha = jnp.exp(m_old - m_new)
            p = jnp.exp(s - jnp.concatenate([m_new] * (tk // LANES), axis=1))
            l_scr[mi] = alpha * l_scr[mi] + jnp.sum(p, axis=-1, keepdims=True)
            acc_scr[mi] = alpha * acc_scr[mi] + jnp.dot(p.astype(BF16), v, preferred_element_type=F32)
            m_scr[mi] = m_new

    @pl.when(j < i)
    def _():
        update(False)

    @pl.when(j == i)
    def _():
        update(True)
        lam = _diff_lambda(lam_ref[...], lam_init)
        o = acc_scr[0] / l_scr[0] - lam * (acc_scr[1] / l_scr[1])
        o_ref[0] = _rms(o, subln_ref[...]) * (1.0 - lam_init)


def diff_attn_prompt(q, k, v, lam_p, subln, lam_init):
    b, l, width = q.shape
    dv = 2 * HEAD_DIM
    n_heads = width // dv
    t = _tile(l, ATTN_BLOCK)
    n = l // t
    qspec = pl.BlockSpec((1, t, dv), lambda bi, h, i, j: (bi, i, h))
    kspec = pl.BlockSpec((1, t, dv), lambda bi, h, i, j: (bi, jnp.minimum(i, j), h))
    return pl.pallas_call(
        functools.partial(_diff_attn_kernel, lam_init=lam_init),
        grid=(b, n_heads, n, n),
        in_specs=[qspec, kspec, kspec,
                  pl.BlockSpec(lam_p.shape, lambda bi, h, i, j: (0, 0)),
                  pl.BlockSpec(subln.shape, lambda bi, h, i, j: (0, 0))],
        out_specs=qspec,
        out_shape=jax.ShapeDtypeStruct((b, l, width), F32),
        scratch_shapes=[pltpu.VMEM((2, t, LANES), F32), pltpu.VMEM((2, t, LANES), F32), pltpu.VMEM((2, t, dv), F32)],
        compiler_params=_cparams(("parallel", "parallel", "parallel", "arbitrary")),
        name="diff_attn_prompt",
    )(q, k, v, lam_p, subln)


def _diff_attn_decode_kernel(pt_ref, q_ref, kn_ref, vn_ref, *rest, lam_init, n_heads, n_slots):
    kc_refs, vc_refs = rest[:n_slots], rest[n_slots:2 * n_slots]
    lam_ref, subln_ref, o_ref, m_scr, l_scr, acc_scr = rest[2 * n_slots:]
    p = pl.program_id(1)
    n_rows = 2 * n_heads
    dv = 2 * HEAD_DIM
    scale = HEAD_DIM ** -0.5
    row = lax.broadcasted_iota(jnp.int32, (n_rows, dv), 0)
    lane = lax.broadcasted_iota(jnp.int32, (n_rows, dv), 1)
    qmat = jnp.where(lane // HEAD_DIM == row % 2, q_ref[0], 0.0)
    cols = kc_refs[0].shape[1]
    col_head = lax.broadcasted_iota(jnp.int32, (n_rows, cols), 1) % n_heads
    own = col_head == lax.broadcasted_iota(jnp.int32, (n_rows, cols), 0) // 2

    @pl.when(p == 0)
    def _():
        m_scr[...] = jnp.full_like(m_scr, -jnp.inf)
        l_scr[...] = jnp.zeros_like(l_scr)
        acc_scr[...] = jnp.zeros_like(acc_scr)

    qb = qmat.astype(BF16)
    for kc_ref, vc_ref in zip(kc_refs, vc_refs):
        s = lax.dot_general(qb, kc_ref[0].astype(BF16), (((1,), (1,)), ((), ())),
                            preferred_element_type=F32) * scale
        s = jnp.where(own, s, -jnp.inf)
        m_old = m_scr[...]
        m_new = jnp.maximum(m_old, jnp.max(s, axis=-1, keepdims=True))
        alpha = jnp.exp(m_old - m_new)
        pr = jnp.exp(s - m_new)
        l_scr[...] = alpha * l_scr[...] + jnp.sum(pr, axis=-1, keepdims=True)
        acc_scr[...] = alpha * acc_scr[...] + jnp.dot(pr.astype(BF16), vc_ref[0].astype(BF16),
                                                      preferred_element_type=F32)
        m_scr[...] = m_new

    @pl.when(p == pl.num_programs(1) - 1)
    def _():
        s_new = jnp.sum(qmat * kn_ref[0], axis=-1, keepdims=True) * scale
        m_old = m_scr[...]
        m_fin = jnp.maximum(m_old, s_new)
        alpha = jnp.exp(m_old - m_fin)
        p_new = jnp.exp(s_new - m_fin)
        l_fin = alpha * l_scr[...] + p_new
        acc = (alpha * acc_scr[...] + p_new * vn_ref[0]) / l_fin
        lam = _diff_lambda(lam_ref[...], lam_init)
        acc = acc * jnp.where(row % 2 == 0, 1.0, -lam)
        subln = subln_ref[...]
        for h in range(n_heads):
            o = acc[2 * h:2 * h + 1] + acc[2 * h + 1:2 * h + 2]
            o_ref[0, h:h + 1, :] = _rms(o, subln) * (1.0 - lam_init)


def diff_attn_decode(q, k_new, v_new, cache_k, cache_v, page_table, lam_p, subln, lam_init):
    b, width = q.shape
    dv = 2 * HEAD_DIM
    n_heads = width // dv
    n_pages = page_table.shape[1]
    rows = cache_k.shape[1]
    n_slots = _tile(n_pages, DECODE_PAGES)
    per_map = lambda t: jnp.repeat(t.reshape(b, n_heads, dv), 2, axis=1)
    vec = pl.BlockSpec((1, 2 * n_heads, dv), lambda bi, p, pt: (bi, 0, 0))
    cache = [pl.BlockSpec((1, rows, dv), functools.partial(
        lambda bi, p, pt, slot: (pt[bi * n_pages + p * n_slots + slot], 0, 0), slot=slot))
        for slot in range(n_slots)]
    out = pl.pallas_call(
        functools.partial(_diff_attn_decode_kernel, lam_init=lam_init, n_heads=n_heads, n_slots=n_slots),
        grid_spec=pltpu.PrefetchScalarGridSpec(
            num_scalar_prefetch=1,
            grid=(b, n_pages // n_slots),
            in_specs=[vec, vec, vec] + cache + cache + [
                pl.BlockSpec(lam_p.shape, lambda bi, p, pt: (0, 0)),
                pl.BlockSpec(subln.shape, lambda bi, p, pt: (0, 0))],
            out_specs=pl.BlockSpec((1, n_heads, dv), lambda bi, p, pt: (bi, 0, 0)),
            scratch_shapes=[pltpu.VMEM((2 * n_heads, 1), F32), pltpu.VMEM((2 * n_heads, 1), F32),
                            pltpu.VMEM((2 * n_heads, dv), F32)],
        ),
        out_shape=jax.ShapeDtypeStruct((b, n_heads, dv), F32),
        compiler_params=_cparams(("parallel", "arbitrary")),
        name="diff_attn_decode",
    )(page_table.reshape(-1), per_map(q), per_map(k_new), per_map(v_new),
      *([cache_k] * n_slots), *([cache_v] * n_slots), lam_p, subln)
    return out.reshape(b, width)


def _lru_coeffs(x, x1, x2, x3, cw, cb, wa, ba, wi, bi, lam):
    xc = x3 * cw[0:1] + x2 * cw[1:2] + x1 * cw[2:3] + x * cw[3:4]
    xc = xc + cb
    r = jax.nn.sigmoid(_dot(xc, wa) + ba)
    ig = jax.nn.sigmoid(_dot(xc, wi) + bi)
    log_a = -LRU_C * r * _softplus(-lam)
    a = jnp.exp(log_a)
    return a, jnp.sqrt(-jnp.tanh(log_a) * (a * a + 1.0)) * (ig * xc)


def _lru_seq_kernel(x_ref, gr_ref, buf_ref, h0_ref, cw_ref, cb_ref, wa_ref, ba_ref, wi_ref, bi_ref, lam_ref,
                    o_ref, hl_ref, carry, h_scr, a_scr, b_scr):
    tl = x_ref.shape[1]
    n_carry = carry.shape[0]

    @pl.when(pl.program_id(1) == 0)
    def _():
        for d in range(n_carry):
            carry[d] = buf_ref[0, d:d + 1, :]
        h_scr[...] = h0_ref[0]

    x = x_ref[0]
    x1 = _shifted(x, carry[n_carry - 1])
    x2 = _shifted(x1, carry[n_carry - 2])
    x3 = _shifted(x2, carry[n_carry - 3])
    for d in range(n_carry):
        carry[d] = x[tl - n_carry + d:tl - n_carry + d + 1, :]
    a, b = _lru_coeffs(x, x1, x2, x3, cw_ref[...], cb_ref[...], wa_ref[...], ba_ref[...], wi_ref[...],
                       bi_ref[...], lam_ref[...])
    a_scr[...] = a
    b_scr[...] = b
    row_id = lax.broadcasted_iota(jnp.int32, (SUBLANES, x.shape[1]), 0)

    def block(tb, h):
        base = pl.multiple_of(tb * SUBLANES, SUBLANES)
        a8, b8 = a_scr[pl.ds(base, SUBLANES), :], b_scr[pl.ds(base, SUBLANES), :]
        hs = jnp.zeros_like(a8)
        for i in range(SUBLANES):
            h = a8[i:i + 1, :] * h + b8[i:i + 1, :]
            hs = jnp.where(row_id == i, h, hs)
        o_ref[0, pl.ds(base, SUBLANES), :] = hs * jax.nn.gelu(gr_ref[0, pl.ds(base, SUBLANES), :])
        return h

    h = lax.fori_loop(0, tl // SUBLANES, block, h_scr[...])
    h_scr[...] = h
    hl_ref[0] = h


def lru_prompt(x, gr, buf, h0, prm):
    bsz, l, w = x.shape
    tl = _tile(l, 512)
    seq = pl.BlockSpec((1, tl, w), lambda i, j: (i, j, 0))
    vec = pl.BlockSpec((1, 1, w), lambda i, j: (i, 0, 0))
    full = lambda a: pl.BlockSpec(a.shape, lambda i, j: (0,) * a.ndim)
    params = [prm[n] for n in ("conv_w", "conv_b", "wa", "ba", "wi", "bi", "lam")]
    nb = buf.shape[1]
    o, hl = pl.pallas_call(
        _lru_seq_kernel,
        grid=(bsz, l // tl),
        in_specs=[seq, seq, pl.BlockSpec((1, nb, w), lambda i, j: (i, 0, 0)), vec] + [full(a) for a in params],
        out_specs=[seq, vec],
        out_shape=[jax.ShapeDtypeStruct((bsz, l, w), F32), jax.ShapeDtypeStruct((bsz, 1, w), F32)],
        scratch_shapes=[pltpu.VMEM((nb, 1, w), F32), pltpu.VMEM((1, w), F32), pltpu.VMEM((tl, w), F32),
                        pltpu.VMEM((tl, w), F32)],
        compiler_params=_cparams(("parallel", "arbitrary")),
        name="lru_prompt",
    )(x, gr, buf, h0.reshape(bsz, 1, w), *params)
    return o, hl.reshape(bsz, w)


def _lru_step_kernel(x_ref, x1_ref, x2_ref, x3_ref, gr_ref, h0_ref, cw_ref, cb_ref, wa_ref, ba_ref, wi_ref,
                     bi_ref, lam_ref, o_ref, h_ref):
    a, b = _lru_coeffs(x_ref[...], x1_ref[...], x2_ref[...], x3_ref[...], cw_ref[...], cb_ref[...], wa_ref[...],
                       ba_ref[...], wi_ref[...], bi_ref[...], lam_ref[...])
    h = a * h0_ref[...] + b
    h_ref[...] = h
    o_ref[...] = h * jax.nn.gelu(gr_ref[...])


def lru_step(x, buf, gr, h0, prm):
    m, w = x.shape
    tm = _tile(m, 512)
    rows = pl.BlockSpec((tm, w), lambda i: (i, 0))
    full = lambda a: pl.BlockSpec(a.shape, lambda i: (0,) * a.ndim)
    params = [prm[n] for n in ("conv_w", "conv_b", "wa", "ba", "wi", "bi", "lam")]
    nb = buf.shape[1]
    return pl.pallas_call(
        _lru_step_kernel,
        grid=(m // tm,),
        in_specs=[rows] * 6 + [full(a) for a in params],
        out_specs=[rows] * 2,
        out_shape=[jax.ShapeDtypeStruct((m, w), F32)] * 2,
        compiler_params=_cparams(("parallel",)),
        name="lru_step",
    )(x, buf[:, nb - 1], buf[:, nb - 2], buf[:, nb - 3], gr, h0, *params)


def _block_diag(w):
    n, d, e = w.shape
    eye = jnp.eye(n, dtype=w.dtype)
    return (eye[:, None, :, None] * w[:, :, None, :]).reshape(n * d, n * e)


def _mix_even(h, g_norm, pos0, s_ret, s_rwkv, buf, wts, is_prompt):
    b, l, d = h.shape
    m = b * l
    gw = d // 2
    u = norm_matmul(h.reshape(m, d), g_norm, wts["ab_w_in"], gw)
    seq = lambda t: t.reshape(b, l, gw)
    flat = lambda t: t.reshape(m, gw)
    qa, ka, va, ga = u[:4]
    cur = [seq(t) for t in u[4:8]]
    buf_new = jnp.concatenate([t[:, l - 1:] for t in cur], axis=-1)
    if is_prompt:
        r, w, k, v, kk, kka, g, bonus, v_hi, v_lo = rwkv_prep(cur, None, buf, wts["rwkv"])
        pos = pos0 + jnp.arange(l, dtype=F32)
        o_a, s_ret_new = retention_prompt(seq(qa), seq(ka), seq(va), seq(ga), pos)
        o_a = flat(o_a)
        o_b, s_rwkv_new = rwkv_scan_prompt(r, w, k, kk, kka, v_hi, v_lo)
    else:
        rows = lambda t: t.reshape(1, m, gw)
        prev = [rows(buf[..., i * gw:(i + 1) * gw]) for i in range(4)]
        r, w, k, v, kk, kka, g, bonus, _, _ = rwkv_prep([rows(t) for t in cur], prev, None, wts["rwkv"])
        o_a, s_ret_new = retention_step(qa, ka, va, ga, s_ret, jnp.float32(pos0))
        o_b, s_rwkv_new = rwkv_step(*(flat(t) for t in (r, w, k, kk, kka, v)), s_rwkv)
    o_b = rwkv_post(flat(o_b), flat(bonus), flat(g), wts["rwkv_ln"], wts["rwkv"]["ones"])
    return o_a, o_b, s_ret_new, s_rwkv_new, buf_new


def _mix_odd(h, g_norm, lru_h, lru_buf, pages, wts, layer, is_prompt):
    b, l, d = h.shape
    m = b * l
    gw = d // 2
    u = norm_matmul(h.reshape(m, d), g_norm, wts["cd_w_in"], gw)
    seq = lambda t: t.reshape(b, l, gw)
    lam_init = 0.8 - 0.6 * math.exp(-0.3 * layer)
    xr = seq(u[3])
    if is_prompt:
        o_c = diff_attn_prompt(seq(u[0]), seq(u[1]), seq(u[2]), wts["diff_lam"], wts["diff_subln"], lam_init)
        o_c = o_c.reshape(m, gw)
        o_d, h_last = lru_prompt(xr, seq(u[4]), lru_buf, lru_h, wts["lru"])
    else:
        cache_k, cache_v, page_table = pages
        o_c = diff_attn_decode(u[0], u[1], u[2], cache_k, cache_v, page_table, wts["diff_lam"],
                               wts["diff_subln"], lam_init)
        o_d, h_last = lru_step(u[3], lru_buf, u[4], lru_h, wts["lru"])
    buf_new = jnp.concatenate([lru_buf, xr], axis=1)[:, l:]
    n_heads = gw // (2 * HEAD_DIM)
    k_new = u[1].reshape(b, l, n_heads, 2 * HEAD_DIM)
    v_new = u[2].reshape(b, l, n_heads, 2 * HEAD_DIM)
    return o_c, o_d.reshape(m, gw), k_new, v_new, h_last, buf_new


def _trunk(x, p, pos0, s_ret, s_rwkv, s_shift, s_lru_h, s_lru_conv, pages, wts, is_prompt):
    b, l, d = x.shape
    m = b * l
    depth = wts["norm_g"].shape[0]
    h = x.reshape(m, d)
    ret_l, rwkv_l, shift_l, k_l, v_l, lh_l, lc_l = [], [], [], [], [], [], []
    for i in range(depth):
        j = i // 2
        g = wts["norm_g"][i]
        gn = lambda n: g[n:n + 1]
        h = ffn_block(h, gn(0), wts["ffn_in"][i][0], wts["ffn_out"][i][0], gn(1))
        if i % 2 == 0:
            o1, o2, sr, sw, sb = _mix_even(h.reshape(b, l, d), gn(2), pos0, s_ret[j], s_rwkv[j], s_shift[j],
                                           wts["even"][j], is_prompt)
            ret_l.append(sr)
            rwkv_l.append(sw)
            shift_l.append(sb)
            w_out = wts["even"][j]["w_out"]
        else:
            pg = None if pages is None else (pages[0][j], pages[1][j], pages[2])
            o1, o2, kn, vn, lh, lc = _mix_odd(h.reshape(b, l, d), gn(2), s_lru_h[j], s_lru_conv[j], pg,
                                              wts["odd"][j], i, is_prompt)
            k_l.append(kn)
            v_l.append(vn)
            lh_l.append(lh)
            lc_l.append(lc)
            w_out = wts["odd"][j]["w_out"]
        h = out_proj(o1, o2, w_out, h, gn(3))
        h = ffn_block(h, gn(4), wts["ffn_in"][i][1], wts["ffn_out"][i][1], gn(5))
        h = ple_block(h, gn(6), wts["ple_gate"][i], p[i].reshape(m, -1), wts["ple"][i], gn(7))
    st = lambda lst: jnp.stack(lst, axis=0)
    return (h.reshape(b, l, d), st(k_l), st(v_l), st(ret_l), st(rwkv_l), st(shift_l), st(lh_l), st(lc_l))


def kernel(x_prompt, x_sample, cache_k, cache_v, state_ret, state_rwkv, state_rwkv_shift, state_lru_h, state_lru_conv, page_table, p_prompt, p_sample, norm_g, ffn_w_in, ffn_w_out, ple_w, ple_gate_w, ab_w_in, ab_w_out, rwkv_mu, rwkv_w0, rwkv_w1, rwkv_w2, rwkv_a0, rwkv_a1, rwkv_a2, rwkv_g1, rwkv_g2, rwkv_kk, rwkv_ka, rwkv_rk, rwkv_ln, cd_w_in, cd_w_out, diff_lam, diff_subln, lru_conv_w, lru_conv_b, lru_wa, lru_ba, lru_wi, lru_bi, lru_lambda):
    depth = norm_g.shape[0]
    n_a, n_c = state_ret.shape[0], state_lru_h.shape[0]
    bp = x_prompt.shape[0]
    gw = ab_w_out.shape[1] // 2
    bf = lambda t: t.astype(BF16)
    row = lambda t: t.reshape(1, -1)
    ones_bd = _block_diag(jnp.ones((gw // HEAD_DIM, HEAD_DIM, HEAD_DIM), BF16))
    wts = {
        "norm_g": norm_g,
        "ffn_in": [[bf(ffn_w_in[i, s]) for s in range(2)] for i in range(depth)],
        "ffn_out": [[bf(ffn_w_out[i, s]) for s in range(2)] for i in range(depth)],
        "ple": [bf(ple_w[i]) for i in range(depth)],
        "ple_gate": [bf(ple_gate_w[i]) for i in range(depth)],
        "even": [{
            "ab_w_in": bf(ab_w_in[j]), "w_out": bf(ab_w_out[j]), "rwkv_ln": rwkv_ln[j],
            "rwkv": {"mu": rwkv_mu[j], "w0": row(rwkv_w0[j]), "w1": bf(rwkv_w1[j]), "w2": bf(rwkv_w2[j]),
                     "a0": row(rwkv_a0[j]), "a1": bf(rwkv_a1[j]), "a2": bf(rwkv_a2[j]),
                     "g1": bf(rwkv_g1[j]), "g2": bf(rwkv_g2[j]), "kk": row(rwkv_kk[j]),
                     "ka": row(rwkv_ka[j]), "rk": row(rwkv_rk[j]), "ones": ones_bd},
        } for j in range(n_a)],
        "odd": [{
            "cd_w_in": bf(cd_w_in[j]), "w_out": bf(cd_w_out[j]), "diff_lam": diff_lam[j],
            "diff_subln": row(diff_subln[j]),
            "lru": {"conv_w": lru_conv_w[j], "conv_b": row(lru_conv_b[j]), "wa": bf(_block_diag(lru_wa[j])),
                    "ba": row(lru_ba[j]), "wi": bf(_block_diag(lru_wi[j])), "bi": row(lru_bi[j]),
                    "lam": row(lru_lambda[j])},
        } for j in range(n_c)],
    }
    zeros = lambda *shape: jnp.zeros(shape, F32)
    yp, kp, vp, rp, wp, sp, hp, cp = _trunk(
        x_prompt, p_prompt, 0.0, [None] * n_a, [None] * n_a,
        zeros(n_a, bp, 1, 4 * gw), zeros(n_c, bp, gw), zeros(n_c, bp, CONV_W - 1, gw),
        None, wts, True)
    past_len = page_table.shape[1] * cache_k.shape[2]
    n_pool, page = cache_k.shape[1], cache_k.shape[2]
    as_rows = lambda c: c.reshape(n_c, n_pool, page * c.shape[3], c.shape[4])
    pages = (as_rows(cache_k), as_rows(cache_v), page_table)
    ys, ks_, vs, rs, ws, ss, hs, cs = _trunk(
        x_sample, p_sample, float(past_len), state_ret, state_rwkv, state_rwkv_shift,
        state_lru_h, state_lru_conv, pages, wts, False)
    return (yp, ys, kp, vp, rp, wp, sp, hp, cp, ks_, vs, rs, ws, ss, hs, cs)
```

```python
import functools
import math

import jax
import jax.numpy as jnp
import numpy as np
from jax import lax
from jax.experimental import pallas as pl
from jax.experimental.pallas import tpu as pltpu

F32 = jnp.float32
BF16 = jnp.bfloat16

HEAD_DIM = 64
CONV_W = 4
LRU_C = 8.0
ROPE_BASE = 10000.0
EPS = 1e-6
RWKV_GN_EPS = 64e-5
RET_CHUNK = 256
ATTN_BLOCK = 512
SCAN_CHUNK = 128
DECODE_PAGES = 8
SCAN_GROUP = 4
LANES = 128
SUBLANES = 8
VMEM_LIMIT = 48 * 1024 * 1024


def _cparams(sem):
    return pltpu.CompilerParams(dimension_semantics=sem, vmem_limit_bytes=VMEM_LIMIT)


def _tile(n, pref):
    t = min(n, pref)
    while n % t:
        t //= 2
    return t


def _rms(x, g):
    return x * lax.rsqrt(jnp.mean(x * x, axis=-1, keepdims=True) + EPS) * g


def _dot(a, b):
    return jnp.dot(a.astype(BF16), b.astype(BF16), preferred_element_type=F32)


def _seg_sum(x, ones_bd):
    hi = x.astype(BF16)
    lo = (x - hi.astype(F32)).astype(BF16)
    return (jnp.dot(hi, ones_bd, preferred_element_type=F32)
            + jnp.dot(lo, ones_bd, preferred_element_type=F32))


def _ffn_kernel(h_ref, gpre_ref, wg_ref, wu_ref, wo_ref, gpost_ref, o_ref, xn_ref, acc_ref):
    j = pl.program_id(1)

    @pl.when(j == 0)
    def _():
        xn_ref[...] = _rms(h_ref[...], gpre_ref[...]).astype(BF16)
        acc_ref[...] = jnp.zeros_like(acc_ref)

    xn = xn_ref[...]
    gate = jnp.dot(xn, wg_ref[...], preferred_element_type=F32)
    up = jnp.dot(xn, wu_ref[...], preferred_element_type=F32)
    act = (gate * jax.nn.sigmoid(gate) * up).astype(BF16)
    acc_ref[...] += jnp.dot(act, wo_ref[...], preferred_element_type=F32)

    @pl.when(j == pl.num_programs(1) - 1)
    def _():
        o_ref[...] = h_ref[...] + 0.5 * _rms(acc_ref[...], gpost_ref[...])


def ffn_block(h, g_pre, w_in, w_out, g_post):
    m, d = h.shape
    f = w_out.shape[0]
    tm, tf = _tile(m, 1024), _tile(f, 512)
    nf = f // tf
    return pl.pallas_call(
        _ffn_kernel,
        grid=(m // tm, nf),
        in_specs=[
            pl.BlockSpec((tm, d), lambda i, j: (i, 0)),
            pl.BlockSpec((1, d), lambda i, j: (0, 0)),
            pl.BlockSpec((d, tf), lambda i, j: (0, j)),
            pl.BlockSpec((d, tf), lambda i, j: (0, j + nf)),
            pl.BlockSpec((tf, d), lambda i, j: (j, 0)),
            pl.BlockSpec((1, d), lambda i, j: (0, 0)),
        ],
        out_specs=pl.BlockSpec((tm, d), lambda i, j: (i, 0)),
        out_shape=jax.ShapeDtypeStruct((m, d), F32),
        scratch_shapes=[pltpu.VMEM((tm, d), BF16), pltpu.VMEM((tm, d), F32)],
        compiler_params=_cparams(("parallel", "arbitrary")),
        name="ffn_block",
    )(h, g_pre, w_in, w_in, w_out, g_post)


def _norm_matmul_kernel(h_ref, g_ref, w_ref, *o_refs):
    xn = _rms(h_ref[...], g_ref[...]).astype(BF16)
    tn = o_refs[0].shape[1]
    for gi, o_ref in enumerate(o_refs):
        o_ref[...] = jnp.dot(xn, w_ref[:, gi * tn:(gi + 1) * tn], preferred_element_type=F32)


def norm_matmul(h, g, w, tn):
    m, d = h.shape
    n = w.shape[1]
    tm = _tile(m, 512)
    rows = pl.BlockSpec((tm, tn), lambda i: (i, 0))
    return pl.pallas_call(
        _norm_matmul_kernel,
        grid=(m // tm,),
        in_specs=[
            pl.BlockSpec((tm, d), lambda i: (i, 0)),
            pl.BlockSpec((1, d), lambda i: (0, 0)),
            pl.BlockSpec((d, n), lambda i: (0, 0)),
        ],
        out_specs=[rows] * (n // tn),
        out_shape=[jax.ShapeDtypeStruct((m, tn), F32)] * (n // tn),
        compiler_params=_cparams(("parallel",)),
        name="norm_matmul",
    )(h, g, w)


def _out_proj_kernel(oa_ref, ob_ref, wa_ref, wb_ref, h_ref, g_ref, o_ref):
    y = _dot(oa_ref[...], wa_ref[...]) + _dot(ob_ref[...], wb_ref[...])
    o_ref[...] = h_ref[...] + _rms(y, g_ref[...])


def out_proj(oa, ob, w, h, g):
    m, d = h.shape
    gw = oa.shape[1]
    tm = _tile(m, 512)
    return pl.pallas_call(
        _out_proj_kernel,
        grid=(m // tm,),
        in_specs=[
            pl.BlockSpec((tm, gw), lambda i: (i, 0)),
            pl.BlockSpec((tm, gw), lambda i: (i, 0)),
            pl.BlockSpec((gw, d), lambda i: (0, 0)),
            pl.BlockSpec((gw, d), lambda i: (1, 0)),
            pl.BlockSpec((tm, d), lambda i: (i, 0)),
            pl.BlockSpec((1, d), lambda i: (0, 0)),
        ],
        out_specs=pl.BlockSpec((tm, d), lambda i: (i, 0)),
        out_shape=jax.ShapeDtypeStruct((m, d), F32),
        compiler_params=_cparams(("parallel",)),
        name="out_proj",
    )(oa, ob, w, w, h, g)


def _ple_kernel(h_ref, g6_ref, wg_ref, p_ref, wp_ref, g7_ref, o_ref):
    h = h_ref[...]
    gate = jax.nn.sigmoid(_dot(_rms(h, g6_ref[...]), wg_ref[...]))
    y = gate * _dot(p_ref[...], wp_ref[...])
    o_ref[...] = h + _rms(y, g7_ref[...])


def ple_block(h, g6, wg, p, wp, g7):
    m, d = h.shape
    pd = p.shape[1]
    tm = _tile(m, 512)
    return pl.pallas_call(
        _ple_kernel,
        grid=(m // tm,),
        in_specs=[
            pl.BlockSpec((tm, d), lambda i: (i, 0)),
            pl.BlockSpec((1, d), lambda i: (0, 0)),
            pl.BlockSpec((d, d), lambda i: (0, 0)),
            pl.BlockSpec((tm, pd), lambda i: (i, 0)),
            pl.BlockSpec((pd, d), lambda i: (0, 0)),
            pl.BlockSpec((1, d), lambda i: (0, 0)),
        ],
        out_specs=pl.BlockSpec((tm, d), lambda i: (i, 0)),
        out_shape=jax.ShapeDtypeStruct((m, d), F32),
        compiler_params=_cparams(("parallel",)),
        name="ple_block",
    )(h, g6, wg, p, wp, g7)


def _retention_tables(n_heads, c, pos):
    lg = jnp.log1p(-jnp.exp2(-5.0 - jnp.arange(n_heads, dtype=F32)))
    idx = jnp.arange(c, dtype=F32)
    rel = idx[:, None] - idx[None, :]
    dmask = jnp.where(rel[None] >= 0, jnp.exp(jnp.maximum(rel, 0.0)[None] * lg[:, None, None]), 0.0)
    rep = lambda t: jnp.repeat(t, HEAD_DIM, axis=-1)
    q_dec = rep(jnp.exp((idx[:, None] + 1.0) * lg[None, :]))
    k_dec = rep(jnp.exp((c - 1.0 - idx[:, None]) * lg[None, :]))
    c_dec = rep(jnp.exp(c * lg)[None, :])
    half = HEAD_DIM // 2
    freq = 1.0 / (ROPE_BASE ** jnp.linspace(0.0, 1.0, half, dtype=F32))
    ang = pos[:, None] * freq[None, :]
    cos, sin = jnp.cos(ang), jnp.sin(ang)
    cos_t = jnp.tile(jnp.concatenate([cos, cos], axis=-1), (1, n_heads))
    sin_t = jnp.tile(jnp.concatenate([-sin, sin], axis=-1), (1, n_heads))
    return dmask, q_dec, k_dec, c_dec, cos_t, sin_t


def _retention_kernel(q_ref, k_ref, v_ref, g_ref, cos_ref, sin_ref, dmask_ref, qdec_ref, kdec_ref,
                      cdec_ref, o_ref, s_ref, s_scr, *, n_heads):
    c = pl.program_id(1)

    @pl.when(c == 0)
    def _():
        s_scr[...] = jnp.zeros_like(s_scr)

    q, k, v, g = q_ref[0], k_ref[0], v_ref[0], g_ref[0]
    cos, sin = cos_ref[...], sin_ref[...]
    width = q.shape[-1]
    lane = lax.broadcasted_iota(jnp.int32, q.shape, 1)
    first_half = (lane % HEAD_DIM) < (HEAD_DIM // 2)

    def rot(x):
        swapped = jnp.where(first_half, pltpu.roll(x, width - HEAD_DIM // 2, 1),
                            pltpu.roll(x, HEAD_DIM // 2, 1))
        return x * cos + swapped * sin

    qr = rot(q)
    kr = rot(k) * (HEAD_DIM ** -0.5)
    kd = kr * kdec_ref[...]
    qdec = qdec_ref[...]
    cdec = cdec_ref[...]
    gate = g * jax.nn.sigmoid(g)
    for h in range(n_heads):
        sl = slice(h * HEAD_DIM, (h + 1) * HEAD_DIM)
        qh = qr[:, sl].astype(BF16)
        kh = kr[:, sl].astype(BF16)
        vh = v[:, sl].astype(BF16)
        att = lax.dot_general(qh, kh, (((1,), (1,)), ((), ())), preferred_element_type=F32) * dmask_ref[h]
        s_old = s_scr[h]
        o = (jnp.dot(att.astype(BF16), vh, preferred_element_type=F32)
             + jnp.dot(qh, s_old.astype(BF16), preferred_element_type=F32) * qdec[:, sl])
        s_scr[h] = s_old * cdec[:, sl] + lax.dot_general(
            kd[:, sl].astype(BF16), vh, (((0,), (0,)), ((), ())), preferred_element_type=F32)
        oc = o - jnp.mean(o, axis=-1, keepdims=True)
        on = oc * lax.rsqrt(jnp.mean(oc * oc, axis=-1, keepdims=True) + EPS)
        o_ref[0, :, sl] = on * gate[:, sl]

    @pl.when(c == pl.num_programs(1) - 1)
    def _():
        s_ref[0] = s_scr[...]


def retention_prompt(q, k, v, g, pos):
    b, l, width = q.shape
    n_heads = width // HEAD_DIM
    c = _tile(l, RET_CHUNK)
    dmask, q_dec, k_dec, c_dec, cos_t, sin_t = _retention_tables(n_heads, c, pos)
    seq = pl.BlockSpec((1, c, width), lambda i, j: (i, j, 0))
    tab = pl.BlockSpec((c, width), lambda i, j: (j, 0))
    fixed = lambda shape: pl.BlockSpec(shape, lambda i, j: (0,) * len(shape))
    return pl.pallas_call(
        functools.partial(_retention_kernel, n_heads=n_heads),
        grid=(b, l // c),
        in_specs=[seq, seq, seq, seq, tab, tab, fixed((n_heads, c, c)), fixed((c, width)),
                  fixed((c, width)), fixed((1, width))],
        out_specs=[seq, pl.BlockSpec((1, n_heads, HEAD_DIM, HEAD_DIM), lambda i, j: (i, 0, 0, 0))],
        out_shape=[jax.ShapeDtypeStruct((b, l, width), F32),
                   jax.ShapeDtypeStruct((b, n_heads, HEAD_DIM, HEAD_DIM), F32)],
        scratch_shapes=[pltpu.VMEM((n_heads, HEAD_DIM, HEAD_DIM), F32)],
        compiler_params=_cparams(("parallel", "arbitrary")),
        name="retention_prompt",
    )(q, k, v, g, cos_t, sin_t, dmask, q_dec, k_dec, c_dec)


def _retention_step_kernel(q_ref, k_ref, v_ref, g_ref, cos_ref, sin_ref, gam_ref, s0_ref, o_ref, s_ref):
    cos, sin = cos_ref[...], sin_ref[...]

    def rot(x):
        half = HEAD_DIM // 2
        return x * cos + jnp.concatenate([x[:, :, half:], x[:, :, :half]], axis=2) * sin

    q = rot(q_ref[...])
    k = rot(k_ref[...]) * (HEAD_DIM ** -0.5)
    v, g = v_ref[...], g_ref[...]
    gam = gam_ref[...]
    s0 = s0_ref[...]
    att = jnp.sum(q * k, axis=2, keepdims=True)
    o = att * v + jnp.sum(q * s0, axis=2, keepdims=True) * gam
    s_ref[...] = s0 * gam + k * v
    oc = o - jnp.mean(o, axis=-1, keepdims=True)
    on = oc * lax.rsqrt(jnp.mean(oc * oc, axis=-1, keepdims=True) + EPS)
    o_ref[...] = on * (g * jax.nn.sigmoid(g))


def retention_step(q, k, v, g, s0, pos):
    b, width = q.shape
    n_heads = width // HEAD_DIM
    bb = _tile(b, 8)
    half = HEAD_DIM // 2
    freq = 1.0 / (ROPE_BASE ** jnp.linspace(0.0, 1.0, half, dtype=F32))
    ang = pos * freq
    cos_c = jnp.concatenate([jnp.cos(ang), jnp.cos(ang)])[:, None]
    sin_c = jnp.concatenate([-jnp.sin(ang), jnp.sin(ang)])[:, None]
    gam = jnp.exp(jnp.log1p(-jnp.exp2(-5.0 - jnp.arange(n_heads, dtype=F32)))).reshape(n_heads, 1, 1)
    col = lambda t: t.reshape(b, n_heads, HEAD_DIM, 1)
    row = lambda t: t.reshape(b, n_heads, 1, HEAD_DIM)
    cspec = pl.BlockSpec((bb, n_heads, HEAD_DIM, 1), lambda i: (i, 0, 0, 0))
    rspec = pl.BlockSpec((bb, n_heads, 1, HEAD_DIM), lambda i: (i, 0, 0, 0))
    sspec = pl.BlockSpec((bb, n_heads, HEAD_DIM, HEAD_DIM), lambda i: (i, 0, 0, 0))
    o, s = pl.pallas_call(
        _retention_step_kernel,
        grid=(b // bb,),
        in_specs=[cspec, cspec, rspec, rspec,
                  pl.BlockSpec((HEAD_DIM, 1), lambda i: (0, 0)), pl.BlockSpec((HEAD_DIM, 1), lambda i: (0, 0)),
                  pl.BlockSpec((n_heads, 1, 1), lambda i: (0, 0, 0)), sspec],
        out_specs=[rspec, sspec],
        out_shape=[jax.ShapeDtypeStruct((b, n_heads, 1, HEAD_DIM), F32),
                   jax.ShapeDtypeStruct((b, n_heads, HEAD_DIM, HEAD_DIM), F32)],
        compiler_params=_cparams(("parallel",)),
        name="retention_step",
    )(col(q), col(k), row(v), row(g), cos_c, sin_c, gam, s0)
    return o.reshape(b, width), s


def _softplus(x):
    return jnp.maximum(x, 0.0) + jnp.log1p(jnp.exp(-jnp.abs(x)))


def _shifted(cur, carry_row):
    if cur.shape[0] == 1:
        return carry_row
    first = lax.broadcasted_iota(jnp.int32, cur.shape, 0) == 0
    return jnp.where(first, carry_row, pltpu.roll(cur, 1, 0))


PREP_PARAMS = ("mu", "w0", "w1", "w2", "a0", "a1", "a2", "g1", "g2", "kk", "ka", "rk", "ones_kk", "ones_rk")
N_PREP_PARAMS = len(PREP_PARAMS)


def _rwkv_prep_kernel(*refs, shift_in_kernel):
    cur_refs = refs[:4]
    if shift_in_kernel:
        buf_ref = refs[4]
        n_in = 5
    else:
        prev_refs = refs[4:8]
        n_in = 8
    (mu_ref, w0_ref, w1_ref, w2_ref, a0_ref, a1_ref, a2_ref, g1_ref, g2_ref, kkp_ref, kap_ref, rk_ref,
     ones_kk_ref, ones_rk_ref) = refs[n_in:n_in + N_PREP_PARAMS]
    n_in += N_PREP_PARAMS
    r_out, w_out, k_out, v_out, kk_out, kka_out, g_out, bonus_out, vhi_out, vlo_out = refs[n_in:n_in + 10]
    cur = [ref[0] for ref in cur_refs]
    width = cur[0].shape[1]
    if shift_in_kernel:
        carry = refs[n_in + 10]

        @pl.when(pl.program_id(1) == 0)
        def _():
            for gi in range(4):
                carry[gi] = buf_ref[0, :, gi * width:(gi + 1) * width]

        prev = [_shifted(x, carry[gi]) for gi, x in enumerate(cur)]
        for gi, x in enumerate(cur):
            carry[gi] = x[x.shape[0] - 1:, :]
    else:
        prev = [ref[0] for ref in prev_refs]
    mu = mu_ref[...]
    lerp = lambda x, xp, i: x + (xp - x) * mu[i:i + 1]
    zr, pz = cur[3], prev[3]
    r = lerp(cur[0], prev[0], 0)
    kx = lerp(cur[1], prev[1], 1)
    vx = lerp(cur[2], prev[2], 2)
    zw, za, zg = lerp(zr, pz, 3), lerp(zr, pz, 4), lerp(zr, pz, 5)
    wpre = w0_ref[...] + _dot(jnp.tanh(_dot(zw, w1_ref[...])), w2_ref[...])
    decay = jnp.exp(-jnp.exp(-_softplus(-wpre) - 0.5))
    a = jax.nn.sigmoid(a0_ref[...] + _dot(_dot(za, a1_ref[...]), a2_ref[...]))
    g = _dot(jax.nn.sigmoid(_dot(zg, g1_ref[...])), g2_ref[...])
    kk = kx * kkp_ref[...]
    kk = kk / jnp.maximum(jnp.sqrt(_seg_sum(kk * kk, ones_kk_ref[...])), 1e-12)
    k32 = kx * (1.0 + (a - 1.0) * kap_ref[...])
    r_out[0] = r
    w_out[0] = decay
    k_out[0] = k32
    v_out[0] = vx
    v_hi = vx.astype(BF16)
    vhi_out[0] = v_hi
    vlo_out[0] = (vx - v_hi.astype(F32)).astype(BF16)
    kk_out[0] = kk
    kka_out[0] = kk * a
    g_out[0] = g
    bonus_out[0] = _seg_sum(r * k32 * rk_ref[...], ones_rk_ref[...]) * vx


def rwkv_prep(cur, prev, buf, prm):
    b, l, w = cur[0].shape
    tl = _tile(l, 256)
    seq = pl.BlockSpec((1, tl, w), lambda i, j: (i, j, 0))
    full = lambda a: pl.BlockSpec(a.shape, lambda i, j: (0,) * a.ndim)
    params = [prm[n] for n in PREP_PARAMS]
    shift = prev is None
    if shift:
        extra, extra_specs = [buf], [pl.BlockSpec((1, 1, 4 * w), lambda i, j: (i, 0, 0))]
        scratch = [pltpu.VMEM((4, 1, w), F32)]
    else:
        extra, extra_specs, scratch = list(prev), [seq] * 4, []
    return pl.pallas_call(
        functools.partial(_rwkv_prep_kernel, shift_in_kernel=shift),
        grid=(b, l // tl),
        in_specs=[seq] * 4 + extra_specs + [full(a) for a in params],
        out_specs=[seq] * 10,
        out_shape=[jax.ShapeDtypeStruct((b, l, w), F32)] * 8 + [jax.ShapeDtypeStruct((b, l, w), BF16)] * 2,
        scratch_shapes=scratch,
        compiler_params=_cparams(("parallel", "arbitrary")),
        name="rwkv_prep",
    )(*cur, *extra, *params)


def _rwkv_post_kernel(o_ref, bonus_ref, g_ref, ln_ref, ones_ref, out_ref):
    ones_bd = ones_ref[...]
    o = o_ref[...]
    inv = 1.0 / HEAD_DIM
    oc = o - _seg_sum(o, ones_bd) * inv
    on = oc * lax.rsqrt(_seg_sum(oc * oc, ones_bd) * inv + RWKV_GN_EPS)
    ln = ln_ref[...]
    out_ref[...] = (on * ln[0:1] + ln[1:2] + bonus_ref[...]) * g_ref[...]


def rwkv_post(o, bonus, g, ln, ones_bd):
    m, w = o.shape
    tm = _tile(m, 512)
    rows = pl.BlockSpec((tm, w), lambda i: (i, 0))
    return pl.pallas_call(
        _rwkv_post_kernel,
        grid=(m // tm,),
        in_specs=[rows, rows, rows, pl.BlockSpec(ln.shape, lambda i: (0, 0)),
                  pl.BlockSpec(ones_bd.shape, lambda i: (0, 0))],
        out_specs=rows,
        out_shape=jax.ShapeDtypeStruct((m, w), F32),
        compiler_params=_cparams(("parallel",)),
        name="rwkv_post",
    )(o, bonus, g, ln, ones_bd)


def _hi_lo(x):
    hi = x.astype(BF16)
    lo = (x - hi.astype(F32)).astype(BF16)
    return jnp.concatenate([hi, lo], axis=1)


def _rwkv_scan_kernel(r_ref, w_ref, k_ref, kk_ref, kka_ref, vp_ref, sel_ref, ones_ref, o_ref, s_ref, s_scr,
                      ot_scr, *, n_heads):
    c = pl.program_id(1)
    n_grp, tc = r_ref.shape[0], r_ref.shape[1]
    head_lanes = LANES // n_heads
    n_kg = HEAD_DIM // head_lanes

    @pl.when(c == 0)
    def _():
        s_scr[...] = jnp.zeros_like(s_scr)

    ot_scr[...] = jnp.zeros_like(ot_scr)
    ones2 = ones_ref[...]
    t_lane = lax.broadcasted_iota(jnp.int32, (n_grp * HEAD_DIM, LANES), 1) % head_lanes
    rows_of = lambda x, g: x[g * HEAD_DIM:(g + 1) * HEAD_DIM]

    def block(tb, carry):
        base = pl.multiple_of(tb * SUBLANES, SUBLANES)
        tiles = {name: [[ref[g, pl.ds(base, SUBLANES), kg * LANES:(kg + 1) * LANES] for kg in range(n_kg)]
                        for g in range(n_grp)]
                 for name, ref in (("kk", kk_ref), ("w", w_ref), ("kka", kka_ref), ("k", k_ref), ("r", r_ref))}
        vp = jnp.concatenate([vp_ref[g, tb] for g in range(n_grp)], axis=0)
        s = [[s_scr[g, kg] for kg in range(n_kg)] for g in range(n_grp)]
        acc = jnp.zeros((n_grp * HEAD_DIM, LANES), F32)
        for i in range(SUBLANES):
            row = lambda name, g, kg: jnp.broadcast_to(tiles[name][g][kg][i:i + 1, :], (HEAD_DIM, LANES))
            key_sum = lambda name: jnp.dot(_hi_lo(jnp.concatenate(
                [sum(s[g][kg] * row(name, g, kg) for kg in range(n_kg)) for g in range(n_grp)], axis=0)),
                ones2, preferred_element_type=F32)
            sa = key_sum("kk")
            vcol = jnp.dot(vp, sel_ref[i], preferred_element_type=F32)
            for g in range(n_grp):
                sa_g, vcol_g = rows_of(sa, g), rows_of(vcol, g)
                for kg in range(n_kg):
                    s[g][kg] = (s[g][kg] * row("w", g, kg) - sa_g * row("kka", g, kg)
                                + vcol_g * row("k", g, kg))
            acc = jnp.where(t_lane == (base + i) % head_lanes, key_sum("r"), acc)
        for g in range(n_grp):
            for kg in range(n_kg):
                s_scr[g, kg] = s[g][kg]
        ot_scr[base // head_lanes] += acc
        return carry

    lax.fori_loop(0, tc // SUBLANES, block, 0)

    for tile in range(tc // head_lanes):
        for g in range(n_grp):
            o_t = rows_of(ot_scr[tile], g).T
            for h in range(n_heads):
                o_ref[g, tile * head_lanes:(tile + 1) * head_lanes, h * HEAD_DIM:(h + 1) * HEAD_DIM] = (
                    o_t[h * head_lanes:(h + 1) * head_lanes, :])

    @pl.when(c == pl.num_programs(1) - 1)
    def _():
        s_ref[...] = s_scr[...]


def _key_group_perm(width):
    n_heads = width // HEAD_DIM
    head_lanes = LANES // n_heads
    n = np.arange(width)
    return (n % LANES) // head_lanes * HEAD_DIM + n // LANES * head_lanes + n % head_lanes


def rwkv_scan_prompt(r, w, k, kk, kka, v_hi, v_lo):
    b, l, width = r.shape
    n_heads = width // HEAD_DIM
    head_lanes = LANES // n_heads
    n_kg = HEAD_DIM // head_lanes
    tc = _tile(l, SCAN_CHUNK)
    grp = _tile(b, SCAN_GROUP)
    vp = jnp.stack([v_hi, v_lo], axis=2).reshape(b, l // SUBLANES, SUBLANES, 2, n_heads, HEAD_DIM)
    vp = vp.transpose(0, 1, 5, 4, 3, 2).reshape(b, l // SUBLANES, HEAD_DIM, LANES)
    kl = np.arange(LANES)
    sel = ((kl[None, :, None] % SUBLANES == np.arange(SUBLANES)[:, None, None])
           & (kl[None, :, None] // head_lanes == kl[None, None, :] // head_lanes))
    k2 = np.arange(2 * LANES)
    ones2 = (k2[:, None] % LANES) // head_lanes == kl[None, :] // head_lanes
    sel, ones2 = jnp.asarray(sel, BF16), jnp.asarray(ones2, BF16)
    seq = pl.BlockSpec((grp, tc, width), lambda i, j: (i, j, 0))
    state = pl.BlockSpec((grp, n_kg, HEAD_DIM, LANES), lambda i, j: (i, 0, 0, 0))
    o, s = pl.pallas_call(
        functools.partial(_rwkv_scan_kernel, n_heads=n_heads),
        grid=(b // grp, l // tc),
        in_specs=[seq, seq, seq, seq, seq,
                  pl.BlockSpec((grp, tc // SUBLANES, HEAD_DIM, LANES), lambda i, j: (i, j, 0, 0)),
                  pl.BlockSpec(sel.shape, lambda i, j: (0, 0, 0)),
                  pl.BlockSpec(ones2.shape, lambda i, j: (0, 0))],
        out_specs=[seq, state],
        out_shape=[jax.ShapeDtypeStruct((b, l, width), F32),
                   jax.ShapeDtypeStruct((b, n_kg, HEAD_DIM, LANES), F32)],
        scratch_shapes=[pltpu.VMEM((grp, n_kg, HEAD_DIM, LANES), F32),
                        pltpu.VMEM((tc // head_lanes, grp * HEAD_DIM, LANES), F32)],
        compiler_params=_cparams(("parallel", "arbitrary")),
        name="rwkv_scan_prompt",
    )(r, w, k, kk, kka, vp, sel, ones2)
    s = s.reshape(b, n_kg, HEAD_DIM, n_heads, head_lanes).transpose(0, 3, 2, 1, 4)
    return o, s.reshape(b, n_heads, HEAD_DIM, HEAD_DIM)


def _rwkv_step_kernel(r_ref, w_ref, k_ref, kk_ref, kka_ref, v_ref, s0_ref, o_ref, s_ref):
    s0 = s0_ref[...]
    sa = -jnp.sum(s0 * kk_ref[...], axis=-1, keepdims=True)
    s = s0 * w_ref[...] + sa * kka_ref[...] + v_ref[...] * k_ref[...]
    s_ref[...] = s
    o_ref[...] = jnp.sum(s * r_ref[...], axis=-1, keepdims=True)


def rwkv_step(r, w, k, kk, kka, v, s0):
    b, width = r.shape
    n_heads = width // HEAD_DIM
    bb = _tile(b, 8)
    row = lambda t: t.reshape(b, n_heads, 1, HEAD_DIM)
    cspec = pl.BlockSpec((bb, n_heads, HEAD_DIM, 1), lambda i: (i, 0, 0, 0))
    rspec = pl.BlockSpec((bb, n_heads, 1, HEAD_DIM), lambda i: (i, 0, 0, 0))
    sspec = pl.BlockSpec((bb, n_heads, HEAD_DIM, HEAD_DIM), lambda i: (i, 0, 0, 0))
    o, s = pl.pallas_call(
        _rwkv_step_kernel,
        grid=(b // bb,),
        in_specs=[rspec] * 5 + [cspec, sspec],
        out_specs=[cspec, sspec],
        out_shape=[jax.ShapeDtypeStruct((b, n_heads, HEAD_DIM, 1), F32),
                   jax.ShapeDtypeStruct((b, n_heads, HEAD_DIM, HEAD_DIM), F32)],
        compiler_params=_cparams(("parallel",)),
        name="rwkv_step",
    )(row(r), row(w), row(k), row(kk), row(kka), v.reshape(b, n_heads, HEAD_DIM, 1), s0)
    return o.reshape(b, width), s


def _diff_lambda(lp, lam_init):
    e1 = jnp.exp(jnp.sum(lp[0:1] * lp[1:2], axis=-1, keepdims=True))
    e2 = jnp.exp(jnp.sum(lp[2:3] * lp[3:4], axis=-1, keepdims=True))
    return e1 - e2 + lam_init


def _diff_attn_kernel(q_ref, k_ref, v_ref, lam_ref, subln_ref, o_ref, m_scr, l_scr, acc_scr, *, lam_init):
    i, j = pl.program_id(2), pl.program_id(3)
    tq, tk = q_ref.shape[1], k_ref.shape[1]
    scale = HEAD_DIM ** -0.5

    @pl.when(j == 0)
    def _():
        m_scr[...] = jnp.full_like(m_scr, -jnp.inf)
        l_scr[...] = jnp.zeros_like(l_scr)
        acc_scr[...] = jnp.zeros_like(acc_scr)

    def update(on_diagonal):
        q = (q_ref[0] * scale).astype(BF16)
        k, v = k_ref[0].astype(BF16), v_ref[0].astype(BF16)
        if on_diagonal:
            visible = (lax.broadcasted_iota(jnp.int32, (tq, tk), 1)
                       <= lax.broadcasted_iota(jnp.int32, (tq, tk), 0))
        for mi in range(2):
            sl = slice(mi * HEAD_DIM, (mi + 1) * HEAD_DIM)
            s = lax.dot_general(q[:, sl], k[:, sl], (((1,), (1,)), ((), ())), preferred_element_type=F32)
            if on_diagonal:
                s = jnp.where(visible, s, -jnp.inf)
            m_old = m_scr[mi]
            m_new = jnp.maximum(m_old, jnp.max(s, axis=-1, keepdims=True))
            alpha = jnp.exp(m_old - m_new)
            p = jnp.exp(s - jnp.concatenate([m_new] * (tk // LANES), axis=1))
            l_scr[mi] = alpha * l_scr[mi] + jnp.sum(p, axis=-1, keepdims=True)
            acc_scr[mi] = alpha * acc_scr[mi] + jnp.dot(p.astype(BF16), v, preferred_element_type=F32)
            m_scr[mi] = m_new

    @pl.when(j < i)
    def _():
        update(False)

    @pl.when(j == i)
    def _():
        update(True)
        lam = _diff_lambda(lam_ref[...], lam_init)
        o = acc_scr[0] / l_scr[0] - lam * (acc_scr[1] / l_scr[1])
        o_ref[0] = _rms(o, subln_ref[...]) * (1.0 - lam_init)


def diff_attn_prompt(q, k, v, lam_p, subln, lam_init):
    b, l, width = q.shape
    dv = 2 * HEAD_DIM
    n_heads = width // dv
    t = _tile(l, ATTN_BLOCK)
    n = l // t
    qspec = pl.BlockSpec((1, t, dv), lambda bi, h, i, j: (bi, i, h))
    kspec = pl.BlockSpec((1, t, dv), lambda bi, h, i, j: (bi, jnp.minimum(i, j), h))
    return pl.pallas_call(
        functools.partial(_diff_attn_kernel, lam_init=lam_init),
        grid=(b, n_heads, n, n),
        in_specs=[qspec, kspec, kspec,
                  pl.BlockSpec(lam_p.shape, lambda bi, h, i, j: (0, 0)),
                  pl.BlockSpec(subln.shape, lambda bi, h, i, j: (0, 0))],
        out_specs=qspec,
        out_shape=jax.ShapeDtypeStruct((b, l, width), F32),
        scratch_shapes=[pltpu.VMEM((2, t, LANES), F32), pltpu.VMEM((2, t, LANES), F32), pltpu.VMEM((2, t, dv), F32)],
        compiler_params=_cparams(("parallel", "parallel", "parallel", "arbitrary")),
        name="diff_attn_prompt",
    )(q, k, v, lam_p, subln)


def _diff_attn_decode_kernel(pt_ref, q_ref, kn_ref, vn_ref, *rest, lam_init, n_heads, n_slots):
    kc_refs, vc_refs = rest[:n_slots], rest[n_slots:2 * n_slots]
    lam_ref, subln_ref, o_ref, m_scr, l_scr, acc_scr = rest[2 * n_slots:]
    p = pl.program_id(1)
    n_rows = 2 * n_heads
    dv = 2 * HEAD_DIM
    scale = HEAD_DIM ** -0.5
    row = lax.broadcasted_iota(jnp.int32, (n_rows, dv), 0)
    lane = lax.broadcasted_iota(jnp.int32, (n_rows, dv), 1)
    qmat = jnp.where(lane // HEAD_DIM == row % 2, q_ref[0], 0.0)
    cols = n_slots * kc_refs[0].shape[1]
    col_head = lax.broadcasted_iota(jnp.int32, (n_rows, cols), 1) % n_heads
    own = col_head == lax.broadcasted_iota(jnp.int32, (n_rows, cols), 0) // 2

    @pl.when(p == 0)
    def _():
        m_scr[...] = jnp.full_like(m_scr, -jnp.inf)
        l_scr[...] = jnp.zeros_like(l_scr)
        acc_scr[...] = jnp.zeros_like(acc_scr)

    s = jnp.concatenate(
        [lax.dot_general(qmat.astype(BF16), kc_ref[0].astype(BF16), (((1,), (1,)), ((), ())),
                         preferred_element_type=F32) for kc_ref in kc_refs], axis=1) * scale
    s = jnp.where(own, s, -jnp.inf)
    m_old = m_scr[...]
    m_new = jnp.maximum(m_old, jnp.max(s, axis=-1, keepdims=True))
    alpha = jnp.exp(m_old - m_new)
    pr = jnp.exp(s - m_new)
    l_scr[...] = alpha * l_scr[...] + jnp.sum(pr, axis=-1, keepdims=True)
    prb = pr.astype(BF16)
    rows_per = kc_refs[0].shape[1]
    pv = sum(jnp.dot(prb[:, i * rows_per:(i + 1) * rows_per], vc_ref[0].astype(BF16), preferred_element_type=F32)
             for i, vc_ref in enumerate(vc_refs))
    acc_scr[...] = alpha * acc_scr[...] + pv
    m_scr[...] = m_new

    @pl.when(p == pl.num_programs(1) - 1)
    def _():
        s_new = jnp.sum(qmat * kn_ref[0], axis=-1, keepdims=True) * scale
        m_old = m_scr[...]
        m_fin = jnp.maximum(m_old, s_new)
        alpha = jnp.exp(m_old - m_fin)
        p_new = jnp.exp(s_new - m_fin)
        l_fin = alpha * l_scr[...] + p_new
        acc = (alpha * acc_scr[...] + p_new * vn_ref[0]) / l_fin
        lam = _diff_lambda(lam_ref[...], lam_init)
        acc = acc * jnp.where(row % 2 == 0, 1.0, -lam)
        subln = subln_ref[...]
        for h in range(n_heads):
            o = acc[2 * h:2 * h + 1] + acc[2 * h + 1:2 * h + 2]
            o_ref[0, h:h + 1, :] = _rms(o, subln) * (1.0 - lam_init)


def diff_attn_decode(q, k_new, v_new, cache_k, cache_v, page_table, lam_p, subln, lam_init):
    b, width = q.shape
    dv = 2 * HEAD_DIM
    n_heads = width // dv
    n_pages = page_table.shape[1]
    rows = cache_k.shape[1]
    n_slots = _tile(n_pages, DECODE_PAGES)
    per_map = lambda t: jnp.repeat(t.reshape(b, n_heads, dv), 2, axis=1)
    vec = pl.BlockSpec((1, 2 * n_heads, dv), lambda bi, p, pt: (bi, 0, 0))
    cache = [pl.BlockSpec((1, rows, dv), functools.partial(
        lambda bi, p, pt, slot: (pt[bi * n_pages + p * n_slots + slot], 0, 0), slot=slot))
        for slot in range(n_slots)]
    out = pl.pallas_call(
        functools.partial(_diff_attn_decode_kernel, lam_init=lam_init, n_heads=n_heads, n_slots=n_slots),
        grid_spec=pltpu.PrefetchScalarGridSpec(
            num_scalar_prefetch=1,
            grid=(b, n_pages // n_slots),
            in_specs=[vec, vec, vec] + cache + cache + [
                pl.BlockSpec(lam_p.shape, lambda bi, p, pt: (0, 0)),
                pl.BlockSpec(subln.shape, lambda bi, p, pt: (0, 0))],
            out_specs=pl.BlockSpec((1, n_heads, dv), lambda bi, p, pt: (bi, 0, 0)),
            scratch_shapes=[pltpu.VMEM((2 * n_heads, 1), F32), pltpu.VMEM((2 * n_heads, 1), F32),
                            pltpu.VMEM((2 * n_heads, dv), F32)],
        ),
        out_shape=jax.ShapeDtypeStruct((b, n_heads, dv), F32),
        compiler_params=_cparams(("parallel", "arbitrary")),
        name="diff_attn_decode",
    )(page_table.reshape(-1), per_map(q), per_map(k_new), per_map(v_new),
      *([cache_k] * n_slots), *([cache_v] * n_slots), lam_p, subln)
    return out.reshape(b, width)


def _lru_coeffs(x, x1, x2, x3, cw, cb, wa, ba, wi, bi, lam):
    xc = x3 * cw[0:1] + x2 * cw[1:2] + x1 * cw[2:3] + x * cw[3:4]
    xc = xc + cb
    r = jax.nn.sigmoid(_dot(xc, wa) + ba)
    ig = jax.nn.sigmoid(_dot(xc, wi) + bi)
    log_a = -LRU_C * r * _softplus(-lam)
    a = jnp.exp(log_a)
    return a, jnp.sqrt(-jnp.tanh(log_a) * (a * a + 1.0)) * (ig * xc)


def _lru_seq_kernel(x_ref, gr_ref, buf_ref, h0_ref, cw_ref, cb_ref, wa_ref, ba_ref, wi_ref, bi_ref, lam_ref,
                    o_ref, hl_ref, carry, h_scr, a_scr, b_scr):
    tl = x_ref.shape[1]
    n_carry = carry.shape[0]

    @pl.when(pl.program_id(1) == 0)
    def _():
        for d in range(n_carry):
            carry[d] = buf_ref[0, d:d + 1, :]
        h_scr[...] = h0_ref[0]

    x = x_ref[0]
    x1 = _shifted(x, carry[n_carry - 1])
    x2 = _shifted(x1, carry[n_carry - 2])
    x3 = _shifted(x2, carry[n_carry - 3])
    for d in range(n_carry):
        carry[d] = x[tl - n_carry + d:tl - n_carry + d + 1, :]
    a, b = _lru_coeffs(x, x1, x2, x3, cw_ref[...], cb_ref[...], wa_ref[...], ba_ref[...], wi_ref[...],
                       bi_ref[...], lam_ref[...])
    a_scr[...] = a
    b_scr[...] = b
    row_id = lax.broadcasted_iota(jnp.int32, (SUBLANES, x.shape[1]), 0)

    def block(tb, h):
        base = pl.multiple_of(tb * SUBLANES, SUBLANES)
        a8, b8 = a_scr[pl.ds(base, SUBLANES), :], b_scr[pl.ds(base, SUBLANES), :]
        hs = jnp.zeros_like(a8)
        for i in range(SUBLANES):
            h = a8[i:i + 1, :] * h + b8[i:i + 1, :]
            hs = jnp.where(row_id == i, h, hs)
        o_ref[0, pl.ds(base, SUBLANES), :] = hs * jax.nn.gelu(gr_ref[0, pl.ds(base, SUBLANES), :])
        return h

    h = lax.fori_loop(0, tl // SUBLANES, block, h_scr[...])
    h_scr[...] = h
    hl_ref[0] = h


def lru_prompt(x, gr, buf, h0, prm):
    bsz, l, w = x.shape
    tl = _tile(l, 512)
    seq = pl.BlockSpec((1, tl, w), lambda i, j: (i, j, 0))
    vec = pl.BlockSpec((1, 1, w), lambda i, j: (i, 0, 0))
    full = lambda a: pl.BlockSpec(a.shape, lambda i, j: (0,) * a.ndim)
    params = [prm[n] for n in ("conv_w", "conv_b", "wa", "ba", "wi", "bi", "lam")]
    nb = buf.shape[1]
    o, hl = pl.pallas_call(
        _lru_seq_kernel,
        grid=(bsz, l // tl),
        in_specs=[seq, seq, pl.BlockSpec((1, nb, w), lambda i, j: (i, 0, 0)), vec] + [full(a) for a in params],
        out_specs=[seq, vec],
        out_shape=[jax.ShapeDtypeStruct((bsz, l, w), F32), jax.ShapeDtypeStruct((bsz, 1, w), F32)],
        scratch_shapes=[pltpu.VMEM((nb, 1, w), F32), pltpu.VMEM((1, w), F32), pltpu.VMEM((tl, w), F32),
                        pltpu.VMEM((tl, w), F32)],
        compiler_params=_cparams(("parallel", "arbitrary")),
        name="lru_prompt",
    )(x, gr, buf, h0.reshape(bsz, 1, w), *params)
    return o, hl.reshape(bsz, w)


def _lru_step_kernel(x_ref, x1_ref, x2_ref, x3_ref, gr_ref, h0_ref, cw_ref, cb_ref, wa_ref, ba_ref, wi_ref,
                     bi_ref, lam_ref, o_ref, h_ref):
    a, b = _lru_coeffs(x_ref[...], x1_ref[...], x2_ref[...], x3_ref[...], cw_ref[...], cb_ref[...], wa_ref[...],
                       ba_ref[...], wi_ref[...], bi_ref[...], lam_ref[...])
    h = a * h0_ref[...] + b
    h_ref[...] = h
    o_ref[...] = h * jax.nn.gelu(gr_ref[...])


def lru_step(x, buf, gr, h0, prm):
    m, w = x.shape
    tm = _tile(m, 512)
    rows = pl.BlockSpec((tm, w), lambda i: (i, 0))
    full = lambda a: pl.BlockSpec(a.shape, lambda i: (0,) * a.ndim)
    params = [prm[n] for n in ("conv_w", "conv_b", "wa", "ba", "wi", "bi", "lam")]
    nb = buf.shape[1]
    return pl.pallas_call(
        _lru_step_kernel,
        grid=(m // tm,),
        in_specs=[rows] * 6 + [full(a) for a in params],
        out_specs=[rows] * 2,
        out_shape=[jax.ShapeDtypeStruct((m, w), F32)] * 2,
        compiler_params=_cparams(("parallel",)),
        name="lru_step",
    )(x, buf[:, nb - 1], buf[:, nb - 2], buf[:, nb - 3], gr, h0, *params)


def _block_diag(w):
    n, d, e = w.shape
    eye = jnp.eye(n, dtype=w.dtype)
    return (eye[:, None, :, None] * w[:, :, None, :]).reshape(n * d, n * e)


def _mix_even(h, g_norm, pos0, s_ret, s_rwkv, buf, wts, is_prompt):
    b, l, d = h.shape
    m = b * l
    gw = d // 2
    u = norm_matmul(h.reshape(m, d), g_norm, wts["ab_w_in"], gw)
    seq = lambda t: t.reshape(b, l, gw)
    flat = lambda t: t.reshape(m, gw)
    qa, ka, va, ga = u[:4]
    to_kg = lambda t: t[..., wts["perm"]]
    to_nat = lambda t: t[..., wts["inv"]]
    cur = [seq(t) for t in u[4:8]]
    last = [t[:, l - 1:] for t in cur]
    buf_new = jnp.concatenate([to_nat(last[0]), to_nat(last[1]), last[2], last[3]], axis=-1)
    buf_kg = [to_kg(buf[..., :gw]), to_kg(buf[..., gw:2 * gw]), buf[..., 2 * gw:3 * gw], buf[..., 3 * gw:]]
    if is_prompt:
        r, w, k, v, kk, kka, g, bonus, v_hi, v_lo = rwkv_prep(cur, None, jnp.concatenate(buf_kg, axis=-1),
                                                              wts["rwkv"])
        pos = pos0 + jnp.arange(l, dtype=F32)
        o_a, s_ret_new = retention_prompt(seq(qa), seq(ka), seq(va), seq(ga), pos)
        o_a = flat(o_a)
        o_b, s_rwkv_new = rwkv_scan_prompt(r, w, k, kk, kka, v_hi, v_lo)
    else:
        rows = lambda t: t.reshape(1, m, gw)
        r, w, k, v, kk, kka, g, bonus, _, _ = rwkv_prep([rows(t) for t in cur], [rows(t) for t in buf_kg], None,
                                                        wts["rwkv"])
        o_a, s_ret_new = retention_step(qa, ka, va, ga, s_ret, jnp.float32(pos0))
        o_b, s_rwkv_new = rwkv_step(*(flat(to_nat(t)) for t in (r, w, k, kk, kka)), flat(v), s_rwkv)
    o_b = rwkv_post(flat(o_b), flat(bonus), flat(g), wts["rwkv_ln"], wts["ones_bd"])
    return o_a, o_b, s_ret_new, s_rwkv_new, buf_new


def _mix_odd(h, g_norm, lru_h, lru_buf, pages, wts, layer, is_prompt):
    b, l, d = h.shape
    m = b * l
    gw = d // 2
    u = norm_matmul(h.reshape(m, d), g_norm, wts["cd_w_in"], gw)
    seq = lambda t: t.reshape(b, l, gw)
    lam_init = 0.8 - 0.6 * math.exp(-0.3 * layer)
    xr = seq(u[3])
    if is_prompt:
        o_c = diff_attn_prompt(seq(u[0]), seq(u[1]), seq(u[2]), wts["diff_lam"], wts["diff_subln"], lam_init)
        o_c = o_c.reshape(m, gw)
        o_d, h_last = lru_prompt(xr, seq(u[4]), lru_buf, lru_h, wts["lru"])
    else:
        cache_k, cache_v, page_table = pages
        o_c = diff_attn_decode(u[0], u[1], u[2], cache_k, cache_v, page_table, wts["diff_lam"],
                               wts["diff_subln"], lam_init)
        o_d, h_last = lru_step(u[3], lru_buf, u[4], lru_h, wts["lru"])
    buf_new = jnp.concatenate([lru_buf, xr], axis=1)[:, l:]
    n_heads = gw // (2 * HEAD_DIM)
    k_new = u[1].reshape(b, l, n_heads, 2 * HEAD_DIM)
    v_new = u[2].reshape(b, l, n_heads, 2 * HEAD_DIM)
    return o_c, o_d.reshape(m, gw), k_new, v_new, h_last, buf_new


def _trunk(x, p, pos0, s_ret, s_rwkv, s_shift, s_lru_h, s_lru_conv, pages, wts, is_prompt):
    b, l, d = x.shape
    m = b * l
    depth = wts["norm_g"].shape[0]
    h = x.reshape(m, d)
    ret_l, rwkv_l, shift_l, k_l, v_l, lh_l, lc_l = [], [], [], [], [], [], []
    for i in range(depth):
        j = i // 2
        g = wts["norm_g"][i]
        gn = lambda n: g[n:n + 1]
        h = ffn_block(h, gn(0), wts["ffn_in"][i][0], wts["ffn_out"][i][0], gn(1))
        if i % 2 == 0:
            o1, o2, sr, sw, sb = _mix_even(h.reshape(b, l, d), gn(2), pos0, s_ret[j], s_rwkv[j], s_shift[j],
                                           wts["even"][j], is_prompt)
            ret_l.append(sr)
            rwkv_l.append(sw)
            shift_l.append(sb)
            w_out = wts["even"][j]["w_out"]
        else:
            pg = None if pages is None else (pages[0][j], pages[1][j], pages[2])
            o1, o2, kn, vn, lh, lc = _mix_odd(h.reshape(b, l, d), gn(2), s_lru_h[j], s_lru_conv[j], pg,
                                              wts["odd"][j], i, is_prompt)
            k_l.append(kn)
            v_l.append(vn)
            lh_l.append(lh)
            lc_l.append(lc)
            w_out = wts["odd"][j]["w_out"]
        h = out_proj(o1, o2, w_out, h, gn(3))
        h = ffn_block(h, gn(4), wts["ffn_in"][i][1], wts["ffn_out"][i][1], gn(5))
        h = ple_block(h, gn(6), wts["ple_gate"][i], p[i].reshape(m, -1), wts["ple"][i], gn(7))
    st = lambda lst: jnp.stack(lst, axis=0)
    return (h.reshape(b, l, d), st(k_l), st(v_l), st(ret_l), st(rwkv_l), st(shift_l), st(lh_l), st(lc_l))


def kernel(x_prompt, x_sample, cache_k, cache_v, state_ret, state_rwkv, state_rwkv_shift, state_lru_h, state_lru_conv, page_table, p_prompt, p_sample, norm_g, ffn_w_in, ffn_w_out, ple_w, ple_gate_w, ab_w_in, ab_w_out, rwkv_mu, rwkv_w0, rwkv_w1, rwkv_w2, rwkv_a0, rwkv_a1, rwkv_a2, rwkv_g1, rwkv_g2, rwkv_kk, rwkv_ka, rwkv_rk, rwkv_ln, cd_w_in, cd_w_out, diff_lam, diff_subln, lru_conv_w, lru_conv_b, lru_wa, lru_ba, lru_wi, lru_bi, lru_lambda):
    depth = norm_g.shape[0]
    n_a, n_c = state_ret.shape[0], state_lru_h.shape[0]
    bp = x_prompt.shape[0]
    gw = ab_w_out.shape[1] // 2
    bf = lambda t: t.astype(BF16)
    row = lambda t: t.reshape(1, -1)
    ones_bd = _block_diag(jnp.ones((gw // HEAD_DIM, HEAD_DIM, HEAD_DIM), BF16))
    perm = _key_group_perm(gw)
    inv = np.argsort(perm)
    kg = lambda t: t[..., perm]
    head_kg = perm // HEAD_DIM
    ones_kk = jnp.asarray(head_kg[:, None] == head_kg[None, :], BF16)
    ones_rk = jnp.asarray(head_kg[:, None] == (np.arange(gw) // HEAD_DIM)[None, :], BF16)

    def ab_in_kg(w):
        cols = [w[:, g * gw:(g + 1) * gw] for g in range(w.shape[1] // gw)]
        cols[4], cols[5] = kg(cols[4]), kg(cols[5])
        return bf(jnp.concatenate(cols, axis=1))

    wts = {
        "norm_g": norm_g,
        "ffn_in": [[bf(ffn_w_in[i, s]) for s in range(2)] for i in range(depth)],
        "ffn_out": [[bf(ffn_w_out[i, s]) for s in range(2)] for i in range(depth)],
        "ple": [bf(ple_w[i]) for i in range(depth)],
        "ple_gate": [bf(ple_gate_w[i]) for i in range(depth)],
        "even": [{
            "ab_w_in": ab_in_kg(ab_w_in[j]), "w_out": bf(ab_w_out[j]), "rwkv_ln": rwkv_ln[j],
            "perm": perm, "inv": inv, "ones_bd": ones_bd,
            "rwkv": {"mu": jnp.concatenate([kg(rwkv_mu[j][:2]), rwkv_mu[j][2:]], axis=0),
                     "w0": kg(row(rwkv_w0[j])), "w1": bf(rwkv_w1[j]), "w2": bf(kg(rwkv_w2[j])),
                     "a0": kg(row(rwkv_a0[j])), "a1": bf(rwkv_a1[j]), "a2": bf(kg(rwkv_a2[j])),
                     "g1": bf(rwkv_g1[j]), "g2": bf(rwkv_g2[j]), "kk": kg(row(rwkv_kk[j])),
                     "ka": kg(row(rwkv_ka[j])), "rk": kg(row(rwkv_rk[j])),
                     "ones_kk": ones_kk, "ones_rk": ones_rk},
        } for j in range(n_a)],
        "odd": [{
            "cd_w_in": bf(cd_w_in[j]), "w_out": bf(cd_w_out[j]), "diff_lam": diff_lam[j],
            "diff_subln": row(diff_subln[j]),
            "lru": {"conv_w": lru_conv_w[j], "conv_b": row(lru_conv_b[j]), "wa": bf(_block_diag(lru_wa[j])),
                    "ba": row(lru_ba[j]), "wi": bf(_block_diag(lru_wi[j])), "bi": row(lru_bi[j]),
                    "lam": row(lru_lambda[j])},
        } for j in range(n_c)],
    }
    zeros = lambda *shape: jnp.zeros(shape, F32)
    yp, kp, vp, rp, wp, sp, hp, cp = _trunk(
        x_prompt, p_prompt, 0.0, [None] * n_a, [None] * n_a,
        zeros(n_a, bp, 1, 4 * gw), zeros(n_c, bp, gw), zeros(n_c, bp, CONV_W - 1, gw),
        None, wts, True)
    past_len = page_table.shape[1] * cache_k.shape[2]
    n_pool, page = cache_k.shape[1], cache_k.shape[2]
    as_rows = lambda c: c.reshape(n_c, n_pool, page * c.shape[3], c.shape[4])
    pages = (as_rows(cache_k), as_rows(cache_v), page_table)
    ys, ks_, vs, rs, ws, ss, hs, cs = _trunk(
        x_sample, p_sample, float(past_len), state_ret, state_rwkv, state_rwkv_shift,
        state_lru_h, state_lru_conv, pages, wts, False)
    return (yp, ys, kp, vp, rp, wp, sp, hp, cp, ks_, vs, rs, ws, ss, hs, cs)
```

```python
import functools
import math

import jax
import jax.numpy as jnp
import numpy as np
from jax import lax
from jax.experimental import pallas as pl
from jax.experimental.pallas import tpu as pltpu

F32 = jnp.float32
BF16 = jnp.bfloat16

HEAD_DIM = 64
CONV_W = 4
LRU_C = 8.0
ROPE_BASE = 10000.0
EPS = 1e-6
RWKV_GN_EPS = 64e-5
RET_CHUNK = 256
ATTN_BLOCK = 512
SCAN_CHUNK = 128
DECODE_PAGES = 16
SCAN_GROUP = 8
LANES = 128
SUBLANES = 8
VMEM_LIMIT = 48 * 1024 * 1024


def _cparams(sem):
    return pltpu.CompilerParams(dimension_semantics=sem, vmem_limit_bytes=VMEM_LIMIT)


def _tile(n, pref):
    t = min(n, pref)
    while n % t:
        t //= 2
    return t


def _rms(x, g):
    return x * lax.rsqrt(jnp.mean(x * x, axis=-1, keepdims=True) + EPS) * g


def _dot(a, b):
    return jnp.dot(a.astype(BF16), b.astype(BF16), preferred_element_type=F32)


def _seg_sum(x, ones_bd):
    hi = x.astype(BF16)
    lo = (x - hi.astype(F32)).astype(BF16)
    return (jnp.dot(hi, ones_bd, preferred_element_type=F32)
            + jnp.dot(lo, ones_bd, preferred_element_type=F32))


def _ffn_kernel(h_ref, gpre_ref, wg_ref, wu_ref, wo_ref, gpost_ref, o_ref, xn_ref, acc_ref):
    j = pl.program_id(1)

    @pl.when(j == 0)
    def _():
        xn_ref[...] = _rms(h_ref[...], gpre_ref[...]).astype(BF16)
        acc_ref[...] = jnp.zeros_like(acc_ref)

    xn = xn_ref[...]
    gate = jnp.dot(xn, wg_ref[...], preferred_element_type=F32)
    up = jnp.dot(xn, wu_ref[...], preferred_element_type=F32)
    act = (gate * jax.nn.sigmoid(gate) * up).astype(BF16)
    acc_ref[...] += jnp.dot(act, wo_ref[...], preferred_element_type=F32)

    @pl.when(j == pl.num_programs(1) - 1)
    def _():
        o_ref[...] = h_ref[...] + 0.5 * _rms(acc_ref[...], gpost_ref[...])


def ffn_block(h, g_pre, w_in, w_out, g_post, layer, half):
    m, d = h.shape
    f = w_out.shape[2]
    tm, tf = _tile(m, 1024), _tile(f, 512)
    nf = f // tf
    return pl.pallas_call(
        _ffn_kernel,
        grid=(m // tm, nf),
        in_specs=[
            pl.BlockSpec((tm, d), lambda i, j: (i, 0)),
            pl.BlockSpec((1, d), lambda i, j: (0, 0)),
            pl.BlockSpec((None, None, d, tf), lambda i, j: (layer, half, 0, j)),
            pl.BlockSpec((None, None, d, tf), lambda i, j: (layer, half, 0, j + nf)),
            pl.BlockSpec((None, None, tf, d), lambda i, j: (layer, half, j, 0)),
            pl.BlockSpec((1, d), lambda i, j: (0, 0)),
        ],
        out_specs=pl.BlockSpec((tm, d), lambda i, j: (i, 0)),
        out_shape=jax.ShapeDtypeStruct((m, d), F32),
        scratch_shapes=[pltpu.VMEM((tm, d), BF16), pltpu.VMEM((tm, d), F32)],
        compiler_params=_cparams(("parallel", "arbitrary")),
        name="ffn_block",
    )(h, g_pre, w_in, w_in, w_out, g_post)


def _norm_matmul_kernel(h_ref, g_ref, w_ref, *o_refs):
    xn = _rms(h_ref[...], g_ref[...]).astype(BF16)
    tn = o_refs[0].shape[1]
    for gi, o_ref in enumerate(o_refs):
        o_ref[...] = jnp.dot(xn, w_ref[:, gi * tn:(gi + 1) * tn], preferred_element_type=F32)


def norm_matmul(h, g, w, tn):
    m, d = h.shape
    n = w.shape[1]
    tm = _tile(m, 512)
    rows = pl.BlockSpec((tm, tn), lambda i: (i, 0))
    return pl.pallas_call(
        _norm_matmul_kernel,
        grid=(m // tm,),
        in_specs=[
            pl.BlockSpec((tm, d), lambda i: (i, 0)),
            pl.BlockSpec((1, d), lambda i: (0, 0)),
            pl.BlockSpec((d, n), lambda i: (0, 0)),
        ],
        out_specs=[rows] * (n // tn),
        out_shape=[jax.ShapeDtypeStruct((m, tn), F32)] * (n // tn),
        compiler_params=_cparams(("parallel",)),
        name="norm_matmul",
    )(h, g, w)


def _out_proj_kernel(oa_ref, ob_ref, wa_ref, wb_ref, h_ref, g_ref, *rest):
    ob = ob_ref[...]
    if len(rest) > 1:
        bonus_ref, gate_ref, ln_ref, ones_ref = rest[:4]
        ones_bd = ones_ref[...]
        inv = 1.0 / HEAD_DIM
        oc = ob - _seg_sum(ob, ones_bd) * inv
        on = oc * lax.rsqrt(_seg_sum(oc * oc, ones_bd) * inv + RWKV_GN_EPS)
        ln = ln_ref[...]
        ob = (on * ln[0:1] + ln[1:2] + bonus_ref[...]) * gate_ref[...]
    o_ref = rest[-1]
    y = _dot(oa_ref[...], wa_ref[...]) + _dot(ob, wb_ref[...])
    o_ref[...] = h_ref[...] + _rms(y, g_ref[...])


def out_proj(oa, ob, w, h, g, rwkv_post=None):
    m, d = h.shape
    gw = oa.shape[1]
    tm = _tile(m, 512)
    rows = pl.BlockSpec((tm, gw), lambda i: (i, 0))
    extra, extra_specs = [], []
    if rwkv_post is not None:
        bonus, gate, ln, ones_bd = rwkv_post
        extra = [bonus, gate, ln, ones_bd]
        extra_specs = [rows, rows, pl.BlockSpec(ln.shape, lambda i: (0, 0)),
                       pl.BlockSpec(ones_bd.shape, lambda i: (0, 0))]
    return pl.pallas_call(
        _out_proj_kernel,
        grid=(m // tm,),
        in_specs=[
            rows, rows,
            pl.BlockSpec((gw, d), lambda i: (0, 0)),
            pl.BlockSpec((gw, d), lambda i: (1, 0)),
            pl.BlockSpec((tm, d), lambda i: (i, 0)),
            pl.BlockSpec((1, d), lambda i: (0, 0)),
        ] + extra_specs,
        out_specs=pl.BlockSpec((tm, d), lambda i: (i, 0)),
        out_shape=jax.ShapeDtypeStruct((m, d), F32),
        compiler_params=_cparams(("parallel",)),
        name="out_proj",
    )(oa, ob, w, w, h, g, *extra)


def _ple_kernel(h_ref, g6_ref, wg_ref, p_ref, wp_ref, g7_ref, o_ref):
    h = h_ref[...]
    gate = jax.nn.sigmoid(_dot(_rms(h, g6_ref[...]), wg_ref[...]))
    y = gate * _dot(p_ref[...], wp_ref[...])
    o_ref[...] = h + _rms(y, g7_ref[...])


def ple_block(h, g6, wg, p, wp, g7, layer):
    m, d = h.shape
    pd = p.shape[2]
    tm = _tile(m, 512)
    return pl.pallas_call(
        _ple_kernel,
        grid=(m // tm,),
        in_specs=[
            pl.BlockSpec((tm, d), lambda i: (i, 0)),
            pl.BlockSpec((1, d), lambda i: (0, 0)),
            pl.BlockSpec((None, d, d), lambda i: (layer, 0, 0)),
            pl.BlockSpec((None, tm, pd), lambda i: (layer, i, 0)),
            pl.BlockSpec((None, pd, d), lambda i: (layer, 0, 0)),
            pl.BlockSpec((1, d), lambda i: (0, 0)),
        ],
        out_specs=pl.BlockSpec((tm, d), lambda i: (i, 0)),
        out_shape=jax.ShapeDtypeStruct((m, d), F32),
        compiler_params=_cparams(("parallel",)),
        name="ple_block",
    )(h, g6, wg, p, wp, g7)


def _retention_tables(n_heads, c, pos):
    lg = jnp.log1p(-jnp.exp2(-5.0 - jnp.arange(n_heads, dtype=F32)))
    idx = jnp.arange(c, dtype=F32)
    rel = idx[:, None] - idx[None, :]
    dmask = jnp.where(rel[None] >= 0, jnp.exp(jnp.maximum(rel, 0.0)[None] * lg[:, None, None]), 0.0)
    rep = lambda t: jnp.repeat(t, HEAD_DIM, axis=-1)
    q_dec = rep(jnp.exp((idx[:, None] + 1.0) * lg[None, :]))
    k_dec = rep(jnp.exp((c - 1.0 - idx[:, None]) * lg[None, :]))
    c_dec = rep(jnp.exp(c * lg)[None, :])
    half = HEAD_DIM // 2
    freq = 1.0 / (ROPE_BASE ** jnp.linspace(0.0, 1.0, half, dtype=F32))
    ang = pos[:, None] * freq[None, :]
    cos, sin = jnp.cos(ang), jnp.sin(ang)
    cos_t = jnp.tile(jnp.concatenate([cos, cos], axis=-1), (1, n_heads))
    sin_t = jnp.tile(jnp.concatenate([-sin, sin], axis=-1), (1, n_heads))
    return dmask, q_dec, k_dec, c_dec, cos_t, sin_t


def _retention_kernel(q_ref, k_ref, v_ref, g_ref, cos_ref, sin_ref, dmask_ref, qdec_ref, kdec_ref,
                      cdec_ref, o_ref, s_ref, s_scr, *, n_heads):
    c = pl.program_id(1)

    @pl.when(c == 0)
    def _():
        s_scr[...] = jnp.zeros_like(s_scr)

    q, k, v, g = q_ref[0], k_ref[0], v_ref[0], g_ref[0]
    cos, sin = cos_ref[...], sin_ref[...]
    width = q.shape[-1]
    lane = lax.broadcasted_iota(jnp.int32, q.shape, 1)
    first_half = (lane % HEAD_DIM) < (HEAD_DIM // 2)

    def rot(x):
        swapped = jnp.where(first_half, pltpu.roll(x, width - HEAD_DIM // 2, 1),
                            pltpu.roll(x, HEAD_DIM // 2, 1))
        return x * cos + swapped * sin

    qr = rot(q)
    kr = rot(k) * (HEAD_DIM ** -0.5)
    kd = kr * kdec_ref[...]
    qdec = qdec_ref[...]
    cdec = cdec_ref[...]
    gate = g * jax.nn.sigmoid(g)
    for h in range(n_heads):
        sl = slice(h * HEAD_DIM, (h + 1) * HEAD_DIM)
        qh = qr[:, sl].astype(BF16)
        kh = kr[:, sl].astype(BF16)
        vh = v[:, sl].astype(BF16)
        att = lax.dot_general(qh, kh, (((1,), (1,)), ((), ())), preferred_element_type=F32) * dmask_ref[h]
        s_old = s_scr[h]
        o = (jnp.dot(att.astype(BF16), vh, preferred_element_type=F32)
             + jnp.dot(qh, s_old.astype(BF16), preferred_element_type=F32) * qdec[:, sl])
        s_scr[h] = s_old * cdec[:, sl] + lax.dot_general(
            kd[:, sl].astype(BF16), vh, (((0,), (0,)), ((), ())), preferred_element_type=F32)
        oc = o - jnp.mean(o, axis=-1, keepdims=True)
        on = oc * lax.rsqrt(jnp.mean(oc * oc, axis=-1, keepdims=True) + EPS)
        o_ref[0, :, sl] = on * gate[:, sl]

    @pl.when(c == pl.num_programs(1) - 1)
    def _():
        s_ref[0] = s_scr[...]


def retention_prompt(q, k, v, g, pos):
    b, l, width = q.shape
    n_heads = width // HEAD_DIM
    c = _tile(l, RET_CHUNK)
    dmask, q_dec, k_dec, c_dec, cos_t, sin_t = _retention_tables(n_heads, c, pos)
    seq = pl.BlockSpec((1, c, width), lambda i, j: (i, j, 0))
    tab = pl.BlockSpec((c, width), lambda i, j: (j, 0))
    fixed = lambda shape: pl.BlockSpec(shape, lambda i, j: (0,) * len(shape))
    return pl.pallas_call(
        functools.partial(_retention_kernel, n_heads=n_heads),
        grid=(b, l // c),
        in_specs=[seq, seq, seq, seq, tab, tab, fixed((n_heads, c, c)), fixed((c, width)),
                  fixed((c, width)), fixed((1, width))],
        out_specs=[seq, pl.BlockSpec((1, n_heads, HEAD_DIM, HEAD_DIM), lambda i, j: (i, 0, 0, 0))],
        out_shape=[jax.ShapeDtypeStruct((b, l, width), F32),
                   jax.ShapeDtypeStruct((b, n_heads, HEAD_DIM, HEAD_DIM), F32)],
        scratch_shapes=[pltpu.VMEM((n_heads, HEAD_DIM, HEAD_DIM), F32)],
        compiler_params=_cparams(("parallel", "arbitrary")),
        name="retention_prompt",
    )(q, k, v, g, cos_t, sin_t, dmask, q_dec, k_dec, c_dec)


def _eye_mask():
    return (lax.broadcasted_iota(jnp.int32, (HEAD_DIM, HEAD_DIM), 0)
            == lax.broadcasted_iota(jnp.int32, (HEAD_DIM, HEAD_DIM), 1))


def _to_col(x_row):
    return jnp.sum(jnp.where(_eye_mask(), x_row, 0.0), axis=-1, keepdims=True)


def _to_row(x_col):
    return jnp.sum(jnp.where(_eye_mask(), x_col, 0.0), axis=-2, keepdims=True)


def _retention_step_kernel(q_ref, k_ref, v_ref, g_ref, cos_ref, sin_ref, gam_ref, s0_ref, o_ref, s_ref):
    cos, sin = cos_ref[...], sin_ref[...]

    def rot(x):
        half = HEAD_DIM // 2
        return x * cos + jnp.concatenate([x[..., half:], x[..., :half]], axis=-1) * sin

    q = _to_col(rot(q_ref[...]))
    k = _to_col(rot(k_ref[...]) * (HEAD_DIM ** -0.5))
    v, g = v_ref[...], g_ref[...]
    gam = gam_ref[...]
    s0 = s0_ref[...]
    att = jnp.sum(q * k, axis=2, keepdims=True)
    o = att * v + jnp.sum(q * s0, axis=2, keepdims=True) * gam
    s_ref[...] = s0 * gam + k * v
    oc = o - jnp.mean(o, axis=-1, keepdims=True)
    on = oc * lax.rsqrt(jnp.mean(oc * oc, axis=-1, keepdims=True) + EPS)
    o_ref[...] = on * (g * jax.nn.sigmoid(g))


def retention_step(q, k, v, g, s0, pos):
    b, width = q.shape
    n_heads = width // HEAD_DIM
    bb = _tile(b, 8)
    half = HEAD_DIM // 2
    freq = 1.0 / (ROPE_BASE ** jnp.linspace(0.0, 1.0, half, dtype=F32))
    ang = pos * freq
    cos_c = jnp.concatenate([jnp.cos(ang), jnp.cos(ang)])[None, :]
    sin_c = jnp.concatenate([-jnp.sin(ang), jnp.sin(ang)])[None, :]
    gam = jnp.exp(jnp.log1p(-jnp.exp2(-5.0 - jnp.arange(n_heads, dtype=F32)))).reshape(n_heads, 1, 1)
    row = lambda t: t.reshape(b, n_heads, 1, HEAD_DIM)
    rspec = pl.BlockSpec((bb, n_heads, 1, HEAD_DIM), lambda i: (i, 0, 0, 0))
    sspec = pl.BlockSpec((bb, n_heads, HEAD_DIM, HEAD_DIM), lambda i: (i, 0, 0, 0))
    o, s = pl.pallas_call(
        _retention_step_kernel,
        grid=(b // bb,),
        in_specs=[rspec, rspec, rspec, rspec,
                  pl.BlockSpec((1, HEAD_DIM), lambda i: (0, 0)), pl.BlockSpec((1, HEAD_DIM), lambda i: (0, 0)),
                  pl.BlockSpec((n_heads, 1, 1), lambda i: (0, 0, 0)), sspec],
        out_specs=[rspec, sspec],
        out_shape=[jax.ShapeDtypeStruct((b, n_heads, 1, HEAD_DIM), F32),
                   jax.ShapeDtypeStruct((b, n_heads, HEAD_DIM, HEAD_DIM), F32)],
        compiler_params=_cparams(("parallel",)),
        name="retention_step",
    )(row(q), row(k), row(v), row(g), cos_c, sin_c, gam, s0)
    return o.reshape(b, width), s


def _softplus(x):
    return jnp.maximum(x, 0.0) + jnp.log1p(jnp.exp(-jnp.abs(x)))


def _shifted(cur, carry_row):
    if cur.shape[0] == 1:
        return carry_row
    first = lax.broadcasted_iota(jnp.int32, cur.shape, 0) == 0
    return jnp.where(first, carry_row, pltpu.roll(cur, 1, 0))


PREP_PARAMS = ("mu", "w0", "w1", "w2", "a0", "a1", "a2", "g1", "g2", "kk", "ka", "rk", "ones_kk", "ones_rk")
N_PREP_PARAMS = len(PREP_PARAMS)


def _rwkv_prep_kernel(*refs, shift_in_kernel):
    cur_refs = refs[:4]
    if shift_in_kernel:
        buf_ref = refs[4]
        n_in = 5
    else:
        prev_refs = refs[4:8]
        n_in = 8
    (mu_ref, w0_ref, w1_ref, w2_ref, a0_ref, a1_ref, a2_ref, g1_ref, g2_ref, kkp_ref, kap_ref, rk_ref,
     ones_kk_ref, ones_rk_ref) = refs[n_in:n_in + N_PREP_PARAMS]
    n_in += N_PREP_PARAMS
    r_out, w_out, k_out, v_out, kk_out, kka_out, g_out, bonus_out, vhi_out, vlo_out = refs[n_in:n_in + 10]
    cur = [ref[0] for ref in cur_refs]
    width = cur[0].shape[1]
    if shift_in_kernel:
        carry = refs[n_in + 10]

        @pl.when(pl.program_id(1) == 0)
        def _():
            for gi in range(4):
                carry[gi] = buf_ref[0, :, gi * width:(gi + 1) * width]

        prev = [_shifted(x, carry[gi]) for gi, x in enumerate(cur)]
        for gi, x in enumerate(cur):
            carry[gi] = x[x.shape[0] - 1:, :]
    else:
        prev = [ref[0] for ref in prev_refs]
    mu = mu_ref[...]
    lerp = lambda x, xp, i: x + (xp - x) * mu[i:i + 1]
    zr, pz = cur[3], prev[3]
    r = lerp(cur[0], prev[0], 0)
    kx = lerp(cur[1], prev[1], 1)
    vx = lerp(cur[2], prev[2], 2)
    zw, za, zg = lerp(zr, pz, 3), lerp(zr, pz, 4), lerp(zr, pz, 5)
    wpre = w0_ref[...] + _dot(jnp.tanh(_dot(zw, w1_ref[...])), w2_ref[...])
    decay = jnp.exp(-jnp.exp(-_softplus(-wpre) - 0.5))
    a = jax.nn.sigmoid(a0_ref[...] + _dot(_dot(za, a1_ref[...]), a2_ref[...]))
    g = _dot(jax.nn.sigmoid(_dot(zg, g1_ref[...])), g2_ref[...])
    kk = kx * kkp_ref[...]
    kk = kk / jnp.maximum(jnp.sqrt(_seg_sum(kk * kk, ones_kk_ref[...])), 1e-12)
    k32 = kx * (1.0 + (a - 1.0) * kap_ref[...])
    r_out[0] = r
    w_out[0] = decay
    k_out[0] = k32
    v_out[0] = vx
    v_hi = vx.astype(BF16)
    vhi_out[0] = v_hi
    vlo_out[0] = (vx - v_hi.astype(F32)).astype(BF16)
    kk_out[0] = kk
    kka_out[0] = kk * a
    g_out[0] = g
    bonus_out[0] = _seg_sum(r * k32 * rk_ref[...], ones_rk_ref[...]) * vx


def rwkv_prep(cur, prev, buf, prm):
    b, l, w = cur[0].shape
    tl = _tile(l, 256)
    seq = pl.BlockSpec((1, tl, w), lambda i, j: (i, j, 0))
    full = lambda a: pl.BlockSpec(a.shape, lambda i, j: (0,) * a.ndim)
    params = [prm[n] for n in PREP_PARAMS]
    shift = prev is None
    if shift:
        extra, extra_specs = [buf], [pl.BlockSpec((1, 1, 4 * w), lambda i, j: (i, 0, 0))]
        scratch = [pltpu.VMEM((4, 1, w), F32)]
    else:
        extra, extra_specs, scratch = list(prev), [seq] * 4, []
    return pl.pallas_call(
        functools.partial(_rwkv_prep_kernel, shift_in_kernel=shift),
        grid=(b, l // tl),
        in_specs=[seq] * 4 + extra_specs + [full(a) for a in params],
        out_specs=[seq] * 10,
        out_shape=[jax.ShapeDtypeStruct((b, l, w), F32)] * 8 + [jax.ShapeDtypeStruct((b, l, w), BF16)] * 2,
        scratch_shapes=scratch,
        compiler_params=_cparams(("parallel", "arbitrary")),
        name="rwkv_prep",
    )(*cur, *extra, *params)


def _hi_lo(x):
    hi = x.astype(BF16)
    lo = (x - hi.astype(F32)).astype(BF16)
    return jnp.concatenate([hi, lo], axis=1)


def _rwkv_scan_kernel(r_ref, w_ref, k_ref, kk_ref, kka_ref, vp_ref, sel_ref, ones_ref, o_ref, s_ref, s_scr,
                      ot_scr, *, n_heads):
    c = pl.program_id(1)
    n_grp, tc = r_ref.shape[0], r_ref.shape[1]
    head_lanes = LANES // n_heads
    n_kg = HEAD_DIM // head_lanes

    @pl.when(c == 0)
    def _():
        s_scr[...] = jnp.zeros_like(s_scr)

    ot_scr[...] = jnp.zeros_like(ot_scr)
    ones2 = ones_ref[...]
    t_lane = lax.broadcasted_iota(jnp.int32, (n_grp * HEAD_DIM, LANES), 1) % head_lanes
    rows_of = lambda x, g: x[g * HEAD_DIM:(g + 1) * HEAD_DIM]

    def block(tb, carry):
        base = pl.multiple_of(tb * SUBLANES, SUBLANES)
        tiles = {name: [[ref[g, pl.ds(base, SUBLANES), kg * LANES:(kg + 1) * LANES] for kg in range(n_kg)]
                        for g in range(n_grp)]
                 for name, ref in (("kk", kk_ref), ("w", w_ref), ("kka", kka_ref), ("k", k_ref), ("r", r_ref))}
        vp = jnp.concatenate([vp_ref[g, tb] for g in range(n_grp)], axis=0)
        s = [[s_scr[g, kg] for kg in range(n_kg)] for g in range(n_grp)]
        acc = jnp.zeros((n_grp * HEAD_DIM, LANES), F32)
        for i in range(SUBLANES):
            row = lambda name, g, kg: jnp.broadcast_to(tiles[name][g][kg][i:i + 1, :], (HEAD_DIM, LANES))
            key_sum = lambda name: jnp.dot(_hi_lo(jnp.concatenate(
                [sum(s[g][kg] * row(name, g, kg) for kg in range(n_kg)) for g in range(n_grp)], axis=0)),
                ones2, preferred_element_type=F32)
            sa = key_sum("kk")
            vcol = jnp.dot(vp, sel_ref[i], preferred_element_type=F32)
            for g in range(n_grp):
                sa_g, vcol_g = rows_of(sa, g), rows_of(vcol, g)
                for kg in range(n_kg):
                    s[g][kg] = (s[g][kg] * row("w", g, kg) - sa_g * row("kka", g, kg)
                                + vcol_g * row("k", g, kg))
            acc = jnp.where(t_lane == (base + i) % head_lanes, key_sum("r"), acc)
        for g in range(n_grp):
            for kg in range(n_kg):
                s_scr[g, kg] = s[g][kg]
        ot_scr[base // head_lanes] += acc
        return carry

    lax.fori_loop(0, tc // SUBLANES, block, 0)

    for tile in range(tc // head_lanes):
        for g in range(n_grp):
            o_t = rows_of(ot_scr[tile], g).T
            for h in range(n_heads):
                o_ref[g, tile * head_lanes:(tile + 1) * head_lanes, h * HEAD_DIM:(h + 1) * HEAD_DIM] = (
                    o_t[h * head_lanes:(h + 1) * head_lanes, :])

    @pl.when(c == pl.num_programs(1) - 1)
    def _():
        s_ref[...] = s_scr[...]


def _key_group_perm(width):
    n_heads = width // HEAD_DIM
    head_lanes = LANES // n_heads
    n = np.arange(width)
    return (n % LANES) // head_lanes * HEAD_DIM + n // LANES * head_lanes + n % head_lanes


def rwkv_scan_prompt(r, w, k, kk, kka, v_hi, v_lo):
    b, l, width = r.shape
    n_heads = width // HEAD_DIM
    head_lanes = LANES // n_heads
    n_kg = HEAD_DIM // head_lanes
    tc = _tile(l, SCAN_CHUNK)
    grp = _tile(b, SCAN_GROUP)
    vp = jnp.stack([v_hi, v_lo], axis=2).reshape(b, l // SUBLANES, SUBLANES, 2, n_heads, HEAD_DIM)
    vp = vp.transpose(0, 1, 5, 4, 3, 2).reshape(b, l // SUBLANES, HEAD_DIM, LANES)
    kl = np.arange(LANES)
    sel = ((kl[None, :, None] % SUBLANES == np.arange(SUBLANES)[:, None, None])
           & (kl[None, :, None] // head_lanes == kl[None, None, :] // head_lanes))
    k2 = np.arange(2 * LANES)
    ones2 = (k2[:, None] % LANES) // head_lanes == kl[None, :] // head_lanes
    sel, ones2 = jnp.asarray(sel, BF16), jnp.asarray(ones2, BF16)
    seq = pl.BlockSpec((grp, tc, width), lambda i, j: (i, j, 0))
    state = pl.BlockSpec((grp, n_kg, HEAD_DIM, LANES), lambda i, j: (i, 0, 0, 0))
    o, s = pl.pallas_call(
        functools.partial(_rwkv_scan_kernel, n_heads=n_heads),
        grid=(b // grp, l // tc),
        in_specs=[seq, seq, seq, seq, seq,
                  pl.BlockSpec((grp, tc // SUBLANES, HEAD_DIM, LANES), lambda i, j: (i, j, 0, 0)),
                  pl.BlockSpec(sel.shape, lambda i, j: (0, 0, 0)),
                  pl.BlockSpec(ones2.shape, lambda i, j: (0, 0))],
        out_specs=[seq, state],
        out_shape=[jax.ShapeDtypeStruct((b, l, width), F32),
                   jax.ShapeDtypeStruct((b, n_kg, HEAD_DIM, LANES), F32)],
        scratch_shapes=[pltpu.VMEM((grp, n_kg, HEAD_DIM, LANES), F32),
                        pltpu.VMEM((tc // head_lanes, grp * HEAD_DIM, LANES), F32)],
        compiler_params=_cparams(("parallel", "arbitrary")),
        name="rwkv_scan_prompt",
    )(r, w, k, kk, kka, vp, sel, ones2)
    s = s.reshape(b, n_kg, HEAD_DIM, n_heads, head_lanes).transpose(0, 3, 2, 1, 4)
    return o, s.reshape(b, n_heads, HEAD_DIM, HEAD_DIM)


def _rwkv_step_kernel(r_ref, w_ref, k_ref, kk_ref, kka_ref, v_ref, s0_ref, o_ref, s_ref):
    s0 = s0_ref[...]
    sa = -jnp.sum(s0 * kk_ref[...], axis=-1, keepdims=True)
    s = s0 * w_ref[...] + sa * kka_ref[...] + _to_col(v_ref[...]) * k_ref[...]
    s_ref[...] = s
    o_ref[...] = _to_row(jnp.sum(s * r_ref[...], axis=-1, keepdims=True))


def rwkv_step(r, w, k, kk, kka, v, s0):
    b, width = r.shape
    n_heads = width // HEAD_DIM
    bb = _tile(b, 8)
    row = lambda t: t.reshape(b, n_heads, 1, HEAD_DIM)
    rspec = pl.BlockSpec((bb, n_heads, 1, HEAD_DIM), lambda i: (i, 0, 0, 0))
    sspec = pl.BlockSpec((bb, n_heads, HEAD_DIM, HEAD_DIM), lambda i: (i, 0, 0, 0))
    o, s = pl.pallas_call(
        _rwkv_step_kernel,
        grid=(b // bb,),
        in_specs=[rspec] * 6 + [sspec],
        out_specs=[rspec, sspec],
        out_shape=[jax.ShapeDtypeStruct((b, n_heads, 1, HEAD_DIM), F32),
                   jax.ShapeDtypeStruct((b, n_heads, HEAD_DIM, HEAD_DIM), F32)],
        compiler_params=_cparams(("parallel",)),
        name="rwkv_step",
    )(row(r), row(w), row(k), row(kk), row(kka), row(v), s0)
    return o.reshape(b, width), s


def _diff_lambda(lp, lam_init):
    e1 = jnp.exp(jnp.sum(lp[0:1] * lp[1:2], axis=-1, keepdims=True))
    e2 = jnp.exp(jnp.sum(lp[2:3] * lp[3:4], axis=-1, keepdims=True))
    return e1 - e2 + lam_init


def _diff_attn_kernel(q_ref, k_ref, v_ref, lam_ref, subln_ref, o_ref, m_scr, l_scr, acc_scr, *, lam_init):
    i, j = pl.program_id(2), pl.program_id(3)
    tq, tk = q_ref.shape[1], k_ref.shape[1]
    scale = HEAD_DIM ** -0.5

    @pl.when(j == 0)
    def _():
        m_scr[...] = jnp.full_like(m_scr, -jnp.inf)
        l_scr[...] = jnp.zeros_like(l_scr)
        acc_scr[...] = jnp.zeros_like(acc_scr)

    def update(on_diagonal):
        q = (q_ref[0] * scale).astype(BF16)
        k, v = k_ref[0].astype(BF16), v_ref[0].astype(BF16)
        if on_diagonal:
            visible = (lax.broadcasted_iota(jnp.int32, (tq, tk), 1)
                       <= lax.broadcasted_iota(jnp.int32, (tq, tk), 0))
        for mi in range(2):
            sl = slice(mi * HEAD_DIM, (mi + 1) * HEAD_DIM)
            s = lax.dot_general(q[:, sl], k[:, sl], (((1,), (1,)), ((), ())), preferred_element_type=F32)
            if on_diagonal:
                s = jnp.where(visible, s, -jnp.inf)
            m_old = m_scr[mi]
            m_new = jnp.maximum(m_old, jnp.max(s, axis=-1, keepdims=True))
            alpha = jnp.exp(m_old - m_new)
            p = jnp.exp(s - jnp.concatenate([m_new] * (tk // LANES), axis=1))
            l_scr[mi] = alpha * l_scr[mi] + jnp.sum(p, axis=-1, keepdims=True)
            acc_scr[mi] = alpha * acc_scr[mi] + jnp.dot(p.astype(BF16), v, preferred_element_type=F32)
            m_scr[mi] = m_new

    @pl.when(j < i)
    def _():
        update(False)

    @pl.when(j == i)
    def _():
        update(True)
        lam = _diff_lambda(lam_ref[...], lam_init)
        o = acc_scr[0] / l_scr[0] - lam * (acc_scr[1] / l_scr[1])
        o_ref[0] = _rms(o, subln_ref[...]) * (1.0 - lam_init)


def diff_attn_prompt(q, k, v, lam_p, subln, lam_init):
    b, l, width = q.shape
    dv = 2 * HEAD_DIM
    n_heads = width // dv
    t = _tile(l, ATTN_BLOCK)
    n = l // t
    qspec = pl.BlockSpec((1, t, dv), lambda bi, h, i, j: (bi, i, h))
    kspec = pl.BlockSpec((1, t, dv), lambda bi, h, i, j: (bi, jnp.minimum(i, j), h))
    return pl.pallas_call(
        functools.partial(_diff_attn_kernel, lam_init=lam_init),
        grid=(b, n_heads, n, n),
        in_specs=[qspec, kspec, kspec,
                  pl.BlockSpec(lam_p.shape, lambda bi, h, i, j: (0, 0)),
                  pl.BlockSpec(subln.shape, lambda bi, h, i, j: (0, 0))],
        out_specs=qspec,
        out_shape=jax.ShapeDtypeStruct((b, l, width), F32),
        scratch_shapes=[pltpu.VMEM((2, t, LANES), F32), pltpu.VMEM((2, t, LANES), F32), pltpu.VMEM((2, t, dv), F32)],
        compiler_params=_cparams(("parallel", "parallel", "parallel", "arbitrary")),
        name="diff_attn_prompt",
    )(q, k, v, lam_p, subln)


def _diff_attn_decode_kernel(pt_ref, q_ref, kn_ref, vn_ref, *rest, lam_init, n_heads, n_slots):
    kc_refs, vc_refs = rest[:n_slots], rest[n_slots:2 * n_slots]
    lam_ref, subln_ref, o_ref, m_scr, l_scr, acc_scr = rest[2 * n_slots:]
    p = pl.program_id(1)
    n_rows = 2 * n_heads
    dv = 2 * HEAD_DIM
    scale = HEAD_DIM ** -0.5
    row = lax.broadcasted_iota(jnp.int32, (n_rows, dv), 0)
    lane = lax.broadcasted_iota(jnp.int32, (n_rows, dv), 1)
    qmat = jnp.where(lane // HEAD_DIM == row % 2, q_ref[0], 0.0)
    cols = n_slots * kc_refs[0].shape[1]
    col_head = lax.broadcasted_iota(jnp.int32, (n_rows, cols), 1) % n_heads
    own = col_head == lax.broadcasted_iota(jnp.int32, (n_rows, cols), 0) // 2

    @pl.when(p == 0)
    def _():
        m_scr[...] = jnp.full_like(m_scr, -jnp.inf)
        l_scr[...] = jnp.zeros_like(l_scr)
        acc_scr[...] = jnp.zeros_like(acc_scr)

    s = jnp.concatenate(
        [lax.dot_general(qmat.astype(BF16), kc_ref[0].astype(BF16), (((1,), (1,)), ((), ())),
                         preferred_element_type=F32) for kc_ref in kc_refs], axis=1) * scale
    s = jnp.where(own, s, -jnp.inf)
    m_old = m_scr[...]
    m_new = jnp.maximum(m_old, jnp.max(s, axis=-1, keepdims=True))
    alpha = jnp.exp(m_old - m_new)
    pr = jnp.exp(s - m_new)
    l_scr[...] = alpha * l_scr[...] + jnp.sum(pr, axis=-1, keepdims=True)
    prb = pr.astype(BF16)
    rows_per = kc_refs[0].shape[1]
    pv = sum(jnp.dot(prb[:, i * rows_per:(i + 1) * rows_per], vc_ref[0].astype(BF16), preferred_element_type=F32)
             for i, vc_ref in enumerate(vc_refs))
    acc_scr[...] = alpha * acc_scr[...] + pv
    m_scr[...] = m_new

    @pl.when(p == pl.num_programs(1) - 1)
    def _():
        s_new = jnp.sum(qmat * kn_ref[0], axis=-1, keepdims=True) * scale
        m_old = m_scr[...]
        m_fin = jnp.maximum(m_old, s_new)
        alpha = jnp.exp(m_old - m_fin)
        p_new = jnp.exp(s_new - m_fin)
        l_fin = alpha * l_scr[...] + p_new
        acc = (alpha * acc_scr[...] + p_new * vn_ref[0]) / l_fin
        lam = _diff_lambda(lam_ref[...], lam_init)
        acc = acc * jnp.where(row % 2 == 0, 1.0, -lam)
        subln = subln_ref[...]
        for h in range(n_heads):
            o = acc[2 * h:2 * h + 1] + acc[2 * h + 1:2 * h + 2]
            o_ref[0, h:h + 1, :] = _rms(o, subln) * (1.0 - lam_init)


def diff_attn_decode(q, k_new, v_new, cache_k, cache_v, page_table, lam_p, subln, lam_init):
    b, width = q.shape
    dv = 2 * HEAD_DIM
    n_heads = width // dv
    n_pages = page_table.shape[1]
    rows = cache_k.shape[1]
    n_slots = _tile(n_pages, DECODE_PAGES)
    per_map = lambda t: jnp.repeat(t.reshape(b, n_heads, dv), 2, axis=1)
    vec = pl.BlockSpec((1, 2 * n_heads, dv), lambda bi, p, pt: (bi, 0, 0))
    cache = [pl.BlockSpec((1, rows, dv), functools.partial(
        lambda bi, p, pt, slot: (pt[bi * n_pages + p * n_slots + slot], 0, 0), slot=slot))
        for slot in range(n_slots)]
    out = pl.pallas_call(
        functools.partial(_diff_attn_decode_kernel, lam_init=lam_init, n_heads=n_heads, n_slots=n_slots),
        grid_spec=pltpu.PrefetchScalarGridSpec(
            num_scalar_prefetch=1,
            grid=(b, n_pages // n_slots),
            in_specs=[vec, vec, vec] + cache + cache + [
                pl.BlockSpec(lam_p.shape, lambda bi, p, pt: (0, 0)),
                pl.BlockSpec(subln.shape, lambda bi, p, pt: (0, 0))],
            out_specs=pl.BlockSpec((1, n_heads, dv), lambda bi, p, pt: (bi, 0, 0)),
            scratch_shapes=[pltpu.VMEM((2 * n_heads, 1), F32), pltpu.VMEM((2 * n_heads, 1), F32),
                            pltpu.VMEM((2 * n_heads, dv), F32)],
        ),
        out_shape=jax.ShapeDtypeStruct((b, n_heads, dv), F32),
        compiler_params=_cparams(("parallel", "arbitrary")),
        name="diff_attn_decode",
    )(page_table.reshape(-1), per_map(q), per_map(k_new), per_map(v_new),
      *([cache_k] * n_slots), *([cache_v] * n_slots), lam_p, subln)
    return out.reshape(b, width)


def _lru_coeffs(x, x1, x2, x3, cw, cb, wa, ba, wi, bi, lam):
    xc = x3 * cw[0:1] + x2 * cw[1:2] + x1 * cw[2:3] + x * cw[3:4]
    xc = xc + cb
    r = jax.nn.sigmoid(_dot(xc, wa) + ba)
    ig = jax.nn.sigmoid(_dot(xc, wi) + bi)
    log_a = -LRU_C * r * _softplus(-lam)
    a = jnp.exp(log_a)
    return a, jnp.sqrt(-jnp.tanh(log_a) * (a * a + 1.0)) * (ig * xc)


def _lru_seq_kernel(x_ref, gr_ref, buf_ref, h0_ref, cw_ref, cb_ref, wa_ref, ba_ref, wi_ref, bi_ref, lam_ref,
                    o_ref, hl_ref, carry, h_scr, a_scr, b_scr):
    tl = x_ref.shape[1]
    n_carry = carry.shape[0]

    @pl.when(pl.program_id(1) == 0)
    def _():
        for d in range(n_carry):
            carry[d] = buf_ref[0, d:d + 1, :]
        h_scr[...] = h0_ref[0]

    x = x_ref[0]
    x1 = _shifted(x, carry[n_carry - 1])
    x2 = _shifted(x1, carry[n_carry - 2])
    x3 = _shifted(x2, carry[n_carry - 3])
    for d in range(n_carry):
        carry[d] = x[tl - n_carry + d:tl - n_carry + d + 1, :]
    a, b = _lru_coeffs(x, x1, x2, x3, cw_ref[...], cb_ref[...], wa_ref[...], ba_ref[...], wi_ref[...],
                       bi_ref[...], lam_ref[...])
    a_scr[...] = a
    b_scr[...] = b
    row_id = lax.broadcasted_iota(jnp.int32, (SUBLANES, x.shape[1]), 0)

    def block(tb, h):
        base = pl.multiple_of(tb * SUBLANES, SUBLANES)
        a8, b8 = a_scr[pl.ds(base, SUBLANES), :], b_scr[pl.ds(base, SUBLANES), :]
        hs = jnp.zeros_like(a8)
        for i in range(SUBLANES):
            h = a8[i:i + 1, :] * h + b8[i:i + 1, :]
            hs = jnp.where(row_id == i, h, hs)
        o_ref[0, pl.ds(base, SUBLANES), :] = hs * jax.nn.gelu(gr_ref[0, pl.ds(base, SUBLANES), :])
        return h

    h = lax.fori_loop(0, tl // SUBLANES, block, h_scr[...])
    h_scr[...] = h
    hl_ref[0] = h


def lru_prompt(x, gr, buf, h0, prm):
    bsz, l, w = x.shape
    tl = _tile(l, 512)
    seq = pl.BlockSpec((1, tl, w), lambda i, j: (i, j, 0))
    vec = pl.BlockSpec((1, 1, w), lambda i, j: (i, 0, 0))
    full = lambda a: pl.BlockSpec(a.shape, lambda i, j: (0,) * a.ndim)
    params = [prm[n] for n in ("conv_w", "conv_b", "wa", "ba", "wi", "bi", "lam")]
    nb = buf.shape[1]
    o, hl = pl.pallas_call(
        _lru_seq_kernel,
        grid=(bsz, l // tl),
        in_specs=[seq, seq, pl.BlockSpec((1, nb, w), lambda i, j: (i, 0, 0)), vec] + [full(a) for a in params],
        out_specs=[seq, vec],
        out_shape=[jax.ShapeDtypeStruct((bsz, l, w), F32), jax.ShapeDtypeStruct((bsz, 1, w), F32)],
        scratch_shapes=[pltpu.VMEM((nb, 1, w), F32), pltpu.VMEM((1, w), F32), pltpu.VMEM((tl, w), F32),
                        pltpu.VMEM((tl, w), F32)],
        compiler_params=_cparams(("parallel", "arbitrary")),
        name="lru_prompt",
    )(x, gr, buf, h0.reshape(bsz, 1, w), *params)
    return o, hl.reshape(bsz, w)


def _lru_step_kernel(x_ref, x1_ref, x2_ref, x3_ref, gr_ref, h0_ref, cw_ref, cb_ref, wa_ref, ba_ref, wi_ref,
                     bi_ref, lam_ref, o_ref, h_ref):
    a, b = _lru_coeffs(x_ref[...], x1_ref[...], x2_ref[...], x3_ref[...], cw_ref[...], cb_ref[...], wa_ref[...],
                       ba_ref[...], wi_ref[...], bi_ref[...], lam_ref[...])
    h = a * h0_ref[...] + b
    h_ref[...] = h
    o_ref[...] = h * jax.nn.gelu(gr_ref[...])


def lru_step(x, buf, gr, h0, prm):
    m, w = x.shape
    tm = _tile(m, 512)
    rows = pl.BlockSpec((tm, w), lambda i: (i, 0))
    full = lambda a: pl.BlockSpec(a.shape, lambda i: (0,) * a.ndim)
    params = [prm[n] for n in ("conv_w", "conv_b", "wa", "ba", "wi", "bi", "lam")]
    nb = buf.shape[1]
    return pl.pallas_call(
        _lru_step_kernel,
        grid=(m // tm,),
        in_specs=[rows] * 6 + [full(a) for a in params],
        out_specs=[rows] * 2,
        out_shape=[jax.ShapeDtypeStruct((m, w), F32)] * 2,
        compiler_params=_cparams(("parallel",)),
        name="lru_step",
    )(x, buf[:, nb - 1], buf[:, nb - 2], buf[:, nb - 3], gr, h0, *params)


def _block_diag(w):
    n, d, e = w.shape
    eye = jnp.eye(n, dtype=w.dtype)
    return (eye[:, None, :, None] * w[:, :, None, :]).reshape(n * d, n * e)


def _mix_even(h, g_norm, pos0, s_ret, s_rwkv, buf, wts, is_prompt):
    b, l, d = h.shape
    m = b * l
    gw = d // 2
    u = norm_matmul(h.reshape(m, d), g_norm, wts["ab_w_in"], gw)
    seq = lambda t: t.reshape(b, l, gw)
    flat = lambda t: t.reshape(m, gw)
    qa, ka, va, ga = u[:4]
    to_kg = lambda t: t[..., wts["perm"]]
    to_nat = lambda t: t[..., wts["inv"]]
    cur = [seq(t) for t in u[4:8]]
    last = [t[:, l - 1:] for t in cur]
    buf_new = jnp.concatenate([to_nat(last[0]), to_nat(last[1]), last[2], last[3]], axis=-1)
    buf_kg = [to_kg(buf[..., :gw]), to_kg(buf[..., gw:2 * gw]), buf[..., 2 * gw:3 * gw], buf[..., 3 * gw:]]
    if is_prompt:
        r, w, k, v, kk, kka, g, bonus, v_hi, v_lo = rwkv_prep(cur, None, jnp.concatenate(buf_kg, axis=-1),
                                                              wts["rwkv"])
        pos = pos0 + jnp.arange(l, dtype=F32)
        o_a, s_ret_new = retention_prompt(seq(qa), seq(ka), seq(va), seq(ga), pos)
        o_a = flat(o_a)
        o_b, s_rwkv_new = rwkv_scan_prompt(r, w, k, kk, kka, v_hi, v_lo)
    else:
        rows = lambda t: t.reshape(1, m, gw)
        r, w, k, v, kk, kka, g, bonus, _, _ = rwkv_prep([rows(t) for t in cur], [rows(t) for t in buf_kg], None,
                                                        wts["rwkv"])
        o_a, s_ret_new = retention_step(qa, ka, va, ga, s_ret, jnp.float32(pos0))
        o_b, s_rwkv_new = rwkv_step(*(flat(to_nat(t)) for t in (r, w, k, kk, kka)), flat(v), s_rwkv)
    post = (flat(bonus), flat(g), wts["rwkv_ln"], wts["ones_bd"])
    return o_a, flat(o_b), post, s_ret_new, s_rwkv_new, buf_new


def _mix_odd(h, g_norm, lru_h, lru_buf, pages, wts, layer, is_prompt):
    b, l, d = h.shape
    m = b * l
    gw = d // 2
    u = norm_matmul(h.reshape(m, d), g_norm, wts["cd_w_in"], gw)
    seq = lambda t: t.reshape(b, l, gw)
    lam_init = 0.8 - 0.6 * math.exp(-0.3 * layer)
    xr = seq(u[3])
    if is_prompt:
        o_c = diff_attn_prompt(seq(u[0]), seq(u[1]), seq(u[2]), wts["diff_lam"], wts["diff_subln"], lam_init)
        o_c = o_c.reshape(m, gw)
        o_d, h_last = lru_prompt(xr, seq(u[4]), lru_buf, lru_h, wts["lru"])
    else:
        cache_k, cache_v, page_table = pages
        o_c = diff_attn_decode(u[0], u[1], u[2], cache_k, cache_v, page_table, wts["diff_lam"],
                               wts["diff_subln"], lam_init)
        o_d, h_last = lru_step(u[3], lru_buf, u[4], lru_h, wts["lru"])
    buf_new = jnp.concatenate([lru_buf, xr], axis=1)[:, l:]
    n_heads = gw // (2 * HEAD_DIM)
    k_new = u[1].reshape(b, l, n_heads, 2 * HEAD_DIM)
    v_new = u[2].reshape(b, l, n_heads, 2 * HEAD_DIM)
    return o_c, o_d.reshape(m, gw), k_new, v_new, h_last, buf_new


def _trunk(x, p, pos0, s_ret, s_rwkv, s_shift, s_lru_h, s_lru_conv, pages, wts, is_prompt):
    b, l, d = x.shape
    m = b * l
    depth = wts["norm_g"].shape[0]
    h = x.reshape(m, d)
    ret_l, rwkv_l, shift_l, k_l, v_l, lh_l, lc_l = [], [], [], [], [], [], []
    for i in range(depth):
        j = i // 2
        g = wts["norm_g"][i]
        gn = lambda n: g[n:n + 1]
        h = ffn_block(h, gn(0), wts["ffn_in"], wts["ffn_out"], gn(1), i, 0)
        post = None
        if i % 2 == 0:
            o1, o2, post, sr, sw, sb = _mix_even(h.reshape(b, l, d), gn(2), pos0, s_ret[j], s_rwkv[j], s_shift[j],
                                                 wts["even"][j], is_prompt)
            ret_l.append(sr)
            rwkv_l.append(sw)
            shift_l.append(sb)
            w_out = wts["even"][j]["w_out"]
        else:
            pg = None if pages is None else (pages[0][j], pages[1][j], pages[2])
            o1, o2, kn, vn, lh, lc = _mix_odd(h.reshape(b, l, d), gn(2), s_lru_h[j], s_lru_conv[j], pg,
                                              wts["odd"][j], i, is_prompt)
            k_l.append(kn)
            v_l.append(vn)
            lh_l.append(lh)
            lc_l.append(lc)
            w_out = wts["odd"][j]["w_out"]
        h = out_proj(o1, o2, w_out, h, gn(3), post)
        h = ffn_block(h, gn(4), wts["ffn_in"], wts["ffn_out"], gn(5), i, 1)
        h = ple_block(h, gn(6), wts["ple_gate"], p.reshape(depth, m, -1), wts["ple"], gn(7), i)
    st = lambda lst: jnp.stack(lst, axis=0)
    return (h.reshape(b, l, d), st(k_l), st(v_l), st(ret_l), st(rwkv_l), st(shift_l), st(lh_l), st(lc_l))


def kernel(x_prompt, x_sample, cache_k, cache_v, state_ret, state_rwkv, state_rwkv_shift, state_lru_h, state_lru_conv, page_table, p_prompt, p_sample, norm_g, ffn_w_in, ffn_w_out, ple_w, ple_gate_w, ab_w_in, ab_w_out, rwkv_mu, rwkv_w0, rwkv_w1, rwkv_w2, rwkv_a0, rwkv_a1, rwkv_a2, rwkv_g1, rwkv_g2, rwkv_kk, rwkv_ka, rwkv_rk, rwkv_ln, cd_w_in, cd_w_out, diff_lam, diff_subln, lru_conv_w, lru_conv_b, lru_wa, lru_ba, lru_wi, lru_bi, lru_lambda):
    depth = norm_g.shape[0]
    n_a, n_c = state_ret.shape[0], state_lru_h.shape[0]
    bp = x_prompt.shape[0]
    gw = ab_w_out.shape[1] // 2
    bf = lambda t: t.astype(BF16)
    row = lambda t: t.reshape(1, -1)
    ones_bd = _block_diag(jnp.ones((gw // HEAD_DIM, HEAD_DIM, HEAD_DIM), BF16))
    perm = _key_group_perm(gw)
    inv = np.argsort(perm)
    kg = lambda t: t[..., perm]
    head_kg = perm // HEAD_DIM
    ones_kk = jnp.asarray(head_kg[:, None] == head_kg[None, :], BF16)
    ones_rk = jnp.asarray(head_kg[:, None] == (np.arange(gw) // HEAD_DIM)[None, :], BF16)

    def ab_in_kg(w):
        cols = [w[:, g * gw:(g + 1) * gw] for g in range(w.shape[1] // gw)]
        cols[4], cols[5] = kg(cols[4]), kg(cols[5])
        return bf(jnp.concatenate(cols, axis=1))

    wts = {
        "norm_g": norm_g,
        "ffn_in": bf(ffn_w_in), "ffn_out": bf(ffn_w_out), "ple": bf(ple_w), "ple_gate": bf(ple_gate_w),
        "even": [{
            "ab_w_in": ab_in_kg(ab_w_in[j]), "w_out": bf(ab_w_out[j]), "rwkv_ln": rwkv_ln[j],
            "perm": perm, "inv": inv, "ones_bd": ones_bd,
            "rwkv": {"mu": jnp.concatenate([kg(rwkv_mu[j][:2]), rwkv_mu[j][2:]], axis=0),
                     "w0": kg(row(rwkv_w0[j])), "w1": bf(rwkv_w1[j]), "w2": bf(kg(rwkv_w2[j])),
                     "a0": kg(row(rwkv_a0[j])), "a1": bf(rwkv_a1[j]), "a2": bf(kg(rwkv_a2[j])),
                     "g1": bf(rwkv_g1[j]), "g2": bf(rwkv_g2[j]), "kk": kg(row(rwkv_kk[j])),
                     "ka": kg(row(rwkv_ka[j])), "rk": kg(row(rwkv_rk[j])),
                     "ones_kk": ones_kk, "ones_rk": ones_rk},
        } for j in range(n_a)],
        "odd": [{
            "cd_w_in": bf(cd_w_in[j]), "w_out": bf(cd_w_out[j]), "diff_lam": diff_lam[j],
            "diff_subln": row(diff_subln[j]),
            "lru": {"conv_w": lru_conv_w[j], "conv_b": row(lru_conv_b[j]), "wa": bf(_block_diag(lru_wa[j])),
                    "ba": row(lru_ba[j]), "wi": bf(_block_diag(lru_wi[j])), "bi": row(lru_bi[j]),
                    "lam": row(lru_lambda[j])},
        } for j in range(n_c)],
    }
    zeros = lambda *shape: jnp.zeros(shape, F32)
    yp, kp, vp, rp, wp, sp, hp, cp = _trunk(
        x_prompt, p_prompt, 0.0, [None] * n_a, [None] * n_a,
        zeros(n_a, bp, 1, 4 * gw), zeros(n_c, bp, gw), zeros(n_c, bp, CONV_W - 1, gw),
        None, wts, True)
    past_len = page_table.shape[1] * cache_k.shape[2]
    n_pool, page = cache_k.shape[1], cache_k.shape[2]
    as_rows = lambda c: c.reshape(n_c, n_pool, page * c.shape[3], c.shape[4])
    pages = (as_rows(cache_k), as_rows(cache_v), page_table)
    ys, ks_, vs, rs, ws, ss, hs, cs = _trunk(
        x_sample, p_sample, float(past_len), state_ret, state_rwkv, state_rwkv_shift,
        state_lru_h, state_lru_conv, pages, wts, False)
    return (yp, ys, kp, vp, rp, wp, sp, hp, cp, ks_, vs, rs, ws, ss, hs, cs)
```

```python
import functools
import math

import jax
import jax.numpy as jnp
import numpy as np
from jax import lax
from jax.experimental import pallas as pl
from jax.experimental.pallas import tpu as pltpu

F32 = jnp.float32
BF16 = jnp.bfloat16

HEAD_DIM = 64
CONV_W = 4
LRU_C = 8.0
ROPE_BASE = 10000.0
EPS = 1e-6
RWKV_GN_EPS = 64e-5
RET_CHUNK = 256
ATTN_BLOCK = 512
SCAN_CHUNK = 128
DECODE_PAGES = 16
SCAN_GROUP = 8
LANES = 128
SUBLANES = 8
VMEM_LIMIT = 48 * 1024 * 1024


def _cparams(sem):
    return pltpu.CompilerParams(dimension_semantics=sem, vmem_limit_bytes=VMEM_LIMIT)


def _tile(n, pref):
    t = min(n, pref)
    while n % t:
        t //= 2
    return t


def _rms(x, g):
    return x * lax.rsqrt(jnp.mean(x * x, axis=-1, keepdims=True) + EPS) * g


def _dot(a, b):
    return jnp.dot(a.astype(BF16), b.astype(BF16), preferred_element_type=F32)


def _seg_sum(x, ones_bd):
    hi = x.astype(BF16)
    lo = (x - hi.astype(F32)).astype(BF16)
    return (jnp.dot(hi, ones_bd, preferred_element_type=F32)
            + jnp.dot(lo, ones_bd, preferred_element_type=F32))


def _ffn_kernel(h_ref, gpre_ref, wg_ref, wu_ref, wo_ref, gpost_ref, o_ref, xn_ref, acc_ref):
    j = pl.program_id(1)

    @pl.when(j == 0)
    def _():
        xn_ref[...] = _rms(h_ref[...], gpre_ref[...]).astype(BF16)
        acc_ref[...] = jnp.zeros_like(acc_ref)

    xn = xn_ref[...]
    gate = jnp.dot(xn, wg_ref[...], preferred_element_type=F32)
    up = jnp.dot(xn, wu_ref[...], preferred_element_type=F32)
    act = (gate * jax.nn.sigmoid(gate) * up).astype(BF16)
    acc_ref[...] += jnp.dot(act, wo_ref[...], preferred_element_type=F32)

    @pl.when(j == pl.num_programs(1) - 1)
    def _():
        o_ref[...] = h_ref[...] + 0.5 * _rms(acc_ref[...], gpost_ref[...])


def ffn_block(h, g_pre, w_in, w_out, g_post, layer, half):
    m, d = h.shape
    f = w_out.shape[2]
    tm, tf = _tile(m, 1024), _tile(f, 512)
    nf = f // tf
    return pl.pallas_call(
        _ffn_kernel,
        grid=(m // tm, nf),
        in_specs=[
            pl.BlockSpec((tm, d), lambda i, j: (i, 0)),
            pl.BlockSpec((1, d), lambda i, j: (0, 0)),
            pl.BlockSpec((None, None, d, tf), lambda i, j: (layer, half, 0, j)),
            pl.BlockSpec((None, None, d, tf), lambda i, j: (layer, half, 0, j + nf)),
            pl.BlockSpec((None, None, tf, d), lambda i, j: (layer, half, j, 0)),
            pl.BlockSpec((1, d), lambda i, j: (0, 0)),
        ],
        out_specs=pl.BlockSpec((tm, d), lambda i, j: (i, 0)),
        out_shape=jax.ShapeDtypeStruct((m, d), F32),
        scratch_shapes=[pltpu.VMEM((tm, d), BF16), pltpu.VMEM((tm, d), F32)],
        compiler_params=_cparams(("parallel", "arbitrary")),
        name="ffn_block",
    )(h, g_pre, w_in, w_in, w_out, g_post)


def _norm_matmul_kernel(h_ref, g_ref, w_ref, *o_refs, n_groups, head_major):
    xn = _rms(h_ref[...], g_ref[...]).astype(BF16)
    tm, tn = o_refs[0].shape
    heads = tn // LANES
    for gi in range(n_groups):
        res = jnp.dot(xn, w_ref[:, gi * tn:(gi + 1) * tn], preferred_element_type=F32)
        o_refs[gi][...] = res
        if gi in head_major:
            hm_ref = o_refs[n_groups + head_major.index(gi)]
            for hh in range(heads):
                hm_ref[pl.ds(hh, tm, stride=heads), :] = res[:, hh * LANES:(hh + 1) * LANES]


def norm_matmul(h, g, w, tn, head_major=()):
    m, d = h.shape
    n = w.shape[1]
    tm = _tile(m, 512)
    heads = tn // LANES
    rows = pl.BlockSpec((tm, tn), lambda i: (i, 0))
    return pl.pallas_call(
        functools.partial(_norm_matmul_kernel, n_groups=n // tn, head_major=tuple(head_major)),
        grid=(m // tm,),
        in_specs=[
            pl.BlockSpec((tm, d), lambda i: (i, 0)),
            pl.BlockSpec((1, d), lambda i: (0, 0)),
            pl.BlockSpec((d, n), lambda i: (0, 0)),
        ],
        out_specs=[rows] * (n // tn) + [pl.BlockSpec((tm * heads, LANES), lambda i: (i, 0))] * len(head_major),
        out_shape=([jax.ShapeDtypeStruct((m, tn), F32)] * (n // tn)
                   + [jax.ShapeDtypeStruct((m * heads, LANES), F32)] * len(head_major)),
        compiler_params=_cparams(("parallel",)),
        name="norm_matmul",
    )(h, g, w)


def _out_proj_kernel(oa_ref, ob_ref, wa_ref, wb_ref, h_ref, g_ref, *rest):
    ob = ob_ref[...]
    if len(rest) > 1:
        bonus_ref, gate_ref, ln_ref, ones_ref = rest[:4]
        ones_bd = ones_ref[...]
        inv = 1.0 / HEAD_DIM
        oc = ob - _seg_sum(ob, ones_bd) * inv
        on = oc * lax.rsqrt(_seg_sum(oc * oc, ones_bd) * inv + RWKV_GN_EPS)
        ln = ln_ref[...]
        ob = (on * ln[0:1] + ln[1:2] + bonus_ref[...]) * gate_ref[...]
    o_ref = rest[-1]
    y = _dot(oa_ref[...], wa_ref[...]) + _dot(ob, wb_ref[...])
    o_ref[...] = h_ref[...] + _rms(y, g_ref[...])


def out_proj(oa, ob, w, h, g, rwkv_post=None):
    m, d = h.shape
    gw = oa.shape[1]
    tm = _tile(m, 512)
    rows = pl.BlockSpec((tm, gw), lambda i: (i, 0))
    extra, extra_specs = [], []
    if rwkv_post is not None:
        bonus, gate, ln, ones_bd = rwkv_post
        extra = [bonus, gate, ln, ones_bd]
        extra_specs = [rows, rows, pl.BlockSpec(ln.shape, lambda i: (0, 0)),
                       pl.BlockSpec(ones_bd.shape, lambda i: (0, 0))]
    return pl.pallas_call(
        _out_proj_kernel,
        grid=(m // tm,),
        in_specs=[
            rows, rows,
            pl.BlockSpec((gw, d), lambda i: (0, 0)),
            pl.BlockSpec((gw, d), lambda i: (1, 0)),
            pl.BlockSpec((tm, d), lambda i: (i, 0)),
            pl.BlockSpec((1, d), lambda i: (0, 0)),
        ] + extra_specs,
        out_specs=pl.BlockSpec((tm, d), lambda i: (i, 0)),
        out_shape=jax.ShapeDtypeStruct((m, d), F32),
        compiler_params=_cparams(("parallel",)),
        name="out_proj",
    )(oa, ob, w, w, h, g, *extra)


def _ple_kernel(h_ref, g6_ref, wg_ref, p_ref, wp_ref, g7_ref, o_ref):
    h = h_ref[...]
    gate = jax.nn.sigmoid(_dot(_rms(h, g6_ref[...]), wg_ref[...]))
    y = gate * _dot(p_ref[...], wp_ref[...])
    o_ref[...] = h + _rms(y, g7_ref[...])


def ple_block(h, g6, wg, p, wp, g7, layer):
    m, d = h.shape
    pd = p.shape[2]
    tm = _tile(m, 512)
    return pl.pallas_call(
        _ple_kernel,
        grid=(m // tm,),
        in_specs=[
            pl.BlockSpec((tm, d), lambda i: (i, 0)),
            pl.BlockSpec((1, d), lambda i: (0, 0)),
            pl.BlockSpec((None, d, d), lambda i: (layer, 0, 0)),
            pl.BlockSpec((None, tm, pd), lambda i: (layer, i, 0)),
            pl.BlockSpec((None, pd, d), lambda i: (layer, 0, 0)),
            pl.BlockSpec((1, d), lambda i: (0, 0)),
        ],
        out_specs=pl.BlockSpec((tm, d), lambda i: (i, 0)),
        out_shape=jax.ShapeDtypeStruct((m, d), F32),
        compiler_params=_cparams(("parallel",)),
        name="ple_block",
    )(h, g6, wg, p, wp, g7)


def _retention_tables(n_heads, c, pos):
    lg = jnp.log1p(-jnp.exp2(-5.0 - jnp.arange(n_heads, dtype=F32)))
    idx = jnp.arange(c, dtype=F32)
    rel = idx[:, None] - idx[None, :]
    dmask = jnp.where(rel[None] >= 0, jnp.exp(jnp.maximum(rel, 0.0)[None] * lg[:, None, None]), 0.0)
    rep = lambda t: jnp.repeat(t, HEAD_DIM, axis=-1)
    q_dec = rep(jnp.exp((idx[:, None] + 1.0) * lg[None, :]))
    k_dec = rep(jnp.exp((c - 1.0 - idx[:, None]) * lg[None, :]))
    c_dec = rep(jnp.exp(c * lg)[None, :])
    half = HEAD_DIM // 2
    freq = 1.0 / (ROPE_BASE ** jnp.linspace(0.0, 1.0, half, dtype=F32))
    ang = pos[:, None] * freq[None, :]
    cos, sin = jnp.cos(ang), jnp.sin(ang)
    cos_t = jnp.tile(jnp.concatenate([cos, cos], axis=-1), (1, n_heads))
    sin_t = jnp.tile(jnp.concatenate([-sin, sin], axis=-1), (1, n_heads))
    return dmask, q_dec, k_dec, c_dec, cos_t, sin_t


def _retention_kernel(q_ref, k_ref, v_ref, g_ref, cos_ref, sin_ref, dmask_ref, qdec_ref, kdec_ref,
                      cdec_ref, ones_ref, o_ref, s_ref, s_scr, *, n_heads):
    c = pl.program_id(1)

    @pl.when(c == 0)
    def _():
        s_scr[...] = jnp.zeros_like(s_scr)

    q, k, v, g = q_ref[0], k_ref[0], v_ref[0], g_ref[0]
    cos, sin = cos_ref[...], sin_ref[...]
    rows, width = q.shape
    lane = lax.broadcasted_iota(jnp.int32, q.shape, 1)
    first_half = (lane % HEAD_DIM) < (HEAD_DIM // 2)

    def rot(x):
        swapped = jnp.where(first_half, pltpu.roll(x, width - HEAD_DIM // 2, 1),
                            pltpu.roll(x, HEAD_DIM // 2, 1))
        return x * cos + swapped * sin

    qr = rot(q)
    kr = rot(k) * (HEAD_DIM ** -0.5)
    kd = kr * kdec_ref[...]
    qdec = qdec_ref[...]
    cdec = cdec_ref[...]
    head_a = lax.broadcasted_iota(jnp.int32, (rows, LANES), 1) < HEAD_DIM
    same_head = (lax.broadcasted_iota(jnp.int32, (LANES, LANES), 0) // HEAD_DIM
                 == lax.broadcasted_iota(jnp.int32, (LANES, LANES), 1) // HEAD_DIM)
    nt = (((1,), (1,)), ((), ()))
    outs = []
    for p in range(n_heads // 2):
        sl = slice(p * LANES, (p + 1) * LANES)
        qp = qr[:, sl].astype(BF16)
        kp = kr[:, sl].astype(BF16)
        vp = v[:, sl].astype(BF16)
        zero = jnp.zeros_like(qp)
        att_a = lax.dot_general(jnp.where(head_a, qp, zero), kp, nt, preferred_element_type=F32) * dmask_ref[2 * p]
        att_b = lax.dot_general(jnp.where(head_a, zero, qp), kp, nt,
                                preferred_element_type=F32) * dmask_ref[2 * p + 1]
        s_old = s_scr[p]
        inner = jnp.where(head_a, jnp.dot(att_a.astype(BF16), vp, preferred_element_type=F32),
                          jnp.dot(att_b.astype(BF16), vp, preferred_element_type=F32))
        outs.append(inner + jnp.dot(qp, s_old.astype(BF16), preferred_element_type=F32) * qdec[:, sl])
        update = lax.dot_general(kd[:, sl].astype(BF16), vp, (((0,), (0,)), ((), ())),
                                 preferred_element_type=F32)
        s_scr[p] = jnp.where(same_head, s_old * cdec[:, sl] + update, 0.0)
    o = jnp.concatenate(outs, axis=1)
    ones_bd = ones_ref[...]
    inv = 1.0 / HEAD_DIM
    oc = o - _seg_sum(o, ones_bd) * inv
    on = oc * lax.rsqrt(_seg_sum(oc * oc, ones_bd) * inv + EPS)
    o_ref[0] = on * (g * jax.nn.sigmoid(g))

    @pl.when(c == pl.num_programs(1) - 1)
    def _():
        for p in range(n_heads // 2):
            s_pair = s_scr[p]
            s_ref[0, 2 * p] = s_pair[:HEAD_DIM, :HEAD_DIM]
            s_ref[0, 2 * p + 1] = s_pair[HEAD_DIM:, HEAD_DIM:]


def retention_prompt(q, k, v, g, pos, ones_bd):
    b, l, width = q.shape
    n_heads = width // HEAD_DIM
    c = _tile(l, RET_CHUNK)
    dmask, q_dec, k_dec, c_dec, cos_t, sin_t = _retention_tables(n_heads, c, pos)
    seq = pl.BlockSpec((1, c, width), lambda i, j: (i, j, 0))
    tab = pl.BlockSpec((c, width), lambda i, j: (j, 0))
    fixed = lambda shape: pl.BlockSpec(shape, lambda i, j: (0,) * len(shape))
    return pl.pallas_call(
        functools.partial(_retention_kernel, n_heads=n_heads),
        grid=(b, l // c),
        in_specs=[seq, seq, seq, seq, tab, tab, fixed((n_heads, c, c)), fixed((c, width)),
                  fixed((c, width)), fixed((1, width)), fixed(ones_bd.shape)],
        out_specs=[seq, pl.BlockSpec((1, n_heads, HEAD_DIM, HEAD_DIM), lambda i, j: (i, 0, 0, 0))],
        out_shape=[jax.ShapeDtypeStruct((b, l, width), F32),
                   jax.ShapeDtypeStruct((b, n_heads, HEAD_DIM, HEAD_DIM), F32)],
        scratch_shapes=[pltpu.VMEM((n_heads // 2, LANES, LANES), F32)],
        compiler_params=_cparams(("parallel", "arbitrary")),
        name="retention_prompt",
    )(q, k, v, g, cos_t, sin_t, dmask, q_dec, k_dec, c_dec, ones_bd)


def _eye_mask():
    return (lax.broadcasted_iota(jnp.int32, (HEAD_DIM, HEAD_DIM), 0)
            == lax.broadcasted_iota(jnp.int32, (HEAD_DIM, HEAD_DIM), 1))


def _to_col(x_row):
    return jnp.sum(jnp.where(_eye_mask(), x_row, 0.0), axis=-1, keepdims=True)


def _to_row(x_col):
    return jnp.sum(jnp.where(_eye_mask(), x_col, 0.0), axis=-2, keepdims=True)


def _retention_step_kernel(q_ref, k_ref, v_ref, g_ref, cos_ref, sin_ref, gam_ref, s0_ref, o_ref, s_ref):
    cos, sin = cos_ref[...], sin_ref[...]

    def rot(x):
        half = HEAD_DIM // 2
        return x * cos + jnp.concatenate([x[..., half:], x[..., :half]], axis=-1) * sin

    q = _to_col(rot(q_ref[...]))
    k = _to_col(rot(k_ref[...]) * (HEAD_DIM ** -0.5))
    v, g = v_ref[...], g_ref[...]
    gam = gam_ref[...]
    s0 = s0_ref[...]
    att = jnp.sum(q * k, axis=2, keepdims=True)
    o = att * v + jnp.sum(q * s0, axis=2, keepdims=True) * gam
    s_ref[...] = s0 * gam + k * v
    oc = o - jnp.mean(o, axis=-1, keepdims=True)
    on = oc * lax.rsqrt(jnp.mean(oc * oc, axis=-1, keepdims=True) + EPS)
    o_ref[...] = on * (g * jax.nn.sigmoid(g))


def retention_step(q, k, v, g, s0, pos):
    b, width = q.shape
    n_heads = width // HEAD_DIM
    bb = _tile(b, 8)
    half = HEAD_DIM // 2
    freq = 1.0 / (ROPE_BASE ** jnp.linspace(0.0, 1.0, half, dtype=F32))
    ang = pos * freq
    cos_c = jnp.concatenate([jnp.cos(ang), jnp.cos(ang)])[None, :]
    sin_c = jnp.concatenate([-jnp.sin(ang), jnp.sin(ang)])[None, :]
    gam = jnp.exp(jnp.log1p(-jnp.exp2(-5.0 - jnp.arange(n_heads, dtype=F32)))).reshape(n_heads, 1, 1)
    row = lambda t: t.reshape(b, n_heads, 1, HEAD_DIM)
    rspec = pl.BlockSpec((bb, n_heads, 1, HEAD_DIM), lambda i: (i, 0, 0, 0))
    sspec = pl.BlockSpec((bb, n_heads, HEAD_DIM, HEAD_DIM), lambda i: (i, 0, 0, 0))
    o, s = pl.pallas_call(
        _retention_step_kernel,
        grid=(b // bb,),
        in_specs=[rspec, rspec, rspec, rspec,
                  pl.BlockSpec((1, HEAD_DIM), lambda i: (0, 0)), pl.BlockSpec((1, HEAD_DIM), lambda i: (0, 0)),
                  pl.BlockSpec((n_heads, 1, 1), lambda i: (0, 0, 0)), sspec],
        out_specs=[rspec, sspec],
        out_shape=[jax.ShapeDtypeStruct((b, n_heads, 1, HEAD_DIM), F32),
                   jax.ShapeDtypeStruct((b, n_heads, HEAD_DIM, HEAD_DIM), F32)],
        compiler_params=_cparams(("parallel",)),
        name="retention_step",
    )(row(q), row(k), row(v), row(g), cos_c, sin_c, gam, s0)
    return o.reshape(b, width), s


def _softplus(x):
    return jnp.maximum(x, 0.0) + jnp.log1p(jnp.exp(-jnp.abs(x)))


def _shifted(cur, carry_row):
    if cur.shape[0] == 1:
        return carry_row
    first = lax.broadcasted_iota(jnp.int32, cur.shape, 0) == 0
    return jnp.where(first, carry_row, pltpu.roll(cur, 1, 0))


PREP_PARAMS = ("mu", "w0", "w1", "w2", "a0", "a1", "a2", "g1", "g2", "kk", "ka", "rk", "ones_kk", "ones_rk")
N_PREP_PARAMS = len(PREP_PARAMS)
N_PREP_OUT = 8


def _rwkv_prep_kernel(*refs, shift_in_kernel):
    cur_refs = refs[:4]
    if shift_in_kernel:
        buf_ref = refs[4]
        n_in = 5
    else:
        prev_refs = refs[4:8]
        n_in = 8
    (mu_ref, w0_ref, w1_ref, w2_ref, a0_ref, a1_ref, a2_ref, g1_ref, g2_ref, kkp_ref, kap_ref, rk_ref,
     ones_kk_ref, ones_rk_ref) = refs[n_in:n_in + N_PREP_PARAMS]
    n_in += N_PREP_PARAMS
    r_out, w_out, k_out, v_out, kk_out, kka_out, g_out, bonus_out = refs[n_in:n_in + N_PREP_OUT]
    cur = [ref[0] for ref in cur_refs]
    width = cur[0].shape[1]
    if shift_in_kernel:
        carry = refs[n_in + N_PREP_OUT]

        @pl.when(pl.program_id(1) == 0)
        def _():
            for gi in range(4):
                carry[gi] = buf_ref[0, :, gi * width:(gi + 1) * width]

        prev = [_shifted(x, carry[gi]) for gi, x in enumerate(cur)]
        for gi, x in enumerate(cur):
            carry[gi] = x[x.shape[0] - 1:, :]
    else:
        prev = [ref[0] for ref in prev_refs]
    mu = mu_ref[...]
    lerp = lambda x, xp, i: x + (xp - x) * mu[i:i + 1]
    zr, pz = cur[3], prev[3]
    r = lerp(cur[0], prev[0], 0)
    kx = lerp(cur[1], prev[1], 1)
    vx = lerp(cur[2], prev[2], 2)
    zw, za, zg = lerp(zr, pz, 3), lerp(zr, pz, 4), lerp(zr, pz, 5)
    wpre = w0_ref[...] + _dot(jnp.tanh(_dot(zw, w1_ref[...])), w2_ref[...])
    decay = jnp.exp(-jnp.exp(-_softplus(-wpre) - 0.5))
    a = jax.nn.sigmoid(a0_ref[...] + _dot(_dot(za, a1_ref[...]), a2_ref[...]))
    g = _dot(jax.nn.sigmoid(_dot(zg, g1_ref[...])), g2_ref[...])
    kk = kx * kkp_ref[...]
    kk = kk / jnp.maximum(jnp.sqrt(_seg_sum(kk * kk, ones_kk_ref[...])), 1e-12)
    k32 = kx * (1.0 + (a - 1.0) * kap_ref[...])
    r_out[0] = r
    w_out[0] = decay
    k_out[0] = k32
    v_out[0] = vx
    kk_out[0] = kk
    kka_out[0] = kk * a
    g_out[0] = g
    bonus_out[0] = _seg_sum(r * k32 * rk_ref[...], ones_rk_ref[...]) * vx


def rwkv_prep(cur, prev, buf, prm):
    b, l, w = cur[0].shape
    tl = _tile(l, 256)
    seq = pl.BlockSpec((1, tl, w), lambda i, j: (i, j, 0))
    full = lambda a: pl.BlockSpec(a.shape, lambda i, j: (0,) * a.ndim)
    params = [prm[n] for n in PREP_PARAMS]
    shift = prev is None
    if shift:
        extra, extra_specs = [buf], [pl.BlockSpec((1, 1, 4 * w), lambda i, j: (i, 0, 0))]
        scratch = [pltpu.VMEM((4, 1, w), F32)]
    else:
        extra, extra_specs, scratch = list(prev), [seq] * 4, []
    return pl.pallas_call(
        functools.partial(_rwkv_prep_kernel, shift_in_kernel=shift),
        grid=(b, l // tl),
        in_specs=[seq] * 4 + extra_specs + [full(a) for a in params],
        out_specs=[seq] * N_PREP_OUT,
        out_shape=[jax.ShapeDtypeStruct((b, l, w), F32)] * N_PREP_OUT,
        scratch_shapes=scratch,
        compiler_params=_cparams(("parallel", "arbitrary")),
        name="rwkv_prep",
    )(*cur, *extra, *params)


def _hi_lo(x):
    hi = x.astype(BF16)
    lo = (x - hi.astype(F32)).astype(BF16)
    return jnp.concatenate([hi, lo], axis=1)


def _value_columns(v8):
    hi = v8.astype(BF16).astype(F32)
    lo = (v8 - hi).astype(BF16).astype(F32)
    stacked = jnp.concatenate([part[:, p * LANES:(p + 1) * LANES]
                               for p in range(v8.shape[1] // LANES) for part in (hi, lo)], axis=0)
    cols = stacked.T
    return jnp.concatenate([cols[:HEAD_DIM], cols[HEAD_DIM:]], axis=1).astype(BF16)


def _rwkv_scan_kernel(r_ref, w_ref, k_ref, kk_ref, kka_ref, v_ref, sel_ref, ones_ref, o_ref, s_ref, s_scr,
                      ot_scr, *, n_heads):
    c = pl.program_id(1)
    n_grp, tc = r_ref.shape[0], r_ref.shape[1]
    head_lanes = LANES // n_heads
    n_kg = HEAD_DIM // head_lanes

    @pl.when(c == 0)
    def _():
        s_scr[...] = jnp.zeros_like(s_scr)

    ot_scr[...] = jnp.zeros_like(ot_scr)
    ones2 = ones_ref[...]
    t_lane = lax.broadcasted_iota(jnp.int32, (n_grp * HEAD_DIM, LANES), 1) % head_lanes
    rows_of = lambda x, g: x[g * HEAD_DIM:(g + 1) * HEAD_DIM]

    def block(tb, carry):
        base = pl.multiple_of(tb * SUBLANES, SUBLANES)
        tiles = {name: [[ref[g, pl.ds(base, SUBLANES), kg * LANES:(kg + 1) * LANES] for kg in range(n_kg)]
                        for g in range(n_grp)]
                 for name, ref in (("kk", kk_ref), ("w", w_ref), ("kka", kka_ref), ("k", k_ref), ("r", r_ref))}
        vp = jnp.concatenate([_value_columns(v_ref[g, pl.ds(base, SUBLANES), :]) for g in range(n_grp)], axis=0)
        s = [[s_scr[g, kg] for kg in range(n_kg)] for g in range(n_grp)]
        acc = jnp.zeros((n_grp * HEAD_DIM, LANES), F32)
        for i in range(SUBLANES):
            row = lambda name, g, kg: jnp.broadcast_to(tiles[name][g][kg][i:i + 1, :], (HEAD_DIM, LANES))
            key_sum = lambda name: jnp.dot(_hi_lo(jnp.concatenate(
                [sum(s[g][kg] * row(name, g, kg) for kg in range(n_kg)) for g in range(n_grp)], axis=0)),
                ones2, preferred_element_type=F32)
            sa = key_sum("kk")
            vcol = jnp.dot(vp, sel_ref[i], preferred_element_type=F32)
            for g in range(n_grp):
                sa_g, vcol_g = rows_of(sa, g), rows_of(vcol, g)
                for kg in range(n_kg):
                    s[g][kg] = (s[g][kg] * row("w", g, kg) - sa_g * row("kka", g, kg)
                                + vcol_g * row("k", g, kg))
            acc = jnp.where(t_lane == (base + i) % head_lanes, key_sum("r"), acc)
        for g in range(n_grp):
            for kg in range(n_kg):
                s_scr[g, kg] = s[g][kg]
        ot_scr[base // head_lanes] += acc
        return carry

    lax.fori_loop(0, tc // SUBLANES, block, 0)

    for tile in range(tc // head_lanes):
        for g in range(n_grp):
            o_t = rows_of(ot_scr[tile], g).T
            for h in range(n_heads):
                o_ref[g, tile * head_lanes:(tile + 1) * head_lanes, h * HEAD_DIM:(h + 1) * HEAD_DIM] = (
                    o_t[h * head_lanes:(h + 1) * head_lanes, :])

    @pl.when(c == pl.num_programs(1) - 1)
    def _():
        s_ref[...] = s_scr[...]


def _key_group_perm(width):
    n_heads = width // HEAD_DIM
    head_lanes = LANES // n_heads
    n = np.arange(width)
    return (n % LANES) // head_lanes * HEAD_DIM + n // LANES * head_lanes + n % head_lanes


def _to_key_group(t):
    width = t.shape[-1]
    n_heads = width // HEAD_DIM
    head_lanes = LANES // n_heads
    split = t.reshape(*t.shape[:-1], n_heads, HEAD_DIM // head_lanes, head_lanes)
    return jnp.swapaxes(split, -3, -2).reshape(t.shape)


def _from_key_group(t):
    width = t.shape[-1]
    n_heads = width // HEAD_DIM
    head_lanes = LANES // n_heads
    split = t.reshape(*t.shape[:-1], HEAD_DIM // head_lanes, n_heads, head_lanes)
    return jnp.swapaxes(split, -3, -2).reshape(t.shape)


def rwkv_scan_prompt(r, w, k, kk, kka, v):
    b, l, width = r.shape
    n_heads = width // HEAD_DIM
    head_lanes = LANES // n_heads
    n_kg = HEAD_DIM // head_lanes
    tc = _tile(l, SCAN_CHUNK)
    grp = _tile(b, SCAN_GROUP)
    kl = np.arange(LANES)
    col_head = 2 * ((kl % HEAD_DIM) // (2 * SUBLANES)) + kl // HEAD_DIM
    sel = ((kl[None, :, None] % SUBLANES == np.arange(SUBLANES)[:, None, None])
           & (col_head[None, :, None] == kl[None, None, :] // head_lanes))
    k2 = np.arange(2 * LANES)
    ones2 = (k2[:, None] % LANES) // head_lanes == kl[None, :] // head_lanes
    sel, ones2 = jnp.asarray(sel, BF16), jnp.asarray(ones2, BF16)
    seq = pl.BlockSpec((grp, tc, width), lambda i, j: (i, j, 0))
    state = pl.BlockSpec((grp, n_kg, HEAD_DIM, LANES), lambda i, j: (i, 0, 0, 0))
    o, s = pl.pallas_call(
        functools.partial(_rwkv_scan_kernel, n_heads=n_heads),
        grid=(b // grp, l // tc),
        in_specs=[seq, seq, seq, seq, seq, seq,
                  pl.BlockSpec(sel.shape, lambda i, j: (0, 0, 0)),
                  pl.BlockSpec(ones2.shape, lambda i, j: (0, 0))],
        out_specs=[seq, state],
        out_shape=[jax.ShapeDtypeStruct((b, l, width), F32),
                   jax.ShapeDtypeStruct((b, n_kg, HEAD_DIM, LANES), F32)],
        scratch_shapes=[pltpu.VMEM((grp, n_kg, HEAD_DIM, LANES), F32),
                        pltpu.VMEM((tc // head_lanes, grp * HEAD_DIM, LANES), F32)],
        compiler_params=_cparams(("parallel", "arbitrary")),
        name="rwkv_scan_prompt",
    )(r, w, k, kk, kka, v, sel, ones2)
    s = s.reshape(b, n_kg, HEAD_DIM, n_heads, head_lanes).transpose(0, 3, 2, 1, 4)
    return o, s.reshape(b, n_heads, HEAD_DIM, HEAD_DIM)


def _rwkv_step_kernel(r_ref, w_ref, k_ref, kk_ref, kka_ref, v_ref, s0_ref, o_ref, s_ref):
    s0 = s0_ref[...]
    sa = -jnp.sum(s0 * kk_ref[...], axis=-1, keepdims=True)
    s = s0 * w_ref[...] + sa * kka_ref[...] + _to_col(v_ref[...]) * k_ref[...]
    s_ref[...] = s
    o_ref[...] = _to_row(jnp.sum(s * r_ref[...], axis=-1, keepdims=True))


def rwkv_step(r, w, k, kk, kka, v, s0):
    b, width = r.shape
    n_heads = width // HEAD_DIM
    bb = _tile(b, 8)
    row = lambda t: t.reshape(b, n_heads, 1, HEAD_DIM)
    rspec = pl.BlockSpec((bb, n_heads, 1, HEAD_DIM), lambda i: (i, 0, 0, 0))
    sspec = pl.BlockSpec((bb, n_heads, HEAD_DIM, HEAD_DIM), lambda i: (i, 0, 0, 0))
    o, s = pl.pallas_call(
        _rwkv_step_kernel,
        grid=(b // bb,),
        in_specs=[rspec] * 6 + [sspec],
        out_specs=[rspec, sspec],
        out_shape=[jax.ShapeDtypeStruct((b, n_heads, 1, HEAD_DIM), F32),
                   jax.ShapeDtypeStruct((b, n_heads, HEAD_DIM, HEAD_DIM), F32)],
        compiler_params=_cparams(("parallel",)),
        name="rwkv_step",
    )(row(r), row(w), row(k), row(kk), row(kka), row(v), s0)
    return o.reshape(b, width), s


def _diff_lambda(lp, lam_init):
    e1 = jnp.exp(jnp.sum(lp[0:1] * lp[1:2], axis=-1, keepdims=True))
    e2 = jnp.exp(jnp.sum(lp[2:3] * lp[3:4], axis=-1, keepdims=True))
    return e1 - e2 + lam_init


def _diff_attn_kernel(q_ref, k_ref, v_ref, lam_ref, subln_ref, o_ref, m_scr, l_scr, acc_scr, *, lam_init):
    i, j = pl.program_id(2), pl.program_id(3)
    tq, tk = q_ref.shape[1], k_ref.shape[1]
    scale = HEAD_DIM ** -0.5

    @pl.when(j == 0)
    def _():
        m_scr[...] = jnp.full_like(m_scr, -jnp.inf)
        l_scr[...] = jnp.zeros_like(l_scr)
        acc_scr[...] = jnp.zeros_like(acc_scr)

    def update(on_diagonal):
        q = (q_ref[0] * scale).astype(BF16)
        k, v = k_ref[0].astype(BF16), v_ref[0].astype(BF16)
        if on_diagonal:
            visible = (lax.broadcasted_iota(jnp.int32, (tq, tk), 1)
                       <= lax.broadcasted_iota(jnp.int32, (tq, tk), 0))
        for mi in range(2):
            sl = slice(mi * HEAD_DIM, (mi + 1) * HEAD_DIM)
            s = lax.dot_general(q[:, sl], k[:, sl], (((1,), (1,)), ((), ())), preferred_element_type=F32)
            if on_diagonal:
                s = jnp.where(visible, s, -jnp.inf)
            m_old = m_scr[mi]
            m_new = jnp.maximum(m_old, jnp.max(s, axis=-1, keepdims=True))
            alpha = jnp.exp(m_old - m_new)
            p = jnp.exp(s - jnp.concatenate([m_new] * (tk // LANES), axis=1))
            l_scr[mi] = alpha * l_scr[mi] + jnp.sum(p, axis=-1, keepdims=True)
            acc_scr[mi] = alpha * acc_scr[mi] + jnp.dot(p.astype(BF16), v, preferred_element_type=F32)
            m_scr[mi] = m_new

    @pl.when(j < i)
    def _():
        update(False)

    @pl.when(j == i)
    def _():
        update(True)
        lam = _diff_lambda(lam_ref[...], lam_init)
        o = acc_scr[0] / l_scr[0] - lam * (acc_scr[1] / l_scr[1])
        o_ref[0] = _rms(o, subln_ref[...]) * (1.0 - lam_init)


def diff_attn_prompt(q, k, v, lam_p, subln, lam_init):
    b, l, width = q.shape
    dv = 2 * HEAD_DIM
    n_heads = width // dv
    t = _tile(l, ATTN_BLOCK)
    n = l // t
    qspec = pl.BlockSpec((1, t, dv), lambda bi, h, i, j: (bi, i, h))
    kspec = pl.BlockSpec((1, t, dv), lambda bi, h, i, j: (bi, jnp.minimum(i, j), h))
    return pl.pallas_call(
        functools.partial(_diff_attn_kernel, lam_init=lam_init),
        grid=(b, n_heads, n, n),
        in_specs=[qspec, kspec, kspec,
                  pl.BlockSpec(lam_p.shape, lambda bi, h, i, j: (0, 0)),
                  pl.BlockSpec(subln.shape, lambda bi, h, i, j: (0, 0))],
        out_specs=qspec,
        out_shape=jax.ShapeDtypeStruct((b, l, width), F32),
        scratch_shapes=[pltpu.VMEM((2, t, LANES), F32), pltpu.VMEM((2, t, LANES), F32), pltpu.VMEM((2, t, dv), F32)],
        compiler_params=_cparams(("parallel", "parallel", "parallel", "arbitrary")),
        name="diff_attn_prompt",
    )(q, k, v, lam_p, subln)


def _diff_attn_decode_kernel(pt_ref, q_ref, kn_ref, vn_ref, *rest, lam_init, n_heads, n_slots):
    kc_refs, vc_refs = rest[:n_slots], rest[n_slots:2 * n_slots]
    lam_ref, subln_ref, o_ref, m_scr, l_scr, acc_scr = rest[2 * n_slots:]
    p = pl.program_id(1)
    n_rows = 2 * n_heads
    dv = 2 * HEAD_DIM
    scale = HEAD_DIM ** -0.5
    row = lax.broadcasted_iota(jnp.int32, (n_rows, dv), 0)
    lane = lax.broadcasted_iota(jnp.int32, (n_rows, dv), 1)
    qmat = jnp.where(lane // HEAD_DIM == row % 2, q_ref[0], 0.0)
    cols = n_slots * kc_refs[0].shape[1]
    col_head = lax.broadcasted_iota(jnp.int32, (n_rows, cols), 1) % n_heads
    own = col_head == lax.broadcasted_iota(jnp.int32, (n_rows, cols), 0) // 2

    @pl.when(p == 0)
    def _():
        m_scr[...] = jnp.full_like(m_scr, -jnp.inf)
        l_scr[...] = jnp.zeros_like(l_scr)
        acc_scr[...] = jnp.zeros_like(acc_scr)

    s = jnp.concatenate(
        [lax.dot_general(qmat.astype(BF16), kc_ref[0].astype(BF16), (((1,), (1,)), ((), ())),
                         preferred_element_type=F32) for kc_ref in kc_refs], axis=1) * scale
    s = jnp.where(own, s, -jnp.inf)
    m_old = m_scr[...]
    m_new = jnp.maximum(m_old, jnp.max(s, axis=-1, keepdims=True))
    alpha = jnp.exp(m_old - m_new)
    pr = jnp.exp(s - m_new)
    l_scr[...] = alpha * l_scr[...] + jnp.sum(pr, axis=-1, keepdims=True)
    prb = pr.astype(BF16)
    rows_per = kc_refs[0].shape[1]
    pv = sum(jnp.dot(prb[:, i * rows_per:(i + 1) * rows_per], vc_ref[0].astype(BF16), preferred_element_type=F32)
             for i, vc_ref in enumerate(vc_refs))
    acc_scr[...] = alpha * acc_scr[...] + pv
    m_scr[...] = m_new

    @pl.when(p == pl.num_programs(1) - 1)
    def _():
        s_new = jnp.sum(qmat * kn_ref[0], axis=-1, keepdims=True) * scale
        m_old = m_scr[...]
        m_fin = jnp.maximum(m_old, s_new)
        alpha = jnp.exp(m_old - m_fin)
        p_new = jnp.exp(s_new - m_fin)
        l_fin = alpha * l_scr[...] + p_new
        acc = (alpha * acc_scr[...] + p_new * vn_ref[0]) / l_fin
        lam = _diff_lambda(lam_ref[...], lam_init)
        acc = acc * jnp.where(row % 2 == 0, 1.0, -lam)
        subln = subln_ref[...]
        for h in range(n_heads):
            o = acc[2 * h:2 * h + 1] + acc[2 * h + 1:2 * h + 2]
            o_ref[0, h:h + 1, :] = _rms(o, subln) * (1.0 - lam_init)


def diff_attn_decode(q, k_new, v_new, cache_k, cache_v, page_table, lam_p, subln, lam_init):
    b, width = q.shape
    dv = 2 * HEAD_DIM
    n_heads = width // dv
    n_pages = page_table.shape[1]
    rows = cache_k.shape[1]
    n_slots = _tile(n_pages, DECODE_PAGES)
    per_map = lambda t: jnp.repeat(t.reshape(b, n_heads, dv), 2, axis=1)
    vec = pl.BlockSpec((1, 2 * n_heads, dv), lambda bi, p, pt: (bi, 0, 0))
    cache = [pl.BlockSpec((1, rows, dv), functools.partial(
        lambda bi, p, pt, slot: (pt[bi * n_pages + p * n_slots + slot], 0, 0), slot=slot))
        for slot in range(n_slots)]
    out = pl.pallas_call(
        functools.partial(_diff_attn_decode_kernel, lam_init=lam_init, n_heads=n_heads, n_slots=n_slots),
        grid_spec=pltpu.PrefetchScalarGridSpec(
            num_scalar_prefetch=1,
            grid=(b, n_pages // n_slots),
            in_specs=[vec, vec, vec] + cache + cache + [
                pl.BlockSpec(lam_p.shape, lambda bi, p, pt: (0, 0)),
                pl.BlockSpec(subln.shape, lambda bi, p, pt: (0, 0))],
            out_specs=pl.BlockSpec((1, n_heads, dv), lambda bi, p, pt: (bi, 0, 0)),
            scratch_shapes=[pltpu.VMEM((2 * n_heads, 1), F32), pltpu.VMEM((2 * n_heads, 1), F32),
                            pltpu.VMEM((2 * n_heads, dv), F32)],
        ),
        out_shape=jax.ShapeDtypeStruct((b, n_heads, dv), F32),
        compiler_params=_cparams(("parallel", "arbitrary")),
        name="diff_attn_decode",
    )(page_table.reshape(-1), per_map(q), per_map(k_new), per_map(v_new),
      *([cache_k] * n_slots), *([cache_v] * n_slots), lam_p, subln)
    return out.reshape(b, width)


def _lru_coeffs(x, x1, x2, x3, cw, cb, wa, ba, wi, bi, lam):
    xc = x3 * cw[0:1] + x2 * cw[1:2] + x1 * cw[2:3] + x * cw[3:4]
    xc = xc + cb
    r = jax.nn.sigmoid(_dot(xc, wa) + ba)
    ig = jax.nn.sigmoid(_dot(xc, wi) + bi)
    log_a = -LRU_C * r * _softplus(-lam)
    a = jnp.exp(log_a)
    return a, jnp.sqrt(-jnp.tanh(log_a) * (a * a + 1.0)) * (ig * xc)


def _lru_seq_kernel(x_ref, gr_ref, buf_ref, h0_ref, cw_ref, cb_ref, wa_ref, ba_ref, wi_ref, bi_ref, lam_ref,
                    o_ref, hl_ref, carry, h_scr, a_scr, b_scr):
    tl = x_ref.shape[1]
    n_carry = carry.shape[0]

    @pl.when(pl.program_id(1) == 0)
    def _():
        for d in range(n_carry):
            carry[d] = buf_ref[0, d:d + 1, :]
        h_scr[...] = h0_ref[0]

    x = x_ref[0]
    x1 = _shifted(x, carry[n_carry - 1])
    x2 = _shifted(x1, carry[n_carry - 2])
    x3 = _shifted(x2, carry[n_carry - 3])
    for d in range(n_carry):
        carry[d] = x[tl - n_carry + d:tl - n_carry + d + 1, :]
    a, b = _lru_coeffs(x, x1, x2, x3, cw_ref[...], cb_ref[...], wa_ref[...], ba_ref[...], wi_ref[...],
                       bi_ref[...], lam_ref[...])
    a_scr[...] = a
    b_scr[...] = b
    row_id = lax.broadcasted_iota(jnp.int32, (SUBLANES, x.shape[1]), 0)

    def block(tb, h):
        base = pl.multiple_of(tb * SUBLANES, SUBLANES)
        a8, b8 = a_scr[pl.ds(base, SUBLANES), :], b_scr[pl.ds(base, SUBLANES), :]
        hs = jnp.zeros_like(a8)
        for i in range(SUBLANES):
            h = a8[i:i + 1, :] * h + b8[i:i + 1, :]
            hs = jnp.where(row_id == i, h, hs)
        o_ref[0, pl.ds(base, SUBLANES), :] = hs * jax.nn.gelu(gr_ref[0, pl.ds(base, SUBLANES), :])
        return h

    h = lax.fori_loop(0, tl // SUBLANES, block, h_scr[...])
    h_scr[...] = h
    hl_ref[0] = h


def lru_prompt(x, gr, buf, h0, prm):
    bsz, l, w = x.shape
    tl = _tile(l, 512)
    seq = pl.BlockSpec((1, tl, w), lambda i, j: (i, j, 0))
    vec = pl.BlockSpec((1, 1, w), lambda i, j: (i, 0, 0))
    full = lambda a: pl.BlockSpec(a.shape, lambda i, j: (0,) * a.ndim)
    params = [prm[n] for n in ("conv_w", "conv_b", "wa", "ba", "wi", "bi", "lam")]
    nb = buf.shape[1]
    o, hl = pl.pallas_call(
        _lru_seq_kernel,
        grid=(bsz, l // tl),
        in_specs=[seq, seq, pl.BlockSpec((1, nb, w), lambda i, j: (i, 0, 0)), vec] + [full(a) for a in params],
        out_specs=[seq, vec],
        out_shape=[jax.ShapeDtypeStruct((bsz, l, w), F32), jax.ShapeDtypeStruct((bsz, 1, w), F32)],
        scratch_shapes=[pltpu.VMEM((nb, 1, w), F32), pltpu.VMEM((1, w), F32), pltpu.VMEM((tl, w), F32),
                        pltpu.VMEM((tl, w), F32)],
        compiler_params=_cparams(("parallel", "arbitrary")),
        name="lru_prompt",
    )(x, gr, buf, h0.reshape(bsz, 1, w), *params)
    return o, hl.reshape(bsz, w)


def _lru_step_kernel(x_ref, x1_ref, x2_ref, x3_ref, gr_ref, h0_ref, cw_ref, cb_ref, wa_ref, ba_ref, wi_ref,
                     bi_ref, lam_ref, o_ref, h_ref):
    a, b = _lru_coeffs(x_ref[...], x1_ref[...], x2_ref[...], x3_ref[...], cw_ref[...], cb_ref[...], wa_ref[...],
                       ba_ref[...], wi_ref[...], bi_ref[...], lam_ref[...])
    h = a * h0_ref[...] + b
    h_ref[...] = h
    o_ref[...] = h * jax.nn.gelu(gr_ref[...])


def lru_step(x, buf, gr, h0, prm):
    m, w = x.shape
    tm = _tile(m, 512)
    rows = pl.BlockSpec((tm, w), lambda i: (i, 0))
    full = lambda a: pl.BlockSpec(a.shape, lambda i: (0,) * a.ndim)
    params = [prm[n] for n in ("conv_w", "conv_b", "wa", "ba", "wi", "bi", "lam")]
    nb = buf.shape[1]
    return pl.pallas_call(
        _lru_step_kernel,
        grid=(m // tm,),
        in_specs=[rows] * 6 + [full(a) for a in params],
        out_specs=[rows] * 2,
        out_shape=[jax.ShapeDtypeStruct((m, w), F32)] * 2,
        compiler_params=_cparams(("parallel",)),
        name="lru_step",
    )(x, buf[:, nb - 1], buf[:, nb - 2], buf[:, nb - 3], gr, h0, *params)


def _block_diag(w):
    n, d, e = w.shape
    eye = jnp.eye(n, dtype=w.dtype)
    return (eye[:, None, :, None] * w[:, :, None, :]).reshape(n * d, n * e)


def _mix_even(h, g_norm, pos0, s_ret, s_rwkv, buf, wts, is_prompt):
    b, l, d = h.shape
    m = b * l
    gw = d // 2
    u = norm_matmul(h.reshape(m, d), g_norm, wts["ab_w_in"], gw)
    seq = lambda t: t.reshape(b, l, gw)
    flat = lambda t: t.reshape(m, gw)
    qa, ka, va, ga = u[:4]
    to_kg, to_nat = _to_key_group, _from_key_group
    cur = [seq(t) for t in u[4:8]]
    last = [t[:, l - 1:] for t in cur]
    buf_new = jnp.concatenate([to_nat(last[0]), to_nat(last[1]), last[2], last[3]], axis=-1)
    buf_kg = [to_kg(buf[..., :gw]), to_kg(buf[..., gw:2 * gw]), buf[..., 2 * gw:3 * gw], buf[..., 3 * gw:]]
    if is_prompt:
        r, w, k, v, kk, kka, g, bonus = rwkv_prep(cur, None, jnp.concatenate(buf_kg, axis=-1), wts["rwkv"])
        pos = pos0 + jnp.arange(l, dtype=F32)
        o_a, s_ret_new = retention_prompt(seq(qa), seq(ka), seq(va), seq(ga), pos, wts["ones_bd"])
        o_a = flat(o_a)
        o_b, s_rwkv_new = rwkv_scan_prompt(r, w, k, kk, kka, v)
    else:
        rows = lambda t: t.reshape(1, m, gw)
        r, w, k, v, kk, kka, g, bonus = rwkv_prep([rows(t) for t in cur], [rows(t) for t in buf_kg], None,
                                                        wts["rwkv"])
        o_a, s_ret_new = retention_step(qa, ka, va, ga, s_ret, jnp.float32(pos0))
        o_b, s_rwkv_new = rwkv_step(*(flat(to_nat(t)) for t in (r, w, k, kk, kka)), flat(v), s_rwkv)
    post = (flat(bonus), flat(g), wts["rwkv_ln"], wts["ones_bd"])
    return o_a, flat(o_b), post, s_ret_new, s_rwkv_new, buf_new


def _mix_odd(h, g_norm, lru_h, lru_buf, pages, wts, layer, is_prompt):
    b, l, d = h.shape
    m = b * l
    gw = d // 2
    u = norm_matmul(h.reshape(m, d), g_norm, wts["cd_w_in"], gw, head_major=(1, 2))
    seq = lambda t: t.reshape(b, l, gw)
    lam_init = 0.8 - 0.6 * math.exp(-0.3 * layer)
    xr = seq(u[3])
    if is_prompt:
        o_c = diff_attn_prompt(seq(u[0]), seq(u[1]), seq(u[2]), wts["diff_lam"], wts["diff_subln"], lam_init)
        o_c = o_c.reshape(m, gw)
        o_d, h_last = lru_prompt(xr, seq(u[4]), lru_buf, lru_h, wts["lru"])
    else:
        cache_k, cache_v, page_table = pages
        o_c = diff_attn_decode(u[0], u[1], u[2], cache_k, cache_v, page_table, wts["diff_lam"],
                               wts["diff_subln"], lam_init)
        o_d, h_last = lru_step(u[3], lru_buf, u[4], lru_h, wts["lru"])
    buf_new = jnp.concatenate([lru_buf, xr], axis=1)[:, l:]
    n_heads = gw // (2 * HEAD_DIM)
    k_new = u[5].reshape(b, l, n_heads, 2 * HEAD_DIM)
    v_new = u[6].reshape(b, l, n_heads, 2 * HEAD_DIM)
    return o_c, o_d.reshape(m, gw), k_new, v_new, h_last, buf_new


def _trunk(x, p, pos0, s_ret, s_rwkv, s_shift, s_lru_h, s_lru_conv, pages, wts, is_prompt):
    b, l, d = x.shape
    m = b * l
    depth = wts["norm_g"].shape[0]
    h = x.reshape(m, d)
    ret_l, rwkv_l, shift_l, k_l, v_l, lh_l, lc_l = [], [], [], [], [], [], []
    for i in range(depth):
        j = i // 2
        g = wts["norm_g"][i]
        gn = lambda n: g[n:n + 1]
        h = ffn_block(h, gn(0), wts["ffn_in"], wts["ffn_out"], gn(1), i, 0)
        post = None
        if i % 2 == 0:
            o1, o2, post, sr, sw, sb = _mix_even(h.reshape(b, l, d), gn(2), pos0, s_ret[j], s_rwkv[j], s_shift[j],
                                                 wts["even"][j], is_prompt)
            ret_l.append(sr)
            rwkv_l.append(sw)
            shift_l.append(sb)
            w_out = wts["even"][j]["w_out"]
        else:
            pg = None if pages is None else (pages[0][j], pages[1][j], pages[2])
            o1, o2, kn, vn, lh, lc = _mix_odd(h.reshape(b, l, d), gn(2), s_lru_h[j], s_lru_conv[j], pg,
                                              wts["odd"][j], i, is_prompt)
            k_l.append(kn)
            v_l.append(vn)
            lh_l.append(lh)
            lc_l.append(lc)
            w_out = wts["odd"][j]["w_out"]
        h = out_proj(o1, o2, w_out, h, gn(3), post)
        h = ffn_block(h, gn(4), wts["ffn_in"], wts["ffn_out"], gn(5), i, 1)
        h = ple_block(h, gn(6), wts["ple_gate"], p.reshape(depth, m, -1), wts["ple"], gn(7), i)
    st = lambda lst: jnp.stack(lst, axis=0)
    return (h.reshape(b, l, d), st(k_l), st(v_l), st(ret_l), st(rwkv_l), st(shift_l), st(lh_l), st(lc_l))


def kernel(x_prompt, x_sample, cache_k, cache_v, state_ret, state_rwkv, state_rwkv_shift, state_lru_h, state_lru_conv, page_table, p_prompt, p_sample, norm_g, ffn_w_in, ffn_w_out, ple_w, ple_gate_w, ab_w_in, ab_w_out, rwkv_mu, rwkv_w0, rwkv_w1, rwkv_w2, rwkv_a0, rwkv_a1, rwkv_a2, rwkv_g1, rwkv_g2, rwkv_kk, rwkv_ka, rwkv_rk, rwkv_ln, cd_w_in, cd_w_out, diff_lam, diff_subln, lru_conv_w, lru_conv_b, lru_wa, lru_ba, lru_wi, lru_bi, lru_lambda):
    depth = norm_g.shape[0]
    n_a, n_c = state_ret.shape[0], state_lru_h.shape[0]
    bp = x_prompt.shape[0]
    gw = ab_w_out.shape[1] // 2
    bf = lambda t: t.astype(BF16)
    row = lambda t: t.reshape(1, -1)
    ones_bd = _block_diag(jnp.ones((gw // HEAD_DIM, HEAD_DIM, HEAD_DIM), BF16))
    perm = _key_group_perm(gw)
    kg = _to_key_group
    head_kg = perm // HEAD_DIM
    ones_kk = jnp.asarray(head_kg[:, None] == head_kg[None, :], BF16)
    ones_rk = jnp.asarray(head_kg[:, None] == (np.arange(gw) // HEAD_DIM)[None, :], BF16)

    def ab_in_kg(w):
        cols = [w[:, g * gw:(g + 1) * gw] for g in range(w.shape[1] // gw)]
        cols[4], cols[5] = kg(cols[4]), kg(cols[5])
        return bf(jnp.concatenate(cols, axis=1))

    wts = {
        "norm_g": norm_g,
        "ffn_in": bf(ffn_w_in), "ffn_out": bf(ffn_w_out), "ple": bf(ple_w), "ple_gate": bf(ple_gate_w),
        "even": [{
            "ab_w_in": ab_in_kg(ab_w_in[j]), "w_out": bf(ab_w_out[j]), "rwkv_ln": rwkv_ln[j],
            "ones_bd": ones_bd,
            "rwkv": {"mu": jnp.concatenate([kg(rwkv_mu[j][:2]), rwkv_mu[j][2:]], axis=0),
                     "w0": kg(row(rwkv_w0[j])), "w1": bf(rwkv_w1[j]), "w2": bf(kg(rwkv_w2[j])),
                     "a0": kg(row(rwkv_a0[j])), "a1": bf(rwkv_a1[j]), "a2": bf(kg(rwkv_a2[j])),
                     "g1": bf(rwkv_g1[j]), "g2": bf(rwkv_g2[j]), "kk": kg(row(rwkv_kk[j])),
                     "ka": kg(row(rwkv_ka[j])), "rk": kg(row(rwkv_rk[j])),
                     "ones_kk": ones_kk, "ones_rk": ones_rk},
        } for j in range(n_a)],
        "odd": [{
            "cd_w_in": bf(cd_w_in[j]), "w_out": bf(cd_w_out[j]), "diff_lam": diff_lam[j],
            "diff_subln": row(diff_subln[j]),
            "lru": {"conv_w": lru_conv_w[j], "conv_b": row(lru_conv_b[j]), "wa": bf(_block_diag(lru_wa[j])),
                    "ba": row(lru_ba[j]), "wi": bf(_block_diag(lru_wi[j])), "bi": row(lru_bi[j]),
                    "lam": row(lru_lambda[j])},
        } for j in range(n_c)],
    }
    zeros = lambda *shape: jnp.zeros(shape, F32)
    yp, kp, vp, rp, wp, sp, hp, cp = _trunk(
        x_prompt, p_prompt, 0.0, [None] * n_a, [None] * n_a,
        zeros(n_a, bp, 1, 4 * gw), zeros(n_c, bp, gw), zeros(n_c, bp, CONV_W - 1, gw),
        None, wts, True)
    past_len = page_table.shape[1] * cache_k.shape[2]
    n_pool, page = cache_k.shape[1], cache_k.shape[2]
    as_rows = lambda c: c.reshape(n_c, n_pool, page * c.shape[3], c.shape[4])
    pages = (as_rows(cache_k), as_rows(cache_v), page_table)
    ys, ks_, vs, rs, ws, ss, hs, cs = _trunk(
        x_sample, p_sample, float(past_len), state_ret, state_rwkv, state_rwkv_shift,
        state_lru_h, state_lru_conv, pages, wts, False)
    return (yp, ys, kp, vp, rp, wp, sp, hp, cp, ks_, vs, rs, ws, ss, hs, cs)
```

```python
import functools
import math

import jax
import jax.numpy as jnp
import numpy as np
from jax import lax
from jax.experimental import pallas as pl
from jax.experimental.pallas import tpu as pltpu

F32 = jnp.float32
BF16 = jnp.bfloat16

HEAD_DIM = 64
CONV_W = 4
LRU_C = 8.0
ROPE_BASE = 10000.0
EPS = 1e-6
RWKV_GN_EPS = 64e-5
RET_CHUNK = 256
ATTN_BLOCK = 512
SCAN_CHUNK = 128
DECODE_PAGES = 16
SCAN_GROUP = 8
LANES = 128
SUBLANES = 8
VMEM_LIMIT = 48 * 1024 * 1024


def _cparams(sem):
    return pltpu.CompilerParams(dimension_semantics=sem, vmem_limit_bytes=VMEM_LIMIT)


def _tile(n, pref):
    t = min(n, pref)
    while n % t:
        t //= 2
    return t


def _rms(x, g):
    return x * lax.rsqrt(jnp.mean(x * x, axis=-1, keepdims=True) + EPS) * g


def _dot(a, b):
    return jnp.dot(a.astype(BF16), b.astype(BF16), preferred_element_type=F32)


def _seg_sum(x, ones_bd):
    hi = x.astype(BF16)
    lo = (x - hi.astype(F32)).astype(BF16)
    return (jnp.dot(hi, ones_bd, preferred_element_type=F32)
            + jnp.dot(lo, ones_bd, preferred_element_type=F32))


def _ffn_kernel(h_ref, gpre_ref, wg_ref, wu_ref, wo_ref, gpost_ref, o_ref, xn_ref, acc_ref):
    j = pl.program_id(1)

    @pl.when(j == 0)
    def _():
        xn_ref[...] = _rms(h_ref[...], gpre_ref[...]).astype(BF16)
        acc_ref[...] = jnp.zeros_like(acc_ref)

    xn = xn_ref[...]
    gate = jnp.dot(xn, wg_ref[...], preferred_element_type=F32)
    up = jnp.dot(xn, wu_ref[...], preferred_element_type=F32)
    act = (gate * jax.nn.sigmoid(gate) * up).astype(BF16)
    acc_ref[...] += jnp.dot(act, wo_ref[...], preferred_element_type=F32)

    @pl.when(j == pl.num_programs(1) - 1)
    def _():
        o_ref[...] = h_ref[...] + 0.5 * _rms(acc_ref[...], gpost_ref[...])


def ffn_block(h, g_pre, w_in, w_out, g_post, layer, half):
    m, d = h.shape
    f = w_out.shape[2]
    tm, tf = _tile(m, 1024), _tile(f, 512)
    nf = f // tf
    return pl.pallas_call(
        _ffn_kernel,
        grid=(m // tm, nf),
        in_specs=[
            pl.BlockSpec((tm, d), lambda i, j: (i, 0)),
            pl.BlockSpec((1, d), lambda i, j: (0, 0)),
            pl.BlockSpec((None, None, d, tf), lambda i, j: (layer, half, 0, j)),
            pl.BlockSpec((None, None, d, tf), lambda i, j: (layer, half, 0, j + nf)),
            pl.BlockSpec((None, None, tf, d), lambda i, j: (layer, half, j, 0)),
            pl.BlockSpec((1, d), lambda i, j: (0, 0)),
        ],
        out_specs=pl.BlockSpec((tm, d), lambda i, j: (i, 0)),
        out_shape=jax.ShapeDtypeStruct((m, d), F32),
        scratch_shapes=[pltpu.VMEM((tm, d), BF16), pltpu.VMEM((tm, d), F32)],
        compiler_params=_cparams(("parallel", "arbitrary")),
        name="ffn_block",
    )(h, g_pre, w_in, w_in, w_out, g_post)


def _norm_matmul_kernel(h_ref, g_ref, w_ref, *o_refs, n_groups, head_major):
    xn = _rms(h_ref[...], g_ref[...]).astype(BF16)
    tm, tn = o_refs[0].shape
    heads = tn // LANES
    for gi in range(n_groups):
        res = jnp.dot(xn, w_ref[:, gi * tn:(gi + 1) * tn], preferred_element_type=F32)
        o_refs[gi][...] = res
        if gi in head_major:
            hm_ref = o_refs[n_groups + head_major.index(gi)]
            for hh in range(heads):
                hm_ref[pl.ds(hh, tm, stride=heads), :] = res[:, hh * LANES:(hh + 1) * LANES]


def norm_matmul(h, g, w, tn, head_major=()):
    m, d = h.shape
    n = w.shape[1]
    tm = _tile(m, 512)
    heads = tn // LANES
    rows = pl.BlockSpec((tm, tn), lambda i: (i, 0))
    return pl.pallas_call(
        functools.partial(_norm_matmul_kernel, n_groups=n // tn, head_major=tuple(head_major)),
        grid=(m // tm,),
        in_specs=[
            pl.BlockSpec((tm, d), lambda i: (i, 0)),
            pl.BlockSpec((1, d), lambda i: (0, 0)),
            pl.BlockSpec((d, n), lambda i: (0, 0)),
        ],
        out_specs=[rows] * (n // tn) + [pl.BlockSpec((tm * heads, LANES), lambda i: (i, 0))] * len(head_major),
        out_shape=([jax.ShapeDtypeStruct((m, tn), F32)] * (n // tn)
                   + [jax.ShapeDtypeStruct((m * heads, LANES), F32)] * len(head_major)),
        compiler_params=_cparams(("parallel",)),
        name="norm_matmul",
    )(h, g, w)


def _out_proj_kernel(oa_ref, ob_ref, wa_ref, wb_ref, h_ref, g_ref, *rest):
    ob = ob_ref[...]
    if len(rest) > 1:
        bonus_ref, gate_ref, ln_ref, ones_ref = rest[:4]
        ones_bd = ones_ref[...]
        inv = 1.0 / HEAD_DIM
        oc = ob - _seg_sum(ob, ones_bd) * inv
        on = oc * lax.rsqrt(_seg_sum(oc * oc, ones_bd) * inv + RWKV_GN_EPS)
        ln = ln_ref[...]
        ob = (on * ln[0:1] + ln[1:2] + bonus_ref[...]) * gate_ref[...]
    o_ref = rest[-1]
    y = _dot(oa_ref[...], wa_ref[...]) + _dot(ob, wb_ref[...])
    o_ref[...] = h_ref[...] + _rms(y, g_ref[...])


def out_proj(oa, ob, w, h, g, rwkv_post=None):
    m, d = h.shape
    gw = oa.shape[1]
    tm = _tile(m, 512)
    rows = pl.BlockSpec((tm, gw), lambda i: (i, 0))
    extra, extra_specs = [], []
    if rwkv_post is not None:
        bonus, gate, ln, ones_bd = rwkv_post
        extra = [bonus, gate, ln, ones_bd]
        extra_specs = [rows, rows, pl.BlockSpec(ln.shape, lambda i: (0, 0)),
                       pl.BlockSpec(ones_bd.shape, lambda i: (0, 0))]
    return pl.pallas_call(
        _out_proj_kernel,
        grid=(m // tm,),
        in_specs=[
            rows, rows,
            pl.BlockSpec((gw, d), lambda i: (0, 0)),
            pl.BlockSpec((gw, d), lambda i: (1, 0)),
            pl.BlockSpec((tm, d), lambda i: (i, 0)),
            pl.BlockSpec((1, d), lambda i: (0, 0)),
        ] + extra_specs,
        out_specs=pl.BlockSpec((tm, d), lambda i: (i, 0)),
        out_shape=jax.ShapeDtypeStruct((m, d), F32),
        compiler_params=_cparams(("parallel",)),
        name="out_proj",
    )(oa, ob, w, w, h, g, *extra)


def _ple_kernel(h_ref, g6_ref, wg_ref, p_ref, wp_ref, g7_ref, o_ref):
    h = h_ref[...]
    gate = jax.nn.sigmoid(_dot(_rms(h, g6_ref[...]), wg_ref[...]))
    y = gate * _dot(p_ref[...], wp_ref[...])
    o_ref[...] = h + _rms(y, g7_ref[...])


def ple_block(h, g6, wg, p, wp, g7, layer):
    m, d = h.shape
    pd = p.shape[2]
    tm = _tile(m, 512)
    return pl.pallas_call(
        _ple_kernel,
        grid=(m // tm,),
        in_specs=[
            pl.BlockSpec((tm, d), lambda i: (i, 0)),
            pl.BlockSpec((1, d), lambda i: (0, 0)),
            pl.BlockSpec((None, d, d), lambda i: (layer, 0, 0)),
            pl.BlockSpec((None, tm, pd), lambda i: (layer, i, 0)),
            pl.BlockSpec((None, pd, d), lambda i: (layer, 0, 0)),
            pl.BlockSpec((1, d), lambda i: (0, 0)),
        ],
        out_specs=pl.BlockSpec((tm, d), lambda i: (i, 0)),
        out_shape=jax.ShapeDtypeStruct((m, d), F32),
        compiler_params=_cparams(("parallel",)),
        name="ple_block",
    )(h, g6, wg, p, wp, g7)


def _retention_tables(n_heads, c, pos):
    lg = jnp.log1p(-jnp.exp2(-5.0 - jnp.arange(n_heads, dtype=F32)))
    idx = jnp.arange(c, dtype=F32)
    rel = idx[:, None] - idx[None, :]
    dmask = jnp.where(rel[None] >= 0, jnp.exp(jnp.maximum(rel, 0.0)[None] * lg[:, None, None]), 0.0)
    rep = lambda t: jnp.repeat(t, HEAD_DIM, axis=-1)
    q_dec = rep(jnp.exp((idx[:, None] + 1.0) * lg[None, :]))
    k_dec = rep(jnp.exp((c - 1.0 - idx[:, None]) * lg[None, :]))
    c_dec = rep(jnp.exp(c * lg)[None, :])
    half = HEAD_DIM // 2
    freq = 1.0 / (ROPE_BASE ** jnp.linspace(0.0, 1.0, half, dtype=F32))
    ang = pos[:, None] * freq[None, :]
    cos, sin = jnp.cos(ang), jnp.sin(ang)
    cos_t = jnp.tile(jnp.concatenate([cos, cos], axis=-1), (1, n_heads))
    sin_t = jnp.tile(jnp.concatenate([-sin, sin], axis=-1), (1, n_heads))
    return dmask, q_dec, k_dec, c_dec, cos_t, sin_t


def _retention_kernel(q_ref, k_ref, v_ref, g_ref, cos_ref, sin_ref, dmask_ref, qdec_ref, kdec_ref,
                      cdec_ref, ones_ref, o_ref, s_ref, s_scr, *, n_heads):
    c = pl.program_id(1)

    @pl.when(c == 0)
    def _():
        s_scr[...] = jnp.zeros_like(s_scr)

    q, k, v, g = q_ref[0], k_ref[0], v_ref[0], g_ref[0]
    cos, sin = cos_ref[...], sin_ref[...]
    rows, width = q.shape
    lane = lax.broadcasted_iota(jnp.int32, q.shape, 1)
    first_half = (lane % HEAD_DIM) < (HEAD_DIM // 2)

    def rot(x):
        swapped = jnp.where(first_half, pltpu.roll(x, width - HEAD_DIM // 2, 1),
                            pltpu.roll(x, HEAD_DIM // 2, 1))
        return x * cos + swapped * sin

    qr = rot(q)
    kr = rot(k) * (HEAD_DIM ** -0.5)
    kd = kr * kdec_ref[...]
    qdec = qdec_ref[...]
    cdec = cdec_ref[...]
    head_a = lax.broadcasted_iota(jnp.int32, (rows, LANES), 1) < HEAD_DIM
    same_head = (lax.broadcasted_iota(jnp.int32, (LANES, LANES), 0) // HEAD_DIM
                 == lax.broadcasted_iota(jnp.int32, (LANES, LANES), 1) // HEAD_DIM)
    nt = (((1,), (1,)), ((), ()))
    outs = []
    for p in range(n_heads // 2):
        sl = slice(p * LANES, (p + 1) * LANES)
        qp = qr[:, sl].astype(BF16)
        kp = kr[:, sl].astype(BF16)
        vp = v[:, sl].astype(BF16)
        zero = jnp.zeros_like(qp)
        att_a = lax.dot_general(jnp.where(head_a, qp, zero), kp, nt, preferred_element_type=F32) * dmask_ref[2 * p]
        att_b = lax.dot_general(jnp.where(head_a, zero, qp), kp, nt,
                                preferred_element_type=F32) * dmask_ref[2 * p + 1]
        s_old = s_scr[p]
        inner = jnp.where(head_a, jnp.dot(att_a.astype(BF16), vp, preferred_element_type=F32),
                          jnp.dot(att_b.astype(BF16), vp, preferred_element_type=F32))
        outs.append(inner + jnp.dot(qp, s_old.astype(BF16), preferred_element_type=F32) * qdec[:, sl])
        update = lax.dot_general(kd[:, sl].astype(BF16), vp, (((0,), (0,)), ((), ())),
                                 preferred_element_type=F32)
        s_scr[p] = jnp.where(same_head, s_old * cdec[:, sl] + update, 0.0)
    o = jnp.concatenate(outs, axis=1)
    ones_bd = ones_ref[...]
    inv = 1.0 / HEAD_DIM
    oc = o - _seg_sum(o, ones_bd) * inv
    on = oc * lax.rsqrt(_seg_sum(oc * oc, ones_bd) * inv + EPS)
    o_ref[0] = on * (g * jax.nn.sigmoid(g))

    @pl.when(c == pl.num_programs(1) - 1)
    def _():
        for p in range(n_heads // 2):
            s_pair = s_scr[p]
            s_ref[0, 2 * p] = s_pair[:HEAD_DIM, :HEAD_DIM]
            s_ref[0, 2 * p + 1] = s_pair[HEAD_DIM:, HEAD_DIM:]


def retention_prompt(q, k, v, g, pos, ones_bd):
    b, l, width = q.shape
    n_heads = width // HEAD_DIM
    c = _tile(l, RET_CHUNK)
    dmask, q_dec, k_dec, c_dec, cos_t, sin_t = _retention_tables(n_heads, c, pos)
    seq = pl.BlockSpec((1, c, width), lambda i, j: (i, j, 0))
    tab = pl.BlockSpec((c, width), lambda i, j: (j, 0))
    fixed = lambda shape: pl.BlockSpec(shape, lambda i, j: (0,) * len(shape))
    return pl.pallas_call(
        functools.partial(_retention_kernel, n_heads=n_heads),
        grid=(b, l // c),
        in_specs=[seq, seq, seq, seq, tab, tab, fixed((n_heads, c, c)), fixed((c, width)),
                  fixed((c, width)), fixed((1, width)), fixed(ones_bd.shape)],
        out_specs=[seq, pl.BlockSpec((1, n_heads, HEAD_DIM, HEAD_DIM), lambda i, j: (i, 0, 0, 0))],
        out_shape=[jax.ShapeDtypeStruct((b, l, width), F32),
                   jax.ShapeDtypeStruct((b, n_heads, HEAD_DIM, HEAD_DIM), F32)],
        scratch_shapes=[pltpu.VMEM((n_heads // 2, LANES, LANES), F32)],
        compiler_params=_cparams(("parallel", "arbitrary")),
        name="retention_prompt",
    )(q, k, v, g, cos_t, sin_t, dmask, q_dec, k_dec, c_dec, ones_bd)


def _eye_mask():
    return (lax.broadcasted_iota(jnp.int32, (HEAD_DIM, HEAD_DIM), 0)
            == lax.broadcasted_iota(jnp.int32, (HEAD_DIM, HEAD_DIM), 1))


def _to_col(x_row):
    return jnp.sum(jnp.where(_eye_mask(), x_row, 0.0), axis=-1, keepdims=True)


def _to_row(x_col):
    return jnp.sum(jnp.where(_eye_mask(), x_col, 0.0), axis=-2, keepdims=True)


def _retention_step_kernel(q_ref, k_ref, v_ref, g_ref, cos_ref, sin_ref, gam_ref, s0_ref, o_ref, s_ref):
    cos, sin = cos_ref[...], sin_ref[...]

    def rot(x):
        half = HEAD_DIM // 2
        return x * cos + jnp.concatenate([x[..., half:], x[..., :half]], axis=-1) * sin

    q = _to_col(rot(q_ref[...]))
    k = _to_col(rot(k_ref[...]) * (HEAD_DIM ** -0.5))
    v, g = v_ref[...], g_ref[...]
    gam = gam_ref[...]
    s0 = s0_ref[...]
    att = jnp.sum(q * k, axis=2, keepdims=True)
    o = att * v + jnp.sum(q * s0, axis=2, keepdims=True) * gam
    s_ref[...] = s0 * gam + k * v
    oc = o - jnp.mean(o, axis=-1, keepdims=True)
    on = oc * lax.rsqrt(jnp.mean(oc * oc, axis=-1, keepdims=True) + EPS)
    o_ref[...] = on * (g * jax.nn.sigmoid(g))


def retention_step(q, k, v, g, s0, pos):
    b, width = q.shape
    n_heads = width // HEAD_DIM
    bb = _tile(b, 8)
    half = HEAD_DIM // 2
    freq = 1.0 / (ROPE_BASE ** jnp.linspace(0.0, 1.0, half, dtype=F32))
    ang = pos * freq
    cos_c = jnp.concatenate([jnp.cos(ang), jnp.cos(ang)])[None, :]
    sin_c = jnp.concatenate([-jnp.sin(ang), jnp.sin(ang)])[None, :]
    gam = jnp.exp(jnp.log1p(-jnp.exp2(-5.0 - jnp.arange(n_heads, dtype=F32)))).reshape(n_heads, 1, 1)
    row = lambda t: t.reshape(b, n_heads, 1, HEAD_DIM)
    rspec = pl.BlockSpec((bb, n_heads, 1, HEAD_DIM), lambda i: (i, 0, 0, 0))
    sspec = pl.BlockSpec((bb, n_heads, HEAD_DIM, HEAD_DIM), lambda i: (i, 0, 0, 0))
    o, s = pl.pallas_call(
        _retention_step_kernel,
        grid=(b // bb,),
        in_specs=[rspec, rspec, rspec, rspec,
                  pl.BlockSpec((1, HEAD_DIM), lambda i: (0, 0)), pl.BlockSpec((1, HEAD_DIM), lambda i: (0, 0)),
                  pl.BlockSpec((n_heads, 1, 1), lambda i: (0, 0, 0)), sspec],
        out_specs=[rspec, sspec],
        out_shape=[jax.ShapeDtypeStruct((b, n_heads, 1, HEAD_DIM), F32),
                   jax.ShapeDtypeStruct((b, n_heads, HEAD_DIM, HEAD_DIM), F32)],
        compiler_params=_cparams(("parallel",)),
        name="retention_step",
    )(row(q), row(k), row(v), row(g), cos_c, sin_c, gam, s0)
    return o.reshape(b, width), s


def _softplus(x):
    return jnp.maximum(x, 0.0) + jnp.log1p(jnp.exp(-jnp.abs(x)))


def _shifted(cur, carry_row):
    if cur.shape[0] == 1:
        return carry_row
    first = lax.broadcasted_iota(jnp.int32, cur.shape, 0) == 0
    return jnp.where(first, carry_row, pltpu.roll(cur, 1, 0))


PREP_PARAMS = ("mu", "w0", "w1", "w2", "a0", "a1", "a2", "g1", "g2", "kk", "ka", "rk", "ones_kk", "ones_rk")
N_PREP_PARAMS = len(PREP_PARAMS)
N_PREP_OUT = 8


def _rwkv_prep_kernel(*refs, shift_in_kernel):
    cur_refs = refs[:4]
    if shift_in_kernel:
        buf_ref = refs[4]
        n_in = 5
    else:
        prev_refs = refs[4:8]
        n_in = 8
    (mu_ref, w0_ref, w1_ref, w2_ref, a0_ref, a1_ref, a2_ref, g1_ref, g2_ref, kkp_ref, kap_ref, rk_ref,
     ones_kk_ref, ones_rk_ref) = refs[n_in:n_in + N_PREP_PARAMS]
    n_in += N_PREP_PARAMS
    r_out, w_out, k_out, v_out, kk_out, kka_out, g_out, bonus_out = refs[n_in:n_in + N_PREP_OUT]
    cur = [ref[0] for ref in cur_refs]
    width = cur[0].shape[1]
    if shift_in_kernel:
        carry = refs[n_in + N_PREP_OUT]

        @pl.when(pl.program_id(1) == 0)
        def _():
            for gi in range(4):
                carry[gi] = buf_ref[0, :, gi * width:(gi + 1) * width]

        prev = [_shifted(x, carry[gi]) for gi, x in enumerate(cur)]
        for gi, x in enumerate(cur):
            carry[gi] = x[x.shape[0] - 1:, :]
    else:
        prev = [ref[0] for ref in prev_refs]
    mu = mu_ref[...]
    lerp = lambda x, xp, i: x + (xp - x) * mu[i:i + 1]
    zr, pz = cur[3], prev[3]
    r = lerp(cur[0], prev[0], 0)
    kx = lerp(cur[1], prev[1], 1)
    vx = lerp(cur[2], prev[2], 2)
    zw, za, zg = lerp(zr, pz, 3), lerp(zr, pz, 4), lerp(zr, pz, 5)
    wpre = w0_ref[...] + _dot(jnp.tanh(_dot(zw, w1_ref[...])), w2_ref[...])
    decay = jnp.exp(-jnp.exp(-_softplus(-wpre) - 0.5))
    a = jax.nn.sigmoid(a0_ref[...] + _dot(_dot(za, a1_ref[...]), a2_ref[...]))
    g = _dot(jax.nn.sigmoid(_dot(zg, g1_ref[...])), g2_ref[...])
    kk = kx * kkp_ref[...]
    kk = kk / jnp.maximum(jnp.sqrt(_seg_sum(kk * kk, ones_kk_ref[...])), 1e-12)
    k32 = kx * (1.0 + (a - 1.0) * kap_ref[...])
    r_out[0] = r
    w_out[0] = decay
    k_out[0] = k32
    v_out[0] = vx
    kk_out[0] = kk
    kka_out[0] = kk * a
    g_out[0] = g
    bonus_out[0] = _seg_sum(r * k32 * rk_ref[...], ones_rk_ref[...]) * vx


def rwkv_prep(cur, prev, buf, prm):
    b, l, w = cur[0].shape
    tl = _tile(l, 256)
    seq = pl.BlockSpec((1, tl, w), lambda i, j: (i, j, 0))
    full = lambda a: pl.BlockSpec(a.shape, lambda i, j: (0,) * a.ndim)
    params = [prm[n] for n in PREP_PARAMS]
    shift = prev is None
    if shift:
        extra, extra_specs = [buf], [pl.BlockSpec((1, 1, 4 * w), lambda i, j: (i, 0, 0))]
        scratch = [pltpu.VMEM((4, 1, w), F32)]
    else:
        extra, extra_specs, scratch = list(prev), [seq] * 4, []
    return pl.pallas_call(
        functools.partial(_rwkv_prep_kernel, shift_in_kernel=shift),
        grid=(b, l // tl),
        in_specs=[seq] * 4 + extra_specs + [full(a) for a in params],
        out_specs=[seq] * N_PREP_OUT,
        out_shape=[jax.ShapeDtypeStruct((b, l, w), F32)] * N_PREP_OUT,
        scratch_shapes=scratch,
        compiler_params=_cparams(("parallel", "arbitrary")),
        name="rwkv_prep",
    )(*cur, *extra, *params)


def _hi_lo(x):
    hi = x.astype(BF16)
    lo = (x - hi.astype(F32)).astype(BF16)
    return jnp.concatenate([hi, lo], axis=1)


def _value_columns(v8):
    hi = v8.astype(BF16).astype(F32)
    lo = (v8 - hi).astype(BF16).astype(F32)
    stacked = jnp.concatenate([part[:, p * LANES:(p + 1) * LANES]
                               for p in range(v8.shape[1] // LANES) for part in (hi, lo)], axis=0)
    cols = stacked.T
    return jnp.concatenate([cols[:HEAD_DIM], cols[HEAD_DIM:]], axis=1).astype(BF16)


def _rwkv_scan_kernel(r_ref, w_ref, k_ref, kk_ref, kka_ref, v_ref, sel_ref, ones_ref, o_ref, s_ref, s_scr,
                      ot_scr, *, n_heads):
    c = pl.program_id(1)
    n_grp, tc = r_ref.shape[0], r_ref.shape[1]
    head_lanes = LANES // n_heads
    n_kg = HEAD_DIM // head_lanes

    @pl.when(c == 0)
    def _():
        s_scr[...] = jnp.zeros_like(s_scr)

    ot_scr[...] = jnp.zeros_like(ot_scr)
    ones2 = ones_ref[...]
    t_lane = lax.broadcasted_iota(jnp.int32, (n_grp * HEAD_DIM, LANES), 1) % head_lanes
    rows_of = lambda x, g: x[g * HEAD_DIM:(g + 1) * HEAD_DIM]

    def block(tb, carry):
        base = pl.multiple_of(tb * SUBLANES, SUBLANES)
        refs = {"kk": kk_ref, "w": w_ref, "kka": kka_ref, "k": k_ref, "r": r_ref}
        vp = jnp.concatenate([_value_columns(v_ref[g, pl.ds(base, SUBLANES), :]) for g in range(n_grp)], axis=0)
        tile = base // head_lanes
        head_sum = lambda x: jnp.dot(_hi_lo(x), ones2, preferred_element_type=F32)
        for i in range(SUBLANES):
            row = lambda name, g, kg: jnp.broadcast_to(
                refs[name][g, pl.ds(base, SUBLANES), kg * LANES:(kg + 1) * LANES][i:i + 1, :], (HEAD_DIM, LANES))
            sa = head_sum(jnp.concatenate(
                [sum(s_scr[g, kg] * row("kk", g, kg) for kg in range(n_kg)) for g in range(n_grp)], axis=0))
            vcol = jnp.dot(vp, sel_ref[i], preferred_element_type=F32)
            reads = []
            for g in range(n_grp):
                sa_g, vcol_g = rows_of(sa, g), rows_of(vcol, g)
                read = None
                for kg in range(n_kg):
                    s_new = (s_scr[g, kg] * row("w", g, kg) - sa_g * row("kka", g, kg)
                             + vcol_g * row("k", g, kg))
                    s_scr[g, kg] = s_new
                    term = s_new * row("r", g, kg)
                    read = term if read is None else read + term
                reads.append(read)
            o = head_sum(jnp.concatenate(reads, axis=0))
            ot_scr[tile] = jnp.where(t_lane == (base + i) % head_lanes, o, ot_scr[tile])
        return carry

    lax.fori_loop(0, tc // SUBLANES, block, 0)

    for tile in range(tc // head_lanes):
        for g in range(n_grp):
            o_t = rows_of(ot_scr[tile], g).T
            for h in range(n_heads):
                o_ref[g, tile * head_lanes:(tile + 1) * head_lanes, h * HEAD_DIM:(h + 1) * HEAD_DIM] = (
                    o_t[h * head_lanes:(h + 1) * head_lanes, :])

    @pl.when(c == pl.num_programs(1) - 1)
    def _():
        s_ref[...] = s_scr[...]


def _key_group_perm(width):
    n_heads = width // HEAD_DIM
    head_lanes = LANES // n_heads
    n = np.arange(width)
    return (n % LANES) // head_lanes * HEAD_DIM + n // LANES * head_lanes + n % head_lanes


def _to_key_group(t):
    width = t.shape[-1]
    n_heads = width // HEAD_DIM
    head_lanes = LANES // n_heads
    split = t.reshape(*t.shape[:-1], n_heads, HEAD_DIM // head_lanes, head_lanes)
    return jnp.swapaxes(split, -3, -2).reshape(t.shape)


def _from_key_group(t):
    width = t.shape[-1]
    n_heads = width // HEAD_DIM
    head_lanes = LANES // n_heads
    split = t.reshape(*t.shape[:-1], HEAD_DIM // head_lanes, n_heads, head_lanes)
    return jnp.swapaxes(split, -3, -2).reshape(t.shape)


def rwkv_scan_prompt(r, w, k, kk, kka, v):
    b, l, width = r.shape
    n_heads = width // HEAD_DIM
    head_lanes = LANES // n_heads
    n_kg = HEAD_DIM // head_lanes
    tc = _tile(l, SCAN_CHUNK)
    grp = _tile(b, SCAN_GROUP)
    kl = np.arange(LANES)
    col_head = 2 * ((kl % HEAD_DIM) // (2 * SUBLANES)) + kl // HEAD_DIM
    sel = ((kl[None, :, None] % SUBLANES == np.arange(SUBLANES)[:, None, None])
           & (col_head[None, :, None] == kl[None, None, :] // head_lanes))
    k2 = np.arange(2 * LANES)
    ones2 = (k2[:, None] % LANES) // head_lanes == kl[None, :] // head_lanes
    sel, ones2 = jnp.asarray(sel, BF16), jnp.asarray(ones2, BF16)
    seq = pl.BlockSpec((grp, tc, width), lambda i, j: (i, j, 0))
    state = pl.BlockSpec((grp, n_kg, HEAD_DIM, LANES), lambda i, j: (i, 0, 0, 0))
    o, s = pl.pallas_call(
        functools.partial(_rwkv_scan_kernel, n_heads=n_heads),
        grid=(b // grp, l // tc),
        in_specs=[seq, seq, seq, seq, seq, seq,
                  pl.BlockSpec(sel.shape, lambda i, j: (0, 0, 0)),
                  pl.BlockSpec(ones2.shape, lambda i, j: (0, 0))],
        out_specs=[seq, state],
        out_shape=[jax.ShapeDtypeStruct((b, l, width), F32),
                   jax.ShapeDtypeStruct((b, n_kg, HEAD_DIM, LANES), F32)],
        scratch_shapes=[pltpu.VMEM((grp, n_kg, HEAD_DIM, LANES), F32),
                        pltpu.VMEM((tc // head_lanes, grp * HEAD_DIM, LANES), F32)],
        compiler_params=_cparams(("parallel", "arbitrary")),
        name="rwkv_scan_prompt",
    )(r, w, k, kk, kka, v, sel, ones2)
    s = s.reshape(b, n_kg, HEAD_DIM, n_heads, head_lanes).transpose(0, 3, 2, 1, 4)
    return o, s.reshape(b, n_heads, HEAD_DIM, HEAD_DIM)


def _rwkv_step_kernel(r_ref, w_ref, k_ref, kk_ref, kka_ref, v_ref, s0_ref, o_ref, s_ref):
    s0 = s0_ref[...]
    sa = -jnp.sum(s0 * kk_ref[...], axis=-1, keepdims=True)
    s = s0 * w_ref[...] + sa * kka_ref[...] + _to_col(v_ref[...]) * k_ref[...]
    s_ref[...] = s
    o_ref[...] = _to_row(jnp.sum(s * r_ref[...], axis=-1, keepdims=True))


def rwkv_step(r, w, k, kk, kka, v, s0):
    b, width = r.shape
    n_heads = width // HEAD_DIM
    bb = _tile(b, 8)
    row = lambda t: t.reshape(b, n_heads, 1, HEAD_DIM)
    rspec = pl.BlockSpec((bb, n_heads, 1, HEAD_DIM), lambda i: (i, 0, 0, 0))
    sspec = pl.BlockSpec((bb, n_heads, HEAD_DIM, HEAD_DIM), lambda i: (i, 0, 0, 0))
    o, s = pl.pallas_call(
        _rwkv_step_kernel,
        grid=(b // bb,),
        in_specs=[rspec] * 6 + [sspec],
        out_specs=[rspec, sspec],
        out_shape=[jax.ShapeDtypeStruct((b, n_heads, 1, HEAD_DIM), F32),
                   jax.ShapeDtypeStruct((b, n_heads, HEAD_DIM, HEAD_DIM), F32)],
        compiler_params=_cparams(("parallel",)),
        name="rwkv_step",
    )(row(r), row(w), row(k), row(kk), row(kka), row(v), s0)
    return o.reshape(b, width), s


def _diff_lambda(lp, lam_init):
    e1 = jnp.exp(jnp.sum(lp[0:1] * lp[1:2], axis=-1, keepdims=True))
    e2 = jnp.exp(jnp.sum(lp[2:3] * lp[3:4], axis=-1, keepdims=True))
    return e1 - e2 + lam_init


def _diff_attn_kernel(q_ref, k_ref, v_ref, lam_ref, subln_ref, o_ref, m_scr, l_scr, acc_scr, *, lam_init):
    i = pl.program_id(2)
    tq = q_ref.shape[1]
    tk = tq
    scale = HEAD_DIM ** -0.5
    m_scr[...] = jnp.full_like(m_scr, -jnp.inf)
    l_scr[...] = jnp.zeros_like(l_scr)
    acc_scr[...] = jnp.zeros_like(acc_scr)
    q = (q_ref[0] * scale).astype(BF16)

    def update(j, on_diagonal):
        rows = pl.ds(pl.multiple_of(j * tk, tk), tk)
        k, v = k_ref[0, rows, :].astype(BF16), v_ref[0, rows, :].astype(BF16)
        if on_diagonal:
            visible = (lax.broadcasted_iota(jnp.int32, (tq, tk), 1)
                       <= lax.broadcasted_iota(jnp.int32, (tq, tk), 0))
        for mi in range(2):
            sl = slice(mi * HEAD_DIM, (mi + 1) * HEAD_DIM)
            s = lax.dot_general(q[:, sl], k[:, sl], (((1,), (1,)), ((), ())), preferred_element_type=F32)
            if on_diagonal:
                s = jnp.where(visible, s, -jnp.inf)
            m_old = m_scr[mi]
            m_new = jnp.maximum(m_old, jnp.max(s, axis=-1, keepdims=True))
            alpha = jnp.exp(m_old - m_new)
            p = jnp.exp(s - jnp.concatenate([m_new] * (tk // LANES), axis=1))
            l_scr[mi] = alpha * l_scr[mi] + jnp.sum(p, axis=-1, keepdims=True)
            acc_scr[mi] = alpha * acc_scr[mi] + jnp.dot(p.astype(BF16), v, preferred_element_type=F32)
            m_scr[mi] = m_new

    def below_diagonal(j, carry):
        update(j, False)
        return carry

    lax.fori_loop(0, i, below_diagonal, 0)
    update(i, True)
    lam = _diff_lambda(lam_ref[...], lam_init)
    o = acc_scr[0] / l_scr[0] - lam * (acc_scr[1] / l_scr[1])
    o_ref[0] = _rms(o, subln_ref[...]) * (1.0 - lam_init)


def diff_attn_prompt(q, k, v, lam_p, subln, lam_init):
    b, l, width = q.shape
    dv = 2 * HEAD_DIM
    n_heads = width // dv
    t = _tile(l, ATTN_BLOCK)
    n = l // t
    qspec = pl.BlockSpec((1, t, dv), lambda bi, h, i: (bi, i, h))
    kspec = pl.BlockSpec((1, l, dv), lambda bi, h, i: (bi, 0, h))
    return pl.pallas_call(
        functools.partial(_diff_attn_kernel, lam_init=lam_init),
        grid=(b, n_heads, n),
        in_specs=[qspec, kspec, kspec,
                  pl.BlockSpec(lam_p.shape, lambda bi, h, i: (0, 0)),
                  pl.BlockSpec(subln.shape, lambda bi, h, i: (0, 0))],
        out_specs=qspec,
        out_shape=jax.ShapeDtypeStruct((b, l, width), F32),
        scratch_shapes=[pltpu.VMEM((2, t, LANES), F32), pltpu.VMEM((2, t, LANES), F32), pltpu.VMEM((2, t, dv), F32)],
        compiler_params=_cparams(("parallel", "parallel", "arbitrary")),
        name="diff_attn_prompt",
    )(q, k, v, lam_p, subln)


def _diff_attn_decode_kernel(pt_ref, q_ref, kn_ref, vn_ref, *rest, lam_init, n_heads, n_slots):
    kc_refs, vc_refs = rest[:n_slots], rest[n_slots:2 * n_slots]
    lam_ref, subln_ref, o_ref, m_scr, l_scr, acc_scr = rest[2 * n_slots:]
    p = pl.program_id(1)
    n_rows = 2 * n_heads
    dv = 2 * HEAD_DIM
    scale = HEAD_DIM ** -0.5
    row = lax.broadcasted_iota(jnp.int32, (n_rows, dv), 0)
    lane = lax.broadcasted_iota(jnp.int32, (n_rows, dv), 1)
    qmat = jnp.where(lane // HEAD_DIM == row % 2, q_ref[0], 0.0)
    cols = n_slots * kc_refs[0].shape[1]
    col_head = lax.broadcasted_iota(jnp.int32, (n_rows, cols), 1) % n_heads
    own = col_head == lax.broadcasted_iota(jnp.int32, (n_rows, cols), 0) // 2

    @pl.when(p == 0)
    def _():
        m_scr[...] = jnp.full_like(m_scr, -jnp.inf)
        l_scr[...] = jnp.zeros_like(l_scr)
        acc_scr[...] = jnp.zeros_like(acc_scr)

    s = jnp.concatenate(
        [lax.dot_general(qmat.astype(BF16), kc_ref[0].astype(BF16), (((1,), (1,)), ((), ())),
                         preferred_element_type=F32) for kc_ref in kc_refs], axis=1) * scale
    s = jnp.where(own, s, -jnp.inf)
    m_old = m_scr[...]
    m_new = jnp.maximum(m_old, jnp.max(s, axis=-1, keepdims=True))
    alpha = jnp.exp(m_old - m_new)
    pr = jnp.exp(s - m_new)
    l_scr[...] = alpha * l_scr[...] + jnp.sum(pr, axis=-1, keepdims=True)
    prb = pr.astype(BF16)
    rows_per = kc_refs[0].shape[1]
    pv = sum(jnp.dot(prb[:, i * rows_per:(i + 1) * rows_per], vc_ref[0].astype(BF16), preferred_element_type=F32)
             for i, vc_ref in enumerate(vc_refs))
    acc_scr[...] = alpha * acc_scr[...] + pv
    m_scr[...] = m_new

    @pl.when(p == pl.num_programs(1) - 1)
    def _():
        s_new = jnp.sum(qmat * kn_ref[0], axis=-1, keepdims=True) * scale
        m_old = m_scr[...]
        m_fin = jnp.maximum(m_old, s_new)
        alpha = jnp.exp(m_old - m_fin)
        p_new = jnp.exp(s_new - m_fin)
        l_fin = alpha * l_scr[...] + p_new
        acc = (alpha * acc_scr[...] + p_new * vn_ref[0]) / l_fin
        lam = _diff_lambda(lam_ref[...], lam_init)
        acc = acc * jnp.where(row % 2 == 0, 1.0, -lam)
        subln = subln_ref[...]
        for h in range(n_heads):
            o = acc[2 * h:2 * h + 1] + acc[2 * h + 1:2 * h + 2]
            o_ref[0, h:h + 1, :] = _rms(o, subln) * (1.0 - lam_init)


def diff_attn_decode(q, k_new, v_new, cache_k, cache_v, page_table, lam_p, subln, lam_init):
    b, width = q.shape
    dv = 2 * HEAD_DIM
    n_heads = width // dv
    n_pages = page_table.shape[1]
    rows = cache_k.shape[1]
    n_slots = _tile(n_pages, DECODE_PAGES)
    per_map = lambda t: jnp.repeat(t.reshape(b, n_heads, dv), 2, axis=1)
    vec = pl.BlockSpec((1, 2 * n_heads, dv), lambda bi, p, pt: (bi, 0, 0))
    cache = [pl.BlockSpec((1, rows, dv), functools.partial(
        lambda bi, p, pt, slot: (pt[bi * n_pages + p * n_slots + slot], 0, 0), slot=slot))
        for slot in range(n_slots)]
    out = pl.pallas_call(
        functools.partial(_diff_attn_decode_kernel, lam_init=lam_init, n_heads=n_heads, n_slots=n_slots),
        grid_spec=pltpu.PrefetchScalarGridSpec(
            num_scalar_prefetch=1,
            grid=(b, n_pages // n_slots),
            in_specs=[vec, vec, vec] + cache + cache + [
                pl.BlockSpec(lam_p.shape, lambda bi, p, pt: (0, 0)),
                pl.BlockSpec(subln.shape, lambda bi, p, pt: (0, 0))],
            out_specs=pl.BlockSpec((1, n_heads, dv), lambda bi, p, pt: (bi, 0, 0)),
            scratch_shapes=[pltpu.VMEM((2 * n_heads, 1), F32), pltpu.VMEM((2 * n_heads, 1), F32),
                            pltpu.VMEM((2 * n_heads, dv), F32)],
        ),
        out_shape=jax.ShapeDtypeStruct((b, n_heads, dv), F32),
        compiler_params=_cparams(("parallel", "arbitrary")),
        name="diff_attn_decode",
    )(page_table.reshape(-1), per_map(q), per_map(k_new), per_map(v_new),
      *([cache_k] * n_slots), *([cache_v] * n_slots), lam_p, subln)
    return out.reshape(b, width)


def _lru_coeffs(x, x1, x2, x3, cw, cb, wa, ba, wi, bi, lam):
    xc = x3 * cw[0:1] + x2 * cw[1:2] + x1 * cw[2:3] + x * cw[3:4]
    xc = xc + cb
    r = jax.nn.sigmoid(_dot(xc, wa) + ba)
    ig = jax.nn.sigmoid(_dot(xc, wi) + bi)
    log_a = -LRU_C * r * _softplus(-lam)
    a = jnp.exp(log_a)
    return a, jnp.sqrt(-jnp.tanh(log_a) * (a * a + 1.0)) * (ig * xc)


def _lru_seq_kernel(x_ref, gr_ref, buf_ref, h0_ref, cw_ref, cb_ref, wa_ref, ba_ref, wi_ref, bi_ref, lam_ref,
                    o_ref, hl_ref, carry, h_scr, a_scr, b_scr):
    tl = x_ref.shape[1]
    n_carry = carry.shape[0]

    @pl.when(pl.program_id(1) == 0)
    def _():
        for d in range(n_carry):
            carry[d] = buf_ref[0, d:d + 1, :]
        h_scr[...] = h0_ref[0]

    x = x_ref[0]
    x1 = _shifted(x, carry[n_carry - 1])
    x2 = _shifted(x1, carry[n_carry - 2])
    x3 = _shifted(x2, carry[n_carry - 3])
    for d in range(n_carry):
        carry[d] = x[tl - n_carry + d:tl - n_carry + d + 1, :]
    a, b = _lru_coeffs(x, x1, x2, x3, cw_ref[...], cb_ref[...], wa_ref[...], ba_ref[...], wi_ref[...],
                       bi_ref[...], lam_ref[...])
    a_scr[...] = a
    b_scr[...] = b
    row_id = lax.broadcasted_iota(jnp.int32, (SUBLANES, x.shape[1]), 0)

    def block(tb, h):
        base = pl.multiple_of(tb * SUBLANES, SUBLANES)
        a8, b8 = a_scr[pl.ds(base, SUBLANES), :], b_scr[pl.ds(base, SUBLANES), :]
        hs = jnp.zeros_like(a8)
        for i in range(SUBLANES):
            h = a8[i:i + 1, :] * h + b8[i:i + 1, :]
            hs = jnp.where(row_id == i, h, hs)
        o_ref[0, pl.ds(base, SUBLANES), :] = hs * jax.nn.gelu(gr_ref[0, pl.ds(base, SUBLANES), :])
        return h

    h = lax.fori_loop(0, tl // SUBLANES, block, h_scr[...])
    h_scr[...] = h
    hl_ref[0] = h


def lru_prompt(x, gr, buf, h0, prm):
    bsz, l, w = x.shape
    tl = _tile(l, 512)
    seq = pl.BlockSpec((1, tl, w), lambda i, j: (i, j, 0))
    vec = pl.BlockSpec((1, 1, w), lambda i, j: (i, 0, 0))
    full = lambda a: pl.BlockSpec(a.shape, lambda i, j: (0,) * a.ndim)
    params = [prm[n] for n in ("conv_w", "conv_b", "wa", "ba", "wi", "bi", "lam")]
    nb = buf.shape[1]
    o, hl = pl.pallas_call(
        _lru_seq_kernel,
        grid=(bsz, l // tl),
        in_specs=[seq, seq, pl.BlockSpec((1, nb, w), lambda i, j: (i, 0, 0)), vec] + [full(a) for a in params],
        out_specs=[seq, vec],
        out_shape=[jax.ShapeDtypeStruct((bsz, l, w), F32), jax.ShapeDtypeStruct((bsz, 1, w), F32)],
        scratch_shapes=[pltpu.VMEM((nb, 1, w), F32), pltpu.VMEM((1, w), F32), pltpu.VMEM((tl, w), F32),
                        pltpu.VMEM((tl, w), F32)],
        compiler_params=_cparams(("parallel", "arbitrary")),
        name="lru_prompt",
    )(x, gr, buf, h0.reshape(bsz, 1, w), *params)
    return o, hl.reshape(bsz, w)


def _lru_step_kernel(x_ref, x1_ref, x2_ref, x3_ref, gr_ref, h0_ref, cw_ref, cb_ref, wa_ref, ba_ref, wi_ref,
                     bi_ref, lam_ref, o_ref, h_ref):
    a, b = _lru_coeffs(x_ref[...], x1_ref[...], x2_ref[...], x3_ref[...], cw_ref[...], cb_ref[...], wa_ref[...],
                       ba_ref[...], wi_ref[...], bi_ref[...], lam_ref[...])
    h = a * h0_ref[...] + b
    h_ref[...] = h
    o_ref[...] = h * jax.nn.gelu(gr_ref[...])


def lru_step(x, buf, gr, h0, prm):
    m, w = x.shape
    tm = _tile(m, 512)
    rows = pl.BlockSpec((tm, w), lambda i: (i, 0))
    full = lambda a: pl.BlockSpec(a.shape, lambda i: (0,) * a.ndim)
    params = [prm[n] for n in ("conv_w", "conv_b", "wa", "ba", "wi", "bi", "lam")]
    nb = buf.shape[1]
    return pl.pallas_call(
        _lru_step_kernel,
        grid=(m // tm,),
        in_specs=[rows] * 6 + [full(a) for a in params],
        out_specs=[rows] * 2,
        out_shape=[jax.ShapeDtypeStruct((m, w), F32)] * 2,
        compiler_params=_cparams(("parallel",)),
        name="lru_step",
    )(x, buf[:, nb - 1], buf[:, nb - 2], buf[:, nb - 3], gr, h0, *params)


def _block_diag(w):
    n, d, e = w.shape
    eye = jnp.eye(n, dtype=w.dtype)
    return (eye[:, None, :, None] * w[:, :, None, :]).reshape(n * d, n * e)


def _mix_even(h, g_norm, pos0, s_ret, s_rwkv, buf, wts, is_prompt):
    b, l, d = h.shape
    m = b * l
    gw = d // 2
    u = norm_matmul(h.reshape(m, d), g_norm, wts["ab_w_in"], gw)
    seq = lambda t: t.reshape(b, l, gw)
    flat = lambda t: t.reshape(m, gw)
    qa, ka, va, ga = u[:4]
    to_kg, to_nat = _to_key_group, _from_key_group
    cur = [seq(t) for t in u[4:8]]
    last = [t[:, l - 1:] for t in cur]
    buf_new = jnp.concatenate([to_nat(last[0]), to_nat(last[1]), last[2], last[3]], axis=-1)
    buf_kg = [to_kg(buf[..., :gw]), to_kg(buf[..., gw:2 * gw]), buf[..., 2 * gw:3 * gw], buf[..., 3 * gw:]]
    if is_prompt:
        r, w, k, v, kk, kka, g, bonus = rwkv_prep(cur, None, jnp.concatenate(buf_kg, axis=-1), wts["rwkv"])
        pos = pos0 + jnp.arange(l, dtype=F32)
        o_a, s_ret_new = retention_prompt(seq(qa), seq(ka), seq(va), seq(ga), pos, wts["ones_bd"])
        o_a = flat(o_a)
        o_b, s_rwkv_new = rwkv_scan_prompt(r, w, k, kk, kka, v)
    else:
        rows = lambda t: t.reshape(1, m, gw)
        r, w, k, v, kk, kka, g, bonus = rwkv_prep([rows(t) for t in cur], [rows(t) for t in buf_kg], None,
                                                        wts["rwkv"])
        o_a, s_ret_new = retention_step(qa, ka, va, ga, s_ret, jnp.float32(pos0))
        o_b, s_rwkv_new = rwkv_step(*(flat(to_nat(t)) for t in (r, w, k, kk, kka)), flat(v), s_rwkv)
    post = (flat(bonus), flat(g), wts["rwkv_ln"], wts["ones_bd"])
    return o_a, flat(o_b), post, s_ret_new, s_rwkv_new, buf_new


def _mix_odd(h, g_norm, lru_h, lru_buf, pages, wts, layer, is_prompt):
    b, l, d = h.shape
    m = b * l
    gw = d // 2
    u = norm_matmul(h.reshape(m, d), g_norm, wts["cd_w_in"], gw, head_major=(1, 2))
    seq = lambda t: t.reshape(b, l, gw)
    lam_init = 0.8 - 0.6 * math.exp(-0.3 * layer)
    xr = seq(u[3])
    if is_prompt:
        o_c = diff_attn_prompt(seq(u[0]), seq(u[1]), seq(u[2]), wts["diff_lam"], wts["diff_subln"], lam_init)
        o_c = o_c.reshape(m, gw)
        o_d, h_last = lru_prompt(xr, seq(u[4]), lru_buf, lru_h, wts["lru"])
    else:
        cache_k, cache_v, page_table = pages
        o_c = diff_attn_decode(u[0], u[1], u[2], cache_k, cache_v, page_table, wts["diff_lam"],
                               wts["diff_subln"], lam_init)
        o_d, h_last = lru_step(u[3], lru_buf, u[4], lru_h, wts["lru"])
    buf_new = jnp.concatenate([lru_buf, xr], axis=1)[:, l:]
    n_heads = gw // (2 * HEAD_DIM)
    k_new = u[5].reshape(b, l, n_heads, 2 * HEAD_DIM)
    v_new = u[6].reshape(b, l, n_heads, 2 * HEAD_DIM)
    return o_c, o_d.reshape(m, gw), k_new, v_new, h_last, buf_new


def _trunk(x, p, pos0, s_ret, s_rwkv, s_shift, s_lru_h, s_lru_conv, pages, wts, is_prompt):
    b, l, d = x.shape
    m = b * l
    depth = wts["norm_g"].shape[0]
    h = x.reshape(m, d)
    ret_l, rwkv_l, shift_l, k_l, v_l, lh_l, lc_l = [], [], [], [], [], [], []
    for i in range(depth):
        j = i // 2
        g = wts["norm_g"][i]
        gn = lambda n: g[n:n + 1]
        h = ffn_block(h, gn(0), wts["ffn_in"], wts["ffn_out"], gn(1), i, 0)
        post = None
        if i % 2 == 0:
            o1, o2, post, sr, sw, sb = _mix_even(h.reshape(b, l, d), gn(2), pos0, s_ret[j], s_rwkv[j], s_shift[j],
                                                 wts["even"][j], is_prompt)
            ret_l.append(sr)
            rwkv_l.append(sw)
            shift_l.append(sb)
            w_out = wts["even"][j]["w_out"]
        else:
            pg = None if pages is None else (pages[0][j], pages[1][j], pages[2])
            o1, o2, kn, vn, lh, lc = _mix_odd(h.reshape(b, l, d), gn(2), s_lru_h[j], s_lru_conv[j], pg,
                                              wts["odd"][j], i, is_prompt)
            k_l.append(kn)
            v_l.append(vn)
            lh_l.append(lh)
            lc_l.append(lc)
            w_out = wts["odd"][j]["w_out"]
        h = out_proj(o1, o2, w_out, h, gn(3), post)
        h = ffn_block(h, gn(4), wts["ffn_in"], wts["ffn_out"], gn(5), i, 1)
        h = ple_block(h, gn(6), wts["ple_gate"], p.reshape(depth, m, -1), wts["ple"], gn(7), i)
    st = lambda lst: jnp.stack(lst, axis=0)
    return (h.reshape(b, l, d), st(k_l), st(v_l), st(ret_l), st(rwkv_l), st(shift_l), st(lh_l), st(lc_l))


def kernel(x_prompt, x_sample, cache_k, cache_v, state_ret, state_rwkv, state_rwkv_shift, state_lru_h, state_lru_conv, page_table, p_prompt, p_sample, norm_g, ffn_w_in, ffn_w_out, ple_w, ple_gate_w, ab_w_in, ab_w_out, rwkv_mu, rwkv_w0, rwkv_w1, rwkv_w2, rwkv_a0, rwkv_a1, rwkv_a2, rwkv_g1, rwkv_g2, rwkv_kk, rwkv_ka, rwkv_rk, rwkv_ln, cd_w_in, cd_w_out, diff_lam, diff_subln, lru_conv_w, lru_conv_b, lru_wa, lru_ba, lru_wi, lru_bi, lru_lambda):
    depth = norm_g.shape[0]
    n_a, n_c = state_ret.shape[0], state_lru_h.shape[0]
    bp = x_prompt.shape[0]
    gw = ab_w_out.shape[1] // 2
    bf = lambda t: t.astype(BF16)
    row = lambda t: t.reshape(1, -1)
    ones_bd = _block_diag(jnp.ones((gw // HEAD_DIM, HEAD_DIM, HEAD_DIM), BF16))
    perm = _key_group_perm(gw)
    kg = _to_key_group
    head_kg = perm // HEAD_DIM
    ones_kk = jnp.asarray(head_kg[:, None] == head_kg[None, :], BF16)
    ones_rk = jnp.asarray(head_kg[:, None] == (np.arange(gw) // HEAD_DIM)[None, :], BF16)

    def ab_in_kg(w):
        cols = [w[:, g * gw:(g + 1) * gw] for g in range(w.shape[1] // gw)]
        cols[4], cols[5] = kg(cols[4]), kg(cols[5])
        return bf(jnp.concatenate(cols, axis=1))

    wts = {
        "norm_g": norm_g,
        "ffn_in": bf(ffn_w_in), "ffn_out": bf(ffn_w_out), "ple": bf(ple_w), "ple_gate": bf(ple_gate_w),
        "even": [{
            "ab_w_in": ab_in_kg(ab_w_in[j]), "w_out": bf(ab_w_out[j]), "rwkv_ln": rwkv_ln[j],
            "ones_bd": ones_bd,
            "rwkv": {"mu": jnp.concatenate([kg(rwkv_mu[j][:2]), rwkv_mu[j][2:]], axis=0),
                     "w0": kg(row(rwkv_w0[j])), "w1": bf(rwkv_w1[j]), "w2": bf(kg(rwkv_w2[j])),
                     "a0": kg(row(rwkv_a0[j])), "a1": bf(rwkv_a1[j]), "a2": bf(kg(rwkv_a2[j])),
                     "g1": bf(rwkv_g1[j]), "g2": bf(rwkv_g2[j]), "kk": kg(row(rwkv_kk[j])),
                     "ka": kg(row(rwkv_ka[j])), "rk": kg(row(rwkv_rk[j])),
                     "ones_kk": ones_kk, "ones_rk": ones_rk},
        } for j in range(n_a)],
        "odd": [{
            "cd_w_in": bf(cd_w_in[j]), "w_out": bf(cd_w_out[j]), "diff_lam": diff_lam[j],
            "diff_subln": row(diff_subln[j]),
            "lru": {"conv_w": lru_conv_w[j], "conv_b": row(lru_conv_b[j]), "wa": bf(_block_diag(lru_wa[j])),
                    "ba": row(lru_ba[j]), "wi": bf(_block_diag(lru_wi[j])), "bi": row(lru_bi[j]),
                    "lam": row(lru_lambda[j])},
        } for j in range(n_c)],
    }
    zeros = lambda *shape: jnp.zeros(shape, F32)
    yp, kp, vp, rp, wp, sp, hp, cp = _trunk(
        x_prompt, p_prompt, 0.0, [None] * n_a, [None] * n_a,
        zeros(n_a, bp, 1, 4 * gw), zeros(n_c, bp, gw), zeros(n_c, bp, CONV_W - 1, gw),
        None, wts, True)
    past_len = page_table.shape[1] * cache_k.shape[2]
    n_pool, page = cache_k.shape[1], cache_k.shape[2]
    as_rows = lambda c: c.reshape(n_c, n_pool, page * c.shape[3], c.shape[4])
    pages = (as_rows(cache_k), as_rows(cache_v), page_table)
    ys, ks_, vs, rs, ws, ss, hs, cs = _trunk(
        x_sample, p_sample, float(past_len), state_ret, state_rwkv, state_rwkv_shift,
        state_lru_h, state_lru_conv, pages, wts, False)
    return (yp, ys, kp, vp, rp, wp, sp, hp, cp, ks_, vs, rs, ws, ss, hs, cs)
```

```python
import functools
import math

import jax
import jax.numpy as jnp
import numpy as np
from jax import lax
from jax.experimental import pallas as pl
from jax.experimental.pallas import tpu as pltpu

F32 = jnp.float32
BF16 = jnp.bfloat16

HEAD_DIM = 64
CONV_W = 4
LRU_C = 8.0
ROPE_BASE = 10000.0
EPS = 1e-6
RWKV_GN_EPS = 64e-5
RET_CHUNK = 256
ATTN_BLOCK = 512
SCAN_CHUNK = 128
DECODE_PAGES = 16
SCAN_GROUP = 8
LANES = 128
SUBLANES = 8
VMEM_LIMIT = 48 * 1024 * 1024


def _cparams(sem):
    return pltpu.CompilerParams(dimension_semantics=sem, vmem_limit_bytes=VMEM_LIMIT)


def _tile(n, pref):
    t = min(n, pref)
    while n % t:
        t //= 2
    return t


def _rms(x, g):
    return x * lax.rsqrt(jnp.mean(x * x, axis=-1, keepdims=True) + EPS) * g


def _dot(a, b):
    return jnp.dot(a.astype(BF16), b.astype(BF16), preferred_element_type=F32)


def _seg_sum(x, ones_bd):
    hi = x.astype(BF16)
    lo = (x - hi.astype(F32)).astype(BF16)
    return (jnp.dot(hi, ones_bd, preferred_element_type=F32)
            + jnp.dot(lo, ones_bd, preferred_element_type=F32))


def _ffn_kernel(h_ref, gpre_ref, wg_ref, wu_ref, wo_ref, gpost_ref, o_ref, xn_ref, acc_ref):
    j = pl.program_id(1)

    @pl.when(j == 0)
    def _():
        xn_ref[...] = _rms(h_ref[...], gpre_ref[...]).astype(BF16)
        acc_ref[...] = jnp.zeros_like(acc_ref)

    xn = xn_ref[...]
    gate = jnp.dot(xn, wg_ref[...], preferred_element_type=F32)
    up = jnp.dot(xn, wu_ref[...], preferred_element_type=F32)
    act = (gate * jax.nn.sigmoid(gate) * up).astype(BF16)
    acc_ref[...] += jnp.dot(act, wo_ref[...], preferred_element_type=F32)

    @pl.when(j == pl.num_programs(1) - 1)
    def _():
        o_ref[...] = h_ref[...] + 0.5 * _rms(acc_ref[...], gpost_ref[...])


def ffn_block(h, g_pre, w_in, w_out, g_post, layer, half):
    m, d = h.shape
    f = w_out.shape[2]
    tm, tf = _tile(m, 1024), _tile(f, 512)
    nf = f // tf
    return pl.pallas_call(
        _ffn_kernel,
        grid=(m // tm, nf),
        in_specs=[
            pl.BlockSpec((tm, d), lambda i, j: (i, 0)),
            pl.BlockSpec((1, d), lambda i, j: (0, 0)),
            pl.BlockSpec((None, None, d, tf), lambda i, j: (layer, half, 0, j)),
            pl.BlockSpec((None, None, d, tf), lambda i, j: (layer, half, 0, j + nf)),
            pl.BlockSpec((None, None, tf, d), lambda i, j: (layer, half, j, 0)),
            pl.BlockSpec((1, d), lambda i, j: (0, 0)),
        ],
        out_specs=pl.BlockSpec((tm, d), lambda i, j: (i, 0)),
        out_shape=jax.ShapeDtypeStruct((m, d), F32),
        scratch_shapes=[pltpu.VMEM((tm, d), BF16), pltpu.VMEM((tm, d), F32)],
        compiler_params=_cparams(("parallel", "arbitrary")),
        name="ffn_block",
    )(h, g_pre, w_in, w_in, w_out, g_post)


def _norm_matmul_kernel(h_ref, g_ref, w_ref, *o_refs, n_groups, head_major):
    xn = _rms(h_ref[...], g_ref[...]).astype(BF16)
    tm, tn = o_refs[0].shape
    heads = tn // LANES
    for gi in range(n_groups):
        res = jnp.dot(xn, w_ref[:, gi * tn:(gi + 1) * tn], preferred_element_type=F32)
        o_refs[gi][...] = res
        if gi in head_major:
            hm_ref = o_refs[n_groups + head_major.index(gi)]
            for hh in range(heads):
                hm_ref[pl.ds(hh, tm, stride=heads), :] = res[:, hh * LANES:(hh + 1) * LANES]


def norm_matmul(h, g, w, tn, head_major=()):
    m, d = h.shape
    n = w.shape[1]
    tm = _tile(m, 512)
    heads = tn // LANES
    rows = pl.BlockSpec((tm, tn), lambda i: (i, 0))
    return pl.pallas_call(
        functools.partial(_norm_matmul_kernel, n_groups=n // tn, head_major=tuple(head_major)),
        grid=(m // tm,),
        in_specs=[
            pl.BlockSpec((tm, d), lambda i: (i, 0)),
            pl.BlockSpec((1, d), lambda i: (0, 0)),
            pl.BlockSpec((d, n), lambda i: (0, 0)),
        ],
        out_specs=[rows] * (n // tn) + [pl.BlockSpec((tm * heads, LANES), lambda i: (i, 0))] * len(head_major),
        out_shape=([jax.ShapeDtypeStruct((m, tn), F32)] * (n // tn)
                   + [jax.ShapeDtypeStruct((m * heads, LANES), F32)] * len(head_major)),
        compiler_params=_cparams(("parallel",)),
        name="norm_matmul",
    )(h, g, w)


def _out_proj_kernel(oa_ref, ob_ref, wa_ref, wb_ref, h_ref, g_ref, *rest):
    ob = ob_ref[...]
    if len(rest) > 1:
        bonus_ref, gate_ref, ln_ref, ones_ref = rest[:4]
        ones_bd = ones_ref[...]
        inv = 1.0 / HEAD_DIM
        oc = ob - _seg_sum(ob, ones_bd) * inv
        on = oc * lax.rsqrt(_seg_sum(oc * oc, ones_bd) * inv + RWKV_GN_EPS)
        ln = ln_ref[...]
        ob = (on * ln[0:1] + ln[1:2] + bonus_ref[...]) * gate_ref[...]
    o_ref = rest[-1]
    y = _dot(oa_ref[...], wa_ref[...]) + _dot(ob, wb_ref[...])
    o_ref[...] = h_ref[...] + _rms(y, g_ref[...])


def out_proj(oa, ob, w, h, g, rwkv_post=None):
    m, d = h.shape
    gw = oa.shape[1]
    tm = _tile(m, 512)
    rows = pl.BlockSpec((tm, gw), lambda i: (i, 0))
    extra, extra_specs = [], []
    if rwkv_post is not None:
        bonus, gate, ln, ones_bd = rwkv_post
        extra = [bonus, gate, ln, ones_bd]
        extra_specs = [rows, rows, pl.BlockSpec(ln.shape, lambda i: (0, 0)),
                       pl.BlockSpec(ones_bd.shape, lambda i: (0, 0))]
    return pl.pallas_call(
        _out_proj_kernel,
        grid=(m // tm,),
        in_specs=[
            rows, rows,
            pl.BlockSpec((gw, d), lambda i: (0, 0)),
            pl.BlockSpec((gw, d), lambda i: (1, 0)),
            pl.BlockSpec((tm, d), lambda i: (i, 0)),
            pl.BlockSpec((1, d), lambda i: (0, 0)),
        ] + extra_specs,
        out_specs=pl.BlockSpec((tm, d), lambda i: (i, 0)),
        out_shape=jax.ShapeDtypeStruct((m, d), F32),
        compiler_params=_cparams(("parallel",)),
        name="out_proj",
    )(oa, ob, w, w, h, g, *extra)


def _ple_kernel(h_ref, g6_ref, wg_ref, p_ref, wp_ref, g7_ref, o_ref):
    h = h_ref[...]
    gate = jax.nn.sigmoid(_dot(_rms(h, g6_ref[...]), wg_ref[...]))
    y = gate * _dot(p_ref[...], wp_ref[...])
    o_ref[...] = h + _rms(y, g7_ref[...])


def ple_block(h, g6, wg, p, wp, g7, layer):
    m, d = h.shape
    pd = p.shape[2]
    tm = _tile(m, 512)
    return pl.pallas_call(
        _ple_kernel,
        grid=(m // tm,),
        in_specs=[
            pl.BlockSpec((tm, d), lambda i: (i, 0)),
            pl.BlockSpec((1, d), lambda i: (0, 0)),
            pl.BlockSpec((None, d, d), lambda i: (layer, 0, 0)),
            pl.BlockSpec((None, tm, pd), lambda i: (layer, i, 0)),
            pl.BlockSpec((None, pd, d), lambda i: (layer, 0, 0)),
            pl.BlockSpec((1, d), lambda i: (0, 0)),
        ],
        out_specs=pl.BlockSpec((tm, d), lambda i: (i, 0)),
        out_shape=jax.ShapeDtypeStruct((m, d), F32),
        compiler_params=_cparams(("parallel",)),
        name="ple_block",
    )(h, g6, wg, p, wp, g7)


def _retention_tables(n_heads, c, pos):
    lg = jnp.log1p(-jnp.exp2(-5.0 - jnp.arange(n_heads, dtype=F32)))
    idx = jnp.arange(c, dtype=F32)
    rel = idx[:, None] - idx[None, :]
    dmask = jnp.where(rel[None] >= 0, jnp.exp(jnp.maximum(rel, 0.0)[None] * lg[:, None, None]), 0.0)
    rep = lambda t: jnp.repeat(t, HEAD_DIM, axis=-1)
    q_dec = rep(jnp.exp((idx[:, None] + 1.0) * lg[None, :]))
    k_dec = rep(jnp.exp((c - 1.0 - idx[:, None]) * lg[None, :]))
    c_dec = rep(jnp.exp(c * lg)[None, :])
    half = HEAD_DIM // 2
    freq = 1.0 / (ROPE_BASE ** jnp.linspace(0.0, 1.0, half, dtype=F32))
    ang = pos[:, None] * freq[None, :]
    cos, sin = jnp.cos(ang), jnp.sin(ang)
    cos_t = jnp.tile(jnp.concatenate([cos, cos], axis=-1), (1, n_heads))
    sin_t = jnp.tile(jnp.concatenate([-sin, sin], axis=-1), (1, n_heads))
    return dmask, q_dec, k_dec, c_dec, cos_t, sin_t


def _retention_kernel(q_ref, k_ref, v_ref, g_ref, cos_ref, sin_ref, dmask_ref, qdec_ref, kdec_ref,
                      cdec_ref, ones_ref, o_ref, s_ref, s_scr, *, n_heads):
    c = pl.program_id(1)

    @pl.when(c == 0)
    def _():
        s_scr[...] = jnp.zeros_like(s_scr)

    q, k, v, g = q_ref[0], k_ref[0], v_ref[0], g_ref[0]
    cos, sin = cos_ref[...], sin_ref[...]
    rows, width = q.shape
    lane = lax.broadcasted_iota(jnp.int32, q.shape, 1)
    first_half = (lane % HEAD_DIM) < (HEAD_DIM // 2)

    def rot(x):
        swapped = jnp.where(first_half, pltpu.roll(x, width - HEAD_DIM // 2, 1),
                            pltpu.roll(x, HEAD_DIM // 2, 1))
        return x * cos + swapped * sin

    qr = rot(q)
    kr = rot(k) * (HEAD_DIM ** -0.5)
    kd = kr * kdec_ref[...]
    qdec = qdec_ref[...]
    cdec = cdec_ref[...]
    head_a = lax.broadcasted_iota(jnp.int32, (rows, LANES), 1) < HEAD_DIM
    same_head = (lax.broadcasted_iota(jnp.int32, (LANES, LANES), 0) // HEAD_DIM
                 == lax.broadcasted_iota(jnp.int32, (LANES, LANES), 1) // HEAD_DIM)
    nt = (((1,), (1,)), ((), ()))
    outs = []
    for p in range(n_heads // 2):
        sl = slice(p * LANES, (p + 1) * LANES)
        qp = qr[:, sl].astype(BF16)
        kp = kr[:, sl].astype(BF16)
        vp = v[:, sl].astype(BF16)
        zero = jnp.zeros_like(qp)
        att_a = lax.dot_general(jnp.where(head_a, qp, zero), kp, nt, preferred_element_type=F32) * dmask_ref[2 * p]
        att_b = lax.dot_general(jnp.where(head_a, zero, qp), kp, nt,
                                preferred_element_type=F32) * dmask_ref[2 * p + 1]
        s_old = s_scr[p]
        inner = jnp.where(head_a, jnp.dot(att_a.astype(BF16), vp, preferred_element_type=F32),
                          jnp.dot(att_b.astype(BF16), vp, preferred_element_type=F32))
        outs.append(inner + jnp.dot(qp, s_old.astype(BF16), preferred_element_type=F32) * qdec[:, sl])
        update = lax.dot_general(kd[:, sl].astype(BF16), vp, (((0,), (0,)), ((), ())),
                                 preferred_element_type=F32)
        s_scr[p] = jnp.where(same_head, s_old * cdec[:, sl] + update, 0.0)
    o = jnp.concatenate(outs, axis=1)
    ones_bd = ones_ref[...]
    inv = 1.0 / HEAD_DIM
    oc = o - _seg_sum(o, ones_bd) * inv
    on = oc * lax.rsqrt(_seg_sum(oc * oc, ones_bd) * inv + EPS)
    o_ref[0] = on * (g * jax.nn.sigmoid(g))

    @pl.when(c == pl.num_programs(1) - 1)
    def _():
        for p in range(n_heads // 2):
            s_pair = s_scr[p]
            s_ref[0, 2 * p] = s_pair[:HEAD_DIM, :HEAD_DIM]
            s_ref[0, 2 * p + 1] = s_pair[HEAD_DIM:, HEAD_DIM:]


def retention_prompt(q, k, v, g, pos, ones_bd):
    b, l, width = q.shape
    n_heads = width // HEAD_DIM
    c = _tile(l, RET_CHUNK)
    dmask, q_dec, k_dec, c_dec, cos_t, sin_t = _retention_tables(n_heads, c, pos)
    seq = pl.BlockSpec((1, c, width), lambda i, j: (i, j, 0))
    tab = pl.BlockSpec((c, width), lambda i, j: (j, 0))
    fixed = lambda shape: pl.BlockSpec(shape, lambda i, j: (0,) * len(shape))
    return pl.pallas_call(
        functools.partial(_retention_kernel, n_heads=n_heads),
        grid=(b, l // c),
        in_specs=[seq, seq, seq, seq, tab, tab, fixed((n_heads, c, c)), fixed((c, width)),
                  fixed((c, width)), fixed((1, width)), fixed(ones_bd.shape)],
        out_specs=[seq, pl.BlockSpec((1, n_heads, HEAD_DIM, HEAD_DIM), lambda i, j: (i, 0, 0, 0))],
        out_shape=[jax.ShapeDtypeStruct((b, l, width), F32),
                   jax.ShapeDtypeStruct((b, n_heads, HEAD_DIM, HEAD_DIM), F32)],
        scratch_shapes=[pltpu.VMEM((n_heads // 2, LANES, LANES), F32)],
        compiler_params=_cparams(("parallel", "arbitrary")),
        name="retention_prompt",
    )(q, k, v, g, cos_t, sin_t, dmask, q_dec, k_dec, c_dec, ones_bd)


def _eye_mask():
    return (lax.broadcasted_iota(jnp.int32, (HEAD_DIM, HEAD_DIM), 0)
            == lax.broadcasted_iota(jnp.int32, (HEAD_DIM, HEAD_DIM), 1))


def _to_col(x_row):
    return jnp.sum(jnp.where(_eye_mask(), x_row, 0.0), axis=-1, keepdims=True)


def _to_row(x_col):
    return jnp.sum(jnp.where(_eye_mask(), x_col, 0.0), axis=-2, keepdims=True)


def _retention_step_kernel(q_ref, k_ref, v_ref, g_ref, cos_ref, sin_ref, gam_ref, s0_ref, o_ref, s_ref):
    cos, sin = cos_ref[...], sin_ref[...]

    def rot(x):
        half = HEAD_DIM // 2
        return x * cos + jnp.concatenate([x[..., half:], x[..., :half]], axis=-1) * sin

    q = _to_col(rot(q_ref[...]))
    k = _to_col(rot(k_ref[...]) * (HEAD_DIM ** -0.5))
    v, g = v_ref[...], g_ref[...]
    gam = gam_ref[...]
    s0 = s0_ref[...]
    att = jnp.sum(q * k, axis=2, keepdims=True)
    o = att * v + jnp.sum(q * s0, axis=2, keepdims=True) * gam
    s_ref[...] = s0 * gam + k * v
    oc = o - jnp.mean(o, axis=-1, keepdims=True)
    on = oc * lax.rsqrt(jnp.mean(oc * oc, axis=-1, keepdims=True) + EPS)
    o_ref[...] = on * (g * jax.nn.sigmoid(g))


def retention_step(q, k, v, g, s0, pos):
    b, width = q.shape
    n_heads = width // HEAD_DIM
    bb = _tile(b, 8)
    half = HEAD_DIM // 2
    freq = 1.0 / (ROPE_BASE ** jnp.linspace(0.0, 1.0, half, dtype=F32))
    ang = pos * freq
    cos_c = jnp.concatenate([jnp.cos(ang), jnp.cos(ang)])[None, :]
    sin_c = jnp.concatenate([-jnp.sin(ang), jnp.sin(ang)])[None, :]
    gam = jnp.exp(jnp.log1p(-jnp.exp2(-5.0 - jnp.arange(n_heads, dtype=F32)))).reshape(n_heads, 1, 1)
    row = lambda t: t.reshape(b, n_heads, 1, HEAD_DIM)
    rspec = pl.BlockSpec((bb, n_heads, 1, HEAD_DIM), lambda i: (i, 0, 0, 0))
    sspec = pl.BlockSpec((bb, n_heads, HEAD_DIM, HEAD_DIM), lambda i: (i, 0, 0, 0))
    o, s = pl.pallas_call(
        _retention_step_kernel,
        grid=(b // bb,),
        in_specs=[rspec, rspec, rspec, rspec,
                  pl.BlockSpec((1, HEAD_DIM), lambda i: (0, 0)), pl.BlockSpec((1, HEAD_DIM), lambda i: (0, 0)),
                  pl.BlockSpec((n_heads, 1, 1), lambda i: (0, 0, 0)), sspec],
        out_specs=[rspec, sspec],
        out_shape=[jax.ShapeDtypeStruct((b, n_heads, 1, HEAD_DIM), F32),
                   jax.ShapeDtypeStruct((b, n_heads, HEAD_DIM, HEAD_DIM), F32)],
        compiler_params=_cparams(("parallel",)),
        name="retention_step",
    )(row(q), row(k), row(v), row(g), cos_c, sin_c, gam, s0)
    return o.reshape(b, width), s


def _softplus(x):
    return jnp.maximum(x, 0.0) + jnp.log1p(jnp.exp(-jnp.abs(x)))


def _shifted(cur, carry_row):
    if cur.shape[0] == 1:
        return carry_row
    first = lax.broadcasted_iota(jnp.int32, cur.shape, 0) == 0
    return jnp.where(first, carry_row, pltpu.roll(cur, 1, 0))


PREP_PARAMS = ("mu", "w0", "w1", "w2", "a0", "a1", "a2", "g1", "g2", "kk", "ka", "rk", "ones_kk", "ones_rk")
N_PREP_PARAMS = len(PREP_PARAMS)
N_PREP_OUT = 8


def _rwkv_prep_kernel(*refs, shift_in_kernel):
    cur_refs = refs[:4]
    if shift_in_kernel:
        buf_ref = refs[4]
        n_in = 5
    else:
        prev_refs = refs[4:8]
        n_in = 8
    (mu_ref, w0_ref, w1_ref, w2_ref, a0_ref, a1_ref, a2_ref, g1_ref, g2_ref, kkp_ref, kap_ref, rk_ref,
     ones_kk_ref, ones_rk_ref) = refs[n_in:n_in + N_PREP_PARAMS]
    n_in += N_PREP_PARAMS
    r_out, w_out, k_out, v_out, kk_out, kka_out, g_out, bonus_out = refs[n_in:n_in + N_PREP_OUT]
    cur = [ref[0] for ref in cur_refs]
    width = cur[0].shape[1]
    if shift_in_kernel:
        carry = refs[n_in + N_PREP_OUT]

        @pl.when(pl.program_id(1) == 0)
        def _():
            for gi in range(4):
                carry[gi] = buf_ref[0, :, gi * width:(gi + 1) * width]

        prev = [_shifted(x, carry[gi]) for gi, x in enumerate(cur)]
        for gi, x in enumerate(cur):
            carry[gi] = x[x.shape[0] - 1:, :]
    else:
        prev = [ref[0] for ref in prev_refs]
    mu = mu_ref[...]
    lerp = lambda x, xp, i: x + (xp - x) * mu[i:i + 1]
    zr, pz = cur[3], prev[3]
    r = lerp(cur[0], prev[0], 0)
    kx = lerp(cur[1], prev[1], 1)
    vx = lerp(cur[2], prev[2], 2)
    zw, za, zg = lerp(zr, pz, 3), lerp(zr, pz, 4), lerp(zr, pz, 5)
    wpre = w0_ref[...] + _dot(jnp.tanh(_dot(zw, w1_ref[...])), w2_ref[...])
    decay = jnp.exp(-jnp.exp(-_softplus(-wpre) - 0.5))
    a = jax.nn.sigmoid(a0_ref[...] + _dot(_dot(za, a1_ref[...]), a2_ref[...]))
    g = _dot(jax.nn.sigmoid(_dot(zg, g1_ref[...])), g2_ref[...])
    kk = kx * kkp_ref[...]
    kk = kk / jnp.maximum(jnp.sqrt(_seg_sum(kk * kk, ones_kk_ref[...])), 1e-12)
    k32 = kx * (1.0 + (a - 1.0) * kap_ref[...])
    r_out[0] = r
    w_out[0] = decay
    k_out[0] = k32
    v_out[0] = vx
    kk_out[0] = kk
    kka_out[0] = kk * a
    g_out[0] = g
    bonus_out[0] = _seg_sum(r * k32 * rk_ref[...], ones_rk_ref[...]) * vx


def rwkv_prep(cur, prev, buf, prm):
    b, l, w = cur[0].shape
    tl = _tile(l, 256)
    seq = pl.BlockSpec((1, tl, w), lambda i, j: (i, j, 0))
    full = lambda a: pl.BlockSpec(a.shape, lambda i, j: (0,) * a.ndim)
    params = [prm[n] for n in PREP_PARAMS]
    shift = prev is None
    if shift:
        extra, extra_specs = [buf], [pl.BlockSpec((1, 1, 4 * w), lambda i, j: (i, 0, 0))]
        scratch = [pltpu.VMEM((4, 1, w), F32)]
    else:
        extra, extra_specs, scratch = list(prev), [seq] * 4, []
    return pl.pallas_call(
        functools.partial(_rwkv_prep_kernel, shift_in_kernel=shift),
        grid=(b, l // tl),
        in_specs=[seq] * 4 + extra_specs + [full(a) for a in params],
        out_specs=[seq] * N_PREP_OUT,
        out_shape=[jax.ShapeDtypeStruct((b, l, w), F32)] * N_PREP_OUT,
        scratch_shapes=scratch,
        compiler_params=_cparams(("parallel", "arbitrary")),
        name="rwkv_prep",
    )(*cur, *extra, *params)


def _hi_lo(x):
    hi = x.astype(BF16)
    lo = (x - hi.astype(F32)).astype(BF16)
    return jnp.concatenate([hi, lo], axis=1)


def _value_columns(v8):
    hi = v8.astype(BF16).astype(F32)
    lo = (v8 - hi).astype(BF16).astype(F32)
    stacked = jnp.concatenate([part[:, p * LANES:(p + 1) * LANES]
                               for p in range(v8.shape[1] // LANES) for part in (hi, lo)], axis=0)
    cols = stacked.T
    return jnp.concatenate([cols[:HEAD_DIM], cols[HEAD_DIM:]], axis=1).astype(BF16)


def _rwkv_scan_kernel(r_ref, w_ref, k_ref, kk_ref, kka_ref, v_ref, sel_ref, ones_ref, o_ref, s_ref, s_scr,
                      ot_scr, *, n_heads):
    c = pl.program_id(1)
    n_grp, tc = r_ref.shape[0], r_ref.shape[1]
    head_lanes = LANES // n_heads
    n_kg = HEAD_DIM // head_lanes

    @pl.when(c == 0)
    def _():
        s_scr[...] = jnp.zeros_like(s_scr)

    ot_scr[...] = jnp.zeros_like(ot_scr)
    ones2 = ones_ref[...]
    t_lane = lax.broadcasted_iota(jnp.int32, (n_grp * HEAD_DIM, LANES), 1) % head_lanes
    rows_of = lambda x, g: x[g * HEAD_DIM:(g + 1) * HEAD_DIM]

    def block(tb, carry):
        base = pl.multiple_of(tb * SUBLANES, SUBLANES)
        refs = {"kk": kk_ref, "w": w_ref, "kka": kka_ref, "k": k_ref, "r": r_ref}
        vp = jnp.concatenate([_value_columns(v_ref[g, pl.ds(base, SUBLANES), :]) for g in range(n_grp)], axis=0)
        tile = base // head_lanes
        head_sum = lambda x: jnp.dot(_hi_lo(x), ones2, preferred_element_type=F32)
        for i in range(SUBLANES):
            row = lambda name, g, kg: jnp.broadcast_to(
                refs[name][g, pl.ds(base, SUBLANES), kg * LANES:(kg + 1) * LANES][i:i + 1, :], (HEAD_DIM, LANES))
            sa = head_sum(jnp.concatenate(
                [sum(s_scr[g, kg] * row("kk", g, kg) for kg in range(n_kg)) for g in range(n_grp)], axis=0))
            vcol = jnp.dot(vp, sel_ref[i], preferred_element_type=F32)
            reads = []
            for g in range(n_grp):
                sa_g, vcol_g = rows_of(sa, g), rows_of(vcol, g)
                read = None
                for kg in range(n_kg):
                    s_new = (s_scr[g, kg] * row("w", g, kg) - sa_g * row("kka", g, kg)
                             + vcol_g * row("k", g, kg))
                    s_scr[g, kg] = s_new
                    term = s_new * row("r", g, kg)
                    read = term if read is None else read + term
                reads.append(read)
            o = head_sum(jnp.concatenate(reads, axis=0))
            ot_scr[tile] = jnp.where(t_lane == (base + i) % head_lanes, o, ot_scr[tile])
        return carry

    lax.fori_loop(0, tc // SUBLANES, block, 0)

    for tile in range(tc // head_lanes):
        for g in range(n_grp):
            o_t = rows_of(ot_scr[tile], g).T
            for h in range(n_heads):
                o_ref[g, tile * head_lanes:(tile + 1) * head_lanes, h * HEAD_DIM:(h + 1) * HEAD_DIM] = (
                    o_t[h * head_lanes:(h + 1) * head_lanes, :])

    @pl.when(c == pl.num_programs(1) - 1)
    def _():
        s_ref[...] = s_scr[...]


def _key_group_perm(width):
    n_heads = width // HEAD_DIM
    head_lanes = LANES // n_heads
    n = np.arange(width)
    return (n % LANES) // head_lanes * HEAD_DIM + n // LANES * head_lanes + n % head_lanes


def _to_key_group(t):
    width = t.shape[-1]
    n_heads = width // HEAD_DIM
    head_lanes = LANES // n_heads
    split = t.reshape(*t.shape[:-1], n_heads, HEAD_DIM // head_lanes, head_lanes)
    return jnp.swapaxes(split, -3, -2).reshape(t.shape)


def _from_key_group(t):
    width = t.shape[-1]
    n_heads = width // HEAD_DIM
    head_lanes = LANES // n_heads
    split = t.reshape(*t.shape[:-1], HEAD_DIM // head_lanes, n_heads, head_lanes)
    return jnp.swapaxes(split, -3, -2).reshape(t.shape)


def rwkv_scan_prompt(r, w, k, kk, kka, v):
    b, l, width = r.shape
    n_heads = width // HEAD_DIM
    head_lanes = LANES // n_heads
    n_kg = HEAD_DIM // head_lanes
    tc = _tile(l, SCAN_CHUNK)
    grp = _tile(b, SCAN_GROUP)
    kl = np.arange(LANES)
    col_head = 2 * ((kl % HEAD_DIM) // (2 * SUBLANES)) + kl // HEAD_DIM
    sel = ((kl[None, :, None] % SUBLANES == np.arange(SUBLANES)[:, None, None])
           & (col_head[None, :, None] == kl[None, None, :] // head_lanes))
    k2 = np.arange(2 * LANES)
    ones2 = (k2[:, None] % LANES) // head_lanes == kl[None, :] // head_lanes
    sel, ones2 = jnp.asarray(sel, BF16), jnp.asarray(ones2, BF16)
    seq = pl.BlockSpec((grp, tc, width), lambda i, j: (i, j, 0))
    state = pl.BlockSpec((grp, n_kg, HEAD_DIM, LANES), lambda i, j: (i, 0, 0, 0))
    o, s = pl.pallas_call(
        functools.partial(_rwkv_scan_kernel, n_heads=n_heads),
        grid=(b // grp, l // tc),
        in_specs=[seq, seq, seq, seq, seq, seq,
                  pl.BlockSpec(sel.shape, lambda i, j: (0, 0, 0)),
                  pl.BlockSpec(ones2.shape, lambda i, j: (0, 0))],
        out_specs=[seq, state],
        out_shape=[jax.ShapeDtypeStruct((b, l, width), F32),
                   jax.ShapeDtypeStruct((b, n_kg, HEAD_DIM, LANES), F32)],
        scratch_shapes=[pltpu.VMEM((grp, n_kg, HEAD_DIM, LANES), F32),
                        pltpu.VMEM((tc // head_lanes, grp * HEAD_DIM, LANES), F32)],
        compiler_params=_cparams(("parallel", "arbitrary")),
        name="rwkv_scan_prompt",
    )(r, w, k, kk, kka, v, sel, ones2)
    s = s.reshape(b, n_kg, HEAD_DIM, n_heads, head_lanes).transpose(0, 3, 2, 1, 4)
    return o, s.reshape(b, n_heads, HEAD_DIM, HEAD_DIM)


def _rwkv_step_kernel(r_ref, w_ref, k_ref, kk_ref, kka_ref, v_ref, s0_ref, o_ref, s_ref):
    s0 = s0_ref[...]
    sa = -jnp.sum(s0 * kk_ref[...], axis=-1, keepdims=True)
    s = s0 * w_ref[...] + sa * kka_ref[...] + _to_col(v_ref[...]) * k_ref[...]
    s_ref[...] = s
    o_ref[...] = _to_row(jnp.sum(s * r_ref[...], axis=-1, keepdims=True))


def rwkv_step(r, w, k, kk, kka, v, s0):
    b, width = r.shape
    n_heads = width // HEAD_DIM
    bb = _tile(b, 8)
    row = lambda t: t.reshape(b, n_heads, 1, HEAD_DIM)
    rspec = pl.BlockSpec((bb, n_heads, 1, HEAD_DIM), lambda i: (i, 0, 0, 0))
    sspec = pl.BlockSpec((bb, n_heads, HEAD_DIM, HEAD_DIM), lambda i: (i, 0, 0, 0))
    o, s = pl.pallas_call(
        _rwkv_step_kernel,
        grid=(b // bb,),
        in_specs=[rspec] * 6 + [sspec],
        out_specs=[rspec, sspec],
        out_shape=[jax.ShapeDtypeStruct((b, n_heads, 1, HEAD_DIM), F32),
                   jax.ShapeDtypeStruct((b, n_heads, HEAD_DIM, HEAD_DIM), F32)],
        compiler_params=_cparams(("parallel",)),
        name="rwkv_step",
    )(row(r), row(w), row(k), row(kk), row(kka), row(v), s0)
    return o.reshape(b, width), s


def _diff_lambda(lp, lam_init):
    e1 = jnp.exp(jnp.sum(lp[0:1] * lp[1:2], axis=-1, keepdims=True))
    e2 = jnp.exp(jnp.sum(lp[2:3] * lp[3:4], axis=-1, keepdims=True))
    return e1 - e2 + lam_init


def _diff_attn_kernel(pt_ref, q_ref, k_ref, v_ref, qs_ref, kns_ref, vns_ref, *rest, lam_init, n_heads, n_pages):
    n_cache = (len(rest) - 7) // 2
    kc_refs, vc_refs = rest[:n_cache], rest[n_cache:2 * n_cache]
    lam_ref, subln_ref, o_ref, os_ref, m_scr, l_scr, acc_scr = rest[2 * n_cache:]
    del pt_ref
    lam = _diff_lambda(lam_ref[...], lam_init)
    _prompt_attention(q_ref, k_ref, v_ref, lam, subln_ref, o_ref, m_scr, l_scr, acc_scr, lam_init)
    for r in range(n_cache // n_pages):
        _decode_attention(r, qs_ref, kns_ref, vns_ref, kc_refs[r * n_pages:(r + 1) * n_pages],
                          vc_refs[r * n_pages:(r + 1) * n_pages], lam, subln_ref, os_ref, lam_init, n_heads)


def _prompt_attention(q_ref, k_ref, v_ref, lam, subln_ref, o_ref, m_scr, l_scr, acc_scr, lam_init):
    i = pl.program_id(2)
    tq = q_ref.shape[1]
    tk = tq
    scale = HEAD_DIM ** -0.5
    m_scr[...] = jnp.full_like(m_scr, -jnp.inf)
    l_scr[...] = jnp.zeros_like(l_scr)
    acc_scr[...] = jnp.zeros_like(acc_scr)
    q = (q_ref[0] * scale).astype(BF16)

    def update(j, on_diagonal):
        rows = pl.ds(pl.multiple_of(j * tk, tk), tk)
        k, v = k_ref[0, rows, :].astype(BF16), v_ref[0, rows, :].astype(BF16)
        if on_diagonal:
            visible = (lax.broadcasted_iota(jnp.int32, (tq, tk), 1)
                       <= lax.broadcasted_iota(jnp.int32, (tq, tk), 0))
        for mi in range(2):
            sl = slice(mi * HEAD_DIM, (mi + 1) * HEAD_DIM)
            s = lax.dot_general(q[:, sl], k[:, sl], (((1,), (1,)), ((), ())), preferred_element_type=F32)
            if on_diagonal:
                s = jnp.where(visible, s, -jnp.inf)
            m_old = m_scr[mi]
            m_new = jnp.maximum(m_old, jnp.max(s, axis=-1, keepdims=True))
            alpha = jnp.exp(m_old - m_new)
            p = jnp.exp(s - jnp.concatenate([m_new] * (tk // LANES), axis=1))
            l_scr[mi] = alpha * l_scr[mi] + jnp.sum(p, axis=-1, keepdims=True)
            acc_scr[mi] = alpha * acc_scr[mi] + jnp.dot(p.astype(BF16), v, preferred_element_type=F32)
            m_scr[mi] = m_new

    def below_diagonal(j, carry):
        update(j, False)
        return carry

    lax.fori_loop(0, i, below_diagonal, 0)
    update(i, True)
    o = acc_scr[0] / l_scr[0] - lam * (acc_scr[1] / l_scr[1])
    o_ref[0] = _rms(o, subln_ref[...]) * (1.0 - lam_init)


def _decode_attention(r, q_ref, kn_ref, vn_ref, kc_refs, vc_refs, lam, subln_ref, o_ref, lam_init, n_heads):
    n_rows = 2 * n_heads
    dv = 2 * HEAD_DIM
    scale = HEAD_DIM ** -0.5
    row = lax.broadcasted_iota(jnp.int32, (n_rows, dv), 0)
    lane = lax.broadcasted_iota(jnp.int32, (n_rows, dv), 1)
    qmat = jnp.where(lane // HEAD_DIM == row % 2, q_ref[r], 0.0)
    rows_per = kc_refs[0].shape[1]
    cols = len(kc_refs) * rows_per
    own = (lax.broadcasted_iota(jnp.int32, (n_rows, cols), 1) % n_heads
           == lax.broadcasted_iota(jnp.int32, (n_rows, cols), 0) // 2)
    s = jnp.concatenate(
        [lax.dot_general(qmat.astype(BF16), kc_ref[0].astype(BF16), (((1,), (1,)), ((), ())),
                         preferred_element_type=F32) for kc_ref in kc_refs], axis=1) * scale
    s = jnp.where(own, s, -jnp.inf)
    s_new = jnp.sum(qmat * kn_ref[r], axis=-1, keepdims=True) * scale
    m = jnp.maximum(jnp.max(s, axis=-1, keepdims=True), s_new)
    pr = jnp.exp(s - m)
    p_new = jnp.exp(s_new - m)
    prb = pr.astype(BF16)
    pv = sum(jnp.dot(prb[:, i * rows_per:(i + 1) * rows_per], vc_ref[0].astype(BF16), preferred_element_type=F32)
             for i, vc_ref in enumerate(vc_refs))
    acc = (pv + p_new * vn_ref[r]) / (jnp.sum(pr, axis=-1, keepdims=True) + p_new)
    acc = acc * jnp.where(row % 2 == 0, 1.0, -lam)
    subln = subln_ref[...]
    for h in range(n_heads):
        o = acc[2 * h:2 * h + 1] + acc[2 * h + 1:2 * h + 2]
        o_ref[r, h:h + 1, :] = _rms(o, subln) * (1.0 - lam_init)


def diff_attn(q, k, v, qs, ks, vs, cache_k, cache_v, page_table, lam_p, subln, lam_init):
    b, l, width = q.shape
    bs, n_pages = page_table.shape
    dv = 2 * HEAD_DIM
    n_heads = width // dv
    t = _tile(l, ATTN_BLOCK)
    n = l // t
    n_steps = b * n_heads * n
    per_step = bs // n_steps
    assert per_step * n_steps == bs, "sample rows must divide evenly over the prompt attention grid"
    rows = cache_k.shape[1]
    step = lambda bi, h, i: (bi * n_heads + h) * n + i
    per_map = lambda x: jnp.repeat(x.reshape(bs, n_heads, dv), 2, axis=1)
    qspec = pl.BlockSpec((1, t, dv), lambda bi, h, i, pt: (bi, i, h))
    kspec = pl.BlockSpec((1, l, dv), lambda bi, h, i, pt: (bi, 0, h))
    vec = pl.BlockSpec((per_step, 2 * n_heads, dv), lambda bi, h, i, pt: (step(bi, h, i), 0, 0))
    cache = [pl.BlockSpec((1, rows, dv), functools.partial(
        lambda bi, h, i, pt, r, slot: (pt[(step(bi, h, i) * per_step + r) * n_pages + slot], 0, 0), r=r, slot=slot))
        for r in range(per_step) for slot in range(n_pages)]
    const = lambda x: pl.BlockSpec(x.shape, lambda bi, h, i, pt: (0, 0))
    o, o_s = pl.pallas_call(
        functools.partial(_diff_attn_kernel, lam_init=lam_init, n_heads=n_heads, n_pages=n_pages),
        grid_spec=pltpu.PrefetchScalarGridSpec(
            num_scalar_prefetch=1,
            grid=(b, n_heads, n),
            in_specs=[qspec, kspec, kspec, vec, vec, vec] + cache + cache + [const(lam_p), const(subln)],
            out_specs=[qspec, pl.BlockSpec((per_step, n_heads, dv), lambda bi, h, i, pt: (step(bi, h, i), 0, 0))],
            scratch_shapes=[pltpu.VMEM((2, t, LANES), F32), pltpu.VMEM((2, t, LANES), F32),
                            pltpu.VMEM((2, t, dv), F32)],
        ),
        out_shape=[jax.ShapeDtypeStruct((b, l, width), F32), jax.ShapeDtypeStruct((bs, n_heads, dv), F32)],
        compiler_params=_cparams(("parallel", "parallel", "arbitrary")),
        name="diff_attn",
    )(page_table.reshape(-1), q, k, v, per_map(qs), per_map(ks), per_map(vs),
      *([cache_k] * (per_step * n_pages)), *([cache_v] * (per_step * n_pages)), lam_p, subln)
    return o, o_s.reshape(bs, width)


def _lru_coeffs(x, x1, x2, x3, cw, cb, wa, ba, wi, bi, lam):
    xc = x3 * cw[0:1] + x2 * cw[1:2] + x1 * cw[2:3] + x * cw[3:4]
    xc = xc + cb
    r = jax.nn.sigmoid(_dot(xc, wa) + ba)
    ig = jax.nn.sigmoid(_dot(xc, wi) + bi)
    log_a = -LRU_C * r * _softplus(-lam)
    a = jnp.exp(log_a)
    return a, jnp.sqrt(-jnp.tanh(log_a) * (a * a + 1.0)) * (ig * xc)


def _lru_seq_kernel(x_ref, gr_ref, buf_ref, h0_ref, cw_ref, cb_ref, wa_ref, ba_ref, wi_ref, bi_ref, lam_ref,
                    o_ref, hl_ref, carry, h_scr, a_scr, b_scr):
    tl = x_ref.shape[1]
    n_carry = carry.shape[0]

    @pl.when(pl.program_id(1) == 0)
    def _():
        for d in range(n_carry):
            carry[d] = buf_ref[0, d:d + 1, :]
        h_scr[...] = h0_ref[0]

    x = x_ref[0]
    x1 = _shifted(x, carry[n_carry - 1])
    x2 = _shifted(x1, carry[n_carry - 2])
    x3 = _shifted(x2, carry[n_carry - 3])
    for d in range(n_carry):
        carry[d] = x[tl - n_carry + d:tl - n_carry + d + 1, :]
    a, b = _lru_coeffs(x, x1, x2, x3, cw_ref[...], cb_ref[...], wa_ref[...], ba_ref[...], wi_ref[...],
                       bi_ref[...], lam_ref[...])
    a_scr[...] = a
    b_scr[...] = b
    row_id = lax.broadcasted_iota(jnp.int32, (SUBLANES, x.shape[1]), 0)

    def block(tb, h):
        base = pl.multiple_of(tb * SUBLANES, SUBLANES)
        a8, b8 = a_scr[pl.ds(base, SUBLANES), :], b_scr[pl.ds(base, SUBLANES), :]
        hs = jnp.zeros_like(a8)
        for i in range(SUBLANES):
            h = a8[i:i + 1, :] * h + b8[i:i + 1, :]
            hs = jnp.where(row_id == i, h, hs)
        o_ref[0, pl.ds(base, SUBLANES), :] = hs * jax.nn.gelu(gr_ref[0, pl.ds(base, SUBLANES), :])
        return h

    h = lax.fori_loop(0, tl // SUBLANES, block, h_scr[...])
    h_scr[...] = h
    hl_ref[0] = h


def lru_prompt(x, gr, buf, h0, prm):
    bsz, l, w = x.shape
    tl = _tile(l, 512)
    seq = pl.BlockSpec((1, tl, w), lambda i, j: (i, j, 0))
    vec = pl.BlockSpec((1, 1, w), lambda i, j: (i, 0, 0))
    full = lambda a: pl.BlockSpec(a.shape, lambda i, j: (0,) * a.ndim)
    params = [prm[n] for n in ("conv_w", "conv_b", "wa", "ba", "wi", "bi", "lam")]
    nb = buf.shape[1]
    o, hl = pl.pallas_call(
        _lru_seq_kernel,
        grid=(bsz, l // tl),
        in_specs=[seq, seq, pl.BlockSpec((1, nb, w), lambda i, j: (i, 0, 0)), vec] + [full(a) for a in params],
        out_specs=[seq, vec],
        out_shape=[jax.ShapeDtypeStruct((bsz, l, w), F32), jax.ShapeDtypeStruct((bsz, 1, w), F32)],
        scratch_shapes=[pltpu.VMEM((nb, 1, w), F32), pltpu.VMEM((1, w), F32), pltpu.VMEM((tl, w), F32),
                        pltpu.VMEM((tl, w), F32)],
        compiler_params=_cparams(("parallel", "arbitrary")),
        name="lru_prompt",
    )(x, gr, buf, h0.reshape(bsz, 1, w), *params)
    return o, hl.reshape(bsz, w)


def _lru_step_kernel(x_ref, x1_ref, x2_ref, x3_ref, gr_ref, h0_ref, cw_ref, cb_ref, wa_ref, ba_ref, wi_ref,
                     bi_ref, lam_ref, o_ref, h_ref):
    a, b = _lru_coeffs(x_ref[...], x1_ref[...], x2_ref[...], x3_ref[...], cw_ref[...], cb_ref[...], wa_ref[...],
                       ba_ref[...], wi_ref[...], bi_ref[...], lam_ref[...])
    h = a * h0_ref[...] + b
    h_ref[...] = h
    o_ref[...] = h * jax.nn.gelu(gr_ref[...])


def lru_step(x, buf, gr, h0, prm):
    m, w = x.shape
    tm = _tile(m, 512)
    rows = pl.BlockSpec((tm, w), lambda i: (i, 0))
    full = lambda a: pl.BlockSpec(a.shape, lambda i: (0,) * a.ndim)
    params = [prm[n] for n in ("conv_w", "conv_b", "wa", "ba", "wi", "bi", "lam")]
    nb = buf.shape[1]
    return pl.pallas_call(
        _lru_step_kernel,
        grid=(m // tm,),
        in_specs=[rows] * 6 + [full(a) for a in params],
        out_specs=[rows] * 2,
        out_shape=[jax.ShapeDtypeStruct((m, w), F32)] * 2,
        compiler_params=_cparams(("parallel",)),
        name="lru_step",
    )(x, buf[:, nb - 1], buf[:, nb - 2], buf[:, nb - 3], gr, h0, *params)


def _block_diag(w):
    n, d, e = w.shape
    eye = jnp.eye(n, dtype=w.dtype)
    return (eye[:, None, :, None] * w[:, :, None, :]).reshape(n * d, n * e)


def _mix_even(h, g_norm, pos0, s_ret, s_rwkv, buf, wts, is_prompt):
    b, l, d = h.shape
    m = b * l
    gw = d // 2
    u = norm_matmul(h.reshape(m, d), g_norm, wts["ab_w_in"], gw)
    seq = lambda t: t.reshape(b, l, gw)
    flat = lambda t: t.reshape(m, gw)
    qa, ka, va, ga = u[:4]
    to_kg, to_nat = _to_key_group, _from_key_group
    cur = [seq(t) for t in u[4:8]]
    last = [t[:, l - 1:] for t in cur]
    buf_new = jnp.concatenate([to_nat(last[0]), to_nat(last[1]), last[2], last[3]], axis=-1)
    buf_kg = [to_kg(buf[..., :gw]), to_kg(buf[..., gw:2 * gw]), buf[..., 2 * gw:3 * gw], buf[..., 3 * gw:]]
    if is_prompt:
        r, w, k, v, kk, kka, g, bonus = rwkv_prep(cur, None, jnp.concatenate(buf_kg, axis=-1), wts["rwkv"])
        pos = pos0 + jnp.arange(l, dtype=F32)
        o_a, s_ret_new = retention_prompt(seq(qa), seq(ka), seq(va), seq(ga), pos, wts["ones_bd"])
        o_a = flat(o_a)
        o_b, s_rwkv_new = rwkv_scan_prompt(r, w, k, kk, kka, v)
    else:
        rows = lambda t: t.reshape(1, m, gw)
        r, w, k, v, kk, kka, g, bonus = rwkv_prep([rows(t) for t in cur], [rows(t) for t in buf_kg], None,
                                                        wts["rwkv"])
        o_a, s_ret_new = retention_step(qa, ka, va, ga, s_ret, jnp.float32(pos0))
        o_b, s_rwkv_new = rwkv_step(*(flat(to_nat(t)) for t in (r, w, k, kk, kka)), flat(v), s_rwkv)
    post = (flat(bonus), flat(g), wts["rwkv_ln"], wts["ones_bd"])
    return o_a, flat(o_b), post, s_ret_new, s_rwkv_new, buf_new


def _mix_odd(h, g_norm, lru_h, lru_buf, wts, is_prompt):
    b, l, d = h.shape
    m = b * l
    gw = d // 2
    u = norm_matmul(h.reshape(m, d), g_norm, wts["cd_w_in"], gw, head_major=(1, 2))
    seq = lambda t: t.reshape(b, l, gw)
    xr = seq(u[3])
    if is_prompt:
        o_c = yield seq(u[0]), seq(u[1]), seq(u[2])
        o_c = o_c.reshape(m, gw)
        o_d, h_last = lru_prompt(xr, seq(u[4]), lru_buf, lru_h, wts["lru"])
    else:
        o_c = yield u[0], u[1], u[2]
        o_d, h_last = lru_step(u[3], lru_buf, u[4], lru_h, wts["lru"])
    buf_new = jnp.concatenate([lru_buf, xr], axis=1)[:, l:]
    n_heads = gw // (2 * HEAD_DIM)
    k_new = u[5].reshape(b, l, n_heads, 2 * HEAD_DIM)
    v_new = u[6].reshape(b, l, n_heads, 2 * HEAD_DIM)
    return o_c, o_d.reshape(m, gw), k_new, v_new, h_last, buf_new


def _advance(gen, value):
    try:
        return gen.send(value), None
    except StopIteration as done:
        return None, done.value


def _trunk(x, p, pos0, s_ret, s_rwkv, s_shift, s_lru_h, s_lru_conv, wts, is_prompt):
    b, l, d = x.shape
    m = b * l
    depth = wts["norm_g"].shape[0]
    h = x.reshape(m, d)
    ret_l, rwkv_l, shift_l, k_l, v_l, lh_l, lc_l = [], [], [], [], [], [], []
    for i in range(depth):
        j = i // 2
        g = wts["norm_g"][i]
        gn = lambda n: g[n:n + 1]
        h = ffn_block(h, gn(0), wts["ffn_in"], wts["ffn_out"], gn(1), i, 0)
        post = None
        if i % 2 == 0:
            o1, o2, post, sr, sw, sb = _mix_even(h.reshape(b, l, d), gn(2), pos0, s_ret[j], s_rwkv[j], s_shift[j],
                                                 wts["even"][j], is_prompt)
            ret_l.append(sr)
            rwkv_l.append(sw)
            shift_l.append(sb)
            w_out = wts["even"][j]["w_out"]
        else:
            o1, o2, kn, vn, lh, lc = yield from _mix_odd(h.reshape(b, l, d), gn(2), s_lru_h[j], s_lru_conv[j],
                                                         wts["odd"][j], is_prompt)
            k_l.append(kn)
            v_l.append(vn)
            lh_l.append(lh)
            lc_l.append(lc)
            w_out = wts["odd"][j]["w_out"]
        h = out_proj(o1, o2, w_out, h, gn(3), post)
        h = ffn_block(h, gn(4), wts["ffn_in"], wts["ffn_out"], gn(5), i, 1)
        h = ple_block(h, gn(6), wts["ple_gate"], p.reshape(depth, m, -1), wts["ple"], gn(7), i)
    st = lambda lst: jnp.stack(lst, axis=0)
    return (h.reshape(b, l, d), st(k_l), st(v_l), st(ret_l), st(rwkv_l), st(shift_l), st(lh_l), st(lc_l))


def kernel(x_prompt, x_sample, cache_k, cache_v, state_ret, state_rwkv, state_rwkv_shift, state_lru_h, state_lru_conv, page_table, p_prompt, p_sample, norm_g, ffn_w_in, ffn_w_out, ple_w, ple_gate_w, ab_w_in, ab_w_out, rwkv_mu, rwkv_w0, rwkv_w1, rwkv_w2, rwkv_a0, rwkv_a1, rwkv_a2, rwkv_g1, rwkv_g2, rwkv_kk, rwkv_ka, rwkv_rk, rwkv_ln, cd_w_in, cd_w_out, diff_lam, diff_subln, lru_conv_w, lru_conv_b, lru_wa, lru_ba, lru_wi, lru_bi, lru_lambda):
    depth = norm_g.shape[0]
    n_a, n_c = state_ret.shape[0], state_lru_h.shape[0]
    bp = x_prompt.shape[0]
    gw = ab_w_out.shape[1] // 2
    bf = lambda t: t.astype(BF16)
    row = lambda t: t.reshape(1, -1)
    ones_bd = _block_diag(jnp.ones((gw // HEAD_DIM, HEAD_DIM, HEAD_DIM), BF16))
    perm = _key_group_perm(gw)
    kg = _to_key_group
    head_kg = perm // HEAD_DIM
    ones_kk = jnp.asarray(head_kg[:, None] == head_kg[None, :], BF16)
    ones_rk = jnp.asarray(head_kg[:, None] == (np.arange(gw) // HEAD_DIM)[None, :], BF16)

    def ab_in_kg(w):
        cols = [w[:, g * gw:(g + 1) * gw] for g in range(w.shape[1] // gw)]
        cols[4], cols[5] = kg(cols[4]), kg(cols[5])
        return bf(jnp.concatenate(cols, axis=1))

    wts = {
        "norm_g": norm_g,
        "ffn_in": bf(ffn_w_in), "ffn_out": bf(ffn_w_out), "ple": bf(ple_w), "ple_gate": bf(ple_gate_w),
        "even": [{
            "ab_w_in": ab_in_kg(ab_w_in[j]), "w_out": bf(ab_w_out[j]), "rwkv_ln": rwkv_ln[j],
            "ones_bd": ones_bd,
            "rwkv": {"mu": jnp.concatenate([kg(rwkv_mu[j][:2]), rwkv_mu[j][2:]], axis=0),
                     "w0": kg(row(rwkv_w0[j])), "w1": bf(rwkv_w1[j]), "w2": bf(kg(rwkv_w2[j])),
                     "a0": kg(row(rwkv_a0[j])), "a1": bf(rwkv_a1[j]), "a2": bf(kg(rwkv_a2[j])),
                     "g1": bf(rwkv_g1[j]), "g2": bf(rwkv_g2[j]), "kk": kg(row(rwkv_kk[j])),
                     "ka": kg(row(rwkv_ka[j])), "rk": kg(row(rwkv_rk[j])),
                     "ones_kk": ones_kk, "ones_rk": ones_rk},
        } for j in range(n_a)],
        "odd": [{
            "cd_w_in": bf(cd_w_in[j]), "w_out": bf(cd_w_out[j]), "diff_lam": diff_lam[j],
            "diff_subln": row(diff_subln[j]),
            "lru": {"conv_w": lru_conv_w[j], "conv_b": row(lru_conv_b[j]), "wa": bf(_block_diag(lru_wa[j])),
                    "ba": row(lru_ba[j]), "wi": bf(_block_diag(lru_wi[j])), "bi": row(lru_bi[j]),
                    "lam": row(lru_lambda[j])},
        } for j in range(n_c)],
    }
    zeros = lambda *shape: jnp.zeros(shape, F32)
    past_len = page_table.shape[1] * cache_k.shape[2]
    n_pool, page = cache_k.shape[1], cache_k.shape[2]
    as_rows = lambda c: c.reshape(n_c, n_pool, page * c.shape[3], c.shape[4])
    pages_k, pages_v = as_rows(cache_k), as_rows(cache_v)
    prompt = _trunk(x_prompt, p_prompt, 0.0, [None] * n_a, [None] * n_a,
                    zeros(n_a, bp, 1, 4 * gw), zeros(n_c, bp, gw), zeros(n_c, bp, CONV_W - 1, gw), wts, True)
    sample = _trunk(x_sample, p_sample, float(past_len), state_ret, state_rwkv, state_rwkv_shift,
                    state_lru_h, state_lru_conv, wts, False)
    (qkv_p, out_p), (qkv_s, out_s) = _advance(prompt, None), _advance(sample, None)
    j = 0
    while out_p is None:
        odd = wts["odd"][j]
        lam_init = 0.8 - 0.6 * math.exp(-0.3 * (2 * j + 1))
        o_p, o_s = diff_attn(*qkv_p, *qkv_s, pages_k[j], pages_v[j], page_table, odd["diff_lam"],
                             odd["diff_subln"], lam_init)
        (qkv_p, out_p), (qkv_s, out_s) = _advance(prompt, o_p), _advance(sample, o_s)
        j += 1
    yp, kp, vp, rp, wp, sp, hp, cp = out_p
    ys, ks_, vs, rs, ws, ss, hs, cs = out_s
    return (yp, ys, kp, vp, rp, wp, sp, hp, cp, ks_, vs, rs, ws, ss, hs, cs)
```

```python
import functools
import math

import jax
import jax.numpy as jnp
import numpy as np
from jax import lax
from jax.experimental import pallas as pl
from jax.experimental.pallas import tpu as pltpu

F32 = jnp.float32
BF16 = jnp.bfloat16

HEAD_DIM = 64
CONV_W = 4
LRU_C = 8.0
ROPE_BASE = 10000.0
EPS = 1e-6
RWKV_GN_EPS = 64e-5
RET_CHUNK = 256
ATTN_BLOCK = 512
SCAN_CHUNK = 128
LRU_GROUP = 2
SCAN_GROUP = 8
LANES = 128
SUBLANES = 8
VMEM_LIMIT = 48 * 1024 * 1024


def _cparams(sem):
    return pltpu.CompilerParams(dimension_semantics=sem, vmem_limit_bytes=VMEM_LIMIT)


def _tile(n, pref):
    t = min(n, pref)
    while n % t:
        t //= 2
    return t


def _rms(x, g):
    return x * lax.rsqrt(jnp.mean(x * x, axis=-1, keepdims=True) + EPS) * g


def _dot(a, b):
    return jnp.dot(a.astype(BF16), b.astype(BF16), preferred_element_type=F32)


def _seg_sum(x, ones_bd):
    hi = x.astype(BF16)
    lo = (x - hi.astype(F32)).astype(BF16)
    return (jnp.dot(hi, ones_bd, preferred_element_type=F32)
            + jnp.dot(lo, ones_bd, preferred_element_type=F32))


def _ffn_kernel(h_ref, gpre_ref, wg_ref, wu_ref, wo_ref, gpost_ref, *rest):
    o_ref, xn_ref, acc_ref = rest[-3:]
    ple = rest[:-3]
    j = pl.program_id(1)

    @pl.when(j == 0)
    def _():
        xn_ref[...] = _rms(h_ref[...], gpre_ref[...]).astype(BF16)
        acc_ref[...] = jnp.zeros_like(acc_ref)

    xn = xn_ref[...]
    gate = jnp.dot(xn, wg_ref[...], preferred_element_type=F32)
    up = jnp.dot(xn, wu_ref[...], preferred_element_type=F32)
    act = (gate * jax.nn.sigmoid(gate) * up).astype(BF16)
    acc_ref[...] += jnp.dot(act, wo_ref[...], preferred_element_type=F32)

    @pl.when(j == pl.num_programs(1) - 1)
    def _():
        h = h_ref[...] + 0.5 * _rms(acc_ref[...], gpost_ref[...])
        if ple:
            g6_ref, wgate_ref, p_ref, wp_ref, g7_ref = ple
            gate_p = jax.nn.sigmoid(_dot(_rms(h, g6_ref[...]), wgate_ref[...]))
            h = h + _rms(gate_p * _dot(p_ref[...], wp_ref[...]), g7_ref[...])
        o_ref[...] = h


def ffn_block(h, g_pre, w_in, w_out, g_post, layer, half, ple=None):
    m, d = h.shape
    f = w_out.shape[2]
    tm, tf = _tile(m, 1024), _tile(f, 512)
    nf = f // tf
    vec = pl.BlockSpec((1, d), lambda i, j: (0, 0))
    extra, extra_specs = [], []
    if ple is not None:
        g6, gate_w, p, emb_w, g7 = ple
        pd = p.shape[2]
        extra = [g6, gate_w, p, emb_w, g7]
        extra_specs = [vec, pl.BlockSpec((None, d, d), lambda i, j: (layer, 0, 0)),
                       pl.BlockSpec((None, tm, pd), lambda i, j: (layer, i, 0)),
                       pl.BlockSpec((None, pd, d), lambda i, j: (layer, 0, 0)), vec]
    return pl.pallas_call(
        _ffn_kernel,
        grid=(m // tm, nf),
        in_specs=[
            pl.BlockSpec((tm, d), lambda i, j: (i, 0)),
            vec,
            pl.BlockSpec((None, None, d, tf), lambda i, j: (layer, half, 0, j)),
            pl.BlockSpec((None, None, d, tf), lambda i, j: (layer, half, 0, j + nf)),
            pl.BlockSpec((None, None, tf, d), lambda i, j: (layer, half, j, 0)),
            vec,
        ] + extra_specs,
        out_specs=pl.BlockSpec((tm, d), lambda i, j: (i, 0)),
        out_shape=jax.ShapeDtypeStruct((m, d), F32),
        scratch_shapes=[pltpu.VMEM((tm, d), BF16), pltpu.VMEM((tm, d), F32)],
        compiler_params=_cparams(("parallel", "arbitrary")),
        name="ffn_block",
    )(h, g_pre, w_in, w_in, w_out, g_post, *extra)


def _norm_matmul_kernel(h_ref, g_ref, w_ref, *o_refs, n_groups, head_major):
    xn = _rms(h_ref[...], g_ref[...]).astype(BF16)
    tm, tn = o_refs[0].shape
    heads = tn // LANES
    for gi in range(n_groups):
        res = jnp.dot(xn, w_ref[:, gi * tn:(gi + 1) * tn], preferred_element_type=F32)
        o_refs[gi][...] = res
        if gi in head_major:
            hm_ref = o_refs[n_groups + head_major.index(gi)]
            for hh in range(heads):
                hm_ref[pl.ds(hh, tm, stride=heads), :] = res[:, hh * LANES:(hh + 1) * LANES]


def norm_matmul(h, g, w, tn, head_major=()):
    m, d = h.shape
    n = w.shape[1]
    tm = _tile(m, 512)
    heads = tn // LANES
    rows = pl.BlockSpec((tm, tn), lambda i: (i, 0))
    return pl.pallas_call(
        functools.partial(_norm_matmul_kernel, n_groups=n // tn, head_major=tuple(head_major)),
        grid=(m // tm,),
        in_specs=[
            pl.BlockSpec((tm, d), lambda i: (i, 0)),
            pl.BlockSpec((1, d), lambda i: (0, 0)),
            pl.BlockSpec((d, n), lambda i: (0, 0)),
        ],
        out_specs=[rows] * (n // tn) + [pl.BlockSpec((tm * heads, LANES), lambda i: (i, 0))] * len(head_major),
        out_shape=([jax.ShapeDtypeStruct((m, tn), F32)] * (n // tn)
                   + [jax.ShapeDtypeStruct((m * heads, LANES), F32)] * len(head_major)),
        compiler_params=_cparams(("parallel",)),
        name="norm_matmul",
    )(h, g, w)


def _out_proj_kernel(oa_ref, ob_ref, wa_ref, wb_ref, h_ref, g_ref, *rest):
    ob = ob_ref[...]
    if len(rest) > 1:
        bonus_ref, gate_ref, ln_ref, ones_ref = rest[:4]
        ones_bd = ones_ref[...]
        inv = 1.0 / HEAD_DIM
        oc = ob - _seg_sum(ob, ones_bd) * inv
        on = oc * lax.rsqrt(_seg_sum(oc * oc, ones_bd) * inv + RWKV_GN_EPS)
        ln = ln_ref[...]
        ob = (on * ln[0:1] + ln[1:2] + bonus_ref[...]) * gate_ref[...]
    o_ref = rest[-1]
    y = _dot(oa_ref[...], wa_ref[...]) + _dot(ob, wb_ref[...])
    o_ref[...] = h_ref[...] + _rms(y, g_ref[...])


def out_proj(oa, ob, w, h, g, rwkv_post=None):
    m, d = h.shape
    gw = oa.shape[1]
    tm = _tile(m, 512)
    rows = pl.BlockSpec((tm, gw), lambda i: (i, 0))
    extra, extra_specs = [], []
    if rwkv_post is not None:
        bonus, gate, ln, ones_bd = rwkv_post
        extra = [bonus, gate, ln, ones_bd]
        extra_specs = [rows, rows, pl.BlockSpec(ln.shape, lambda i: (0, 0)),
                       pl.BlockSpec(ones_bd.shape, lambda i: (0, 0))]
    return pl.pallas_call(
        _out_proj_kernel,
        grid=(m // tm,),
        in_specs=[
            rows, rows,
            pl.BlockSpec((gw, d), lambda i: (0, 0)),
            pl.BlockSpec((gw, d), lambda i: (1, 0)),
            pl.BlockSpec((tm, d), lambda i: (i, 0)),
            pl.BlockSpec((1, d), lambda i: (0, 0)),
        ] + extra_specs,
        out_specs=pl.BlockSpec((tm, d), lambda i: (i, 0)),
        out_shape=jax.ShapeDtypeStruct((m, d), F32),
        compiler_params=_cparams(("parallel",)),
        name="out_proj",
    )(oa, ob, w, w, h, g, *extra)


def _retention_tables(n_heads, c, pos):
    lg = jnp.log1p(-jnp.exp2(-5.0 - jnp.arange(n_heads, dtype=F32)))
    idx = jnp.arange(c, dtype=F32)
    rel = idx[:, None] - idx[None, :]
    dmask = jnp.where(rel[None] >= 0, jnp.exp(jnp.maximum(rel, 0.0)[None] * lg[:, None, None]), 0.0)
    rep = lambda t: jnp.repeat(t, HEAD_DIM, axis=-1)
    q_dec = rep(jnp.exp((idx[:, None] + 1.0) * lg[None, :]))
    k_dec = rep(jnp.exp((c - 1.0 - idx[:, None]) * lg[None, :]))
    c_dec = rep(jnp.exp(c * lg)[None, :])
    half = HEAD_DIM // 2
    freq = 1.0 / (ROPE_BASE ** jnp.linspace(0.0, 1.0, half, dtype=F32))
    ang = pos[:, None] * freq[None, :]
    cos, sin = jnp.cos(ang), jnp.sin(ang)
    cos_t = jnp.tile(jnp.concatenate([cos, cos], axis=-1), (1, n_heads))
    sin_t = jnp.tile(jnp.concatenate([-sin, sin], axis=-1), (1, n_heads))
    return dmask, q_dec, k_dec, c_dec, cos_t, sin_t


def _retention_kernel(q_ref, k_ref, v_ref, g_ref, cos_ref, sin_ref, dmask_ref, qdec_ref, kdec_ref,
                      cdec_ref, ones_ref, o_ref, s_ref, s_scr, *, n_heads):
    c = pl.program_id(1)

    @pl.when(c == 0)
    def _():
        s_scr[...] = jnp.zeros_like(s_scr)

    q, k, v, g = q_ref[0], k_ref[0], v_ref[0], g_ref[0]
    cos, sin = cos_ref[...], sin_ref[...]
    rows, width = q.shape
    lane = lax.broadcasted_iota(jnp.int32, q.shape, 1)
    first_half = (lane % HEAD_DIM) < (HEAD_DIM // 2)

    def rot(x):
        swapped = jnp.where(first_half, pltpu.roll(x, width - HEAD_DIM // 2, 1),
                            pltpu.roll(x, HEAD_DIM // 2, 1))
        return x * cos + swapped * sin

    qr = rot(q)
    kr = rot(k) * (HEAD_DIM ** -0.5)
    kd = kr * kdec_ref[...]
    qdec = qdec_ref[...]
    cdec = cdec_ref[...]
    head_a = lax.broadcasted_iota(jnp.int32, (rows, LANES), 1) < HEAD_DIM
    same_head = (lax.broadcasted_iota(jnp.int32, (LANES, LANES), 0) // HEAD_DIM
                 == lax.broadcasted_iota(jnp.int32, (LANES, LANES), 1) // HEAD_DIM)
    nt = (((1,), (1,)), ((), ()))
    outs = []
    for p in range(n_heads // 2):
        sl = slice(p * LANES, (p + 1) * LANES)
        qp = qr[:, sl].astype(BF16)
        kp = kr[:, sl].astype(BF16)
        vp = v[:, sl].astype(BF16)
        zero = jnp.zeros_like(qp)
        att_a = lax.dot_general(jnp.where(head_a, qp, zero), kp, nt, preferred_element_type=F32) * dmask_ref[2 * p]
        att_b = lax.dot_general(jnp.where(head_a, zero, qp), kp, nt,
                                preferred_element_type=F32) * dmask_ref[2 * p + 1]
        s_old = s_scr[p]
        inner = jnp.where(head_a, jnp.dot(att_a.astype(BF16), vp, preferred_element_type=F32),
                          jnp.dot(att_b.astype(BF16), vp, preferred_element_type=F32))
        outs.append(inner + jnp.dot(qp, s_old.astype(BF16), preferred_element_type=F32) * qdec[:, sl])
        update = lax.dot_general(kd[:, sl].astype(BF16), vp, (((0,), (0,)), ((), ())),
                                 preferred_element_type=F32)
        s_scr[p] = jnp.where(same_head, s_old * cdec[:, sl] + update, 0.0)
    o = jnp.concatenate(outs, axis=1)
    ones_bd = ones_ref[...]
    inv = 1.0 / HEAD_DIM
    oc = o - _seg_sum(o, ones_bd) * inv
    on = oc * lax.rsqrt(_seg_sum(oc * oc, ones_bd) * inv + EPS)
    o_ref[0] = on * (g * jax.nn.sigmoid(g))

    @pl.when(c == pl.num_programs(1) - 1)
    def _():
        for p in range(n_heads // 2):
            s_pair = s_scr[p]
            s_ref[0, 2 * p] = s_pair[:HEAD_DIM, :HEAD_DIM]
            s_ref[0, 2 * p + 1] = s_pair[HEAD_DIM:, HEAD_DIM:]


def retention_prompt(q, k, v, g, pos, ones_bd):
    b, l, width = q.shape
    n_heads = width // HEAD_DIM
    c = _tile(l, RET_CHUNK)
    dmask, q_dec, k_dec, c_dec, cos_t, sin_t = _retention_tables(n_heads, c, pos)
    seq = pl.BlockSpec((1, c, width), lambda i, j: (i, j, 0))
    tab = pl.BlockSpec((c, width), lambda i, j: (j, 0))
    fixed = lambda shape: pl.BlockSpec(shape, lambda i, j: (0,) * len(shape))
    return pl.pallas_call(
        functools.partial(_retention_kernel, n_heads=n_heads),
        grid=(b, l // c),
        in_specs=[seq, seq, seq, seq, tab, tab, fixed((n_heads, c, c)), fixed((c, width)),
                  fixed((c, width)), fixed((1, width)), fixed(ones_bd.shape)],
        out_specs=[seq, pl.BlockSpec((1, n_heads, HEAD_DIM, HEAD_DIM), lambda i, j: (i, 0, 0, 0))],
        out_shape=[jax.ShapeDtypeStruct((b, l, width), F32),
                   jax.ShapeDtypeStruct((b, n_heads, HEAD_DIM, HEAD_DIM), F32)],
        scratch_shapes=[pltpu.VMEM((n_heads // 2, LANES, LANES), F32)],
        compiler_params=_cparams(("parallel", "arbitrary")),
        name="retention_prompt",
    )(q, k, v, g, cos_t, sin_t, dmask, q_dec, k_dec, c_dec, ones_bd)


def _eye_mask():
    return (lax.broadcasted_iota(jnp.int32, (HEAD_DIM, HEAD_DIM), 0)
            == lax.broadcasted_iota(jnp.int32, (HEAD_DIM, HEAD_DIM), 1))


def _to_col(x_row):
    return jnp.sum(jnp.where(_eye_mask(), x_row, 0.0), axis=-1, keepdims=True)


def _to_row(x_col):
    return jnp.sum(jnp.where(_eye_mask(), x_col, 0.0), axis=-2, keepdims=True)


def _retention_step_kernel(q_ref, k_ref, v_ref, g_ref, cos_ref, sin_ref, gam_ref, s0_ref, o_ref, s_ref):
    cos, sin = cos_ref[...], sin_ref[...]

    def rot(x):
        half = HEAD_DIM // 2
        return x * cos + jnp.concatenate([x[..., half:], x[..., :half]], axis=-1) * sin

    q = _to_col(rot(q_ref[...]))
    k = _to_col(rot(k_ref[...]) * (HEAD_DIM ** -0.5))
    v, g = v_ref[...], g_ref[...]
    gam = gam_ref[...]
    s0 = s0_ref[...]
    att = jnp.sum(q * k, axis=2, keepdims=True)
    o = att * v + jnp.sum(q * s0, axis=2, keepdims=True) * gam
    s_ref[...] = s0 * gam + k * v
    oc = o - jnp.mean(o, axis=-1, keepdims=True)
    on = oc * lax.rsqrt(jnp.mean(oc * oc, axis=-1, keepdims=True) + EPS)
    o_ref[...] = on * (g * jax.nn.sigmoid(g))


def retention_step(q, k, v, g, s0, pos):
    b, width = q.shape
    n_heads = width // HEAD_DIM
    bb = _tile(b, 8)
    half = HEAD_DIM // 2
    freq = 1.0 / (ROPE_BASE ** jnp.linspace(0.0, 1.0, half, dtype=F32))
    ang = pos * freq
    cos_c = jnp.concatenate([jnp.cos(ang), jnp.cos(ang)])[None, :]
    sin_c = jnp.concatenate([-jnp.sin(ang), jnp.sin(ang)])[None, :]
    gam = jnp.exp(jnp.log1p(-jnp.exp2(-5.0 - jnp.arange(n_heads, dtype=F32)))).reshape(n_heads, 1, 1)
    row = lambda t: t.reshape(b, n_heads, 1, HEAD_DIM)
    rspec = pl.BlockSpec((bb, n_heads, 1, HEAD_DIM), lambda i: (i, 0, 0, 0))
    sspec = pl.BlockSpec((bb, n_heads, HEAD_DIM, HEAD_DIM), lambda i: (i, 0, 0, 0))
    o, s = pl.pallas_call(
        _retention_step_kernel,
        grid=(b // bb,),
        in_specs=[rspec, rspec, rspec, rspec,
                  pl.BlockSpec((1, HEAD_DIM), lambda i: (0, 0)), pl.BlockSpec((1, HEAD_DIM), lambda i: (0, 0)),
                  pl.BlockSpec((n_heads, 1, 1), lambda i: (0, 0, 0)), sspec],
        out_specs=[rspec, sspec],
        out_shape=[jax.ShapeDtypeStruct((b, n_heads, 1, HEAD_DIM), F32),
                   jax.ShapeDtypeStruct((b, n_heads, HEAD_DIM, HEAD_DIM), F32)],
        compiler_params=_cparams(("parallel",)),
        name="retention_step",
    )(row(q), row(k), row(v), row(g), cos_c, sin_c, gam, s0)
    return o.reshape(b, width), s


def _softplus(x):
    return jnp.maximum(x, 0.0) + jnp.log1p(jnp.exp(-jnp.abs(x)))


def _shifted(cur, carry_row):
    if cur.shape[0] == 1:
        return carry_row
    first = lax.broadcasted_iota(jnp.int32, cur.shape, 0) == 0
    return jnp.where(first, carry_row, pltpu.roll(cur, 1, 0))


PREP_PARAMS = ("mu", "w0", "w1", "w2", "a0", "a1", "a2", "g1", "g2", "kk", "ka", "rk", "ones_kk", "ones_rk")
N_PREP_PARAMS = len(PREP_PARAMS)
N_PREP_OUT = 8


def _rwkv_prep_kernel(*refs, shift_in_kernel):
    cur_refs = refs[:4]
    if shift_in_kernel:
        buf_ref = refs[4]
        n_in = 5
    else:
        prev_refs = refs[4:8]
        n_in = 8
    (mu_ref, w0_ref, w1_ref, w2_ref, a0_ref, a1_ref, a2_ref, g1_ref, g2_ref, kkp_ref, kap_ref, rk_ref,
     ones_kk_ref, ones_rk_ref) = refs[n_in:n_in + N_PREP_PARAMS]
    n_in += N_PREP_PARAMS
    r_out, w_out, k_out, v_out, kk_out, kka_out, g_out, bonus_out = refs[n_in:n_in + N_PREP_OUT]
    cur = [ref[0] for ref in cur_refs]
    width = cur[0].shape[1]
    if shift_in_kernel:
        carry = refs[n_in + N_PREP_OUT]

        @pl.when(pl.program_id(1) == 0)
        def _():
            for gi in range(4):
                carry[gi] = buf_ref[0, :, gi * width:(gi + 1) * width]

        prev = [_shifted(x, carry[gi]) for gi, x in enumerate(cur)]
        for gi, x in enumerate(cur):
            carry[gi] = x[x.shape[0] - 1:, :]
    else:
        prev = [ref[0] for ref in prev_refs]
    mu = mu_ref[...]
    lerp = lambda x, xp, i: x + (xp - x) * mu[i:i + 1]
    zr, pz = cur[3], prev[3]
    r = lerp(cur[0], prev[0], 0)
    kx = lerp(cur[1], prev[1], 1)
    vx = lerp(cur[2], prev[2], 2)
    zw, za, zg = lerp(zr, pz, 3), lerp(zr, pz, 4), lerp(zr, pz, 5)
    wpre = w0_ref[...] + _dot(jnp.tanh(_dot(zw, w1_ref[...])), w2_ref[...])
    decay = jnp.exp(-jnp.exp(-_softplus(-wpre) - 0.5))
    a = jax.nn.sigmoid(a0_ref[...] + _dot(_dot(za, a1_ref[...]), a2_ref[...]))
    g = _dot(jax.nn.sigmoid(_dot(zg, g1_ref[...])), g2_ref[...])
    kk = kx * kkp_ref[...]
    kk = kk / jnp.maximum(jnp.sqrt(_seg_sum(kk * kk, ones_kk_ref[...])), 1e-12)
    k32 = kx * (1.0 + (a - 1.0) * kap_ref[...])
    r_out[0] = r
    w_out[0] = decay
    k_out[0] = k32
    v_out[0] = vx
    kk_out[0] = kk
    kka_out[0] = kk * a
    g_out[0] = g
    bonus_out[0] = _seg_sum(r * k32 * rk_ref[...], ones_rk_ref[...]) * vx


def rwkv_prep(cur, prev, buf, prm):
    b, l, w = cur[0].shape
    tl = _tile(l, 256)
    seq = pl.BlockSpec((1, tl, w), lambda i, j: (i, j, 0))
    full = lambda a: pl.BlockSpec(a.shape, lambda i, j: (0,) * a.ndim)
    params = [prm[n] for n in PREP_PARAMS]
    shift = prev is None
    if shift:
        extra, extra_specs = [buf], [pl.BlockSpec((1, 1, 4 * w), lambda i, j: (i, 0, 0))]
        scratch = [pltpu.VMEM((4, 1, w), F32)]
    else:
        extra, extra_specs, scratch = list(prev), [seq] * 4, []
    return pl.pallas_call(
        functools.partial(_rwkv_prep_kernel, shift_in_kernel=shift),
        grid=(b, l // tl),
        in_specs=[seq] * 4 + extra_specs + [full(a) for a in params],
        out_specs=[seq] * N_PREP_OUT,
        out_shape=[jax.ShapeDtypeStruct((b, l, w), F32)] * N_PREP_OUT,
        scratch_shapes=scratch,
        compiler_params=_cparams(("parallel", "arbitrary")),
        name="rwkv_prep",
    )(*cur, *extra, *params)


def _hi_lo(x):
    hi = x.astype(BF16)
    lo = (x - hi.astype(F32)).astype(BF16)
    return jnp.concatenate([hi, lo], axis=1)


def _value_columns(v8):
    hi = v8.astype(BF16).astype(F32)
    lo = (v8 - hi).astype(BF16).astype(F32)
    stacked = jnp.concatenate([part[:, p * LANES:(p + 1) * LANES]
                               for p in range(v8.shape[1] // LANES) for part in (hi, lo)], axis=0)
    cols = stacked.T
    return jnp.concatenate([cols[:HEAD_DIM], cols[HEAD_DIM:]], axis=1).astype(BF16)


def _rwkv_scan_kernel(r_ref, w_ref, k_ref, kk_ref, kka_ref, v_ref, sel_ref, ones_ref, o_ref, s_ref, s_scr,
                      ot_scr, *, n_heads):
    c = pl.program_id(1)
    n_grp, tc = r_ref.shape[0], r_ref.shape[1]
    head_lanes = LANES // n_heads
    n_kg = HEAD_DIM // head_lanes

    @pl.when(c == 0)
    def _():
        s_scr[...] = jnp.zeros_like(s_scr)

    ot_scr[...] = jnp.zeros_like(ot_scr)
    ones2 = ones_ref[...]
    t_lane = lax.broadcasted_iota(jnp.int32, (n_grp * HEAD_DIM, LANES), 1) % head_lanes
    rows_of = lambda x, g: x[g * HEAD_DIM:(g + 1) * HEAD_DIM]

    def block(tb, carry):
        base = pl.multiple_of(tb * SUBLANES, SUBLANES)
        refs = {"kk": kk_ref, "w": w_ref, "kka": kka_ref, "k": k_ref, "r": r_ref}
        vp = jnp.concatenate([_value_columns(v_ref[g, pl.ds(base, SUBLANES), :]) for g in range(n_grp)], axis=0)
        tile = base // head_lanes
        head_sum = lambda x: jnp.dot(_hi_lo(x), ones2, preferred_element_type=F32)
        for i in range(SUBLANES):
            row = lambda name, g, kg: jnp.broadcast_to(
                refs[name][g, pl.ds(base, SUBLANES), kg * LANES:(kg + 1) * LANES][i:i + 1, :], (HEAD_DIM, LANES))
            sa = head_sum(jnp.concatenate(
                [sum(s_scr[g, kg] * row("kk", g, kg) for kg in range(n_kg)) for g in range(n_grp)], axis=0))
            vcol = jnp.dot(vp, sel_ref[i], preferred_element_type=F32)
            reads = []
            for g in range(n_grp):
                sa_g, vcol_g = rows_of(sa, g), rows_of(vcol, g)
                read = None
                for kg in range(n_kg):
                    s_new = (s_scr[g, kg] * row("w", g, kg) - sa_g * row("kka", g, kg)
                             + vcol_g * row("k", g, kg))
                    s_scr[g, kg] = s_new
                    term = s_new * row("r", g, kg)
                    read = term if read is None else read + term
                reads.append(read)
            o = head_sum(jnp.concatenate(reads, axis=0))
            ot_scr[tile] = jnp.where(t_lane == (base + i) % head_lanes, o, ot_scr[tile])
        return carry

    lax.fori_loop(0, tc // SUBLANES, block, 0)

    for tile in range(tc // head_lanes):
        for g in range(n_grp):
            o_t = rows_of(ot_scr[tile], g).T
            for h in range(n_heads):
                o_ref[g, tile * head_lanes:(tile + 1) * head_lanes, h * HEAD_DIM:(h + 1) * HEAD_DIM] = (
                    o_t[h * head_lanes:(h + 1) * head_lanes, :])

    @pl.when(c == pl.num_programs(1) - 1)
    def _():
        s_ref[...] = s_scr[...]


def _key_group_perm(width):
    n_heads = width // HEAD_DIM
    head_lanes = LANES // n_heads
    n = np.arange(width)
    return (n % LANES) // head_lanes * HEAD_DIM + n // LANES * head_lanes + n % head_lanes


def _to_key_group(t):
    width = t.shape[-1]
    n_heads = width // HEAD_DIM
    head_lanes = LANES // n_heads
    split = t.reshape(*t.shape[:-1], n_heads, HEAD_DIM // head_lanes, head_lanes)
    return jnp.swapaxes(split, -3, -2).reshape(t.shape)


def _from_key_group(t):
    width = t.shape[-1]
    n_heads = width // HEAD_DIM
    head_lanes = LANES // n_heads
    split = t.reshape(*t.shape[:-1], HEAD_DIM // head_lanes, n_heads, head_lanes)
    return jnp.swapaxes(split, -3, -2).reshape(t.shape)


def rwkv_scan_prompt(r, w, k, kk, kka, v):
    b, l, width = r.shape
    n_heads = width // HEAD_DIM
    head_lanes = LANES // n_heads
    n_kg = HEAD_DIM // head_lanes
    tc = _tile(l, SCAN_CHUNK)
    grp = _tile(b, SCAN_GROUP)
    kl = np.arange(LANES)
    col_head = 2 * ((kl % HEAD_DIM) // (2 * SUBLANES)) + kl // HEAD_DIM
    sel = ((kl[None, :, None] % SUBLANES == np.arange(SUBLANES)[:, None, None])
           & (col_head[None, :, None] == kl[None, None, :] // head_lanes))
    k2 = np.arange(2 * LANES)
    ones2 = (k2[:, None] % LANES) // head_lanes == kl[None, :] // head_lanes
    sel, ones2 = jnp.asarray(sel, BF16), jnp.asarray(ones2, BF16)
    seq = pl.BlockSpec((grp, tc, width), lambda i, j: (i, j, 0))
    state = pl.BlockSpec((grp, n_kg, HEAD_DIM, LANES), lambda i, j: (i, 0, 0, 0))
    o, s = pl.pallas_call(
        functools.partial(_rwkv_scan_kernel, n_heads=n_heads),
        grid=(b // grp, l // tc),
        in_specs=[seq, seq, seq, seq, seq, seq,
                  pl.BlockSpec(sel.shape, lambda i, j: (0, 0, 0)),
                  pl.BlockSpec(ones2.shape, lambda i, j: (0, 0))],
        out_specs=[seq, state],
        out_shape=[jax.ShapeDtypeStruct((b, l, width), F32),
                   jax.ShapeDtypeStruct((b, n_kg, HEAD_DIM, LANES), F32)],
        scratch_shapes=[pltpu.VMEM((grp, n_kg, HEAD_DIM, LANES), F32),
                        pltpu.VMEM((tc // head_lanes, grp * HEAD_DIM, LANES), F32)],
        compiler_params=_cparams(("parallel", "arbitrary")),
        name="rwkv_scan_prompt",
    )(r, w, k, kk, kka, v, sel, ones2)
    s = s.reshape(b, n_kg, HEAD_DIM, n_heads, head_lanes).transpose(0, 3, 2, 1, 4)
    return o, s.reshape(b, n_heads, HEAD_DIM, HEAD_DIM)


def _rwkv_step_kernel(r_ref, w_ref, k_ref, kk_ref, kka_ref, v_ref, s0_ref, o_ref, s_ref):
    s0 = s0_ref[...]
    sa = -jnp.sum(s0 * kk_ref[...], axis=-1, keepdims=True)
    s = s0 * w_ref[...] + sa * kka_ref[...] + _to_col(v_ref[...]) * k_ref[...]
    s_ref[...] = s
    o_ref[...] = _to_row(jnp.sum(s * r_ref[...], axis=-1, keepdims=True))


def rwkv_step(r, w, k, kk, kka, v, s0):
    b, width = r.shape
    n_heads = width // HEAD_DIM
    bb = _tile(b, 8)
    row = lambda t: t.reshape(b, n_heads, 1, HEAD_DIM)
    rspec = pl.BlockSpec((bb, n_heads, 1, HEAD_DIM), lambda i: (i, 0, 0, 0))
    sspec = pl.BlockSpec((bb, n_heads, HEAD_DIM, HEAD_DIM), lambda i: (i, 0, 0, 0))
    o, s = pl.pallas_call(
        _rwkv_step_kernel,
        grid=(b // bb,),
        in_specs=[rspec] * 6 + [sspec],
        out_specs=[rspec, sspec],
        out_shape=[jax.ShapeDtypeStruct((b, n_heads, 1, HEAD_DIM), F32),
                   jax.ShapeDtypeStruct((b, n_heads, HEAD_DIM, HEAD_DIM), F32)],
        compiler_params=_cparams(("parallel",)),
        name="rwkv_step",
    )(row(r), row(w), row(k), row(kk), row(kka), row(v), s0)
    return o.reshape(b, width), s


def _diff_lambda(lp, lam_init):
    e1 = jnp.exp(jnp.sum(lp[0:1] * lp[1:2], axis=-1, keepdims=True))
    e2 = jnp.exp(jnp.sum(lp[2:3] * lp[3:4], axis=-1, keepdims=True))
    return e1 - e2 + lam_init


def _diff_attn_kernel(pt_ref, q_ref, k_ref, v_ref, qs_ref, kns_ref, vns_ref, *rest, lam_init, n_heads, n_pages):
    n_cache = (len(rest) - 7) // 2
    kc_refs, vc_refs = rest[:n_cache], rest[n_cache:2 * n_cache]
    lam_ref, subln_ref, o_ref, os_ref, m_scr, l_scr, acc_scr = rest[2 * n_cache:]
    del pt_ref
    lam = _diff_lambda(lam_ref[...], lam_init)
    _prompt_attention(q_ref, k_ref, v_ref, lam, subln_ref, o_ref, m_scr, l_scr, acc_scr, lam_init)
    for r in range(n_cache // n_pages):
        _decode_attention(r, qs_ref, kns_ref, vns_ref, kc_refs[r * n_pages:(r + 1) * n_pages],
                          vc_refs[r * n_pages:(r + 1) * n_pages], lam, subln_ref, os_ref, lam_init, n_heads)


def _prompt_attention(q_ref, k_ref, v_ref, lam, subln_ref, o_ref, m_scr, l_scr, acc_scr, lam_init):
    i = pl.program_id(2)
    tq = q_ref.shape[1]
    tk = tq
    scale = HEAD_DIM ** -0.5
    m_scr[...] = jnp.full_like(m_scr, -jnp.inf)
    l_scr[...] = jnp.zeros_like(l_scr)
    acc_scr[...] = jnp.zeros_like(acc_scr)
    q = (q_ref[0] * scale).astype(BF16)

    def update(j, on_diagonal):
        rows = pl.ds(pl.multiple_of(j * tk, tk), tk)
        k, v = k_ref[0, rows, :].astype(BF16), v_ref[0, rows, :].astype(BF16)
        if on_diagonal:
            visible = (lax.broadcasted_iota(jnp.int32, (tq, tk), 1)
                       <= lax.broadcasted_iota(jnp.int32, (tq, tk), 0))
        for mi in range(2):
            sl = slice(mi * HEAD_DIM, (mi + 1) * HEAD_DIM)
            s = lax.dot_general(q[:, sl], k[:, sl], (((1,), (1,)), ((), ())), preferred_element_type=F32)
            if on_diagonal:
                s = jnp.where(visible, s, -jnp.inf)
            m_old = m_scr[mi]
            m_new = jnp.maximum(m_old, jnp.max(s, axis=-1, keepdims=True))
            alpha = jnp.exp(m_old - m_new)
            p = jnp.exp(s - jnp.concatenate([m_new] * (tk // LANES), axis=1))
            l_scr[mi] = alpha * l_scr[mi] + jnp.sum(p, axis=-1, keepdims=True)
            acc_scr[mi] = alpha * acc_scr[mi] + jnp.dot(p.astype(BF16), v, preferred_element_type=F32)
            m_scr[mi] = m_new

    def below_diagonal(j, carry):
        update(j, False)
        return carry

    lax.fori_loop(0, i, below_diagonal, 0)
    update(i, True)
    o = acc_scr[0] / l_scr[0] - lam * (acc_scr[1] / l_scr[1])
    o_ref[0] = _rms(o, subln_ref[...]) * (1.0 - lam_init)


def _decode_attention(r, q_ref, kn_ref, vn_ref, kc_refs, vc_refs, lam, subln_ref, o_ref, lam_init, n_heads):
    n_rows = 2 * n_heads
    dv = 2 * HEAD_DIM
    scale = HEAD_DIM ** -0.5
    row = lax.broadcasted_iota(jnp.int32, (n_rows, dv), 0)
    lane = lax.broadcasted_iota(jnp.int32, (n_rows, dv), 1)
    qmat = jnp.where(lane // HEAD_DIM == row % 2, q_ref[r], 0.0)
    rows_per = kc_refs[0].shape[1]
    cols = len(kc_refs) * rows_per
    own = (lax.broadcasted_iota(jnp.int32, (n_rows, cols), 1) % n_heads
           == lax.broadcasted_iota(jnp.int32, (n_rows, cols), 0) // 2)
    s = jnp.concatenate(
        [lax.dot_general(qmat.astype(BF16), kc_ref[0].astype(BF16), (((1,), (1,)), ((), ())),
                         preferred_element_type=F32) for kc_ref in kc_refs], axis=1) * scale
    s = jnp.where(own, s, -jnp.inf)
    s_new = jnp.sum(qmat * kn_ref[r], axis=-1, keepdims=True) * scale
    m = jnp.maximum(jnp.max(s, axis=-1, keepdims=True), s_new)
    pr = jnp.exp(s - m)
    p_new = jnp.exp(s_new - m)
    prb = pr.astype(BF16)
    pv = sum(jnp.dot(prb[:, i * rows_per:(i + 1) * rows_per], vc_ref[0].astype(BF16), preferred_element_type=F32)
             for i, vc_ref in enumerate(vc_refs))
    acc = (pv + p_new * vn_ref[r]) / (jnp.sum(pr, axis=-1, keepdims=True) + p_new)
    acc = acc * jnp.where(row % 2 == 0, 1.0, -lam)
    subln = subln_ref[...]
    for h in range(n_heads):
        o = acc[2 * h:2 * h + 1] + acc[2 * h + 1:2 * h + 2]
        o_ref[r, h:h + 1, :] = _rms(o, subln) * (1.0 - lam_init)


def diff_attn(q, k, v, qs, ks, vs, cache_k, cache_v, page_table, lam_p, subln, lam_init):
    b, l, width = q.shape
    bs, n_pages = page_table.shape
    dv = 2 * HEAD_DIM
    n_heads = width // dv
    t = _tile(l, ATTN_BLOCK)
    n = l // t
    n_steps = b * n_heads * n
    per_step = bs // n_steps
    assert per_step * n_steps == bs, "sample rows must divide evenly over the prompt attention grid"
    rows = cache_k.shape[1]
    step = lambda bi, h, i: (bi * n_heads + h) * n + i
    per_map = lambda x: jnp.repeat(x.reshape(bs, n_heads, dv), 2, axis=1)
    qspec = pl.BlockSpec((1, t, dv), lambda bi, h, i, pt: (bi, i, h))
    kspec = pl.BlockSpec((1, l, dv), lambda bi, h, i, pt: (bi, 0, h))
    vec = pl.BlockSpec((per_step, 2 * n_heads, dv), lambda bi, h, i, pt: (step(bi, h, i), 0, 0))
    cache = [pl.BlockSpec((1, rows, dv), functools.partial(
        lambda bi, h, i, pt, r, slot: (pt[(step(bi, h, i) * per_step + r) * n_pages + slot], 0, 0), r=r, slot=slot))
        for r in range(per_step) for slot in range(n_pages)]
    const = lambda x: pl.BlockSpec(x.shape, lambda bi, h, i, pt: (0, 0))
    o, o_s = pl.pallas_call(
        functools.partial(_diff_attn_kernel, lam_init=lam_init, n_heads=n_heads, n_pages=n_pages),
        grid_spec=pltpu.PrefetchScalarGridSpec(
            num_scalar_prefetch=1,
            grid=(b, n_heads, n),
            in_specs=[qspec, kspec, kspec, vec, vec, vec] + cache + cache + [const(lam_p), const(subln)],
            out_specs=[qspec, pl.BlockSpec((per_step, n_heads, dv), lambda bi, h, i, pt: (step(bi, h, i), 0, 0))],
            scratch_shapes=[pltpu.VMEM((2, t, LANES), F32), pltpu.VMEM((2, t, LANES), F32),
                            pltpu.VMEM((2, t, dv), F32)],
        ),
        out_shape=[jax.ShapeDtypeStruct((b, l, width), F32), jax.ShapeDtypeStruct((bs, n_heads, dv), F32)],
        compiler_params=_cparams(("parallel", "parallel", "arbitrary")),
        name="diff_attn",
    )(page_table.reshape(-1), q, k, v, per_map(qs), per_map(ks), per_map(vs),
      *([cache_k] * (per_step * n_pages)), *([cache_v] * (per_step * n_pages)), lam_p, subln)
    return o, o_s.reshape(bs, width)


def _lru_coeffs(x, x1, x2, x3, cw, cb, wa, ba, wi, bi, lam):
    xc = x3 * cw[0:1] + x2 * cw[1:2] + x1 * cw[2:3] + x * cw[3:4]
    xc = xc + cb
    r = jax.nn.sigmoid(_dot(xc, wa) + ba)
    ig = jax.nn.sigmoid(_dot(xc, wi) + bi)
    log_a = -LRU_C * r * _softplus(-lam)
    a = jnp.exp(log_a)
    return a, jnp.sqrt(-jnp.tanh(log_a) * (a * a + 1.0)) * (ig * xc)


def _lru_seq_kernel(x_ref, gr_ref, buf_ref, h0_ref, cw_ref, cb_ref, wa_ref, ba_ref, wi_ref, bi_ref, lam_ref,
                    o_ref, hl_ref, carry, h_scr, a_scr, b_scr):
    n_grp, tl = x_ref.shape[0], x_ref.shape[1]
    n_carry = carry.shape[1]

    @pl.when(pl.program_id(1) == 0)
    def _():
        for g in range(n_grp):
            for d in range(n_carry):
                carry[g, d] = buf_ref[g, d:d + 1, :]
        h_scr[...] = h0_ref[...]

    for g in range(n_grp):
        x = x_ref[g]
        x1 = _shifted(x, carry[g, n_carry - 1])
        x2 = _shifted(x1, carry[g, n_carry - 2])
        x3 = _shifted(x2, carry[g, n_carry - 3])
        for d in range(n_carry):
            carry[g, d] = x[tl - n_carry + d:tl - n_carry + d + 1, :]
        a_scr[g], b_scr[g] = _lru_coeffs(x, x1, x2, x3, cw_ref[...], cb_ref[...], wa_ref[...], ba_ref[...],
                                         wi_ref[...], bi_ref[...], lam_ref[...])
    row_id = lax.broadcasted_iota(jnp.int32, (SUBLANES, x_ref.shape[2]), 0)

    def block(tb, hs):
        base = pl.multiple_of(tb * SUBLANES, SUBLANES)
        rows = pl.ds(base, SUBLANES)
        a8 = [a_scr[g, rows, :] for g in range(n_grp)]
        b8 = [b_scr[g, rows, :] for g in range(n_grp)]
        hs = list(hs)
        out = [jnp.zeros_like(a8[0]) for _ in range(n_grp)]
        for i in range(SUBLANES):
            for g in range(n_grp):
                hs[g] = a8[g][i:i + 1, :] * hs[g] + b8[g][i:i + 1, :]
                out[g] = jnp.where(row_id == i, hs[g], out[g])
        for g in range(n_grp):
            o_ref[g, rows, :] = out[g] * jax.nn.gelu(gr_ref[g, rows, :])
        return tuple(hs)

    hs = lax.fori_loop(0, tl // SUBLANES, block, tuple(h_scr[g] for g in range(n_grp)))
    for g in range(n_grp):
        h_scr[g] = hs[g]
        hl_ref[g] = hs[g]


def lru_prompt(x, gr, buf, h0, prm):
    bsz, l, w = x.shape
    tl = _tile(l, 512)
    grp = _tile(bsz, LRU_GROUP)
    seq = pl.BlockSpec((grp, tl, w), lambda i, j: (i, j, 0))
    vec = pl.BlockSpec((grp, 1, w), lambda i, j: (i, 0, 0))
    full = lambda a: pl.BlockSpec(a.shape, lambda i, j: (0,) * a.ndim)
    params = [prm[n] for n in ("conv_w", "conv_b", "wa", "ba", "wi", "bi", "lam")]
    nb = buf.shape[1]
    o, hl = pl.pallas_call(
        _lru_seq_kernel,
        grid=(bsz // grp, l // tl),
        in_specs=[seq, seq, pl.BlockSpec((grp, nb, w), lambda i, j: (i, 0, 0)), vec] + [full(a) for a in params],
        out_specs=[seq, vec],
        out_shape=[jax.ShapeDtypeStruct((bsz, l, w), F32), jax.ShapeDtypeStruct((bsz, 1, w), F32)],
        scratch_shapes=[pltpu.VMEM((grp, nb, 1, w), F32), pltpu.VMEM((grp, 1, w), F32),
                        pltpu.VMEM((grp, tl, w), F32), pltpu.VMEM((grp, tl, w), F32)],
        compiler_params=_cparams(("parallel", "arbitrary")),
        name="lru_prompt",
    )(x, gr, buf, h0.reshape(bsz, 1, w), *params)
    return o, hl.reshape(bsz, w)


def _lru_step_kernel(x_ref, x1_ref, x2_ref, x3_ref, gr_ref, h0_ref, cw_ref, cb_ref, wa_ref, ba_ref, wi_ref,
                     bi_ref, lam_ref, o_ref, h_ref):
    a, b = _lru_coeffs(x_ref[...], x1_ref[...], x2_ref[...], x3_ref[...], cw_ref[...], cb_ref[...], wa_ref[...],
                       ba_ref[...], wi_ref[...], bi_ref[...], lam_ref[...])
    h = a * h0_ref[...] + b
    h_ref[...] = h
    o_ref[...] = h * jax.nn.gelu(gr_ref[...])


def lru_step(x, buf, gr, h0, prm):
    m, w = x.shape
    tm = _tile(m, 512)
    rows = pl.BlockSpec((tm, w), lambda i: (i, 0))
    full = lambda a: pl.BlockSpec(a.shape, lambda i: (0,) * a.ndim)
    params = [prm[n] for n in ("conv_w", "conv_b", "wa", "ba", "wi", "bi", "lam")]
    nb = buf.shape[1]
    return pl.pallas_call(
        _lru_step_kernel,
        grid=(m // tm,),
        in_specs=[rows] * 6 + [full(a) for a in params],
        out_specs=[rows] * 2,
        out_shape=[jax.ShapeDtypeStruct((m, w), F32)] * 2,
        compiler_params=_cparams(("parallel",)),
        name="lru_step",
    )(x, buf[:, nb - 1], buf[:, nb - 2], buf[:, nb - 3], gr, h0, *params)


def _block_diag(w):
    n, d, e = w.shape
    eye = jnp.eye(n, dtype=w.dtype)
    return (eye[:, None, :, None] * w[:, :, None, :]).reshape(n * d, n * e)


def _mix_even(h, g_norm, pos0, s_ret, s_rwkv, buf, wts, is_prompt):
    b, l, d = h.shape
    m = b * l
    gw = d // 2
    u = norm_matmul(h.reshape(m, d), g_norm, wts["ab_w_in"], gw)
    seq = lambda t: t.reshape(b, l, gw)
    flat = lambda t: t.reshape(m, gw)
    qa, ka, va, ga = u[:4]
    to_kg, to_nat = _to_key_group, _from_key_group
    cur = [seq(t) for t in u[4:8]]
    last = [t[:, l - 1:] for t in cur]
    buf_new = jnp.concatenate([to_nat(last[0]), to_nat(last[1]), last[2], last[3]], axis=-1)
    buf_kg = [to_kg(buf[..., :gw]), to_kg(buf[..., gw:2 * gw]), buf[..., 2 * gw:3 * gw], buf[..., 3 * gw:]]
    if is_prompt:
        r, w, k, v, kk, kka, g, bonus = rwkv_prep(cur, None, jnp.concatenate(buf_kg, axis=-1), wts["rwkv"])
        pos = pos0 + jnp.arange(l, dtype=F32)
        o_a, s_ret_new = retention_prompt(seq(qa), seq(ka), seq(va), seq(ga), pos, wts["ones_bd"])
        o_a = flat(o_a)
        o_b, s_rwkv_new = rwkv_scan_prompt(r, w, k, kk, kka, v)
    else:
        rows = lambda t: t.reshape(1, m, gw)
        r, w, k, v, kk, kka, g, bonus = rwkv_prep([rows(t) for t in cur], [rows(t) for t in buf_kg], None,
                                                        wts["rwkv"])
        o_a, s_ret_new = retention_step(qa, ka, va, ga, s_ret, jnp.float32(pos0))
        o_b, s_rwkv_new = rwkv_step(*(flat(to_nat(t)) for t in (r, w, k, kk, kka)), flat(v), s_rwkv)
    post = (flat(bonus), flat(g), wts["rwkv_ln"], wts["ones_bd"])
    return o_a, flat(o_b), post, s_ret_new, s_rwkv_new, buf_new


def _mix_odd(h, g_norm, lru_h, lru_buf, wts, is_prompt):
    b, l, d = h.shape
    m = b * l
    gw = d // 2
    u = norm_matmul(h.reshape(m, d), g_norm, wts["cd_w_in"], gw, head_major=(1, 2))
    seq = lambda t: t.reshape(b, l, gw)
    xr = seq(u[3])
    if is_prompt:
        o_c = yield seq(u[0]), seq(u[1]), seq(u[2])
        o_c = o_c.reshape(m, gw)
        o_d, h_last = lru_prompt(xr, seq(u[4]), lru_buf, lru_h, wts["lru"])
    else:
        o_c = yield u[0], u[1], u[2]
        o_d, h_last = lru_step(u[3], lru_buf, u[4], lru_h, wts["lru"])
    buf_new = jnp.concatenate([lru_buf, xr], axis=1)[:, l:]
    n_heads = gw // (2 * HEAD_DIM)
    k_new = u[5].reshape(b, l, n_heads, 2 * HEAD_DIM)
    v_new = u[6].reshape(b, l, n_heads, 2 * HEAD_DIM)
    return o_c, o_d.reshape(m, gw), k_new, v_new, h_last, buf_new


def _advance(gen, value):
    try:
        return gen.send(value), None
    except StopIteration as done:
        return None, done.value


def _trunk(x, p, pos0, s_ret, s_rwkv, s_shift, s_lru_h, s_lru_conv, wts, is_prompt):
    b, l, d = x.shape
    m = b * l
    depth = wts["norm_g"].shape[0]
    h = x.reshape(m, d)
    ret_l, rwkv_l, shift_l, k_l, v_l, lh_l, lc_l = [], [], [], [], [], [], []
    for i in range(depth):
        j = i // 2
        g = wts["norm_g"][i]
        gn = lambda n: g[n:n + 1]
        h = ffn_block(h, gn(0), wts["ffn_in"], wts["ffn_out"], gn(1), i, 0)
        post = None
        if i % 2 == 0:
            o1, o2, post, sr, sw, sb = _mix_even(h.reshape(b, l, d), gn(2), pos0, s_ret[j], s_rwkv[j], s_shift[j],
                                                 wts["even"][j], is_prompt)
            ret_l.append(sr)
            rwkv_l.append(sw)
            shift_l.append(sb)
            w_out = wts["even"][j]["w_out"]
        else:
            o1, o2, kn, vn, lh, lc = yield from _mix_odd(h.reshape(b, l, d), gn(2), s_lru_h[j], s_lru_conv[j],
                                                         wts["odd"][j], is_prompt)
            k_l.append(kn)
            v_l.append(vn)
            lh_l.append(lh)
            lc_l.append(lc)
            w_out = wts["odd"][j]["w_out"]
        h = out_proj(o1, o2, w_out, h, gn(3), post)
        h = ffn_block(h, gn(4), wts["ffn_in"], wts["ffn_out"], gn(5), i, 1,
                      ple=(gn(6), wts["ple_gate"], p.reshape(depth, m, -1), wts["ple"], gn(7)))
    st = lambda lst: jnp.stack(lst, axis=0)
    return (h.reshape(b, l, d), st(k_l), st(v_l), st(ret_l), st(rwkv_l), st(shift_l), st(lh_l), st(lc_l))


def kernel(x_prompt, x_sample, cache_k, cache_v, state_ret, state_rwkv, state_rwkv_shift, state_lru_h, state_lru_conv, page_table, p_prompt, p_sample, norm_g, ffn_w_in, ffn_w_out, ple_w, ple_gate_w, ab_w_in, ab_w_out, rwkv_mu, rwkv_w0, rwkv_w1, rwkv_w2, rwkv_a0, rwkv_a1, rwkv_a2, rwkv_g1, rwkv_g2, rwkv_kk, rwkv_ka, rwkv_rk, rwkv_ln, cd_w_in, cd_w_out, diff_lam, diff_subln, lru_conv_w, lru_conv_b, lru_wa, lru_ba, lru_wi, lru_bi, lru_lambda):
    depth = norm_g.shape[0]
    n_a, n_c = state_ret.shape[0], state_lru_h.shape[0]
    bp = x_prompt.shape[0]
    gw = ab_w_out.shape[1] // 2
    bf = lambda t: t.astype(BF16)
    row = lambda t: t.reshape(1, -1)
    ones_bd = _block_diag(jnp.ones((gw // HEAD_DIM, HEAD_DIM, HEAD_DIM), BF16))
    perm = _key_group_perm(gw)
    kg = _to_key_group
    head_kg = perm // HEAD_DIM
    ones_kk = jnp.asarray(head_kg[:, None] == head_kg[None, :], BF16)
    ones_rk = jnp.asarray(head_kg[:, None] == (np.arange(gw) // HEAD_DIM)[None, :], BF16)

    def ab_in_kg(w):
        cols = [w[:, g * gw:(g + 1) * gw] for g in range(w.shape[1] // gw)]
        cols[4], cols[5] = kg(cols[4]), kg(cols[5])
        return bf(jnp.concatenate(cols, axis=1))

    wts = {
        "norm_g": norm_g,
        "ffn_in": bf(ffn_w_in), "ffn_out": bf(ffn_w_out), "ple": bf(ple_w), "ple_gate": bf(ple_gate_w),
        "even": [{
            "ab_w_in": ab_in_kg(ab_w_in[j]), "w_out": bf(ab_w_out[j]), "rwkv_ln": rwkv_ln[j],
            "ones_bd": ones_bd,
            "rwkv": {"mu": jnp.concatenate([kg(rwkv_mu[j][:2]), rwkv_mu[j][2:]], axis=0),
                     "w0": kg(row(rwkv_w0[j])), "w1": bf(rwkv_w1[j]), "w2": bf(kg(rwkv_w2[j])),
                     "a0": kg(row(rwkv_a0[j])), "a1": bf(rwkv_a1[j]), "a2": bf(kg(rwkv_a2[j])),
                     "g1": bf(rwkv_g1[j]), "g2": bf(rwkv_g2[j]), "kk": kg(row(rwkv_kk[j])),
                     "ka": kg(row(rwkv_ka[j])), "rk": kg(row(rwkv_rk[j])),
                     "ones_kk": ones_kk, "ones_rk": ones_rk},
        } for j in range(n_a)],
        "odd": [{
            "cd_w_in": bf(cd_w_in[j]), "w_out": bf(cd_w_out[j]), "diff_lam": diff_lam[j],
            "diff_subln": row(diff_subln[j]),
            "lru": {"conv_w": lru_conv_w[j], "conv_b": row(lru_conv_b[j]), "wa": bf(_block_diag(lru_wa[j])),
                    "ba": row(lru_ba[j]), "wi": bf(_block_diag(lru_wi[j])), "bi": row(lru_bi[j]),
                    "lam": row(lru_lambda[j])},
        } for j in range(n_c)],
    }
    zeros = lambda *shape: jnp.zeros(shape, F32)
    past_len = page_table.shape[1] * cache_k.shape[2]
    n_pool, page = cache_k.shape[1], cache_k.shape[2]
    as_rows = lambda c: c.reshape(n_c, n_pool, page * c.shape[3], c.shape[4])
    pages_k, pages_v = as_rows(cache_k), as_rows(cache_v)
    prompt = _trunk(x_prompt, p_prompt, 0.0, [None] * n_a, [None] * n_a,
                    zeros(n_a, bp, 1, 4 * gw), zeros(n_c, bp, gw), zeros(n_c, bp, CONV_W - 1, gw), wts, True)
    sample = _trunk(x_sample, p_sample, float(past_len), state_ret, state_rwkv, state_rwkv_shift,
                    state_lru_h, state_lru_conv, wts, False)
    (qkv_p, out_p), (qkv_s, out_s) = _advance(prompt, None), _advance(sample, None)
    j = 0
    while out_p is None:
        odd = wts["odd"][j]
        lam_init = 0.8 - 0.6 * math.exp(-0.3 * (2 * j + 1))
        o_p, o_s = diff_attn(*qkv_p, *qkv_s, pages_k[j], pages_v[j], page_table, odd["diff_lam"],
                             odd["diff_subln"], lam_init)
        (qkv_p, out_p), (qkv_s, out_s) = _advance(prompt, o_p), _advance(sample, o_s)
        j += 1
    yp, kp, vp, rp, wp, sp, hp, cp = out_p
    ys, ks_, vs, rs, ws, ss, hs, cs = out_s
    return (yp, ys, kp, vp, rp, wp, sp, hp, cp, ks_, vs, rs, ws, ss, hs, cs)
```

```python
import functools
import math

import jax
import jax.numpy as jnp
import numpy as np
from jax import lax
from jax.experimental import pallas as pl
from jax.experimental.pallas import tpu as pltpu

F32 = jnp.float32
BF16 = jnp.bfloat16

HEAD_DIM = 64
CONV_W = 4
LRU_C = 8.0
ROPE_BASE = 10000.0
EPS = 1e-6
RWKV_GN_EPS = 64e-5
RET_CHUNK = 256
ATTN_BLOCK = 512
SCAN_CHUNK = 128
LRU_GROUP = 2
SCAN_GROUP = 8
LANES = 128
SUBLANES = 8
VMEM_LIMIT = 48 * 1024 * 1024


def _cparams(sem):
    return pltpu.CompilerParams(dimension_semantics=sem, vmem_limit_bytes=VMEM_LIMIT)


def _tile(n, pref):
    t = min(n, pref)
    while n % t:
        t //= 2
    return t


def _rms(x, g):
    return x * lax.rsqrt(jnp.mean(x * x, axis=-1, keepdims=True) + EPS) * g


def _dot(a, b):
    return jnp.dot(a.astype(BF16), b.astype(BF16), preferred_element_type=F32)


def _seg_sum(x, ones_bd):
    hi = x.astype(BF16)
    lo = (x - hi.astype(F32)).astype(BF16)
    return (jnp.dot(hi, ones_bd, preferred_element_type=F32)
            + jnp.dot(lo, ones_bd, preferred_element_type=F32))


def _ffn_kernel(h_ref, gpre_ref, wg_ref, wu_ref, wo_ref, gpost_ref, *rest):
    o_ref, xn_ref, acc_ref = rest[-3:]
    ple = rest[:-3]
    j = pl.program_id(1)

    @pl.when(j == 0)
    def _():
        xn_ref[...] = _rms(h_ref[...], gpre_ref[...]).astype(BF16)
        acc_ref[...] = jnp.zeros_like(acc_ref)

    xn = xn_ref[...]
    gate = jnp.dot(xn, wg_ref[...], preferred_element_type=F32)
    up = jnp.dot(xn, wu_ref[...], preferred_element_type=F32)
    act = (gate * jax.nn.sigmoid(gate) * up).astype(BF16)
    acc_ref[...] += jnp.dot(act, wo_ref[...], preferred_element_type=F32)

    @pl.when(j == pl.num_programs(1) - 1)
    def _():
        h = h_ref[...] + 0.5 * _rms(acc_ref[...], gpost_ref[...])
        if ple:
            g6_ref, wgate_ref, p_ref, wp_ref, g7_ref = ple
            gate_p = jax.nn.sigmoid(_dot(_rms(h, g6_ref[...]), wgate_ref[...]))
            h = h + _rms(gate_p * _dot(p_ref[...], wp_ref[...]), g7_ref[...])
        o_ref[...] = h


def ffn_block(h, g_pre, w_in, w_out, g_post, layer, half, ple=None):
    m, d = h.shape
    f = w_out.shape[2]
    tm, tf = _tile(m, 1024), _tile(f, 512)
    nf = f // tf
    vec = pl.BlockSpec((1, d), lambda i, j: (0, 0))
    extra, extra_specs = [], []
    if ple is not None:
        g6, gate_w, p, emb_w, g7 = ple
        pd = p.shape[2]
        extra = [g6, gate_w, p, emb_w, g7]
        extra_specs = [vec, pl.BlockSpec((None, d, d), lambda i, j: (layer, 0, 0)),
                       pl.BlockSpec((None, tm, pd), lambda i, j: (layer, i, 0)),
                       pl.BlockSpec((None, pd, d), lambda i, j: (layer, 0, 0)), vec]
    return pl.pallas_call(
        _ffn_kernel,
        grid=(m // tm, nf),
        in_specs=[
            pl.BlockSpec((tm, d), lambda i, j: (i, 0)),
            vec,
            pl.BlockSpec((None, None, d, tf), lambda i, j: (layer, half, 0, j)),
            pl.BlockSpec((None, None, d, tf), lambda i, j: (layer, half, 0, j + nf)),
            pl.BlockSpec((None, None, tf, d), lambda i, j: (layer, half, j, 0)),
            vec,
        ] + extra_specs,
        out_specs=pl.BlockSpec((tm, d), lambda i, j: (i, 0)),
        out_shape=jax.ShapeDtypeStruct((m, d), F32),
        scratch_shapes=[pltpu.VMEM((tm, d), BF16), pltpu.VMEM((tm, d), F32)],
        compiler_params=_cparams(("parallel", "arbitrary")),
        name="ffn_block",
    )(h, g_pre, w_in, w_in, w_out, g_post, *extra)


def _norm_matmul_kernel(h_ref, g_ref, w_ref, *o_refs, n_groups, head_major):
    xn = _rms(h_ref[...], g_ref[...]).astype(BF16)
    tm, tn = o_refs[0].shape
    heads = tn // LANES
    for gi in range(n_groups):
        res = jnp.dot(xn, w_ref[:, gi * tn:(gi + 1) * tn], preferred_element_type=F32)
        o_refs[gi][...] = res
        if gi in head_major:
            hm_ref = o_refs[n_groups + head_major.index(gi)]
            for hh in range(heads):
                hm_ref[pl.ds(hh, tm, stride=heads), :] = res[:, hh * LANES:(hh + 1) * LANES]


def norm_matmul(h, g, w, tn, head_major=()):
    m, d = h.shape
    n = w.shape[1]
    tm = _tile(m, 512)
    heads = tn // LANES
    rows = pl.BlockSpec((tm, tn), lambda i: (i, 0))
    return pl.pallas_call(
        functools.partial(_norm_matmul_kernel, n_groups=n // tn, head_major=tuple(head_major)),
        grid=(m // tm,),
        in_specs=[
            pl.BlockSpec((tm, d), lambda i: (i, 0)),
            pl.BlockSpec((1, d), lambda i: (0, 0)),
            pl.BlockSpec((d, n), lambda i: (0, 0)),
        ],
        out_specs=[rows] * (n // tn) + [pl.BlockSpec((tm * heads, LANES), lambda i: (i, 0))] * len(head_major),
        out_shape=([jax.ShapeDtypeStruct((m, tn), F32)] * (n // tn)
                   + [jax.ShapeDtypeStruct((m * heads, LANES), F32)] * len(head_major)),
        compiler_params=_cparams(("parallel",)),
        name="norm_matmul",
    )(h, g, w)


def _out_proj_kernel(oa_ref, ob_ref, wa_ref, wb_ref, h_ref, g_ref, *rest):
    ob = ob_ref[...]
    if len(rest) > 1:
        bonus_ref, gate_ref, ln_ref, ones_ref = rest[:4]
        ones_bd = ones_ref[...]
        inv = 1.0 / HEAD_DIM
        oc = ob - _seg_sum(ob, ones_bd) * inv
        on = oc * lax.rsqrt(_seg_sum(oc * oc, ones_bd) * inv + RWKV_GN_EPS)
        ln = ln_ref[...]
        ob = (on * ln[0:1] + ln[1:2] + bonus_ref[...]) * gate_ref[...]
    o_ref = rest[-1]
    y = _dot(oa_ref[...], wa_ref[...]) + _dot(ob, wb_ref[...])
    o_ref[...] = h_ref[...] + _rms(y, g_ref[...])


def out_proj(oa, ob, w, h, g, rwkv_post=None):
    m, d = h.shape
    gw = oa.shape[1]
    tm = _tile(m, 512)
    rows = pl.BlockSpec((tm, gw), lambda i: (i, 0))
    extra, extra_specs = [], []
    if rwkv_post is not None:
        bonus, gate, ln, ones_bd = rwkv_post
        extra = [bonus, gate, ln, ones_bd]
        extra_specs = [rows, rows, pl.BlockSpec(ln.shape, lambda i: (0, 0)),
                       pl.BlockSpec(ones_bd.shape, lambda i: (0, 0))]
    return pl.pallas_call(
        _out_proj_kernel,
        grid=(m // tm,),
        in_specs=[
            rows, rows,
            pl.BlockSpec((gw, d), lambda i: (0, 0)),
            pl.BlockSpec((gw, d), lambda i: (1, 0)),
            pl.BlockSpec((tm, d), lambda i: (i, 0)),
            pl.BlockSpec((1, d), lambda i: (0, 0)),
        ] + extra_specs,
        out_specs=pl.BlockSpec((tm, d), lambda i: (i, 0)),
        out_shape=jax.ShapeDtypeStruct((m, d), F32),
        compiler_params=_cparams(("parallel",)),
        name="out_proj",
    )(oa, ob, w, w, h, g, *extra)


def _retention_tables(n_heads, c, pos):
    lg = jnp.log1p(-jnp.exp2(-5.0 - jnp.arange(n_heads, dtype=F32)))
    idx = jnp.arange(c, dtype=F32)
    rel = idx[:, None] - idx[None, :]
    dmask = jnp.where(rel[None] >= 0, jnp.exp(jnp.maximum(rel, 0.0)[None] * lg[:, None, None]), 0.0)
    rep = lambda t: jnp.repeat(t, HEAD_DIM, axis=-1)
    q_dec = rep(jnp.exp((idx[:, None] + 1.0) * lg[None, :]))
    k_dec = rep(jnp.exp((c - 1.0 - idx[:, None]) * lg[None, :]))
    c_dec = rep(jnp.exp(c * lg)[None, :])
    half = HEAD_DIM // 2
    freq = 1.0 / (ROPE_BASE ** jnp.linspace(0.0, 1.0, half, dtype=F32))
    ang = pos[:, None] * freq[None, :]
    cos, sin = jnp.cos(ang), jnp.sin(ang)
    cos_t = jnp.tile(jnp.concatenate([cos, cos], axis=-1), (1, n_heads))
    sin_t = jnp.tile(jnp.concatenate([-sin, sin], axis=-1), (1, n_heads))
    return dmask, q_dec, k_dec, c_dec, cos_t, sin_t


def _retention_kernel(q_ref, k_ref, v_ref, g_ref, cos_ref, sin_ref, dmask_ref, qdec_ref, kdec_ref,
                      cdec_ref, ones_ref, o_ref, s_ref, s_scr, *, n_heads):
    c = pl.program_id(1)

    @pl.when(c == 0)
    def _():
        s_scr[...] = jnp.zeros_like(s_scr)

    q, k, v, g = q_ref[0], k_ref[0], v_ref[0], g_ref[0]
    cos, sin = cos_ref[...], sin_ref[...]
    rows, width = q.shape
    lane = lax.broadcasted_iota(jnp.int32, q.shape, 1)
    first_half = (lane % HEAD_DIM) < (HEAD_DIM // 2)

    def rot(x):
        swapped = jnp.where(first_half, pltpu.roll(x, width - HEAD_DIM // 2, 1),
                            pltpu.roll(x, HEAD_DIM // 2, 1))
        return x * cos + swapped * sin

    qr = rot(q)
    kr = rot(k) * (HEAD_DIM ** -0.5)
    kd = kr * kdec_ref[...]
    qdec = qdec_ref[...]
    cdec = cdec_ref[...]
    head_a = lax.broadcasted_iota(jnp.int32, (rows, LANES), 1) < HEAD_DIM
    same_head = (lax.broadcasted_iota(jnp.int32, (LANES, LANES), 0) // HEAD_DIM
                 == lax.broadcasted_iota(jnp.int32, (LANES, LANES), 1) // HEAD_DIM)
    nt = (((1,), (1,)), ((), ()))
    outs = []
    for p in range(n_heads // 2):
        sl = slice(p * LANES, (p + 1) * LANES)
        qp = qr[:, sl].astype(BF16)
        kp = kr[:, sl].astype(BF16)
        vp = v[:, sl].astype(BF16)
        zero = jnp.zeros_like(qp)
        att_a = lax.dot_general(jnp.where(head_a, qp, zero), kp, nt, preferred_element_type=F32) * dmask_ref[2 * p]
        att_b = lax.dot_general(jnp.where(head_a, zero, qp), kp, nt,
                                preferred_element_type=F32) * dmask_ref[2 * p + 1]
        s_old = s_scr[p]
        inner = jnp.where(head_a, jnp.dot(att_a.astype(BF16), vp, preferred_element_type=F32),
                          jnp.dot(att_b.astype(BF16), vp, preferred_element_type=F32))
        outs.append(inner + jnp.dot(qp, s_old.astype(BF16), preferred_element_type=F32) * qdec[:, sl])
        update = lax.dot_general(kd[:, sl].astype(BF16), vp, (((0,), (0,)), ((), ())),
                                 preferred_element_type=F32)
        s_scr[p] = jnp.where(same_head, s_old * cdec[:, sl] + update, 0.0)
    o = jnp.concatenate(outs, axis=1)
    ones_bd = ones_ref[...]
    inv = 1.0 / HEAD_DIM
    oc = o - _seg_sum(o, ones_bd) * inv
    on = oc * lax.rsqrt(_seg_sum(oc * oc, ones_bd) * inv + EPS)
    o_ref[0] = on * (g * jax.nn.sigmoid(g))

    @pl.when(c == pl.num_programs(1) - 1)
    def _():
        for p in range(n_heads // 2):
            s_pair = s_scr[p]
            s_ref[0, 2 * p] = s_pair[:HEAD_DIM, :HEAD_DIM]
            s_ref[0, 2 * p + 1] = s_pair[HEAD_DIM:, HEAD_DIM:]


def retention_prompt(q, k, v, g, pos, ones_bd):
    b, l, width = q.shape
    n_heads = width // HEAD_DIM
    c = _tile(l, RET_CHUNK)
    dmask, q_dec, k_dec, c_dec, cos_t, sin_t = _retention_tables(n_heads, c, pos)
    seq = pl.BlockSpec((1, c, width), lambda i, j: (i, j, 0))
    tab = pl.BlockSpec((c, width), lambda i, j: (j, 0))
    fixed = lambda shape: pl.BlockSpec(shape, lambda i, j: (0,) * len(shape))
    return pl.pallas_call(
        functools.partial(_retention_kernel, n_heads=n_heads),
        grid=(b, l // c),
        in_specs=[seq, seq, seq, seq, tab, tab, fixed((n_heads, c, c)), fixed((c, width)),
                  fixed((c, width)), fixed((1, width)), fixed(ones_bd.shape)],
        out_specs=[seq, pl.BlockSpec((1, n_heads, HEAD_DIM, HEAD_DIM), lambda i, j: (i, 0, 0, 0))],
        out_shape=[jax.ShapeDtypeStruct((b, l, width), F32),
                   jax.ShapeDtypeStruct((b, n_heads, HEAD_DIM, HEAD_DIM), F32)],
        scratch_shapes=[pltpu.VMEM((n_heads // 2, LANES, LANES), F32)],
        compiler_params=_cparams(("parallel", "arbitrary")),
        name="retention_prompt",
    )(q, k, v, g, cos_t, sin_t, dmask, q_dec, k_dec, c_dec, ones_bd)


def _eye_mask():
    return (lax.broadcasted_iota(jnp.int32, (HEAD_DIM, HEAD_DIM), 0)
            == lax.broadcasted_iota(jnp.int32, (HEAD_DIM, HEAD_DIM), 1))


def _to_col(x_row):
    return jnp.sum(jnp.where(_eye_mask(), x_row, 0.0), axis=-1, keepdims=True)


def _to_row(x_col):
    return jnp.sum(jnp.where(_eye_mask(), x_col, 0.0), axis=-2, keepdims=True)


def _retention_step_kernel(q_ref, k_ref, v_ref, g_ref, cos_ref, sin_ref, gam_ref, s0_ref, o_ref, s_ref):
    cos, sin = cos_ref[...], sin_ref[...]

    def rot(x):
        half = HEAD_DIM // 2
        return x * cos + jnp.concatenate([x[..., half:], x[..., :half]], axis=-1) * sin

    q = _to_col(rot(q_ref[...]))
    k = _to_col(rot(k_ref[...]) * (HEAD_DIM ** -0.5))
    v, g = v_ref[...], g_ref[...]
    gam = gam_ref[...]
    s0 = s0_ref[...]
    att = jnp.sum(q * k, axis=2, keepdims=True)
    o = att * v + jnp.sum(q * s0, axis=2, keepdims=True) * gam
    s_ref[...] = s0 * gam + k * v
    oc = o - jnp.mean(o, axis=-1, keepdims=True)
    on = oc * lax.rsqrt(jnp.mean(oc * oc, axis=-1, keepdims=True) + EPS)
    o_ref[...] = on * (g * jax.nn.sigmoid(g))


def retention_step(q, k, v, g, s0, pos):
    b, width = q.shape
    n_heads = width // HEAD_DIM
    bb = _tile(b, 8)
    half = HEAD_DIM // 2
    freq = 1.0 / (ROPE_BASE ** jnp.linspace(0.0, 1.0, half, dtype=F32))
    ang = pos * freq
    cos_c = jnp.concatenate([jnp.cos(ang), jnp.cos(ang)])[None, :]
    sin_c = jnp.concatenate([-jnp.sin(ang), jnp.sin(ang)])[None, :]
    gam = jnp.exp(jnp.log1p(-jnp.exp2(-5.0 - jnp.arange(n_heads, dtype=F32)))).reshape(n_heads, 1, 1)
    row = lambda t: t.reshape(b, n_heads, 1, HEAD_DIM)
    rspec = pl.BlockSpec((bb, n_heads, 1, HEAD_DIM), lambda i: (i, 0, 0, 0))
    sspec = pl.BlockSpec((bb, n_heads, HEAD_DIM, HEAD_DIM), lambda i: (i, 0, 0, 0))
    o, s = pl.pallas_call(
        _retention_step_kernel,
        grid=(b // bb,),
        in_specs=[rspec, rspec, rspec, rspec,
                  pl.BlockSpec((1, HEAD_DIM), lambda i: (0, 0)), pl.BlockSpec((1, HEAD_DIM), lambda i: (0, 0)),
                  pl.BlockSpec((n_heads, 1, 1), lambda i: (0, 0, 0)), sspec],
        out_specs=[rspec, sspec],
        out_shape=[jax.ShapeDtypeStruct((b, n_heads, 1, HEAD_DIM), F32),
                   jax.ShapeDtypeStruct((b, n_heads, HEAD_DIM, HEAD_DIM), F32)],
        compiler_params=_cparams(("parallel",)),
        name="retention_step",
    )(row(q), row(k), row(v), row(g), cos_c, sin_c, gam, s0)
    return o.reshape(b, width), s


def _softplus(x):
    return jnp.maximum(x, 0.0) + jnp.log1p(jnp.exp(-jnp.abs(x)))


def _shifted(cur, carry_row):
    if cur.shape[0] == 1:
        return carry_row
    first = lax.broadcasted_iota(jnp.int32, cur.shape, 0) == 0
    return jnp.where(first, carry_row, pltpu.roll(cur, 1, 0))


PREP_PARAMS = ("mu", "w0", "w1", "w2", "a0", "a1", "a2", "g1", "g2", "kk", "ka", "rk", "ones_kk", "ones_rk")
N_PREP_PARAMS = len(PREP_PARAMS)
N_PREP_OUT = 8


def _rwkv_prep_kernel(*refs, shift_in_kernel):
    cur_refs = refs[:4]
    if shift_in_kernel:
        buf_ref = refs[4]
        n_in = 5
    else:
        prev_refs = refs[4:8]
        n_in = 8
    (mu_ref, w0_ref, w1_ref, w2_ref, a0_ref, a1_ref, a2_ref, g1_ref, g2_ref, kkp_ref, kap_ref, rk_ref,
     ones_kk_ref, ones_rk_ref) = refs[n_in:n_in + N_PREP_PARAMS]
    n_in += N_PREP_PARAMS
    r_out, w_out, k_out, v_out, kk_out, kka_out, g_out, bonus_out = refs[n_in:n_in + N_PREP_OUT]
    cur = [ref[0] for ref in cur_refs]
    width = cur[0].shape[1]
    if shift_in_kernel:
        carry = refs[n_in + N_PREP_OUT]

        @pl.when(pl.program_id(1) == 0)
        def _():
            for gi in range(4):
                carry[gi] = buf_ref[0, :, gi * width:(gi + 1) * width]

        prev = [_shifted(x, carry[gi]) for gi, x in enumerate(cur)]
        for gi, x in enumerate(cur):
            carry[gi] = x[x.shape[0] - 1:, :]
    else:
        prev = [ref[0] for ref in prev_refs]
    mu = mu_ref[...]
    lerp = lambda x, xp, i: x + (xp - x) * mu[i:i + 1]
    zr, pz = cur[3], prev[3]
    r = lerp(cur[0], prev[0], 0)
    kx = lerp(cur[1], prev[1], 1)
    vx = lerp(cur[2], prev[2], 2)
    zw, za, zg = lerp(zr, pz, 3), lerp(zr, pz, 4), lerp(zr, pz, 5)
    wpre = w0_ref[...] + _dot(jnp.tanh(_dot(zw, w1_ref[...])), w2_ref[...])
    decay = jnp.exp(-jnp.exp(-_softplus(-wpre) - 0.5))
    a = jax.nn.sigmoid(a0_ref[...] + _dot(_dot(za, a1_ref[...]), a2_ref[...]))
    g = _dot(jax.nn.sigmoid(_dot(zg, g1_ref[...])), g2_ref[...])
    kk = kx * kkp_ref[...]
    kk = kk / jnp.maximum(jnp.sqrt(_seg_sum(kk * kk, ones_kk_ref[...])), 1e-12)
    k32 = kx * (1.0 + (a - 1.0) * kap_ref[...])
    r_out[0] = r
    w_out[0] = decay
    k_out[0] = k32
    v_out[0] = vx
    kk_out[0] = kk
    kka_out[0] = kk * a
    g_out[0] = g
    bonus_out[0] = _seg_sum(r * k32 * rk_ref[...], ones_rk_ref[...]) * vx


def rwkv_prep(cur, prev, buf, prm):
    b, l, w = cur[0].shape
    tl = _tile(l, 256)
    seq = pl.BlockSpec((1, tl, w), lambda i, j: (i, j, 0))
    full = lambda a: pl.BlockSpec(a.shape, lambda i, j: (0,) * a.ndim)
    params = [prm[n] for n in PREP_PARAMS]
    shift = prev is None
    if shift:
        extra, extra_specs = [buf], [pl.BlockSpec((1, 1, 4 * w), lambda i, j: (i, 0, 0))]
        scratch = [pltpu.VMEM((4, 1, w), F32)]
    else:
        extra, extra_specs, scratch = list(prev), [seq] * 4, []
    return pl.pallas_call(
        functools.partial(_rwkv_prep_kernel, shift_in_kernel=shift),
        grid=(b, l // tl),
        in_specs=[seq] * 4 + extra_specs + [full(a) for a in params],
        out_specs=[seq] * N_PREP_OUT,
        out_shape=[jax.ShapeDtypeStruct((b, l, w), F32)] * N_PREP_OUT,
        scratch_shapes=scratch,
        compiler_params=_cparams(("parallel", "arbitrary")),
        name="rwkv_prep",
    )(*cur, *extra, *params)


def _hi_lo(x):
    hi = x.astype(BF16)
    lo = (x - hi.astype(F32)).astype(BF16)
    return jnp.concatenate([hi, lo], axis=1)


def _value_columns(v8):
    hi = v8.astype(BF16).astype(F32)
    lo = (v8 - hi).astype(BF16).astype(F32)
    stacked = jnp.concatenate([part[:, p * LANES:(p + 1) * LANES]
                               for p in range(v8.shape[1] // LANES) for part in (hi, lo)], axis=0)
    cols = stacked.T
    return jnp.concatenate([cols[:HEAD_DIM], cols[HEAD_DIM:]], axis=1).astype(BF16)


def _rwkv_scan_kernel(r_ref, w_ref, k_ref, kk_ref, kka_ref, v_ref, sel_ref, ones_ref, o_ref, s_ref, s_scr,
                      ot_scr, *, n_heads):
    c = pl.program_id(1)
    n_grp, tc = r_ref.shape[0], r_ref.shape[1]
    head_lanes = LANES // n_heads
    n_kg = HEAD_DIM // head_lanes

    @pl.when(c == 0)
    def _():
        s_scr[...] = jnp.zeros_like(s_scr)

    ot_scr[...] = jnp.zeros_like(ot_scr)
    ones1 = ones_ref[...]
    t_lane = lax.broadcasted_iota(jnp.int32, (n_grp * HEAD_DIM, LANES), 1) % head_lanes
    rows_of = lambda x, g: x[g * HEAD_DIM:(g + 1) * HEAD_DIM]

    def block(tb, carry):
        base = pl.multiple_of(tb * SUBLANES, SUBLANES)
        refs = {"kk": kk_ref, "w": w_ref, "kka": kka_ref, "k": k_ref, "r": r_ref}
        vp = jnp.concatenate([_value_columns(v_ref[g, pl.ds(base, SUBLANES), :]) for g in range(n_grp)], axis=0)
        tile = base // head_lanes
        head_sum = lambda x: jnp.dot(x.astype(BF16), ones1, preferred_element_type=F32)
        for i in range(SUBLANES):
            row = lambda name, g, kg: jnp.broadcast_to(
                refs[name][g, pl.ds(base, SUBLANES), kg * LANES:(kg + 1) * LANES][i:i + 1, :], (HEAD_DIM, LANES))
            sa = head_sum(jnp.concatenate(
                [sum(s_scr[g, kg] * row("kk", g, kg) for kg in range(n_kg)) for g in range(n_grp)], axis=0))
            vcol = jnp.dot(vp, sel_ref[i], preferred_element_type=F32)
            reads = []
            for g in range(n_grp):
                sa_g, vcol_g = rows_of(sa, g), rows_of(vcol, g)
                read = None
                for kg in range(n_kg):
                    s_new = (s_scr[g, kg] * row("w", g, kg) - sa_g * row("kka", g, kg)
                             + vcol_g * row("k", g, kg))
                    s_scr[g, kg] = s_new
                    term = s_new * row("r", g, kg)
                    read = term if read is None else read + term
                reads.append(read)
            o = head_sum(jnp.concatenate(reads, axis=0))
            ot_scr[tile] = jnp.where(t_lane == (base + i) % head_lanes, o, ot_scr[tile])
        return carry

    lax.fori_loop(0, tc // SUBLANES, block, 0)

    for tile in range(tc // head_lanes):
        for g in range(n_grp):
            o_t = rows_of(ot_scr[tile], g).T
            for h in range(n_heads):
                o_ref[g, tile * head_lanes:(tile + 1) * head_lanes, h * HEAD_DIM:(h + 1) * HEAD_DIM] = (
                    o_t[h * head_lanes:(h + 1) * head_lanes, :])

    @pl.when(c == pl.num_programs(1) - 1)
    def _():
        s_ref[...] = s_scr[...]


def _key_group_perm(width):
    n_heads = width // HEAD_DIM
    head_lanes = LANES // n_heads
    n = np.arange(width)
    return (n % LANES) // head_lanes * HEAD_DIM + n // LANES * head_lanes + n % head_lanes


def _to_key_group(t):
    width = t.shape[-1]
    n_heads = width // HEAD_DIM
    head_lanes = LANES // n_heads
    split = t.reshape(*t.shape[:-1], n_heads, HEAD_DIM // head_lanes, head_lanes)
    return jnp.swapaxes(split, -3, -2).reshape(t.shape)


def _from_key_group(t):
    width = t.shape[-1]
    n_heads = width // HEAD_DIM
    head_lanes = LANES // n_heads
    split = t.reshape(*t.shape[:-1], HEAD_DIM // head_lanes, n_heads, head_lanes)
    return jnp.swapaxes(split, -3, -2).reshape(t.shape)


def rwkv_scan_prompt(r, w, k, kk, kka, v):
    b, l, width = r.shape
    n_heads = width // HEAD_DIM
    head_lanes = LANES // n_heads
    n_kg = HEAD_DIM // head_lanes
    tc = _tile(l, SCAN_CHUNK)
    grp = _tile(b, SCAN_GROUP)
    kl = np.arange(LANES)
    col_head = 2 * ((kl % HEAD_DIM) // (2 * SUBLANES)) + kl // HEAD_DIM
    sel = ((kl[None, :, None] % SUBLANES == np.arange(SUBLANES)[:, None, None])
           & (col_head[None, :, None] == kl[None, None, :] // head_lanes))
    ones1 = kl[:, None] // head_lanes == kl[None, :] // head_lanes
    sel, ones1 = jnp.asarray(sel, BF16), jnp.asarray(ones1, BF16)
    seq = pl.BlockSpec((grp, tc, width), lambda i, j: (i, j, 0))
    state = pl.BlockSpec((grp, n_kg, HEAD_DIM, LANES), lambda i, j: (i, 0, 0, 0))
    o, s = pl.pallas_call(
        functools.partial(_rwkv_scan_kernel, n_heads=n_heads),
        grid=(b // grp, l // tc),
        in_specs=[seq, seq, seq, seq, seq, seq,
                  pl.BlockSpec(sel.shape, lambda i, j: (0, 0, 0)),
                  pl.BlockSpec(ones1.shape, lambda i, j: (0, 0))],
        out_specs=[seq, state],
        out_shape=[jax.ShapeDtypeStruct((b, l, width), F32),
                   jax.ShapeDtypeStruct((b, n_kg, HEAD_DIM, LANES), F32)],
        scratch_shapes=[pltpu.VMEM((grp, n_kg, HEAD_DIM, LANES), F32),
                        pltpu.VMEM((tc // head_lanes, grp * HEAD_DIM, LANES), F32)],
        compiler_params=_cparams(("parallel", "arbitrary")),
        name="rwkv_scan_prompt",
    )(r, w, k, kk, kka, v, sel, ones1)
    s = s.reshape(b, n_kg, HEAD_DIM, n_heads, head_lanes).transpose(0, 3, 2, 1, 4)
    return o, s.reshape(b, n_heads, HEAD_DIM, HEAD_DIM)


def _rwkv_step_kernel(r_ref, w_ref, k_ref, kk_ref, kka_ref, v_ref, s0_ref, o_ref, s_ref):
    s0 = s0_ref[...]
    sa = -jnp.sum(s0 * kk_ref[...], axis=-1, keepdims=True)
    s = s0 * w_ref[...] + sa * kka_ref[...] + _to_col(v_ref[...]) * k_ref[...]
    s_ref[...] = s
    o_ref[...] = _to_row(jnp.sum(s * r_ref[...], axis=-1, keepdims=True))


def rwkv_step(r, w, k, kk, kka, v, s0):
    b, width = r.shape
    n_heads = width // HEAD_DIM
    bb = _tile(b, 8)
    row = lambda t: t.reshape(b, n_heads, 1, HEAD_DIM)
    rspec = pl.BlockSpec((bb, n_heads, 1, HEAD_DIM), lambda i: (i, 0, 0, 0))
    sspec = pl.BlockSpec((bb, n_heads, HEAD_DIM, HEAD_DIM), lambda i: (i, 0, 0, 0))
    o, s = pl.pallas_call(
        _rwkv_step_kernel,
        grid=(b // bb,),
        in_specs=[rspec] * 6 + [sspec],
        out_specs=[rspec, sspec],
        out_shape=[jax.ShapeDtypeStruct((b, n_heads, 1, HEAD_DIM), F32),
                   jax.ShapeDtypeStruct((b, n_heads, HEAD_DIM, HEAD_DIM), F32)],
        compiler_params=_cparams(("parallel",)),
        name="rwkv_step",
    )(row(r), row(w), row(k), row(kk), row(kka), row(v), s0)
    return o.reshape(b, width), s


def _diff_lambda(lp, lam_init):
    e1 = jnp.exp(jnp.sum(lp[0:1] * lp[1:2], axis=-1, keepdims=True))
    e2 = jnp.exp(jnp.sum(lp[2:3] * lp[3:4], axis=-1, keepdims=True))
    return e1 - e2 + lam_init


def _diff_attn_kernel(pt_ref, q_ref, k_ref, v_ref, qs_ref, kns_ref, vns_ref, *rest, lam_init, n_heads, n_pages):
    n_cache = (len(rest) - 7) // 2
    kc_refs, vc_refs = rest[:n_cache], rest[n_cache:2 * n_cache]
    lam_ref, subln_ref, o_ref, os_ref, m_scr, l_scr, acc_scr = rest[2 * n_cache:]
    del pt_ref
    lam = _diff_lambda(lam_ref[...], lam_init)
    _prompt_attention(q_ref, k_ref, v_ref, lam, subln_ref, o_ref, m_scr, l_scr, acc_scr, lam_init)
    for r in range(n_cache // n_pages):
        _decode_attention(r, qs_ref, kns_ref, vns_ref, kc_refs[r * n_pages:(r + 1) * n_pages],
                          vc_refs[r * n_pages:(r + 1) * n_pages], lam, subln_ref, os_ref, lam_init, n_heads)


def _prompt_attention(q_ref, k_ref, v_ref, lam, subln_ref, o_ref, m_scr, l_scr, acc_scr, lam_init):
    i = pl.program_id(2)
    tq = q_ref.shape[1]
    tk = tq
    scale = HEAD_DIM ** -0.5
    m_scr[...] = jnp.full_like(m_scr, -jnp.inf)
    l_scr[...] = jnp.zeros_like(l_scr)
    acc_scr[...] = jnp.zeros_like(acc_scr)
    q = (q_ref[0] * scale).astype(BF16)

    def update(j, on_diagonal):
        rows = pl.ds(pl.multiple_of(j * tk, tk), tk)
        k, v = k_ref[0, rows, :].astype(BF16), v_ref[0, rows, :].astype(BF16)
        if on_diagonal:
            visible = (lax.broadcasted_iota(jnp.int32, (tq, tk), 1)
                       <= lax.broadcasted_iota(jnp.int32, (tq, tk), 0))
        for mi in range(2):
            sl = slice(mi * HEAD_DIM, (mi + 1) * HEAD_DIM)
            s = lax.dot_general(q[:, sl], k[:, sl], (((1,), (1,)), ((), ())), preferred_element_type=F32)
            if on_diagonal:
                s = jnp.where(visible, s, -jnp.inf)
            m_old = m_scr[mi]
            m_new = jnp.maximum(m_old, jnp.max(s, axis=-1, keepdims=True))
            alpha = jnp.exp(m_old - m_new)
            p = jnp.exp(s - jnp.concatenate([m_new] * (tk // LANES), axis=1))
            l_scr[mi] = alpha * l_scr[mi] + jnp.sum(p, axis=-1, keepdims=True)
            acc_scr[mi] = alpha * acc_scr[mi] + jnp.dot(p.astype(BF16), v, preferred_element_type=F32)
            m_scr[mi] = m_new

    def below_diagonal(j, carry):
        update(j, False)
        return carry

    lax.fori_loop(0, i, below_diagonal, 0)
    update(i, True)
    o = acc_scr[0] / l_scr[0] - lam * (acc_scr[1] / l_scr[1])
    o_ref[0] = _rms(o, subln_ref[...]) * (1.0 - lam_init)


def _decode_attention(r, q_ref, kn_ref, vn_ref, kc_refs, vc_refs, lam, subln_ref, o_ref, lam_init, n_heads):
    n_rows = 2 * n_heads
    dv = 2 * HEAD_DIM
    scale = HEAD_DIM ** -0.5
    row = lax.broadcasted_iota(jnp.int32, (n_rows, dv), 0)
    lane = lax.broadcasted_iota(jnp.int32, (n_rows, dv), 1)
    qmat = jnp.where(lane // HEAD_DIM == row % 2, q_ref[r], 0.0)
    rows_per = kc_refs[0].shape[1]
    cols = len(kc_refs) * rows_per
    own = (lax.broadcasted_iota(jnp.int32, (n_rows, cols), 1) % n_heads
           == lax.broadcasted_iota(jnp.int32, (n_rows, cols), 0) // 2)
    s = jnp.concatenate(
        [lax.dot_general(qmat.astype(BF16), kc_ref[0].astype(BF16), (((1,), (1,)), ((), ())),
                         preferred_element_type=F32) for kc_ref in kc_refs], axis=1) * scale
    s = jnp.where(own, s, -jnp.inf)
    s_new = jnp.sum(qmat * kn_ref[r], axis=-1, keepdims=True) * scale
    m = jnp.maximum(jnp.max(s, axis=-1, keepdims=True), s_new)
    pr = jnp.exp(s - m)
    p_new = jnp.exp(s_new - m)
    prb = pr.astype(BF16)
    pv = sum(jnp.dot(prb[:, i * rows_per:(i + 1) * rows_per], vc_ref[0].astype(BF16), preferred_element_type=F32)
             for i, vc_ref in enumerate(vc_refs))
    acc = (pv + p_new * vn_ref[r]) / (jnp.sum(pr, axis=-1, keepdims=True) + p_new)
    acc = acc * jnp.where(row % 2 == 0, 1.0, -lam)
    subln = subln_ref[...]
    for h in range(n_heads):
        o = acc[2 * h:2 * h + 1] + acc[2 * h + 1:2 * h + 2]
        o_ref[r, h:h + 1, :] = _rms(o, subln) * (1.0 - lam_init)


def diff_attn(q, k, v, qs, ks, vs, cache_k, cache_v, page_table, lam_p, subln, lam_init):
    b, l, width = q.shape
    bs, n_pages = page_table.shape
    dv = 2 * HEAD_DIM
    n_heads = width // dv
    t = _tile(l, ATTN_BLOCK)
    n = l // t
    n_steps = b * n_heads * n
    per_step = bs // n_steps
    assert per_step * n_steps == bs, "sample rows must divide evenly over the prompt attention grid"
    rows = cache_k.shape[1]
    step = lambda bi, h, i: (bi * n_heads + h) * n + i
    per_map = lambda x: jnp.repeat(x.reshape(bs, n_heads, dv), 2, axis=1)
    qspec = pl.BlockSpec((1, t, dv), lambda bi, h, i, pt: (bi, i, h))
    kspec = pl.BlockSpec((1, l, dv), lambda bi, h, i, pt: (bi, 0, h))
    vec = pl.BlockSpec((per_step, 2 * n_heads, dv), lambda bi, h, i, pt: (step(bi, h, i), 0, 0))
    cache = [pl.BlockSpec((1, rows, dv), functools.partial(
        lambda bi, h, i, pt, r, slot: (pt[(step(bi, h, i) * per_step + r) * n_pages + slot], 0, 0), r=r, slot=slot))
        for r in range(per_step) for slot in range(n_pages)]
    const = lambda x: pl.BlockSpec(x.shape, lambda bi, h, i, pt: (0, 0))
    o, o_s = pl.pallas_call(
        functools.partial(_diff_attn_kernel, lam_init=lam_init, n_heads=n_heads, n_pages=n_pages),
        grid_spec=pltpu.PrefetchScalarGridSpec(
            num_scalar_prefetch=1,
            grid=(b, n_heads, n),
            in_specs=[qspec, kspec, kspec, vec, vec, vec] + cache + cache + [const(lam_p), const(subln)],
            out_specs=[qspec, pl.BlockSpec((per_step, n_heads, dv), lambda bi, h, i, pt: (step(bi, h, i), 0, 0))],
            scratch_shapes=[pltpu.VMEM((2, t, LANES), F32), pltpu.VMEM((2, t, LANES), F32),
                            pltpu.VMEM((2, t, dv), F32)],
        ),
        out_shape=[jax.ShapeDtypeStruct((b, l, width), F32), jax.ShapeDtypeStruct((bs, n_heads, dv), F32)],
        compiler_params=_cparams(("parallel", "parallel", "arbitrary")),
        name="diff_attn",
    )(page_table.reshape(-1), q, k, v, per_map(qs), per_map(ks), per_map(vs),
      *([cache_k] * (per_step * n_pages)), *([cache_v] * (per_step * n_pages)), lam_p, subln)
    return o, o_s.reshape(bs, width)


def _lru_coeffs(x, x1, x2, x3, cw, cb, wa, ba, wi, bi, lam):
    xc = x3 * cw[0:1] + x2 * cw[1:2] + x1 * cw[2:3] + x * cw[3:4]
    xc = xc + cb
    r = jax.nn.sigmoid(_dot(xc, wa) + ba)
    ig = jax.nn.sigmoid(_dot(xc, wi) + bi)
    log_a = -LRU_C * r * _softplus(-lam)
    a = jnp.exp(log_a)
    return a, jnp.sqrt(-jnp.tanh(log_a) * (a * a + 1.0)) * (ig * xc)


def _lru_seq_kernel(x_ref, gr_ref, buf_ref, h0_ref, cw_ref, cb_ref, wa_ref, ba_ref, wi_ref, bi_ref, lam_ref,
                    o_ref, hl_ref, carry, h_scr, a_scr, b_scr):
    n_grp, tl = x_ref.shape[0], x_ref.shape[1]
    n_carry = carry.shape[1]

    @pl.when(pl.program_id(1) == 0)
    def _():
        for g in range(n_grp):
            for d in range(n_carry):
                carry[g, d] = buf_ref[g, d:d + 1, :]
        h_scr[...] = h0_ref[...]

    for g in range(n_grp):
        x = x_ref[g]
        x1 = _shifted(x, carry[g, n_carry - 1])
        x2 = _shifted(x1, carry[g, n_carry - 2])
        x3 = _shifted(x2, carry[g, n_carry - 3])
        for d in range(n_carry):
            carry[g, d] = x[tl - n_carry + d:tl - n_carry + d + 1, :]
        a_scr[g], b_scr[g] = _lru_coeffs(x, x1, x2, x3, cw_ref[...], cb_ref[...], wa_ref[...], ba_ref[...],
                                         wi_ref[...], bi_ref[...], lam_ref[...])
    row_id = lax.broadcasted_iota(jnp.int32, (SUBLANES, x_ref.shape[2]), 0)

    def block(tb, hs):
        base = pl.multiple_of(tb * SUBLANES, SUBLANES)
        rows = pl.ds(base, SUBLANES)
        a8 = [a_scr[g, rows, :] for g in range(n_grp)]
        b8 = [b_scr[g, rows, :] for g in range(n_grp)]
        hs = list(hs)
        out = [jnp.zeros_like(a8[0]) for _ in range(n_grp)]
        for i in range(SUBLANES):
            for g in range(n_grp):
                hs[g] = a8[g][i:i + 1, :] * hs[g] + b8[g][i:i + 1, :]
                out[g] = jnp.where(row_id == i, hs[g], out[g])
        for g in range(n_grp):
            o_ref[g, rows, :] = out[g] * jax.nn.gelu(gr_ref[g, rows, :])
        return tuple(hs)

    hs = lax.fori_loop(0, tl // SUBLANES, block, tuple(h_scr[g] for g in range(n_grp)))
    for g in range(n_grp):
        h_scr[g] = hs[g]
        hl_ref[g] = hs[g]


def lru_prompt(x, gr, buf, h0, prm):
    bsz, l, w = x.shape
    tl = _tile(l, 512)
    grp = _tile(bsz, LRU_GROUP)
    seq = pl.BlockSpec((grp, tl, w), lambda i, j: (i, j, 0))
    vec = pl.BlockSpec((grp, 1, w), lambda i, j: (i, 0, 0))
    full = lambda a: pl.BlockSpec(a.shape, lambda i, j: (0,) * a.ndim)
    params = [prm[n] for n in ("conv_w", "conv_b", "wa", "ba", "wi", "bi", "lam")]
    nb = buf.shape[1]
    o, hl = pl.pallas_call(
        _lru_seq_kernel,
        grid=(bsz // grp, l // tl),
        in_specs=[seq, seq, pl.BlockSpec((grp, nb, w), lambda i, j: (i, 0, 0)), vec] + [full(a) for a in params],
        out_specs=[seq, vec],
        out_shape=[jax.ShapeDtypeStruct((bsz, l, w), F32), jax.ShapeDtypeStruct((bsz, 1, w), F32)],
        scratch_shapes=[pltpu.VMEM((grp, nb, 1, w), F32), pltpu.VMEM((grp, 1, w), F32),
                        pltpu.VMEM((grp, tl, w), F32), pltpu.VMEM((grp, tl, w), F32)],
        compiler_params=_cparams(("parallel", "arbitrary")),
        name="lru_prompt",
    )(x, gr, buf, h0.reshape(bsz, 1, w), *params)
    return o, hl.reshape(bsz, w)


def _lru_step_kernel(x_ref, x1_ref, x2_ref, x3_ref, gr_ref, h0_ref, cw_ref, cb_ref, wa_ref, ba_ref, wi_ref,
                     bi_ref, lam_ref, o_ref, h_ref):
    a, b = _lru_coeffs(x_ref[...], x1_ref[...], x2_ref[...], x3_ref[...], cw_ref[...], cb_ref[...], wa_ref[...],
                       ba_ref[...], wi_ref[...], bi_ref[...], lam_ref[...])
    h = a * h0_ref[...] + b
    h_ref[...] = h
    o_ref[...] = h * jax.nn.gelu(gr_ref[...])


def lru_step(x, buf, gr, h0, prm):
    m, w = x.shape
    tm = _tile(m, 512)
    rows = pl.BlockSpec((tm, w), lambda i: (i, 0))
    full = lambda a: pl.BlockSpec(a.shape, lambda i: (0,) * a.ndim)
    params = [prm[n] for n in ("conv_w", "conv_b", "wa", "ba", "wi", "bi", "lam")]
    nb = buf.shape[1]
    return pl.pallas_call(
        _lru_step_kernel,
        grid=(m // tm,),
        in_specs=[rows] * 6 + [full(a) for a in params],
        out_specs=[rows] * 2,
        out_shape=[jax.ShapeDtypeStruct((m, w), F32)] * 2,
        compiler_params=_cparams(("parallel",)),
        name="lru_step",
    )(x, buf[:, nb - 1], buf[:, nb - 2], buf[:, nb - 3], gr, h0, *params)


def _block_diag(w):
    n, d, e = w.shape
    eye = jnp.eye(n, dtype=w.dtype)
    return (eye[:, None, :, None] * w[:, :, None, :]).reshape(n * d, n * e)


def _mix_even(h, g_norm, pos0, s_ret, s_rwkv, buf, wts, is_prompt):
    b, l, d = h.shape
    m = b * l
    gw = d // 2
    u = norm_matmul(h.reshape(m, d), g_norm, wts["ab_w_in"], gw)
    seq = lambda t: t.reshape(b, l, gw)
    flat = lambda t: t.reshape(m, gw)
    qa, ka, va, ga = u[:4]
    to_kg, to_nat = _to_key_group, _from_key_group
    cur = [seq(t) for t in u[4:8]]
    last = [t[:, l - 1:] for t in cur]
    buf_new = jnp.concatenate([to_nat(last[0]), to_nat(last[1]), last[2], last[3]], axis=-1)
    buf_kg = [to_kg(buf[..., :gw]), to_kg(buf[..., gw:2 * gw]), buf[..., 2 * gw:3 * gw], buf[..., 3 * gw:]]
    if is_prompt:
        r, w, k, v, kk, kka, g, bonus = rwkv_prep(cur, None, jnp.concatenate(buf_kg, axis=-1), wts["rwkv"])
        pos = pos0 + jnp.arange(l, dtype=F32)
        o_a, s_ret_new = retention_prompt(seq(qa), seq(ka), seq(va), seq(ga), pos, wts["ones_bd"])
        o_a = flat(o_a)
        o_b, s_rwkv_new = rwkv_scan_prompt(r, w, k, kk, kka, v)
    else:
        rows = lambda t: t.reshape(1, m, gw)
        r, w, k, v, kk, kka, g, bonus = rwkv_prep([rows(t) for t in cur], [rows(t) for t in buf_kg], None,
                                                        wts["rwkv"])
        o_a, s_ret_new = retention_step(qa, ka, va, ga, s_ret, jnp.float32(pos0))
        o_b, s_rwkv_new = rwkv_step(*(flat(to_nat(t)) for t in (r, w, k, kk, kka)), flat(v), s_rwkv)
    post = (flat(bonus), flat(g), wts["rwkv_ln"], wts["ones_bd"])
    return o_a, flat(o_b), post, s_ret_new, s_rwkv_new, buf_new


def _mix_odd(h, g_norm, lru_h, lru_buf, wts, is_prompt):
    b, l, d = h.shape
    m = b * l
    gw = d // 2
    u = norm_matmul(h.reshape(m, d), g_norm, wts["cd_w_in"], gw, head_major=(1, 2))
    seq = lambda t: t.reshape(b, l, gw)
    xr = seq(u[3])
    if is_prompt:
        o_c = yield seq(u[0]), seq(u[1]), seq(u[2])
        o_c = o_c.reshape(m, gw)
        o_d, h_last = lru_prompt(xr, seq(u[4]), lru_buf, lru_h, wts["lru"])
    else:
        o_c = yield u[0], u[1], u[2]
        o_d, h_last = lru_step(u[3], lru_buf, u[4], lru_h, wts["lru"])
    buf_new = jnp.concatenate([lru_buf, xr], axis=1)[:, l:]
    n_heads = gw // (2 * HEAD_DIM)
    k_new = u[5].reshape(b, l, n_heads, 2 * HEAD_DIM)
    v_new = u[6].reshape(b, l, n_heads, 2 * HEAD_DIM)
    return o_c, o_d.reshape(m, gw), k_new, v_new, h_last, buf_new


def _advance(gen, value):
    try:
        return gen.send(value), None
    except StopIteration as done:
        return None, done.value


def _trunk(x, p, pos0, s_ret, s_rwkv, s_shift, s_lru_h, s_lru_conv, wts, is_prompt):
    b, l, d = x.shape
    m = b * l
    depth = wts["norm_g"].shape[0]
    h = x.reshape(m, d)
    ret_l, rwkv_l, shift_l, k_l, v_l, lh_l, lc_l = [], [], [], [], [], [], []
    for i in range(depth):
        j = i // 2
        g = wts["norm_g"][i]
        gn = lambda n: g[n:n + 1]
        h = ffn_block(h, gn(0), wts["ffn_in"], wts["ffn_out"], gn(1), i, 0)
        post = None
        if i % 2 == 0:
            o1, o2, post, sr, sw, sb = _mix_even(h.reshape(b, l, d), gn(2), pos0, s_ret[j], s_rwkv[j], s_shift[j],
                                                 wts["even"][j], is_prompt)
            ret_l.append(sr)
            rwkv_l.append(sw)
            shift_l.append(sb)
            w_out = wts["even"][j]["w_out"]
        else:
            o1, o2, kn, vn, lh, lc = yield from _mix_odd(h.reshape(b, l, d), gn(2), s_lru_h[j], s_lru_conv[j],
                                                         wts["odd"][j], is_prompt)
            k_l.append(kn)
            v_l.append(vn)
            lh_l.append(lh)
            lc_l.append(lc)
            w_out = wts["odd"][j]["w_out"]
        h = out_proj(o1, o2, w_out, h, gn(3), post)
        h = ffn_block(h, gn(4), wts["ffn_in"], wts["ffn_out"], gn(5), i, 1,
                      ple=(gn(6), wts["ple_gate"], p.reshape(depth, m, -1), wts["ple"], gn(7)))
    st = lambda lst: jnp.stack(lst, axis=0)
    return (h.reshape(b, l, d), st(k_l), st(v_l), st(ret_l), st(rwkv_l), st(shift_l), st(lh_l), st(lc_l))


def kernel(x_prompt, x_sample, cache_k, cache_v, state_ret, state_rwkv, state_rwkv_shift, state_lru_h, state_lru_conv, page_table, p_prompt, p_sample, norm_g, ffn_w_in, ffn_w_out, ple_w, ple_gate_w, ab_w_in, ab_w_out, rwkv_mu, rwkv_w0, rwkv_w1, rwkv_w2, rwkv_a0, rwkv_a1, rwkv_a2, rwkv_g1, rwkv_g2, rwkv_kk, rwkv_ka, rwkv_rk, rwkv_ln, cd_w_in, cd_w_out, diff_lam, diff_subln, lru_conv_w, lru_conv_b, lru_wa, lru_ba, lru_wi, lru_bi, lru_lambda):
    depth = norm_g.shape[0]
    n_a, n_c = state_ret.shape[0], state_lru_h.shape[0]
    bp = x_prompt.shape[0]
    gw = ab_w_out.shape[1] // 2
    bf = lambda t: t.astype(BF16)
    row = lambda t: t.reshape(1, -1)
    ones_bd = _block_diag(jnp.ones((gw // HEAD_DIM, HEAD_DIM, HEAD_DIM), BF16))
    perm = _key_group_perm(gw)
    kg = _to_key_group
    head_kg = perm // HEAD_DIM
    ones_kk = jnp.asarray(head_kg[:, None] == head_kg[None, :], BF16)
    ones_rk = jnp.asarray(head_kg[:, None] == (np.arange(gw) // HEAD_DIM)[None, :], BF16)

    def ab_in_kg(w):
        cols = [w[:, g * gw:(g + 1) * gw] for g in range(w.shape[1] // gw)]
        cols[4], cols[5] = kg(cols[4]), kg(cols[5])
        return bf(jnp.concatenate(cols, axis=1))

    wts = {
        "norm_g": norm_g,
        "ffn_in": bf(ffn_w_in), "ffn_out": bf(ffn_w_out), "ple": bf(ple_w), "ple_gate": bf(ple_gate_w),
        "even": [{
            "ab_w_in": ab_in_kg(ab_w_in[j]), "w_out": bf(ab_w_out[j]), "rwkv_ln": rwkv_ln[j],
            "ones_bd": ones_bd,
            "rwkv": {"mu": jnp.concatenate([kg(rwkv_mu[j][:2]), rwkv_mu[j][2:]], axis=0),
                     "w0": kg(row(rwkv_w0[j])), "w1": bf(rwkv_w1[j]), "w2": bf(kg(rwkv_w2[j])),
                     "a0": kg(row(rwkv_a0[j])), "a1": bf(rwkv_a1[j]), "a2": bf(kg(rwkv_a2[j])),
                     "g1": bf(rwkv_g1[j]), "g2": bf(rwkv_g2[j]), "kk": kg(row(rwkv_kk[j])),
                     "ka": kg(row(rwkv_ka[j])), "rk": kg(row(rwkv_rk[j])),
                     "ones_kk": ones_kk, "ones_rk": ones_rk},
        } for j in range(n_a)],
        "odd": [{
            "cd_w_in": bf(cd_w_in[j]), "w_out": bf(cd_w_out[j]), "diff_lam": diff_lam[j],
            "diff_subln": row(diff_subln[j]),
            "lru": {"conv_w": lru_conv_w[j], "conv_b": row(lru_conv_b[j]), "wa": bf(_block_diag(lru_wa[j])),
                    "ba": row(lru_ba[j]), "wi": bf(_block_diag(lru_wi[j])), "bi": row(lru_bi[j]),
                    "lam": row(lru_lambda[j])},
        } for j in range(n_c)],
    }
    zeros = lambda *shape: jnp.zeros(shape, F32)
    past_len = page_table.shape[1] * cache_k.shape[2]
    n_pool, page = cache_k.shape[1], cache_k.shape[2]
    as_rows = lambda c: c.reshape(n_c, n_pool, page * c.shape[3], c.shape[4])
    pages_k, pages_v = as_rows(cache_k), as_rows(cache_v)
    prompt = _trunk(x_prompt, p_prompt, 0.0, [None] * n_a, [None] * n_a,
                    zeros(n_a, bp, 1, 4 * gw), zeros(n_c, bp, gw), zeros(n_c, bp, CONV_W - 1, gw), wts, True)
    sample = _trunk(x_sample, p_sample, float(past_len), state_ret, state_rwkv, state_rwkv_shift,
                    state_lru_h, state_lru_conv, wts, False)
    (qkv_p, out_p), (qkv_s, out_s) = _advance(prompt, None), _advance(sample, None)
    j = 0
    while out_p is None:
        odd = wts["odd"][j]
        lam_init = 0.8 - 0.6 * math.exp(-0.3 * (2 * j + 1))
        o_p, o_s = diff_attn(*qkv_p, *qkv_s, pages_k[j], pages_v[j], page_table, odd["diff_lam"],
                             odd["diff_subln"], lam_init)
        (qkv_p, out_p), (qkv_s, out_s) = _advance(prompt, o_p), _advance(sample, o_s)
        j += 1
    yp, kp, vp, rp, wp, sp, hp, cp = out_p
    ys, ks_, vs, rs, ws, ss, hs, cs = out_s
    return (yp, ys, kp, vp, rp, wp, sp, hp, cp, ks_, vs, rs, ws, ss, hs, cs)
```

```python
import functools
import math

import jax
import jax.numpy as jnp
import numpy as np
from jax import lax
from jax.experimental import pallas as pl
from jax.experimental.pallas import tpu as pltpu

F32 = jnp.float32
BF16 = jnp.bfloat16

HEAD_DIM = 64
CONV_W = 4
LRU_C = 8.0
ROPE_BASE = 10000.0
EPS = 1e-6
RWKV_GN_EPS = 64e-5
RET_CHUNK = 256
ATTN_BLOCK = 512
SCAN_CHUNK = 128
LRU_GROUP = 2
SCAN_GROUP = 8
LANES = 128
SUBLANES = 8
VMEM_LIMIT = 48 * 1024 * 1024


def _cparams(sem):
    return pltpu.CompilerParams(dimension_semantics=sem, vmem_limit_bytes=VMEM_LIMIT)


def _tile(n, pref):
    t = min(n, pref)
    while n % t:
        t //= 2
    return t


def _rms(x, g):
    return x * lax.rsqrt(jnp.mean(x * x, axis=-1, keepdims=True) + EPS) * g


def _dot(a, b):
    return jnp.dot(a.astype(BF16), b.astype(BF16), preferred_element_type=F32)


def _seg_sum(x, ones_bd):
    hi = x.astype(BF16)
    lo = (x - hi.astype(F32)).astype(BF16)
    return (jnp.dot(hi, ones_bd, preferred_element_type=F32)
            + jnp.dot(lo, ones_bd, preferred_element_type=F32))


def _ffn_kernel(h_ref, gpre_ref, wg_ref, wu_ref, wo_ref, gpost_ref, *rest):
    o_ref, xn_ref, acc_ref = rest[-3:]
    ple = rest[:-3]
    j = pl.program_id(1)

    @pl.when(j == 0)
    def _():
        xn_ref[...] = _rms(h_ref[...], gpre_ref[...]).astype(BF16)
        acc_ref[...] = jnp.zeros_like(acc_ref)

    xn = xn_ref[...]
    gate = jnp.dot(xn, wg_ref[...], preferred_element_type=F32)
    up = jnp.dot(xn, wu_ref[...], preferred_element_type=F32)
    act = (gate * jax.nn.sigmoid(gate) * up).astype(BF16)
    acc_ref[...] += jnp.dot(act, wo_ref[...], preferred_element_type=F32)

    @pl.when(j == pl.num_programs(1) - 1)
    def _():
        h = h_ref[...] + 0.5 * _rms(acc_ref[...], gpost_ref[...])
        if ple:
            g6_ref, wgate_ref, p_ref, wp_ref, g7_ref = ple
            gate_p = jax.nn.sigmoid(_dot(_rms(h, g6_ref[...]), wgate_ref[...]))
            h = h + _rms(gate_p * _dot(p_ref[...], wp_ref[...]), g7_ref[...])
        o_ref[...] = h


def ffn_block(h, g_pre, w_in, w_out, g_post, layer, half, ple=None):
    m, d = h.shape
    f = w_out.shape[2]
    tm, tf = _tile(m, 1024), _tile(f, 512)
    nf = f // tf
    vec = pl.BlockSpec((1, d), lambda i, j: (0, 0))
    extra, extra_specs = [], []
    if ple is not None:
        g6, gate_w, p, emb_w, g7 = ple
        pd = p.shape[2]
        extra = [g6, gate_w, p, emb_w, g7]
        extra_specs = [vec, pl.BlockSpec((None, d, d), lambda i, j: (layer, 0, 0)),
                       pl.BlockSpec((None, tm, pd), lambda i, j: (layer, i, 0)),
                       pl.BlockSpec((None, pd, d), lambda i, j: (layer, 0, 0)), vec]
    return pl.pallas_call(
        _ffn_kernel,
        grid=(m // tm, nf),
        in_specs=[
            pl.BlockSpec((tm, d), lambda i, j: (i, 0)),
            vec,
            pl.BlockSpec((None, None, d, tf), lambda i, j: (layer, half, 0, j)),
            pl.BlockSpec((None, None, d, tf), lambda i, j: (layer, half, 0, j + nf)),
            pl.BlockSpec((None, None, tf, d), lambda i, j: (layer, half, j, 0)),
            vec,
        ] + extra_specs,
        out_specs=pl.BlockSpec((tm, d), lambda i, j: (i, 0)),
        out_shape=jax.ShapeDtypeStruct((m, d), F32),
        scratch_shapes=[pltpu.VMEM((tm, d), BF16), pltpu.VMEM((tm, d), F32)],
        compiler_params=_cparams(("parallel", "arbitrary")),
        name="ffn_block",
    )(h, g_pre, w_in, w_in, w_out, g_post, *extra)


def _norm_matmul_kernel(h_ref, g_ref, w_ref, *o_refs, n_groups, head_major):
    xn = _rms(h_ref[...], g_ref[...]).astype(BF16)
    tm, tn = o_refs[0].shape
    heads = tn // LANES
    for gi in range(n_groups):
        res = jnp.dot(xn, w_ref[:, gi * tn:(gi + 1) * tn], preferred_element_type=F32)
        o_refs[gi][...] = res
        if gi in head_major:
            hm_ref = o_refs[n_groups + head_major.index(gi)]
            for hh in range(heads):
                hm_ref[pl.ds(hh, tm, stride=heads), :] = res[:, hh * LANES:(hh + 1) * LANES]


def norm_matmul(h, g, w, tn, head_major=()):
    m, d = h.shape
    n = w.shape[1]
    tm = _tile(m, 512)
    heads = tn // LANES
    rows = pl.BlockSpec((tm, tn), lambda i: (i, 0))
    return pl.pallas_call(
        functools.partial(_norm_matmul_kernel, n_groups=n // tn, head_major=tuple(head_major)),
        grid=(m // tm,),
        in_specs=[
            pl.BlockSpec((tm, d), lambda i: (i, 0)),
            pl.BlockSpec((1, d), lambda i: (0, 0)),
            pl.BlockSpec((d, n), lambda i: (0, 0)),
        ],
        out_specs=[rows] * (n // tn) + [pl.BlockSpec((tm * heads, LANES), lambda i: (i, 0))] * len(head_major),
        out_shape=([jax.ShapeDtypeStruct((m, tn), F32)] * (n // tn)
                   + [jax.ShapeDtypeStruct((m * heads, LANES), F32)] * len(head_major)),
        compiler_params=_cparams(("parallel",)),
        name="norm_matmul",
    )(h, g, w)


def _out_proj_kernel(oa_ref, ob_ref, wa_ref, wb_ref, h_ref, g_ref, *rest):
    ob = ob_ref[...]
    if len(rest) > 1:
        bonus_ref, gate_ref, ln_ref, ones_ref = rest[:4]
        ones_bd = ones_ref[...]
        inv = 1.0 / HEAD_DIM
        oc = ob - _seg_sum(ob, ones_bd) * inv
        on = oc * lax.rsqrt(_seg_sum(oc * oc, ones_bd) * inv + RWKV_GN_EPS)
        ln = ln_ref[...]
        ob = (on * ln[0:1] + ln[1:2] + bonus_ref[...]) * gate_ref[...]
    o_ref = rest[-1]
    y = _dot(oa_ref[...], wa_ref[...]) + _dot(ob, wb_ref[...])
    o_ref[...] = h_ref[...] + _rms(y, g_ref[...])


def out_proj(oa, ob, w, h, g, rwkv_post=None):
    m, d = h.shape
    gw = oa.shape[1]
    tm = _tile(m, 512)
    rows = pl.BlockSpec((tm, gw), lambda i: (i, 0))
    extra, extra_specs = [], []
    if rwkv_post is not None:
        bonus, gate, ln, ones_bd = rwkv_post
        extra = [bonus, gate, ln, ones_bd]
        extra_specs = [rows, rows, pl.BlockSpec(ln.shape, lambda i: (0, 0)),
                       pl.BlockSpec(ones_bd.shape, lambda i: (0, 0))]
    return pl.pallas_call(
        _out_proj_kernel,
        grid=(m // tm,),
        in_specs=[
            rows, rows,
            pl.BlockSpec((gw, d), lambda i: (0, 0)),
            pl.BlockSpec((gw, d), lambda i: (1, 0)),
            pl.BlockSpec((tm, d), lambda i: (i, 0)),
            pl.BlockSpec((1, d), lambda i: (0, 0)),
        ] + extra_specs,
        out_specs=pl.BlockSpec((tm, d), lambda i: (i, 0)),
        out_shape=jax.ShapeDtypeStruct((m, d), F32),
        compiler_params=_cparams(("parallel",)),
        name="out_proj",
    )(oa, ob, w, w, h, g, *extra)


def _retention_tables(n_heads, c, pos):
    lg = jnp.log1p(-jnp.exp2(-5.0 - jnp.arange(n_heads, dtype=F32)))
    idx = jnp.arange(c, dtype=F32)
    rel = idx[:, None] - idx[None, :]
    dmask = jnp.where(rel[None] >= 0, jnp.exp(jnp.maximum(rel, 0.0)[None] * lg[:, None, None]), 0.0)
    rep = lambda t: jnp.repeat(t, HEAD_DIM, axis=-1)
    q_dec = rep(jnp.exp((idx[:, None] + 1.0) * lg[None, :]))
    k_dec = rep(jnp.exp((c - 1.0 - idx[:, None]) * lg[None, :]))
    c_dec = rep(jnp.exp(c * lg)[None, :])
    half = HEAD_DIM // 2
    freq = 1.0 / (ROPE_BASE ** jnp.linspace(0.0, 1.0, half, dtype=F32))
    ang = pos[:, None] * freq[None, :]
    cos, sin = jnp.cos(ang), jnp.sin(ang)
    cos_t = jnp.tile(jnp.concatenate([cos, cos], axis=-1), (1, n_heads))
    sin_t = jnp.tile(jnp.concatenate([-sin, sin], axis=-1), (1, n_heads))
    return dmask, q_dec, k_dec, c_dec, cos_t, sin_t


def _retention_kernel(q_ref, k_ref, v_ref, g_ref, cos_ref, sin_ref, dmask_ref, qdec_ref, kdec_ref,
                      cdec_ref, ones_ref, o_ref, s_ref, s_scr, *, n_heads):
    c = pl.program_id(1)

    @pl.when(c == 0)
    def _():
        s_scr[...] = jnp.zeros_like(s_scr)

    q, k, v, g = q_ref[0], k_ref[0], v_ref[0], g_ref[0]
    cos, sin = cos_ref[...], sin_ref[...]
    rows, width = q.shape
    lane = lax.broadcasted_iota(jnp.int32, q.shape, 1)
    first_half = (lane % HEAD_DIM) < (HEAD_DIM // 2)

    def rot(x):
        swapped = jnp.where(first_half, pltpu.roll(x, width - HEAD_DIM // 2, 1),
                            pltpu.roll(x, HEAD_DIM // 2, 1))
        return x * cos + swapped * sin

    qr = rot(q)
    kr = rot(k) * (HEAD_DIM ** -0.5)
    kd = kr * kdec_ref[...]
    qdec = qdec_ref[...]
    cdec = cdec_ref[...]
    head_a = lax.broadcasted_iota(jnp.int32, (rows, LANES), 1) < HEAD_DIM
    same_head = (lax.broadcasted_iota(jnp.int32, (LANES, LANES), 0) // HEAD_DIM
                 == lax.broadcasted_iota(jnp.int32, (LANES, LANES), 1) // HEAD_DIM)
    nt = (((1,), (1,)), ((), ()))
    outs = []
    for p in range(n_heads // 2):
        sl = slice(p * LANES, (p + 1) * LANES)
        qp = qr[:, sl].astype(BF16)
        kp = kr[:, sl].astype(BF16)
        vp = v[:, sl].astype(BF16)
        zero = jnp.zeros_like(qp)
        att_a = lax.dot_general(jnp.where(head_a, qp, zero), kp, nt, preferred_element_type=F32) * dmask_ref[2 * p]
        att_b = lax.dot_general(jnp.where(head_a, zero, qp), kp, nt,
                                preferred_element_type=F32) * dmask_ref[2 * p + 1]
        s_old = s_scr[p]
        inner = jnp.where(head_a, jnp.dot(att_a.astype(BF16), vp, preferred_element_type=F32),
                          jnp.dot(att_b.astype(BF16), vp, preferred_element_type=F32))
        outs.append(inner + jnp.dot(qp, s_old.astype(BF16), preferred_element_type=F32) * qdec[:, sl])
        update = lax.dot_general(kd[:, sl].astype(BF16), vp, (((0,), (0,)), ((), ())),
                                 preferred_element_type=F32)
        s_scr[p] = jnp.where(same_head, s_old * cdec[:, sl] + update, 0.0)
    o = jnp.concatenate(outs, axis=1)
    ones_bd = ones_ref[...]
    inv = 1.0 / HEAD_DIM
    oc = o - _seg_sum(o, ones_bd) * inv
    on = oc * lax.rsqrt(_seg_sum(oc * oc, ones_bd) * inv + EPS)
    o_ref[0] = on * (g * jax.nn.sigmoid(g))

    @pl.when(c == pl.num_programs(1) - 1)
    def _():
        for p in range(n_heads // 2):
            s_pair = s_scr[p]
            s_ref[0, 2 * p] = s_pair[:HEAD_DIM, :HEAD_DIM]
            s_ref[0, 2 * p + 1] = s_pair[HEAD_DIM:, HEAD_DIM:]


def retention_prompt(q, k, v, g, pos, ones_bd):
    b, l, width = q.shape
    n_heads = width // HEAD_DIM
    c = _tile(l, RET_CHUNK)
    dmask, q_dec, k_dec, c_dec, cos_t, sin_t = _retention_tables(n_heads, c, pos)
    seq = pl.BlockSpec((1, c, width), lambda i, j: (i, j, 0))
    tab = pl.BlockSpec((c, width), lambda i, j: (j, 0))
    fixed = lambda shape: pl.BlockSpec(shape, lambda i, j: (0,) * len(shape))
    return pl.pallas_call(
        functools.partial(_retention_kernel, n_heads=n_heads),
        grid=(b, l // c),
        in_specs=[seq, seq, seq, seq, tab, tab, fixed((n_heads, c, c)), fixed((c, width)),
                  fixed((c, width)), fixed((1, width)), fixed(ones_bd.shape)],
        out_specs=[seq, pl.BlockSpec((1, n_heads, HEAD_DIM, HEAD_DIM), lambda i, j: (i, 0, 0, 0))],
        out_shape=[jax.ShapeDtypeStruct((b, l, width), F32),
                   jax.ShapeDtypeStruct((b, n_heads, HEAD_DIM, HEAD_DIM), F32)],
        scratch_shapes=[pltpu.VMEM((n_heads // 2, LANES, LANES), F32)],
        compiler_params=_cparams(("parallel", "arbitrary")),
        name="retention_prompt",
    )(q, k, v, g, cos_t, sin_t, dmask, q_dec, k_dec, c_dec, ones_bd)


def _eye_mask():
    return (lax.broadcasted_iota(jnp.int32, (HEAD_DIM, HEAD_DIM), 0)
            == lax.broadcasted_iota(jnp.int32, (HEAD_DIM, HEAD_DIM), 1))


def _to_col(x_row):
    return jnp.sum(jnp.where(_eye_mask(), x_row, 0.0), axis=-1, keepdims=True)


def _to_row(x_col):
    return jnp.sum(jnp.where(_eye_mask(), x_col, 0.0), axis=-2, keepdims=True)


def _retention_step_kernel(q_ref, k_ref, v_ref, g_ref, cos_ref, sin_ref, gam_ref, s0_ref, o_ref, s_ref):
    cos, sin = cos_ref[...], sin_ref[...]

    def rot(x):
        half = HEAD_DIM // 2
        return x * cos + jnp.concatenate([x[..., half:], x[..., :half]], axis=-1) * sin

    q = _to_col(rot(q_ref[...]))
    k = _to_col(rot(k_ref[...]) * (HEAD_DIM ** -0.5))
    v, g = v_ref[...], g_ref[...]
    gam = gam_ref[...]
    s0 = s0_ref[...]
    att = jnp.sum(q * k, axis=2, keepdims=True)
    o = att * v + jnp.sum(q * s0, axis=2, keepdims=True) * gam
    s_ref[...] = s0 * gam + k * v
    oc = o - jnp.mean(o, axis=-1, keepdims=True)
    on = oc * lax.rsqrt(jnp.mean(oc * oc, axis=-1, keepdims=True) + EPS)
    o_ref[...] = on * (g * jax.nn.sigmoid(g))


def retention_step(q, k, v, g, s0, pos):
    b, width = q.shape
    n_heads = width // HEAD_DIM
    bb = _tile(b, 8)
    half = HEAD_DIM // 2
    freq = 1.0 / (ROPE_BASE ** jnp.linspace(0.0, 1.0, half, dtype=F32))
    ang = pos * freq
    cos_c = jnp.concatenate([jnp.cos(ang), jnp.cos(ang)])[None, :]
    sin_c = jnp.concatenate([-jnp.sin(ang), jnp.sin(ang)])[None, :]
    gam = jnp.exp(jnp.log1p(-jnp.exp2(-5.0 - jnp.arange(n_heads, dtype=F32)))).reshape(n_heads, 1, 1)
    row = lambda t: t.reshape(b, n_heads, 1, HEAD_DIM)
    rspec = pl.BlockSpec((bb, n_heads, 1, HEAD_DIM), lambda i: (i, 0, 0, 0))
    sspec = pl.BlockSpec((bb, n_heads, HEAD_DIM, HEAD_DIM), lambda i: (i, 0, 0, 0))
    o, s = pl.pallas_call(
        _retention_step_kernel,
        grid=(b // bb,),
        in_specs=[rspec, rspec, rspec, rspec,
                  pl.BlockSpec((1, HEAD_DIM), lambda i: (0, 0)), pl.BlockSpec((1, HEAD_DIM), lambda i: (0, 0)),
                  pl.BlockSpec((n_heads, 1, 1), lambda i: (0, 0, 0)), sspec],
        out_specs=[rspec, sspec],
        out_shape=[jax.ShapeDtypeStruct((b, n_heads, 1, HEAD_DIM), F32),
                   jax.ShapeDtypeStruct((b, n_heads, HEAD_DIM, HEAD_DIM), F32)],
        compiler_params=_cparams(("parallel",)),
        name="retention_step",
    )(row(q), row(k), row(v), row(g), cos_c, sin_c, gam, s0)
    return o.reshape(b, width), s


def _softplus(x):
    return jnp.maximum(x, 0.0) + jnp.log1p(jnp.exp(-jnp.abs(x)))


def _shifted(cur, carry_row):
    if cur.shape[0] == 1:
        return carry_row
    first = lax.broadcasted_iota(jnp.int32, cur.shape, 0) == 0
    return jnp.where(first, carry_row, pltpu.roll(cur, 1, 0))


PREP_PARAMS = ("mu", "w0", "w1", "w2", "a0", "a1", "a2", "g1", "g2", "kk", "ka", "rk", "ones_kk", "ones_rk")
N_PREP_PARAMS = len(PREP_PARAMS)
N_PREP_OUT = 8


def _rwkv_prep_kernel(*refs, project):
    n_in = 4 if project else 8
    (mu_ref, w0_ref, w1_ref, w2_ref, a0_ref, a1_ref, a2_ref, g1_ref, g2_ref, kkp_ref, kap_ref, rk_ref,
     ones_kk_ref, ones_rk_ref) = refs[n_in:n_in + N_PREP_PARAMS]
    n_out = n_in + N_PREP_PARAMS
    r_out, w_out, k_out, v_out, kk_out, kka_out, g_out, bonus_out = refs[n_out:n_out + N_PREP_OUT]
    if project:
        h_ref, gnorm_ref, wproj_ref, buf_ref = refs[:4]
        last_ref, carry = refs[n_out + N_PREP_OUT:]
        width = wproj_ref.shape[1] // 4
        xn = _rms(h_ref[0], gnorm_ref[...]).astype(BF16)
        cur = [jnp.dot(xn, wproj_ref[:, gi * width:(gi + 1) * width], preferred_element_type=F32)
               for gi in range(4)]

        @pl.when(pl.program_id(1) == 0)
        def _():
            for gi in range(4):
                carry[gi] = buf_ref[0, :, gi * width:(gi + 1) * width]

        prev = [_shifted(x, carry[gi]) for gi, x in enumerate(cur)]
        for gi, x in enumerate(cur):
            carry[gi] = x[x.shape[0] - 1:, :]
            last_ref[0, :, gi * width:(gi + 1) * width] = x[x.shape[0] - 1:, :]
    else:
        cur = [ref[0] for ref in refs[:4]]
        prev = [ref[0] for ref in refs[4:8]]
    mu = mu_ref[...]
    lerp = lambda x, xp, i: x + (xp - x) * mu[i:i + 1]
    zr, pz = cur[3], prev[3]
    r = lerp(cur[0], prev[0], 0)
    kx = lerp(cur[1], prev[1], 1)
    vx = lerp(cur[2], prev[2], 2)
    zw, za, zg = lerp(zr, pz, 3), lerp(zr, pz, 4), lerp(zr, pz, 5)
    wpre = w0_ref[...] + _dot(jnp.tanh(_dot(zw, w1_ref[...])), w2_ref[...])
    decay = jnp.exp(-jnp.exp(-_softplus(-wpre) - 0.5))
    a = jax.nn.sigmoid(a0_ref[...] + _dot(_dot(za, a1_ref[...]), a2_ref[...]))
    g = _dot(jax.nn.sigmoid(_dot(zg, g1_ref[...])), g2_ref[...])
    kk = kx * kkp_ref[...]
    n_tiles = kk.shape[1] // LANES

    def head_sum(x, ones_ref):
        tile_sum = sum(x[:, t * LANES:(t + 1) * LANES] for t in range(n_tiles))
        return jnp.dot(_hi_lo(tile_sum), ones_ref[...], preferred_element_type=F32)

    kk_norm = head_sum(kk * kk, ones_kk_ref)
    kk = kk / jnp.maximum(jnp.sqrt(jnp.concatenate([kk_norm] * n_tiles, axis=1)), 1e-12)
    k32 = kx * (1.0 + (a - 1.0) * kap_ref[...])
    r_out[0] = r
    w_out[0] = decay
    k_out[0] = k32
    v_out[0] = vx
    kk_out[0] = kk
    kka_out[0] = kk * a
    g_out[0] = g
    bonus_out[0] = head_sum(r * k32 * rk_ref[...], ones_rk_ref) * vx


def rwkv_prep(cur, prev, prm):
    b, l, w = cur[0].shape
    tl = _tile(l, 256)
    seq = pl.BlockSpec((1, tl, w), lambda i, j: (i, j, 0))
    full = lambda a: pl.BlockSpec(a.shape, lambda i, j: (0,) * a.ndim)
    params = [prm[n] for n in PREP_PARAMS]
    return pl.pallas_call(
        functools.partial(_rwkv_prep_kernel, project=False),
        grid=(b, l // tl),
        in_specs=[seq] * 8 + [full(a) for a in params],
        out_specs=[seq] * N_PREP_OUT,
        out_shape=[jax.ShapeDtypeStruct((b, l, w), F32)] * N_PREP_OUT,
        compiler_params=_cparams(("parallel", "arbitrary")),
        name="rwkv_prep",
    )(*cur, *prev, *params)


def rwkv_project_prep(h, g_norm, w_proj, buf, prm):
    b, l, d = h.shape
    w = w_proj.shape[1] // 4
    tl = _tile(l, 512)
    seq = pl.BlockSpec((1, tl, w), lambda i, j: (i, j, 0))
    row = pl.BlockSpec((1, 1, 4 * w), lambda i, j: (i, 0, 0))
    full = lambda a: pl.BlockSpec(a.shape, lambda i, j: (0,) * a.ndim)
    params = [prm[n] for n in PREP_PARAMS]
    *outs, last = pl.pallas_call(
        functools.partial(_rwkv_prep_kernel, project=True),
        grid=(b, l // tl),
        in_specs=[pl.BlockSpec((1, tl, d), lambda i, j: (i, j, 0)), full(g_norm), full(w_proj), row]
        + [full(a) for a in params],
        out_specs=[seq] * N_PREP_OUT + [row],
        out_shape=[jax.ShapeDtypeStruct((b, l, w), F32)] * N_PREP_OUT + [jax.ShapeDtypeStruct((b, 1, 4 * w), F32)],
        scratch_shapes=[pltpu.VMEM((4, 1, w), F32)],
        compiler_params=_cparams(("parallel", "arbitrary")),
        name="rwkv_project_prep",
    )(h, g_norm, w_proj, buf, *params)
    return outs, last


def _hi_lo(x):
    hi = x.astype(BF16)
    lo = (x - hi.astype(F32)).astype(BF16)
    return jnp.concatenate([hi, lo], axis=1)


def _value_columns(v8):
    hi = v8.astype(BF16).astype(F32)
    lo = (v8 - hi).astype(BF16).astype(F32)
    stacked = jnp.concatenate([part[:, p * LANES:(p + 1) * LANES]
                               for p in range(v8.shape[1] // LANES) for part in (hi, lo)], axis=0)
    cols = stacked.T
    return jnp.concatenate([cols[:HEAD_DIM], cols[HEAD_DIM:]], axis=1).astype(BF16)


def _rwkv_scan_kernel(r_ref, w_ref, k_ref, kk_ref, kka_ref, v_ref, sel_ref, ones_ref, o_ref, s_ref, s_scr,
                      ot_scr, *, n_heads):
    c = pl.program_id(1)
    n_grp, tc = r_ref.shape[0], r_ref.shape[1]
    head_lanes = LANES // n_heads
    n_kg = HEAD_DIM // head_lanes

    @pl.when(c == 0)
    def _():
        s_scr[...] = jnp.zeros_like(s_scr)

    ot_scr[...] = jnp.zeros_like(ot_scr)
    ones1 = ones_ref[...]
    t_lane = lax.broadcasted_iota(jnp.int32, (n_grp * HEAD_DIM, LANES), 1) % head_lanes
    rows_of = lambda x, g: x[g * HEAD_DIM:(g + 1) * HEAD_DIM]

    def block(tb, carry):
        base = pl.multiple_of(tb * SUBLANES, SUBLANES)
        refs = {"kk": kk_ref, "w": w_ref, "kka": kka_ref, "k": k_ref, "r": r_ref}
        vp = jnp.concatenate([_value_columns(v_ref[g, pl.ds(base, SUBLANES), :]) for g in range(n_grp)], axis=0)
        tile = base // head_lanes
        head_sum = lambda x: jnp.dot(x.astype(BF16), ones1, preferred_element_type=F32)
        for i in range(SUBLANES):
            row = lambda name, g, kg: jnp.broadcast_to(
                refs[name][g, pl.ds(base, SUBLANES), kg * LANES:(kg + 1) * LANES][i:i + 1, :], (HEAD_DIM, LANES))
            sa = head_sum(jnp.concatenate(
                [sum(s_scr[g, kg] * row("kk", g, kg) for kg in range(n_kg)) for g in range(n_grp)], axis=0))
            vcol = jnp.dot(vp, sel_ref[i], preferred_element_type=F32)
            reads = []
            for g in range(n_grp):
                sa_g, vcol_g = rows_of(sa, g), rows_of(vcol, g)
                read = None
                for kg in range(n_kg):
                    s_new = (s_scr[g, kg] * row("w", g, kg) - sa_g * row("kka", g, kg)
                             + vcol_g * row("k", g, kg))
                    s_scr[g, kg] = s_new
                    term = s_new * row("r", g, kg)
                    read = term if read is None else read + term
                reads.append(read)
            o = head_sum(jnp.concatenate(reads, axis=0))
            ot_scr[tile] = jnp.where(t_lane == (base + i) % head_lanes, o, ot_scr[tile])
        return carry

    lax.fori_loop(0, tc // SUBLANES, block, 0)

    for tile in range(tc // head_lanes):
        for g in range(n_grp):
            o_t = rows_of(ot_scr[tile], g).T
            for h in range(n_heads):
                o_ref[g, tile * head_lanes:(tile + 1) * head_lanes, h * HEAD_DIM:(h + 1) * HEAD_DIM] = (
                    o_t[h * head_lanes:(h + 1) * head_lanes, :])

    @pl.when(c == pl.num_programs(1) - 1)
    def _():
        s_ref[...] = s_scr[...]


def _key_group_perm(width):
    n_heads = width // HEAD_DIM
    head_lanes = LANES // n_heads
    n = np.arange(width)
    return (n % LANES) // head_lanes * HEAD_DIM + n // LANES * head_lanes + n % head_lanes


def _to_key_group(t):
    width = t.shape[-1]
    n_heads = width // HEAD_DIM
    head_lanes = LANES // n_heads
    split = t.reshape(*t.shape[:-1], n_heads, HEAD_DIM // head_lanes, head_lanes)
    return jnp.swapaxes(split, -3, -2).reshape(t.shape)


def _from_key_group(t):
    width = t.shape[-1]
    n_heads = width // HEAD_DIM
    head_lanes = LANES // n_heads
    split = t.reshape(*t.shape[:-1], HEAD_DIM // head_lanes, n_heads, head_lanes)
    return jnp.swapaxes(split, -3, -2).reshape(t.shape)


def rwkv_scan_prompt(r, w, k, kk, kka, v):
    b, l, width = r.shape
    n_heads = width // HEAD_DIM
    head_lanes = LANES // n_heads
    n_kg = HEAD_DIM // head_lanes
    tc = _tile(l, SCAN_CHUNK)
    grp = _tile(b, SCAN_GROUP)
    kl = np.arange(LANES)
    col_head = 2 * ((kl % HEAD_DIM) // (2 * SUBLANES)) + kl // HEAD_DIM
    sel = ((kl[None, :, None] % SUBLANES == np.arange(SUBLANES)[:, None, None])
           & (col_head[None, :, None] == kl[None, None, :] // head_lanes))
    ones1 = kl[:, None] // head_lanes == kl[None, :] // head_lanes
    sel, ones1 = jnp.asarray(sel, BF16), jnp.asarray(ones1, BF16)
    seq = pl.BlockSpec((grp, tc, width), lambda i, j: (i, j, 0))
    state = pl.BlockSpec((grp, n_kg, HEAD_DIM, LANES), lambda i, j: (i, 0, 0, 0))
    o, s = pl.pallas_call(
        functools.partial(_rwkv_scan_kernel, n_heads=n_heads),
        grid=(b // grp, l // tc),
        in_specs=[seq, seq, seq, seq, seq, seq,
                  pl.BlockSpec(sel.shape, lambda i, j: (0, 0, 0)),
                  pl.BlockSpec(ones1.shape, lambda i, j: (0, 0))],
        out_specs=[seq, state],
        out_shape=[jax.ShapeDtypeStruct((b, l, width), F32),
                   jax.ShapeDtypeStruct((b, n_kg, HEAD_DIM, LANES), F32)],
        scratch_shapes=[pltpu.VMEM((grp, n_kg, HEAD_DIM, LANES), F32),
                        pltpu.VMEM((tc // head_lanes, grp * HEAD_DIM, LANES), F32)],
        compiler_params=_cparams(("parallel", "arbitrary")),
        name="rwkv_scan_prompt",
    )(r, w, k, kk, kka, v, sel, ones1)
    s = s.reshape(b, n_kg, HEAD_DIM, n_heads, head_lanes).transpose(0, 3, 2, 1, 4)
    return o, s.reshape(b, n_heads, HEAD_DIM, HEAD_DIM)


def _rwkv_step_kernel(r_ref, w_ref, k_ref, kk_ref, kka_ref, v_ref, s0_ref, o_ref, s_ref):
    s0 = s0_ref[...]
    sa = -jnp.sum(s0 * kk_ref[...], axis=-1, keepdims=True)
    s = s0 * w_ref[...] + sa * kka_ref[...] + _to_col(v_ref[...]) * k_ref[...]
    s_ref[...] = s
    o_ref[...] = _to_row(jnp.sum(s * r_ref[...], axis=-1, keepdims=True))


def rwkv_step(r, w, k, kk, kka, v, s0):
    b, width = r.shape
    n_heads = width // HEAD_DIM
    bb = _tile(b, 8)
    row = lambda t: t.reshape(b, n_heads, 1, HEAD_DIM)
    rspec = pl.BlockSpec((bb, n_heads, 1, HEAD_DIM), lambda i: (i, 0, 0, 0))
    sspec = pl.BlockSpec((bb, n_heads, HEAD_DIM, HEAD_DIM), lambda i: (i, 0, 0, 0))
    o, s = pl.pallas_call(
        _rwkv_step_kernel,
        grid=(b // bb,),
        in_specs=[rspec] * 6 + [sspec],
        out_specs=[rspec, sspec],
        out_shape=[jax.ShapeDtypeStruct((b, n_heads, 1, HEAD_DIM), F32),
                   jax.ShapeDtypeStruct((b, n_heads, HEAD_DIM, HEAD_DIM), F32)],
        compiler_params=_cparams(("parallel",)),
        name="rwkv_step",
    )(row(r), row(w), row(k), row(kk), row(kka), row(v), s0)
    return o.reshape(b, width), s


def _diff_lambda(lp, lam_init):
    e1 = jnp.exp(jnp.sum(lp[0:1] * lp[1:2], axis=-1, keepdims=True))
    e2 = jnp.exp(jnp.sum(lp[2:3] * lp[3:4], axis=-1, keepdims=True))
    return e1 - e2 + lam_init


def _diff_attn_kernel(pt_ref, q_ref, k_ref, v_ref, qs_ref, kns_ref, vns_ref, *rest, lam_init, n_heads, n_pages):
    n_cache = (len(rest) - 7) // 2
    kc_refs, vc_refs = rest[:n_cache], rest[n_cache:2 * n_cache]
    lam_ref, subln_ref, o_ref, os_ref, m_scr, l_scr, acc_scr = rest[2 * n_cache:]
    del pt_ref
    lam = _diff_lambda(lam_ref[...], lam_init)
    _prompt_attention(q_ref, k_ref, v_ref, lam, subln_ref, o_ref, m_scr, l_scr, acc_scr, lam_init)
    for r in range(n_cache // n_pages):
        _decode_attention(r, qs_ref, kns_ref, vns_ref, kc_refs[r * n_pages:(r + 1) * n_pages],
                          vc_refs[r * n_pages:(r + 1) * n_pages], lam, subln_ref, os_ref, lam_init, n_heads)


def _prompt_attention(q_ref, k_ref, v_ref, lam, subln_ref, o_ref, m_scr, l_scr, acc_scr, lam_init):
    i = pl.program_id(2)
    tq = q_ref.shape[1]
    tk = tq
    scale = HEAD_DIM ** -0.5
    m_scr[...] = jnp.full_like(m_scr, -jnp.inf)
    l_scr[...] = jnp.zeros_like(l_scr)
    acc_scr[...] = jnp.zeros_like(acc_scr)
    q = (q_ref[0] * scale).astype(BF16)

    def update(j, on_diagonal):
        rows = pl.ds(pl.multiple_of(j * tk, tk), tk)
        k, v = k_ref[0, rows, :].astype(BF16), v_ref[0, rows, :].astype(BF16)
        if on_diagonal:
            visible = (lax.broadcasted_iota(jnp.int32, (tq, tk), 1)
                       <= lax.broadcasted_iota(jnp.int32, (tq, tk), 0))
        for mi in range(2):
            sl = slice(mi * HEAD_DIM, (mi + 1) * HEAD_DIM)
            s = lax.dot_general(q[:, sl], k[:, sl], (((1,), (1,)), ((), ())), preferred_element_type=F32)
            if on_diagonal:
                s = jnp.where(visible, s, -jnp.inf)
            m_old = m_scr[mi]
            m_new = jnp.maximum(m_old, jnp.max(s, axis=-1, keepdims=True))
            alpha = jnp.exp(m_old - m_new)
            p = jnp.exp(s - jnp.concatenate([m_new] * (tk // LANES), axis=1))
            l_scr[mi] = alpha * l_scr[mi] + jnp.sum(p, axis=-1, keepdims=True)
            acc_scr[mi] = alpha * acc_scr[mi] + jnp.dot(p.astype(BF16), v, preferred_element_type=F32)
            m_scr[mi] = m_new

    def below_diagonal(j, carry):
        update(j, False)
        return carry

    lax.fori_loop(0, i, below_diagonal, 0)
    update(i, True)
    o = acc_scr[0] / l_scr[0] - lam * (acc_scr[1] / l_scr[1])
    o_ref[0] = _rms(o, subln_ref[...]) * (1.0 - lam_init)


def _decode_attention(r, q_ref, kn_ref, vn_ref, kc_refs, vc_refs, lam, subln_ref, o_ref, lam_init, n_heads):
    n_rows = 2 * n_heads
    dv = 2 * HEAD_DIM
    scale = HEAD_DIM ** -0.5
    row = lax.broadcasted_iota(jnp.int32, (n_rows, dv), 0)
    lane = lax.broadcasted_iota(jnp.int32, (n_rows, dv), 1)
    qmat = jnp.where(lane // HEAD_DIM == row % 2, q_ref[r], 0.0)
    rows_per = kc_refs[0].shape[1]
    cols = len(kc_refs) * rows_per
    own = (lax.broadcasted_iota(jnp.int32, (n_rows, cols), 1) % n_heads
           == lax.broadcasted_iota(jnp.int32, (n_rows, cols), 0) // 2)
    s = jnp.concatenate(
        [lax.dot_general(qmat.astype(BF16), kc_ref[0].astype(BF16), (((1,), (1,)), ((), ())),
                         preferred_element_type=F32) for kc_ref in kc_refs], axis=1) * scale
    s = jnp.where(own, s, -jnp.inf)
    s_new = jnp.sum(qmat * kn_ref[r], axis=-1, keepdims=True) * scale
    m = jnp.maximum(jnp.max(s, axis=-1, keepdims=True), s_new)
    pr = jnp.exp(s - m)
    p_new = jnp.exp(s_new - m)
    prb = pr.astype(BF16)
    pv = sum(jnp.dot(prb[:, i * rows_per:(i + 1) * rows_per], vc_ref[0].astype(BF16), preferred_element_type=F32)
             for i, vc_ref in enumerate(vc_refs))
    acc = (pv + p_new * vn_ref[r]) / (jnp.sum(pr, axis=-1, keepdims=True) + p_new)
    acc = acc * jnp.where(row % 2 == 0, 1.0, -lam)
    subln = subln_ref[...]
    for h in range(n_heads):
        o = acc[2 * h:2 * h + 1] + acc[2 * h + 1:2 * h + 2]
        o_ref[r, h:h + 1, :] = _rms(o, subln) * (1.0 - lam_init)


def diff_attn(q, k, v, qs, ks, vs, cache_k, cache_v, page_table, lam_p, subln, lam_init):
    b, l, width = q.shape
    bs, n_pages = page_table.shape
    dv = 2 * HEAD_DIM
    n_heads = width // dv
    t = _tile(l, ATTN_BLOCK)
    n = l // t
    n_steps = b * n_heads * n
    per_step = bs // n_steps
    assert per_step * n_steps == bs, "sample rows must divide evenly over the prompt attention grid"
    rows = cache_k.shape[1]
    step = lambda bi, h, i: (bi * n_heads + h) * n + i
    per_map = lambda x: jnp.repeat(x.reshape(bs, n_heads, dv), 2, axis=1)
    qspec = pl.BlockSpec((1, t, dv), lambda bi, h, i, pt: (bi, i, h))
    kspec = pl.BlockSpec((1, l, dv), lambda bi, h, i, pt: (bi, 0, h))
    vec = pl.BlockSpec((per_step, 2 * n_heads, dv), lambda bi, h, i, pt: (step(bi, h, i), 0, 0))
    cache = [pl.BlockSpec((1, rows, dv), functools.partial(
        lambda bi, h, i, pt, r, slot: (pt[(step(bi, h, i) * per_step + r) * n_pages + slot], 0, 0), r=r, slot=slot))
        for r in range(per_step) for slot in range(n_pages)]
    const = lambda x: pl.BlockSpec(x.shape, lambda bi, h, i, pt: (0, 0))
    o, o_s = pl.pallas_call(
        functools.partial(_diff_attn_kernel, lam_init=lam_init, n_heads=n_heads, n_pages=n_pages),
        grid_spec=pltpu.PrefetchScalarGridSpec(
            num_scalar_prefetch=1,
            grid=(b, n_heads, n),
            in_specs=[qspec, kspec, kspec, vec, vec, vec] + cache + cache + [const(lam_p), const(subln)],
            out_specs=[qspec, pl.BlockSpec((per_step, n_heads, dv), lambda bi, h, i, pt: (step(bi, h, i), 0, 0))],
            scratch_shapes=[pltpu.VMEM((2, t, LANES), F32), pltpu.VMEM((2, t, LANES), F32),
                            pltpu.VMEM((2, t, dv), F32)],
        ),
        out_shape=[jax.ShapeDtypeStruct((b, l, width), F32), jax.ShapeDtypeStruct((bs, n_heads, dv), F32)],
        compiler_params=_cparams(("parallel", "parallel", "arbitrary")),
        name="diff_attn",
    )(page_table.reshape(-1), q, k, v, per_map(qs), per_map(ks), per_map(vs),
      *([cache_k] * (per_step * n_pages)), *([cache_v] * (per_step * n_pages)), lam_p, subln)
    return o, o_s.reshape(bs, width)


def _lru_coeffs(x, x1, x2, x3, cw, cb, wa, ba, wi, bi, lam):
    xc = x3 * cw[0:1] + x2 * cw[1:2] + x1 * cw[2:3] + x * cw[3:4]
    xc = xc + cb
    r = jax.nn.sigmoid(_dot(xc, wa) + ba)
    ig = jax.nn.sigmoid(_dot(xc, wi) + bi)
    log_a = -LRU_C * r * _softplus(-lam)
    a = jnp.exp(log_a)
    return a, jnp.sqrt(-jnp.tanh(log_a) * (a * a + 1.0)) * (ig * xc)


def _lru_seq_kernel(x_ref, gr_ref, buf_ref, h0_ref, cw_ref, cb_ref, wa_ref, ba_ref, wi_ref, bi_ref, lam_ref,
                    o_ref, hl_ref, carry, h_scr, a_scr, b_scr):
    n_grp, tl = x_ref.shape[0], x_ref.shape[1]
    n_carry = carry.shape[1]

    @pl.when(pl.program_id(1) == 0)
    def _():
        for g in range(n_grp):
            for d in range(n_carry):
                carry[g, d] = buf_ref[g, d:d + 1, :]
        h_scr[...] = h0_ref[...]

    for g in range(n_grp):
        x = x_ref[g]
        x1 = _shifted(x, carry[g, n_carry - 1])
        x2 = _shifted(x1, carry[g, n_carry - 2])
        x3 = _shifted(x2, carry[g, n_carry - 3])
        for d in range(n_carry):
            carry[g, d] = x[tl - n_carry + d:tl - n_carry + d + 1, :]
        a_scr[g], b_scr[g] = _lru_coeffs(x, x1, x2, x3, cw_ref[...], cb_ref[...], wa_ref[...], ba_ref[...],
                                         wi_ref[...], bi_ref[...], lam_ref[...])
    row_id = lax.broadcasted_iota(jnp.int32, (SUBLANES, x_ref.shape[2]), 0)

    def block(tb, hs):
        base = pl.multiple_of(tb * SUBLANES, SUBLANES)
        rows = pl.ds(base, SUBLANES)
        a8 = [a_scr[g, rows, :] for g in range(n_grp)]
        b8 = [b_scr[g, rows, :] for g in range(n_grp)]
        hs = list(hs)
        out = [jnp.zeros_like(a8[0]) for _ in range(n_grp)]
        for i in range(SUBLANES):
            for g in range(n_grp):
                hs[g] = a8[g][i:i + 1, :] * hs[g] + b8[g][i:i + 1, :]
                out[g] = jnp.where(row_id == i, hs[g], out[g])
        for g in range(n_grp):
            o_ref[g, rows, :] = out[g] * jax.nn.gelu(gr_ref[g, rows, :])
        return tuple(hs)

    hs = lax.fori_loop(0, tl // SUBLANES, block, tuple(h_scr[g] for g in range(n_grp)))
    for g in range(n_grp):
        h_scr[g] = hs[g]
        hl_ref[g] = hs[g]


def lru_prompt(x, gr, buf, h0, prm):
    bsz, l, w = x.shape
    tl = _tile(l, 512)
    grp = _tile(bsz, LRU_GROUP)
    seq = pl.BlockSpec((grp, tl, w), lambda i, j: (i, j, 0))
    vec = pl.BlockSpec((grp, 1, w), lambda i, j: (i, 0, 0))
    full = lambda a: pl.BlockSpec(a.shape, lambda i, j: (0,) * a.ndim)
    params = [prm[n] for n in ("conv_w", "conv_b", "wa", "ba", "wi", "bi", "lam")]
    nb = buf.shape[1]
    o, hl = pl.pallas_call(
        _lru_seq_kernel,
        grid=(bsz // grp, l // tl),
        in_specs=[seq, seq, pl.BlockSpec((grp, nb, w), lambda i, j: (i, 0, 0)), vec] + [full(a) for a in params],
        out_specs=[seq, vec],
        out_shape=[jax.ShapeDtypeStruct((bsz, l, w), F32), jax.ShapeDtypeStruct((bsz, 1, w), F32)],
        scratch_shapes=[pltpu.VMEM((grp, nb, 1, w), F32), pltpu.VMEM((grp, 1, w), F32),
                        pltpu.VMEM((grp, tl, w), F32), pltpu.VMEM((grp, tl, w), F32)],
        compiler_params=_cparams(("parallel", "arbitrary")),
        name="lru_prompt",
    )(x, gr, buf, h0.reshape(bsz, 1, w), *params)
    return o, hl.reshape(bsz, w)


def _lru_step_kernel(x_ref, x1_ref, x2_ref, x3_ref, gr_ref, h0_ref, cw_ref, cb_ref, wa_ref, ba_ref, wi_ref,
                     bi_ref, lam_ref, o_ref, h_ref):
    a, b = _lru_coeffs(x_ref[...], x1_ref[...], x2_ref[...], x3_ref[...], cw_ref[...], cb_ref[...], wa_ref[...],
                       ba_ref[...], wi_ref[...], bi_ref[...], lam_ref[...])
    h = a * h0_ref[...] + b
    h_ref[...] = h
    o_ref[...] = h * jax.nn.gelu(gr_ref[...])


def lru_step(x, buf, gr, h0, prm):
    m, w = x.shape
    tm = _tile(m, 512)
    rows = pl.BlockSpec((tm, w), lambda i: (i, 0))
    full = lambda a: pl.BlockSpec(a.shape, lambda i: (0,) * a.ndim)
    params = [prm[n] for n in ("conv_w", "conv_b", "wa", "ba", "wi", "bi", "lam")]
    nb = buf.shape[1]
    return pl.pallas_call(
        _lru_step_kernel,
        grid=(m // tm,),
        in_specs=[rows] * 6 + [full(a) for a in params],
        out_specs=[rows] * 2,
        out_shape=[jax.ShapeDtypeStruct((m, w), F32)] * 2,
        compiler_params=_cparams(("parallel",)),
        name="lru_step",
    )(x, buf[:, nb - 1], buf[:, nb - 2], buf[:, nb - 3], gr, h0, *params)


def _block_diag(w):
    n, d, e = w.shape
    eye = jnp.eye(n, dtype=w.dtype)
    return (eye[:, None, :, None] * w[:, :, None, :]).reshape(n * d, n * e)


def _mix_even(h, g_norm, pos0, s_ret, s_rwkv, buf, wts, is_prompt):
    b, l, d = h.shape
    m = b * l
    gw = d // 2
    seq = lambda t: t.reshape(b, l, gw)
    flat = lambda t: t.reshape(m, gw)
    to_kg, to_nat = _to_key_group, _from_key_group
    groups = lambda t: [t[..., i * gw:(i + 1) * gw] for i in range(4)]
    br, bk, bv, bz = groups(buf)
    buf_kg = [to_kg(br), to_kg(bk), bv, bz]
    w_in = wts["ab_w_in"]
    if is_prompt:
        qa, ka, va, ga = norm_matmul(h.reshape(m, d), g_norm, w_in[:, :4 * gw], gw)
        (r, w, k, v, kk, kka, g, bonus), last = rwkv_project_prep(
            h, g_norm, w_in[:, 4 * gw:], jnp.concatenate(buf_kg, axis=-1), wts["rwkv"])
        last = groups(last)
        pos = pos0 + jnp.arange(l, dtype=F32)
        o_a, s_ret_new = retention_prompt(seq(qa), seq(ka), seq(va), seq(ga), pos, wts["ones_bd"])
        o_a = flat(o_a)
        o_b, s_rwkv_new = rwkv_scan_prompt(r, w, k, kk, kka, v)
    else:
        u = norm_matmul(h.reshape(m, d), g_norm, w_in, gw)
        qa, ka, va, ga = u[:4]
        last = [seq(t) for t in u[4:8]]
        rows = lambda t: t.reshape(1, m, gw)
        r, w, k, v, kk, kka, g, bonus = rwkv_prep([rows(t) for t in last], [rows(t) for t in buf_kg], wts["rwkv"])
        o_a, s_ret_new = retention_step(qa, ka, va, ga, s_ret, jnp.float32(pos0))
        o_b, s_rwkv_new = rwkv_step(*(flat(to_nat(t)) for t in (r, w, k, kk, kka)), flat(v), s_rwkv)
    buf_new = jnp.concatenate([to_nat(last[0]), to_nat(last[1]), last[2], last[3]], axis=-1)
    post = (flat(bonus), flat(g), wts["rwkv_ln"], wts["ones_bd"])
    return o_a, flat(o_b), post, s_ret_new, s_rwkv_new, buf_new


def _mix_odd(h, g_norm, lru_h, lru_buf, wts, is_prompt):
    b, l, d = h.shape
    m = b * l
    gw = d // 2
    u = norm_matmul(h.reshape(m, d), g_norm, wts["cd_w_in"], gw, head_major=(1, 2))
    seq = lambda t: t.reshape(b, l, gw)
    xr = seq(u[3])
    if is_prompt:
        o_c = yield seq(u[0]), seq(u[1]), seq(u[2])
        o_c = o_c.reshape(m, gw)
        o_d, h_last = lru_prompt(xr, seq(u[4]), lru_buf, lru_h, wts["lru"])
    else:
        o_c = yield u[0], u[1], u[2]
        o_d, h_last = lru_step(u[3], lru_buf, u[4], lru_h, wts["lru"])
    buf_new = jnp.concatenate([lru_buf, xr], axis=1)[:, l:]
    n_heads = gw // (2 * HEAD_DIM)
    k_new = u[5].reshape(b, l, n_heads, 2 * HEAD_DIM)
    v_new = u[6].reshape(b, l, n_heads, 2 * HEAD_DIM)
    return o_c, o_d.reshape(m, gw), k_new, v_new, h_last, buf_new


def _advance(gen, value):
    try:
        return gen.send(value), None
    except StopIteration as done:
        return None, done.value


def _trunk(x, p, pos0, s_ret, s_rwkv, s_shift, s_lru_h, s_lru_conv, wts, is_prompt):
    b, l, d = x.shape
    m = b * l
    depth = wts["norm_g"].shape[0]
    h = x.reshape(m, d)
    ret_l, rwkv_l, shift_l, k_l, v_l, lh_l, lc_l = [], [], [], [], [], [], []
    for i in range(depth):
        j = i // 2
        g = wts["norm_g"][i]
        gn = lambda n: g[n:n + 1]
        h = ffn_block(h, gn(0), wts["ffn_in"], wts["ffn_out"], gn(1), i, 0)
        post = None
        if i % 2 == 0:
            o1, o2, post, sr, sw, sb = _mix_even(h.reshape(b, l, d), gn(2), pos0, s_ret[j], s_rwkv[j], s_shift[j],
                                                 wts["even"][j], is_prompt)
            ret_l.append(sr)
            rwkv_l.append(sw)
            shift_l.append(sb)
            w_out = wts["even"][j]["w_out"]
        else:
            o1, o2, kn, vn, lh, lc = yield from _mix_odd(h.reshape(b, l, d), gn(2), s_lru_h[j], s_lru_conv[j],
                                                         wts["odd"][j], is_prompt)
            k_l.append(kn)
            v_l.append(vn)
            lh_l.append(lh)
            lc_l.append(lc)
            w_out = wts["odd"][j]["w_out"]
        h = out_proj(o1, o2, w_out, h, gn(3), post)
        h = ffn_block(h, gn(4), wts["ffn_in"], wts["ffn_out"], gn(5), i, 1,
                      ple=(gn(6), wts["ple_gate"], p.reshape(depth, m, -1), wts["ple"], gn(7)))
    st = lambda lst: jnp.stack(lst, axis=0)
    return (h.reshape(b, l, d), st(k_l), st(v_l), st(ret_l), st(rwkv_l), st(shift_l), st(lh_l), st(lc_l))


def kernel(x_prompt, x_sample, cache_k, cache_v, state_ret, state_rwkv, state_rwkv_shift, state_lru_h, state_lru_conv, page_table, p_prompt, p_sample, norm_g, ffn_w_in, ffn_w_out, ple_w, ple_gate_w, ab_w_in, ab_w_out, rwkv_mu, rwkv_w0, rwkv_w1, rwkv_w2, rwkv_a0, rwkv_a1, rwkv_a2, rwkv_g1, rwkv_g2, rwkv_kk, rwkv_ka, rwkv_rk, rwkv_ln, cd_w_in, cd_w_out, diff_lam, diff_subln, lru_conv_w, lru_conv_b, lru_wa, lru_ba, lru_wi, lru_bi, lru_lambda):
    depth = norm_g.shape[0]
    n_a, n_c = state_ret.shape[0], state_lru_h.shape[0]
    bp = x_prompt.shape[0]
    gw = ab_w_out.shape[1] // 2
    bf = lambda t: t.astype(BF16)
    row = lambda t: t.reshape(1, -1)
    ones_bd = _block_diag(jnp.ones((gw // HEAD_DIM, HEAD_DIM, HEAD_DIM), BF16))
    perm = _key_group_perm(gw)
    kg = _to_key_group
    head_kg = perm // HEAD_DIM
    tile_head = np.tile(head_kg[:LANES], 2)
    ones_kk = jnp.asarray(tile_head[:, None] == head_kg[None, :LANES], BF16)
    ones_rk = jnp.asarray(tile_head[:, None] == (np.arange(gw) // HEAD_DIM)[None, :], BF16)

    def ab_in_kg(w):
        cols = [w[:, g * gw:(g + 1) * gw] for g in range(w.shape[1] // gw)]
        cols[4], cols[5] = kg(cols[4]), kg(cols[5])
        return bf(jnp.concatenate(cols, axis=1))

    wts = {
        "norm_g": norm_g,
        "ffn_in": bf(ffn_w_in), "ffn_out": bf(ffn_w_out), "ple": bf(ple_w), "ple_gate": bf(ple_gate_w),
        "even": [{
            "ab_w_in": ab_in_kg(ab_w_in[j]), "w_out": bf(ab_w_out[j]), "rwkv_ln": rwkv_ln[j],
            "ones_bd": ones_bd,
            "rwkv": {"mu": jnp.concatenate([kg(rwkv_mu[j][:2]), rwkv_mu[j][2:]], axis=0),
                     "w0": kg(row(rwkv_w0[j])), "w1": bf(rwkv_w1[j]), "w2": bf(kg(rwkv_w2[j])),
                     "a0": kg(row(rwkv_a0[j])), "a1": bf(rwkv_a1[j]), "a2": bf(kg(rwkv_a2[j])),
                     "g1": bf(rwkv_g1[j]), "g2": bf(rwkv_g2[j]), "kk": kg(row(rwkv_kk[j])),
                     "ka": kg(row(rwkv_ka[j])), "rk": kg(row(rwkv_rk[j])),
                     "ones_kk": ones_kk, "ones_rk": ones_rk},
        } for j in range(n_a)],
        "odd": [{
            "cd_w_in": bf(cd_w_in[j]), "w_out": bf(cd_w_out[j]), "diff_lam": diff_lam[j],
            "diff_subln": row(diff_subln[j]),
            "lru": {"conv_w": lru_conv_w[j], "conv_b": row(lru_conv_b[j]), "wa": bf(_block_diag(lru_wa[j])),
                    "ba": row(lru_ba[j]), "wi": bf(_block_diag(lru_wi[j])), "bi": row(lru_bi[j]),
                    "lam": row(lru_lambda[j])},
        } for j in range(n_c)],
    }
    zeros = lambda *shape: jnp.zeros(shape, F32)
    past_len = page_table.shape[1] * cache_k.shape[2]
    n_pool, page = cache_k.shape[1], cache_k.shape[2]
    as_rows = lambda c: c.reshape(n_c, n_pool, page * c.shape[3], c.shape[4])
    pages_k, pages_v = as_rows(cache_k), as_rows(cache_v)
    prompt = _trunk(x_prompt, p_prompt, 0.0, [None] * n_a, [None] * n_a,
                    zeros(n_a, bp, 1, 4 * gw), zeros(n_c, bp, gw), zeros(n_c, bp, CONV_W - 1, gw), wts, True)
    sample = _trunk(x_sample, p_sample, float(past_len), state_ret, state_rwkv, state_rwkv_shift,
                    state_lru_h, state_lru_conv, wts, False)
    (qkv_p, out_p), (qkv_s, out_s) = _advance(prompt, None), _advance(sample, None)
    j = 0
    while out_p is None:
        odd = wts["odd"][j]
        lam_init = 0.8 - 0.6 * math.exp(-0.3 * (2 * j + 1))
        o_p, o_s = diff_attn(*qkv_p, *qkv_s, pages_k[j], pages_v[j], page_table, odd["diff_lam"],
                             odd["diff_subln"], lam_init)
        (qkv_p, out_p), (qkv_s, out_s) = _advance(prompt, o_p), _advance(sample, o_s)
        j += 1
    yp, kp, vp, rp, wp, sp, hp, cp = out_p
    ys, ks_, vs, rs, ws, ss, hs, cs = out_s
    return (yp, ys, kp, vp, rp, wp, sp, hp, cp, ks_, vs, rs, ws, ss, hs, cs)
```

```python
import functools
import math

import jax
import jax.numpy as jnp
import numpy as np
from jax import lax
from jax.experimental import pallas as pl
from jax.experimental.pallas import tpu as pltpu

F32 = jnp.float32
BF16 = jnp.bfloat16

HEAD_DIM = 64
CONV_W = 4
LRU_C = 8.0
ROPE_BASE = 10000.0
EPS = 1e-6
RWKV_GN_EPS = 64e-5
RET_CHUNK = 256
ATTN_BLOCK = 512
SCAN_CHUNK = 128
LRU_GROUP = 2
SCAN_GROUP = 8
LANES = 128
SUBLANES = 8
VMEM_LIMIT = 48 * 1024 * 1024


def _cparams(sem):
    return pltpu.CompilerParams(dimension_semantics=sem, vmem_limit_bytes=VMEM_LIMIT)


def _tile(n, pref):
    t = min(n, pref)
    while n % t:
        t //= 2
    return t


def _rms(x, g):
    return x * lax.rsqrt(jnp.mean(x * x, axis=-1, keepdims=True) + EPS) * g


def _dot(a, b):
    return jnp.dot(a.astype(BF16), b.astype(BF16), preferred_element_type=F32)


def _seg_sum(x, ones_bd):
    hi = x.astype(BF16)
    lo = (x - hi.astype(F32)).astype(BF16)
    return (jnp.dot(hi, ones_bd, preferred_element_type=F32)
            + jnp.dot(lo, ones_bd, preferred_element_type=F32))


def _ffn_kernel(h_ref, gpre_ref, wg_ref, wu_ref, wo_ref, gpost_ref, *rest):
    o_ref, xn_ref, acc_ref = rest[-3:]
    ple = rest[:-3]
    j = pl.program_id(1)

    @pl.when(j == 0)
    def _():
        xn_ref[...] = _rms(h_ref[...], gpre_ref[...]).astype(BF16)
        acc_ref[...] = jnp.zeros_like(acc_ref)

    xn = xn_ref[...]
    gate = jnp.dot(xn, wg_ref[...], preferred_element_type=F32)
    up = jnp.dot(xn, wu_ref[...], preferred_element_type=F32)
    act = (gate * jax.nn.sigmoid(gate) * up).astype(BF16)
    acc_ref[...] += jnp.dot(act, wo_ref[...], preferred_element_type=F32)

    @pl.when(j == pl.num_programs(1) - 1)
    def _():
        h = h_ref[...] + 0.5 * _rms(acc_ref[...], gpost_ref[...])
        if ple:
            g6_ref, wgate_ref, p_ref, wp_ref, g7_ref = ple
            gate_p = jax.nn.sigmoid(_dot(_rms(h, g6_ref[...]), wgate_ref[...]))
            h = h + _rms(gate_p * _dot(p_ref[...], wp_ref[...]), g7_ref[...])
        o_ref[...] = h


def ffn_block(h, g_pre, w_in, w_out, g_post, layer, half, ple=None):
    m, d = h.shape
    f = w_out.shape[2]
    tm, tf = _tile(m, 1024), _tile(f, 512)
    nf = f // tf
    vec = pl.BlockSpec((1, d), lambda i, j: (0, 0))
    extra, extra_specs = [], []
    if ple is not None:
        g6, gate_w, p, emb_w, g7 = ple
        pd = p.shape[2]
        extra = [g6, gate_w, p, emb_w, g7]
        extra_specs = [vec, pl.BlockSpec((None, d, d), lambda i, j: (layer, 0, 0)),
                       pl.BlockSpec((None, tm, pd), lambda i, j: (layer, i, 0)),
                       pl.BlockSpec((None, pd, d), lambda i, j: (layer, 0, 0)), vec]
    return pl.pallas_call(
        _ffn_kernel,
        grid=(m // tm, nf),
        in_specs=[
            pl.BlockSpec((tm, d), lambda i, j: (i, 0)),
            vec,
            pl.BlockSpec((None, None, d, tf), lambda i, j: (layer, half, 0, j)),
            pl.BlockSpec((None, None, d, tf), lambda i, j: (layer, half, 0, j + nf)),
            pl.BlockSpec((None, None, tf, d), lambda i, j: (layer, half, j, 0)),
            vec,
        ] + extra_specs,
        out_specs=pl.BlockSpec((tm, d), lambda i, j: (i, 0)),
        out_shape=jax.ShapeDtypeStruct((m, d), F32),
        scratch_shapes=[pltpu.VMEM((tm, d), BF16), pltpu.VMEM((tm, d), F32)],
        compiler_params=_cparams(("parallel", "arbitrary")),
        name="ffn_block",
    )(h, g_pre, w_in, w_in, w_out, g_post, *extra)


def _norm_matmul_kernel(h_ref, g_ref, w_ref, *o_refs, n_groups, head_major):
    xn = _rms(h_ref[...], g_ref[...]).astype(BF16)
    tm, tn = o_refs[0].shape
    heads = tn // LANES
    for gi in range(n_groups):
        res = jnp.dot(xn, w_ref[:, gi * tn:(gi + 1) * tn], preferred_element_type=F32)
        o_refs[gi][...] = res
        if gi in head_major:
            hm_ref = o_refs[n_groups + head_major.index(gi)]
            for hh in range(heads):
                hm_ref[pl.ds(hh, tm, stride=heads), :] = res[:, hh * LANES:(hh + 1) * LANES]


def norm_matmul(h, g, w, tn, head_major=()):
    m, d = h.shape
    n = w.shape[1]
    tm = _tile(m, 512)
    heads = tn // LANES
    rows = pl.BlockSpec((tm, tn), lambda i: (i, 0))
    return pl.pallas_call(
        functools.partial(_norm_matmul_kernel, n_groups=n // tn, head_major=tuple(head_major)),
        grid=(m // tm,),
        in_specs=[
            pl.BlockSpec((tm, d), lambda i: (i, 0)),
            pl.BlockSpec((1, d), lambda i: (0, 0)),
            pl.BlockSpec((d, n), lambda i: (0, 0)),
        ],
        out_specs=[rows] * (n // tn) + [pl.BlockSpec((tm * heads, LANES), lambda i: (i, 0))] * len(head_major),
        out_shape=([jax.ShapeDtypeStruct((m, tn), F32)] * (n // tn)
                   + [jax.ShapeDtypeStruct((m * heads, LANES), F32)] * len(head_major)),
        compiler_params=_cparams(("parallel",)),
        name="norm_matmul",
    )(h, g, w)


def _out_proj_kernel(oa_ref, ob_ref, wa_ref, wb_ref, h_ref, g_ref, *rest):
    ob = ob_ref[...]
    if len(rest) > 1:
        bonus_ref, gate_ref, ln_ref, ones_ref = rest[:4]
        ones_bd = ones_ref[...]
        inv = 1.0 / HEAD_DIM
        oc = ob - _seg_sum(ob, ones_bd) * inv
        on = oc * lax.rsqrt(_seg_sum(oc * oc, ones_bd) * inv + RWKV_GN_EPS)
        ln = ln_ref[...]
        ob = (on * ln[0:1] + ln[1:2] + bonus_ref[...]) * gate_ref[...]
    o_ref = rest[-1]
    y = _dot(oa_ref[...], wa_ref[...]) + _dot(ob, wb_ref[...])
    o_ref[...] = h_ref[...] + _rms(y, g_ref[...])


def out_proj(oa, ob, w, h, g, rwkv_post=None):
    m, d = h.shape
    gw = oa.shape[1]
    tm = _tile(m, 512)
    rows = pl.BlockSpec((tm, gw), lambda i: (i, 0))
    extra, extra_specs = [], []
    if rwkv_post is not None:
        bonus, gate, ln, ones_bd = rwkv_post
        extra = [bonus, gate, ln, ones_bd]
        extra_specs = [rows, rows, pl.BlockSpec(ln.shape, lambda i: (0, 0)),
                       pl.BlockSpec(ones_bd.shape, lambda i: (0, 0))]
    return pl.pallas_call(
        _out_proj_kernel,
        grid=(m // tm,),
        in_specs=[
            rows, rows,
            pl.BlockSpec((gw, d), lambda i: (0, 0)),
            pl.BlockSpec((gw, d), lambda i: (1, 0)),
            pl.BlockSpec((tm, d), lambda i: (i, 0)),
            pl.BlockSpec((1, d), lambda i: (0, 0)),
        ] + extra_specs,
        out_specs=pl.BlockSpec((tm, d), lambda i: (i, 0)),
        out_shape=jax.ShapeDtypeStruct((m, d), F32),
        compiler_params=_cparams(("parallel",)),
        name="out_proj",
    )(oa, ob, w, w, h, g, *extra)


def _retention_tables(n_heads, c, pos):
    lg = jnp.log1p(-jnp.exp2(-5.0 - jnp.arange(n_heads, dtype=F32)))
    idx = jnp.arange(c, dtype=F32)
    rel = idx[:, None] - idx[None, :]
    dmask = jnp.where(rel[None] >= 0, jnp.exp(jnp.maximum(rel, 0.0)[None] * lg[:, None, None]), 0.0)
    rep = lambda t: jnp.repeat(t, HEAD_DIM, axis=-1)
    q_dec = rep(jnp.exp((idx[:, None] + 1.0) * lg[None, :]))
    k_dec = rep(jnp.exp((c - 1.0 - idx[:, None]) * lg[None, :]))
    c_dec = rep(jnp.exp(c * lg)[None, :])
    half = HEAD_DIM // 2
    freq = 1.0 / (ROPE_BASE ** jnp.linspace(0.0, 1.0, half, dtype=F32))
    ang = pos[:, None] * freq[None, :]
    cos, sin = jnp.cos(ang), jnp.sin(ang)
    cos_t = jnp.tile(jnp.concatenate([cos, cos], axis=-1), (1, n_heads))
    sin_t = jnp.tile(jnp.concatenate([-sin, sin], axis=-1), (1, n_heads))
    return dmask, q_dec, k_dec, c_dec, cos_t, sin_t


def _retention_kernel(q_ref, k_ref, v_ref, g_ref, cos_ref, sin_ref, dmask_ref, qdec_ref, kdec_ref,
                      cdec_ref, ones_ref, o_ref, s_ref, s_scr, *, n_heads):
    c = pl.program_id(1)

    @pl.when(c == 0)
    def _():
        s_scr[...] = jnp.zeros_like(s_scr)

    q, k, v, g = q_ref[0], k_ref[0], v_ref[0], g_ref[0]
    cos, sin = cos_ref[...], sin_ref[...]
    rows, width = q.shape
    lane = lax.broadcasted_iota(jnp.int32, q.shape, 1)
    first_half = (lane % HEAD_DIM) < (HEAD_DIM // 2)

    def rot(x):
        swapped = jnp.where(first_half, pltpu.roll(x, width - HEAD_DIM // 2, 1),
                            pltpu.roll(x, HEAD_DIM // 2, 1))
        return x * cos + swapped * sin

    qr = rot(q)
    kr = rot(k) * (HEAD_DIM ** -0.5)
    kd = kr * kdec_ref[...]
    qdec = qdec_ref[...]
    cdec = cdec_ref[...]
    head_a = lax.broadcasted_iota(jnp.int32, (rows, LANES), 1) < HEAD_DIM
    same_head = (lax.broadcasted_iota(jnp.int32, (LANES, LANES), 0) // HEAD_DIM
                 == lax.broadcasted_iota(jnp.int32, (LANES, LANES), 1) // HEAD_DIM)
    nt = (((1,), (1,)), ((), ()))
    outs = []
    for p in range(n_heads // 2):
        sl = slice(p * LANES, (p + 1) * LANES)
        qp = qr[:, sl].astype(BF16)
        kp = kr[:, sl].astype(BF16)
        vp = v[:, sl].astype(BF16)
        zero = jnp.zeros_like(qp)
        att_a = lax.dot_general(jnp.where(head_a, qp, zero), kp, nt, preferred_element_type=F32) * dmask_ref[2 * p]
        att_b = lax.dot_general(jnp.where(head_a, zero, qp), kp, nt,
                                preferred_element_type=F32) * dmask_ref[2 * p + 1]
        s_old = s_scr[p]
        inner = jnp.where(head_a, jnp.dot(att_a.astype(BF16), vp, preferred_element_type=F32),
                          jnp.dot(att_b.astype(BF16), vp, preferred_element_type=F32))
        outs.append(inner + jnp.dot(qp, s_old.astype(BF16), preferred_element_type=F32) * qdec[:, sl])
        update = lax.dot_general(kd[:, sl].astype(BF16), vp, (((0,), (0,)), ((), ())),
                                 preferred_element_type=F32)
        s_scr[p] = jnp.where(same_head, s_old * cdec[:, sl] + update, 0.0)
    o = jnp.concatenate(outs, axis=1)
    ones_bd = ones_ref[...]
    inv = 1.0 / HEAD_DIM
    oc = o - _seg_sum(o, ones_bd) * inv
    on = oc * lax.rsqrt(_seg_sum(oc * oc, ones_bd) * inv + EPS)
    o_ref[0] = on * (g * jax.nn.sigmoid(g))

    @pl.when(c == pl.num_programs(1) - 1)
    def _():
        for p in range(n_heads // 2):
            s_pair = s_scr[p]
            s_ref[0, 2 * p] = s_pair[:HEAD_DIM, :HEAD_DIM]
            s_ref[0, 2 * p + 1] = s_pair[HEAD_DIM:, HEAD_DIM:]


def retention_prompt(q, k, v, g, pos, ones_bd):
    b, l, width = q.shape
    n_heads = width // HEAD_DIM
    c = _tile(l, RET_CHUNK)
    dmask, q_dec, k_dec, c_dec, cos_t, sin_t = _retention_tables(n_heads, c, pos)
    seq = pl.BlockSpec((1, c, width), lambda i, j: (i, j, 0))
    tab = pl.BlockSpec((c, width), lambda i, j: (j, 0))
    fixed = lambda shape: pl.BlockSpec(shape, lambda i, j: (0,) * len(shape))
    return pl.pallas_call(
        functools.partial(_retention_kernel, n_heads=n_heads),
        grid=(b, l // c),
        in_specs=[seq, seq, seq, seq, tab, tab, fixed((n_heads, c, c)), fixed((c, width)),
                  fixed((c, width)), fixed((1, width)), fixed(ones_bd.shape)],
        out_specs=[seq, pl.BlockSpec((1, n_heads, HEAD_DIM, HEAD_DIM), lambda i, j: (i, 0, 0, 0))],
        out_shape=[jax.ShapeDtypeStruct((b, l, width), F32),
                   jax.ShapeDtypeStruct((b, n_heads, HEAD_DIM, HEAD_DIM), F32)],
        scratch_shapes=[pltpu.VMEM((n_heads // 2, LANES, LANES), F32)],
        compiler_params=_cparams(("parallel", "arbitrary")),
        name="retention_prompt",
    )(q, k, v, g, cos_t, sin_t, dmask, q_dec, k_dec, c_dec, ones_bd)


def _retention_step_kernel(q_ref, k_ref, v_ref, g_ref, cos_ref, sin_ref, gam_ref, s0_ref, o_ref, s_ref):
    half = HEAD_DIM // 2
    cos, sin = cos_ref[...], sin_ref[...]
    rot = lambda x: x * cos + jnp.concatenate([x[half:], x[:half]], axis=0) * sin
    q = rot(q_ref[...])
    k = rot(k_ref[...]) * (HEAD_DIM ** -0.5)
    v, g = v_ref[...], g_ref[...]
    gam = gam_ref[0]
    att = jnp.sum(q * k, axis=0, keepdims=True)
    cross = jnp.zeros_like(v)
    for ki in range(HEAD_DIM):
        s_row = s0_ref[0, ki]
        cross = cross + q[ki:ki + 1, :] * s_row
        s_ref[0, ki] = s_row * gam + k[ki:ki + 1, :] * v
    o = att * v + cross * gam
    oc = o - jnp.mean(o, axis=0, keepdims=True)
    on = oc * lax.rsqrt(jnp.mean(oc * oc, axis=0, keepdims=True) + EPS)
    o_ref[...] = on * (g * jax.nn.sigmoid(g))


def _batch_last(s):
    return jnp.transpose(s, (1, 2, 3, 0))


def _batch_first(s):
    return jnp.transpose(s, (3, 0, 1, 2))


def retention_step(q, k, v, g, s0, pos):
    b, width = q.shape
    n_heads = width // HEAD_DIM
    half = HEAD_DIM // 2
    freq = 1.0 / (ROPE_BASE ** jnp.linspace(0.0, 1.0, half, dtype=F32))
    ang = pos * freq
    lanes = lambda t: jnp.broadcast_to(t[:, None], (HEAD_DIM, b))
    cos_c = lanes(jnp.concatenate([jnp.cos(ang), jnp.cos(ang)]))
    sin_c = lanes(jnp.concatenate([-jnp.sin(ang), jnp.sin(ang)]))
    gam = jnp.exp(jnp.log1p(-jnp.exp2(-5.0 - jnp.arange(n_heads, dtype=F32))))
    gam = jnp.broadcast_to(gam[:, None, None], (n_heads, 1, b))
    vec = pl.BlockSpec((HEAD_DIM, b), lambda h: (h, 0))
    table = pl.BlockSpec((HEAD_DIM, b), lambda h: (0, 0))
    sspec = pl.BlockSpec((1, HEAD_DIM, HEAD_DIM, b), lambda h: (h, 0, 0, 0))
    o, s = pl.pallas_call(
        _retention_step_kernel,
        grid=(n_heads,),
        in_specs=[vec, vec, vec, vec, table, table, pl.BlockSpec((1, 1, b), lambda h: (h, 0, 0)), sspec],
        out_specs=[vec, sspec],
        out_shape=[jax.ShapeDtypeStruct((width, b), F32),
                   jax.ShapeDtypeStruct((n_heads, HEAD_DIM, HEAD_DIM, b), F32)],
        compiler_params=_cparams(("parallel",)),
        name="retention_step",
    )(q.T, k.T, v.T, g.T, cos_c, sin_c, gam, _batch_last(s0))
    return o.T, _batch_first(s)


def _softplus(x):
    return jnp.maximum(x, 0.0) + jnp.log1p(jnp.exp(-jnp.abs(x)))


def _shifted(cur, carry_row):
    if cur.shape[0] == 1:
        return carry_row
    first = lax.broadcasted_iota(jnp.int32, cur.shape, 0) == 0
    return jnp.where(first, carry_row, pltpu.roll(cur, 1, 0))


PREP_PARAMS = ("mu", "w0", "w1", "w2", "a0", "a1", "a2", "g1", "g2", "kk", "ka", "rk", "ones_kk", "ones_rk")
N_PREP_PARAMS = len(PREP_PARAMS)
N_PREP_OUT = 8


def _rwkv_prep_kernel(*refs, project):
    n_in = 4 if project else 8
    (mu_ref, w0_ref, w1_ref, w2_ref, a0_ref, a1_ref, a2_ref, g1_ref, g2_ref, kkp_ref, kap_ref, rk_ref,
     ones_kk_ref, ones_rk_ref) = refs[n_in:n_in + N_PREP_PARAMS]
    n_out = n_in + N_PREP_PARAMS
    r_out, w_out, k_out, v_out, kk_out, kka_out, g_out, bonus_out = refs[n_out:n_out + N_PREP_OUT]
    if project:
        h_ref, gnorm_ref, wproj_ref, buf_ref = refs[:4]
        last_ref, carry = refs[n_out + N_PREP_OUT:]
        width = wproj_ref.shape[1] // 4
        xn = _rms(h_ref[0], gnorm_ref[...]).astype(BF16)
        cur = [jnp.dot(xn, wproj_ref[:, gi * width:(gi + 1) * width], preferred_element_type=F32)
               for gi in range(4)]

        @pl.when(pl.program_id(1) == 0)
        def _():
            for gi in range(4):
                carry[gi] = buf_ref[0, :, gi * width:(gi + 1) * width]

        prev = [_shifted(x, carry[gi]) for gi, x in enumerate(cur)]
        for gi, x in enumerate(cur):
            carry[gi] = x[x.shape[0] - 1:, :]
            last_ref[0, :, gi * width:(gi + 1) * width] = x[x.shape[0] - 1:, :]
    else:
        cur = [ref[0] for ref in refs[:4]]
        prev = [ref[0] for ref in refs[4:8]]
    mu = mu_ref[...]
    lerp = lambda x, xp, i: x + (xp - x) * mu[i:i + 1]
    zr, pz = cur[3], prev[3]
    r = lerp(cur[0], prev[0], 0)
    kx = lerp(cur[1], prev[1], 1)
    vx = lerp(cur[2], prev[2], 2)
    zw, za, zg = lerp(zr, pz, 3), lerp(zr, pz, 4), lerp(zr, pz, 5)
    wpre = w0_ref[...] + _dot(jnp.tanh(_dot(zw, w1_ref[...])), w2_ref[...])
    decay = jnp.exp(-jnp.exp(-_softplus(-wpre) - 0.5))
    a = jax.nn.sigmoid(a0_ref[...] + _dot(_dot(za, a1_ref[...]), a2_ref[...]))
    g = _dot(jax.nn.sigmoid(_dot(zg, g1_ref[...])), g2_ref[...])
    kk = kx * kkp_ref[...]
    n_tiles = kk.shape[1] // LANES

    def head_sum(x, ones_ref):
        tile_sum = sum(x[:, t * LANES:(t + 1) * LANES] for t in range(n_tiles))
        return jnp.dot(_hi_lo(tile_sum), ones_ref[...], preferred_element_type=F32)

    kk_norm = head_sum(kk * kk, ones_kk_ref)
    kk = kk / jnp.maximum(jnp.sqrt(jnp.concatenate([kk_norm] * n_tiles, axis=1)), 1e-12)
    k32 = kx * (1.0 + (a - 1.0) * kap_ref[...])
    r_out[0] = r
    w_out[0] = decay
    k_out[0] = k32
    v_out[0] = vx
    kk_out[0] = kk
    kka_out[0] = kk * a
    g_out[0] = g
    bonus_out[0] = head_sum(r * k32 * rk_ref[...], ones_rk_ref) * vx


def rwkv_prep(cur, prev, prm):
    b, l, w = cur[0].shape
    tl = _tile(l, 256)
    seq = pl.BlockSpec((1, tl, w), lambda i, j: (i, j, 0))
    full = lambda a: pl.BlockSpec(a.shape, lambda i, j: (0,) * a.ndim)
    params = [prm[n] for n in PREP_PARAMS]
    return pl.pallas_call(
        functools.partial(_rwkv_prep_kernel, project=False),
        grid=(b, l // tl),
        in_specs=[seq] * 8 + [full(a) for a in params],
        out_specs=[seq] * N_PREP_OUT,
        out_shape=[jax.ShapeDtypeStruct((b, l, w), F32)] * N_PREP_OUT,
        compiler_params=_cparams(("parallel", "arbitrary")),
        name="rwkv_prep",
    )(*cur, *prev, *params)


def rwkv_project_prep(h, g_norm, w_proj, buf, prm):
    b, l, d = h.shape
    w = w_proj.shape[1] // 4
    tl = _tile(l, 512)
    seq = pl.BlockSpec((1, tl, w), lambda i, j: (i, j, 0))
    row = pl.BlockSpec((1, 1, 4 * w), lambda i, j: (i, 0, 0))
    full = lambda a: pl.BlockSpec(a.shape, lambda i, j: (0,) * a.ndim)
    params = [prm[n] for n in PREP_PARAMS]
    *outs, last = pl.pallas_call(
        functools.partial(_rwkv_prep_kernel, project=True),
        grid=(b, l // tl),
        in_specs=[pl.BlockSpec((1, tl, d), lambda i, j: (i, j, 0)), full(g_norm), full(w_proj), row]
        + [full(a) for a in params],
        out_specs=[seq] * N_PREP_OUT + [row],
        out_shape=[jax.ShapeDtypeStruct((b, l, w), F32)] * N_PREP_OUT + [jax.ShapeDtypeStruct((b, 1, 4 * w), F32)],
        scratch_shapes=[pltpu.VMEM((4, 1, w), F32)],
        compiler_params=_cparams(("parallel", "arbitrary")),
        name="rwkv_project_prep",
    )(h, g_norm, w_proj, buf, *params)
    return outs, last


def _hi_lo(x):
    hi = x.astype(BF16)
    lo = (x - hi.astype(F32)).astype(BF16)
    return jnp.concatenate([hi, lo], axis=1)


def _value_columns(v8):
    hi = v8.astype(BF16).astype(F32)
    lo = (v8 - hi).astype(BF16).astype(F32)
    stacked = jnp.concatenate([part[:, p * LANES:(p + 1) * LANES]
                               for p in range(v8.shape[1] // LANES) for part in (hi, lo)], axis=0)
    cols = stacked.T
    return jnp.concatenate([cols[:HEAD_DIM], cols[HEAD_DIM:]], axis=1).astype(BF16)


def _rwkv_scan_kernel(r_ref, w_ref, k_ref, kk_ref, kka_ref, v_ref, sel_ref, ones_ref, o_ref, s_ref, s_scr,
                      ot_scr, *, n_heads):
    c = pl.program_id(1)
    n_grp, tc = r_ref.shape[0], r_ref.shape[1]
    head_lanes = LANES // n_heads
    n_kg = HEAD_DIM // head_lanes

    @pl.when(c == 0)
    def _():
        s_scr[...] = jnp.zeros_like(s_scr)

    ot_scr[...] = jnp.zeros_like(ot_scr)
    ones1 = ones_ref[...]
    t_lane = lax.broadcasted_iota(jnp.int32, (n_grp * HEAD_DIM, LANES), 1) % head_lanes
    rows_of = lambda x, g: x[g * HEAD_DIM:(g + 1) * HEAD_DIM]

    def block(tb, carry):
        base = pl.multiple_of(tb * SUBLANES, SUBLANES)
        refs = {"kk": kk_ref, "w": w_ref, "kka": kka_ref, "k": k_ref, "r": r_ref}
        vp = jnp.concatenate([_value_columns(v_ref[g, pl.ds(base, SUBLANES), :]) for g in range(n_grp)], axis=0)
        tile = base // head_lanes
        head_sum = lambda x: jnp.dot(x.astype(BF16), ones1, preferred_element_type=F32)
        for i in range(SUBLANES):
            row = lambda name, g, kg: jnp.broadcast_to(
                refs[name][g, pl.ds(base, SUBLANES), kg * LANES:(kg + 1) * LANES][i:i + 1, :], (HEAD_DIM, LANES))
            sa = head_sum(jnp.concatenate(
                [sum(s_scr[g, kg] * row("kk", g, kg) for kg in range(n_kg)) for g in range(n_grp)], axis=0))
            vcol = jnp.dot(vp, sel_ref[i], preferred_element_type=F32)
            reads = []
            for g in range(n_grp):
                sa_g, vcol_g = rows_of(sa, g), rows_of(vcol, g)
                read = None
                for kg in range(n_kg):
                    s_new = (s_scr[g, kg] * row("w", g, kg) - sa_g * row("kka", g, kg)
                             + vcol_g * row("k", g, kg))
                    s_scr[g, kg] = s_new
                    term = s_new * row("r", g, kg)
                    read = term if read is None else read + term
                reads.append(read)
            o = head_sum(jnp.concatenate(reads, axis=0))
            ot_scr[tile] = jnp.where(t_lane == (base + i) % head_lanes, o, ot_scr[tile])
        return carry

    lax.fori_loop(0, tc // SUBLANES, block, 0)

    for tile in range(tc // head_lanes):
        for g in range(n_grp):
            o_t = rows_of(ot_scr[tile], g).T
            for h in range(n_heads):
                o_ref[g, tile * head_lanes:(tile + 1) * head_lanes, h * HEAD_DIM:(h + 1) * HEAD_DIM] = (
                    o_t[h * head_lanes:(h + 1) * head_lanes, :])

    @pl.when(c == pl.num_programs(1) - 1)
    def _():
        s_ref[...] = s_scr[...]


def _key_group_perm(width):
    n_heads = width // HEAD_DIM
    head_lanes = LANES // n_heads
    n = np.arange(width)
    return (n % LANES) // head_lanes * HEAD_DIM + n // LANES * head_lanes + n % head_lanes


def _to_key_group(t):
    width = t.shape[-1]
    n_heads = width // HEAD_DIM
    head_lanes = LANES // n_heads
    split = t.reshape(*t.shape[:-1], n_heads, HEAD_DIM // head_lanes, head_lanes)
    return jnp.swapaxes(split, -3, -2).reshape(t.shape)


def _from_key_group(t):
    width = t.shape[-1]
    n_heads = width // HEAD_DIM
    head_lanes = LANES // n_heads
    split = t.reshape(*t.shape[:-1], HEAD_DIM // head_lanes, n_heads, head_lanes)
    return jnp.swapaxes(split, -3, -2).reshape(t.shape)


def rwkv_scan_prompt(r, w, k, kk, kka, v):
    b, l, width = r.shape
    n_heads = width // HEAD_DIM
    head_lanes = LANES // n_heads
    n_kg = HEAD_DIM // head_lanes
    tc = _tile(l, SCAN_CHUNK)
    grp = _tile(b, SCAN_GROUP)
    kl = np.arange(LANES)
    col_head = 2 * ((kl % HEAD_DIM) // (2 * SUBLANES)) + kl // HEAD_DIM
    sel = ((kl[None, :, None] % SUBLANES == np.arange(SUBLANES)[:, None, None])
           & (col_head[None, :, None] == kl[None, None, :] // head_lanes))
    ones1 = kl[:, None] // head_lanes == kl[None, :] // head_lanes
    sel, ones1 = jnp.asarray(sel, BF16), jnp.asarray(ones1, BF16)
    seq = pl.BlockSpec((grp, tc, width), lambda i, j: (i, j, 0))
    state = pl.BlockSpec((grp, n_kg, HEAD_DIM, LANES), lambda i, j: (i, 0, 0, 0))
    o, s = pl.pallas_call(
        functools.partial(_rwkv_scan_kernel, n_heads=n_heads),
        grid=(b // grp, l // tc),
        in_specs=[seq, seq, seq, seq, seq, seq,
                  pl.BlockSpec(sel.shape, lambda i, j: (0, 0, 0)),
                  pl.BlockSpec(ones1.shape, lambda i, j: (0, 0))],
        out_specs=[seq, state],
        out_shape=[jax.ShapeDtypeStruct((b, l, width), F32),
                   jax.ShapeDtypeStruct((b, n_kg, HEAD_DIM, LANES), F32)],
        scratch_shapes=[pltpu.VMEM((grp, n_kg, HEAD_DIM, LANES), F32),
                        pltpu.VMEM((tc // head_lanes, grp * HEAD_DIM, LANES), F32)],
        compiler_params=_cparams(("parallel", "arbitrary")),
        name="rwkv_scan_prompt",
    )(r, w, k, kk, kka, v, sel, ones1)
    s = s.reshape(b, n_kg, HEAD_DIM, n_heads, head_lanes).transpose(0, 3, 2, 1, 4)
    return o, s.reshape(b, n_heads, HEAD_DIM, HEAD_DIM)


def _rwkv_step_kernel(r_ref, w_ref, k_ref, kk_ref, kka_ref, v_ref, s0_ref, o_ref, s_ref):
    r, w, k, kk, kka, v = (ref[...] for ref in (r_ref, w_ref, k_ref, kk_ref, kka_ref, v_ref))
    for vi in range(HEAD_DIM):
        s_row = s0_ref[0, vi]
        sa = -jnp.sum(s_row * kk, axis=0, keepdims=True)
        s_new = s_row * w + sa * kka + v[vi:vi + 1, :] * k
        s_ref[0, vi] = s_new
        o_ref[vi:vi + 1, :] = jnp.sum(s_new * r, axis=0, keepdims=True)


def rwkv_step(r, w, k, kk, kka, v, s0):
    b, width = r.shape
    n_heads = width // HEAD_DIM
    vec = pl.BlockSpec((HEAD_DIM, b), lambda h: (h, 0))
    sspec = pl.BlockSpec((1, HEAD_DIM, HEAD_DIM, b), lambda h: (h, 0, 0, 0))
    o, s = pl.pallas_call(
        _rwkv_step_kernel,
        grid=(n_heads,),
        in_specs=[vec] * 6 + [sspec],
        out_specs=[vec, sspec],
        out_shape=[jax.ShapeDtypeStruct((width, b), F32),
                   jax.ShapeDtypeStruct((n_heads, HEAD_DIM, HEAD_DIM, b), F32)],
        compiler_params=_cparams(("parallel",)),
        name="rwkv_step",
    )(r.T, w.T, k.T, kk.T, kka.T, v.T, _batch_last(s0))
    return o.T, _batch_first(s)


def _diff_lambda(lp, lam_init):
    e1 = jnp.exp(jnp.sum(lp[0:1] * lp[1:2], axis=-1, keepdims=True))
    e2 = jnp.exp(jnp.sum(lp[2:3] * lp[3:4], axis=-1, keepdims=True))
    return e1 - e2 + lam_init


def _diff_attn_kernel(pt_ref, q_ref, k_ref, v_ref, qs_ref, kns_ref, vns_ref, *rest, lam_init, n_heads, n_pages):
    n_cache = (len(rest) - 7) // 2
    kc_refs, vc_refs = rest[:n_cache], rest[n_cache:2 * n_cache]
    lam_ref, subln_ref, o_ref, os_ref, m_scr, l_scr, acc_scr = rest[2 * n_cache:]
    del pt_ref
    lam = _diff_lambda(lam_ref[...], lam_init)
    _prompt_attention(q_ref, k_ref, v_ref, lam, subln_ref, o_ref, m_scr, l_scr, acc_scr, lam_init)
    for r in range(n_cache // n_pages):
        _decode_attention(r, qs_ref, kns_ref, vns_ref, kc_refs[r * n_pages:(r + 1) * n_pages],
                          vc_refs[r * n_pages:(r + 1) * n_pages], lam, subln_ref, os_ref, lam_init, n_heads)


def _prompt_attention(q_ref, k_ref, v_ref, lam, subln_ref, o_ref, m_scr, l_scr, acc_scr, lam_init):
    i = pl.program_id(2)
    tq = q_ref.shape[1]
    tk = tq
    scale = HEAD_DIM ** -0.5
    m_scr[...] = jnp.full_like(m_scr, -jnp.inf)
    l_scr[...] = jnp.zeros_like(l_scr)
    acc_scr[...] = jnp.zeros_like(acc_scr)
    q = (q_ref[0] * scale).astype(BF16)

    def update(j, on_diagonal):
        rows = pl.ds(pl.multiple_of(j * tk, tk), tk)
        k, v = k_ref[0, rows, :].astype(BF16), v_ref[0, rows, :].astype(BF16)
        if on_diagonal:
            visible = (lax.broadcasted_iota(jnp.int32, (tq, tk), 1)
                       <= lax.broadcasted_iota(jnp.int32, (tq, tk), 0))
        for mi in range(2):
            sl = slice(mi * HEAD_DIM, (mi + 1) * HEAD_DIM)
            s = lax.dot_general(q[:, sl], k[:, sl], (((1,), (1,)), ((), ())), preferred_element_type=F32)
            if on_diagonal:
                s = jnp.where(visible, s, -jnp.inf)
            m_old = m_scr[mi]
            m_new = jnp.maximum(m_old, jnp.max(s, axis=-1, keepdims=True))
            alpha = jnp.exp(m_old - m_new)
            p = jnp.exp(s - jnp.concatenate([m_new] * (tk // LANES), axis=1))
            l_scr[mi] = alpha * l_scr[mi] + jnp.sum(p, axis=-1, keepdims=True)
            acc_scr[mi] = alpha * acc_scr[mi] + jnp.dot(p.astype(BF16), v, preferred_element_type=F32)
            m_scr[mi] = m_new

    def below_diagonal(j, carry):
        update(j, False)
        return carry

    lax.fori_loop(0, i, below_diagonal, 0)
    update(i, True)
    o = acc_scr[0] / l_scr[0] - lam * (acc_scr[1] / l_scr[1])
    o_ref[0] = _rms(o, subln_ref[...]) * (1.0 - lam_init)


def _decode_attention(r, q_ref, kn_ref, vn_ref, kc_refs, vc_refs, lam, subln_ref, o_ref, lam_init, n_heads):
    n_rows = 2 * n_heads
    dv = 2 * HEAD_DIM
    scale = HEAD_DIM ** -0.5
    row = lax.broadcasted_iota(jnp.int32, (n_rows, dv), 0)
    lane = lax.broadcasted_iota(jnp.int32, (n_rows, dv), 1)
    qmat = jnp.where(lane // HEAD_DIM == row % 2, q_ref[r], 0.0)
    rows_per = kc_refs[0].shape[1]
    cols = len(kc_refs) * rows_per
    own = (lax.broadcasted_iota(jnp.int32, (n_rows, cols), 1) % n_heads
           == lax.broadcasted_iota(jnp.int32, (n_rows, cols), 0) // 2)
    s = jnp.concatenate(
        [lax.dot_general(qmat.astype(BF16), kc_ref[0].astype(BF16), (((1,), (1,)), ((), ())),
                         preferred_element_type=F32) for kc_ref in kc_refs], axis=1) * scale
    s = jnp.where(own, s, -jnp.inf)
    s_new = jnp.sum(qmat * kn_ref[r], axis=-1, keepdims=True) * scale
    m = jnp.maximum(jnp.max(s, axis=-1, keepdims=True), s_new)
    pr = jnp.exp(s - m)
    p_new = jnp.exp(s_new - m)
    prb = pr.astype(BF16)
    pv = sum(jnp.dot(prb[:, i * rows_per:(i + 1) * rows_per], vc_ref[0].astype(BF16), preferred_element_type=F32)
             for i, vc_ref in enumerate(vc_refs))
    acc = (pv + p_new * vn_ref[r]) / (jnp.sum(pr, axis=-1, keepdims=True) + p_new)
    acc = acc * jnp.where(row % 2 == 0, 1.0, -lam)
    subln = subln_ref[...]
    for h in range(n_heads):
        o = acc[2 * h:2 * h + 1] + acc[2 * h + 1:2 * h + 2]
        o_ref[r, h:h + 1, :] = _rms(o, subln) * (1.0 - lam_init)


def diff_attn(q, k, v, qs, ks, vs, cache_k, cache_v, page_table, lam_p, subln, lam_init):
    b, l, width = q.shape
    bs, n_pages = page_table.shape
    dv = 2 * HEAD_DIM
    n_heads = width // dv
    t = _tile(l, ATTN_BLOCK)
    n = l // t
    n_steps = b * n_heads * n
    per_step = bs // n_steps
    assert per_step * n_steps == bs, "sample rows must divide evenly over the prompt attention grid"
    rows = cache_k.shape[1]
    step = lambda bi, h, i: (bi * n_heads + h) * n + i
    per_map = lambda x: jnp.repeat(x.reshape(bs, n_heads, dv), 2, axis=1)
    qspec = pl.BlockSpec((1, t, dv), lambda bi, h, i, pt: (bi, i, h))
    kspec = pl.BlockSpec((1, l, dv), lambda bi, h, i, pt: (bi, 0, h))
    vec = pl.BlockSpec((per_step, 2 * n_heads, dv), lambda bi, h, i, pt: (step(bi, h, i), 0, 0))
    cache = [pl.BlockSpec((1, rows, dv), functools.partial(
        lambda bi, h, i, pt, r, slot: (pt[(step(bi, h, i) * per_step + r) * n_pages + slot], 0, 0), r=r, slot=slot))
        for r in range(per_step) for slot in range(n_pages)]
    const = lambda x: pl.BlockSpec(x.shape, lambda bi, h, i, pt: (0, 0))
    o, o_s = pl.pallas_call(
        functools.partial(_diff_attn_kernel, lam_init=lam_init, n_heads=n_heads, n_pages=n_pages),
        grid_spec=pltpu.PrefetchScalarGridSpec(
            num_scalar_prefetch=1,
            grid=(b, n_heads, n),
            in_specs=[qspec, kspec, kspec, vec, vec, vec] + cache + cache + [const(lam_p), const(subln)],
            out_specs=[qspec, pl.BlockSpec((per_step, n_heads, dv), lambda bi, h, i, pt: (step(bi, h, i), 0, 0))],
            scratch_shapes=[pltpu.VMEM((2, t, LANES), F32), pltpu.VMEM((2, t, LANES), F32),
                            pltpu.VMEM((2, t, dv), F32)],
        ),
        out_shape=[jax.ShapeDtypeStruct((b, l, width), F32), jax.ShapeDtypeStruct((bs, n_heads, dv), F32)],
        compiler_params=_cparams(("parallel", "parallel", "arbitrary")),
        name="diff_attn",
    )(page_table.reshape(-1), q, k, v, per_map(qs), per_map(ks), per_map(vs),
      *([cache_k] * (per_step * n_pages)), *([cache_v] * (per_step * n_pages)), lam_p, subln)
    return o, o_s.reshape(bs, width)


def _lru_coeffs(x, x1, x2, x3, cw, cb, wa, ba, wi, bi, lam):
    xc = x3 * cw[0:1] + x2 * cw[1:2] + x1 * cw[2:3] + x * cw[3:4]
    xc = xc + cb
    r = jax.nn.sigmoid(_dot(xc, wa) + ba)
    ig = jax.nn.sigmoid(_dot(xc, wi) + bi)
    log_a = -LRU_C * r * _softplus(-lam)
    a = jnp.exp(log_a)
    return a, jnp.sqrt(-jnp.tanh(log_a) * (a * a + 1.0)) * (ig * xc)


def _lru_seq_kernel(x_ref, gr_ref, buf_ref, h0_ref, cw_ref, cb_ref, wa_ref, ba_ref, wi_ref, bi_ref, lam_ref,
                    o_ref, hl_ref, carry, h_scr, a_scr, b_scr):
    n_grp, tl = x_ref.shape[0], x_ref.shape[1]
    n_carry = carry.shape[1]

    @pl.when(pl.program_id(1) == 0)
    def _():
        for g in range(n_grp):
            for d in range(n_carry):
                carry[g, d] = buf_ref[g, d:d + 1, :]
        h_scr[...] = h0_ref[...]

    for g in range(n_grp):
        x = x_ref[g]
        x1 = _shifted(x, carry[g, n_carry - 1])
        x2 = _shifted(x1, carry[g, n_carry - 2])
        x3 = _shifted(x2, carry[g, n_carry - 3])
        for d in range(n_carry):
            carry[g, d] = x[tl - n_carry + d:tl - n_carry + d + 1, :]
        a_scr[g], b_scr[g] = _lru_coeffs(x, x1, x2, x3, cw_ref[...], cb_ref[...], wa_ref[...], ba_ref[...],
                                         wi_ref[...], bi_ref[...], lam_ref[...])
    row_id = lax.broadcasted_iota(jnp.int32, (SUBLANES, x_ref.shape[2]), 0)

    def block(tb, hs):
        base = pl.multiple_of(tb * SUBLANES, SUBLANES)
        rows = pl.ds(base, SUBLANES)
        a8 = [a_scr[g, rows, :] for g in range(n_grp)]
        b8 = [b_scr[g, rows, :] for g in range(n_grp)]
        hs = list(hs)
        out = [jnp.zeros_like(a8[0]) for _ in range(n_grp)]
        for i in range(SUBLANES):
            for g in range(n_grp):
                hs[g] = a8[g][i:i + 1, :] * hs[g] + b8[g][i:i + 1, :]
                out[g] = jnp.where(row_id == i, hs[g], out[g])
        for g in range(n_grp):
            o_ref[g, rows, :] = out[g] * jax.nn.gelu(gr_ref[g, rows, :])
        return tuple(hs)

    hs = lax.fori_loop(0, tl // SUBLANES, block, tuple(h_scr[g] for g in range(n_grp)))
    for g in range(n_grp):
        h_scr[g] = hs[g]
        hl_ref[g] = hs[g]


def lru_prompt(x, gr, buf, h0, prm):
    bsz, l, w = x.shape
    tl = _tile(l, 512)
    grp = _tile(bsz, LRU_GROUP)
    seq = pl.BlockSpec((grp, tl, w), lambda i, j: (i, j, 0))
    vec = pl.BlockSpec((grp, 1, w), lambda i, j: (i, 0, 0))
    full = lambda a: pl.BlockSpec(a.shape, lambda i, j: (0,) * a.ndim)
    params = [prm[n] for n in ("conv_w", "conv_b", "wa", "ba", "wi", "bi", "lam")]
    nb = buf.shape[1]
    o, hl = pl.pallas_call(
        _lru_seq_kernel,
        grid=(bsz // grp, l // tl),
        in_specs=[seq, seq, pl.BlockSpec((grp, nb, w), lambda i, j: (i, 0, 0)), vec] + [full(a) for a in params],
        out_specs=[seq, vec],
        out_shape=[jax.ShapeDtypeStruct((bsz, l, w), F32), jax.ShapeDtypeStruct((bsz, 1, w), F32)],
        scratch_shapes=[pltpu.VMEM((grp, nb, 1, w), F32), pltpu.VMEM((grp, 1, w), F32),
                        pltpu.VMEM((grp, tl, w), F32), pltpu.VMEM((grp, tl, w), F32)],
        compiler_params=_cparams(("parallel", "arbitrary")),
        name="lru_prompt",
    )(x, gr, buf, h0.reshape(bsz, 1, w), *params)
    return o, hl.reshape(bsz, w)


def _lru_step_kernel(x_ref, x1_ref, x2_ref, x3_ref, gr_ref, h0_ref, cw_ref, cb_ref, wa_ref, ba_ref, wi_ref,
                     bi_ref, lam_ref, o_ref, h_ref):
    a, b = _lru_coeffs(x_ref[...], x1_ref[...], x2_ref[...], x3_ref[...], cw_ref[...], cb_ref[...], wa_ref[...],
                       ba_ref[...], wi_ref[...], bi_ref[...], lam_ref[...])
    h = a * h0_ref[...] + b
    h_ref[...] = h
    o_ref[...] = h * jax.nn.gelu(gr_ref[...])


def lru_step(x, buf, gr, h0, prm):
    m, w = x.shape
    tm = _tile(m, 512)
    rows = pl.BlockSpec((tm, w), lambda i: (i, 0))
    full = lambda a: pl.BlockSpec(a.shape, lambda i: (0,) * a.ndim)
    params = [prm[n] for n in ("conv_w", "conv_b", "wa", "ba", "wi", "bi", "lam")]
    nb = buf.shape[1]
    return pl.pallas_call(
        _lru_step_kernel,
        grid=(m // tm,),
        in_specs=[rows] * 6 + [full(a) for a in params],
        out_specs=[rows] * 2,
        out_shape=[jax.ShapeDtypeStruct((m, w), F32)] * 2,
        compiler_params=_cparams(("parallel",)),
        name="lru_step",
    )(x, buf[:, nb - 1], buf[:, nb - 2], buf[:, nb - 3], gr, h0, *params)


def _block_diag(w):
    n, d, e = w.shape
    eye = jnp.eye(n, dtype=w.dtype)
    return (eye[:, None, :, None] * w[:, :, None, :]).reshape(n * d, n * e)


def _mix_even(h, g_norm, pos0, s_ret, s_rwkv, buf, wts, is_prompt):
    b, l, d = h.shape
    m = b * l
    gw = d // 2
    seq = lambda t: t.reshape(b, l, gw)
    flat = lambda t: t.reshape(m, gw)
    to_kg, to_nat = _to_key_group, _from_key_group
    groups = lambda t: [t[..., i * gw:(i + 1) * gw] for i in range(4)]
    br, bk, bv, bz = groups(buf)
    buf_kg = [to_kg(br), to_kg(bk), bv, bz]
    w_in = wts["ab_w_in"]
    if is_prompt:
        qa, ka, va, ga = norm_matmul(h.reshape(m, d), g_norm, w_in[:, :4 * gw], gw)
        (r, w, k, v, kk, kka, g, bonus), last = rwkv_project_prep(
            h, g_norm, w_in[:, 4 * gw:], jnp.concatenate(buf_kg, axis=-1), wts["rwkv"])
        last = groups(last)
        pos = pos0 + jnp.arange(l, dtype=F32)
        o_a, s_ret_new = retention_prompt(seq(qa), seq(ka), seq(va), seq(ga), pos, wts["ones_bd"])
        o_a = flat(o_a)
        o_b, s_rwkv_new = rwkv_scan_prompt(r, w, k, kk, kka, v)
    else:
        u = norm_matmul(h.reshape(m, d), g_norm, w_in, gw)
        qa, ka, va, ga = u[:4]
        last = [seq(t) for t in u[4:8]]
        rows = lambda t: t.reshape(1, m, gw)
        r, w, k, v, kk, kka, g, bonus = rwkv_prep([rows(t) for t in last], [rows(t) for t in buf_kg], wts["rwkv"])
        o_a, s_ret_new = retention_step(qa, ka, va, ga, s_ret, jnp.float32(pos0))
        o_b, s_rwkv_new = rwkv_step(*(flat(to_nat(t)) for t in (r, w, k, kk, kka)), flat(v), s_rwkv)
    buf_new = jnp.concatenate([to_nat(last[0]), to_nat(last[1]), last[2], last[3]], axis=-1)
    post = (flat(bonus), flat(g), wts["rwkv_ln"], wts["ones_bd"])
    return o_a, flat(o_b), post, s_ret_new, s_rwkv_new, buf_new


def _mix_odd(h, g_norm, lru_h, lru_buf, wts, is_prompt):
    b, l, d = h.shape
    m = b * l
    gw = d // 2
    u = norm_matmul(h.reshape(m, d), g_norm, wts["cd_w_in"], gw, head_major=(1, 2))
    seq = lambda t: t.reshape(b, l, gw)
    xr = seq(u[3])
    if is_prompt:
        o_c = yield seq(u[0]), seq(u[1]), seq(u[2])
        o_c = o_c.reshape(m, gw)
        o_d, h_last = lru_prompt(xr, seq(u[4]), lru_buf, lru_h, wts["lru"])
    else:
        o_c = yield u[0], u[1], u[2]
        o_d, h_last = lru_step(u[3], lru_buf, u[4], lru_h, wts["lru"])
    buf_new = jnp.concatenate([lru_buf, xr], axis=1)[:, l:]
    n_heads = gw // (2 * HEAD_DIM)
    k_new = u[5].reshape(b, l, n_heads, 2 * HEAD_DIM)
    v_new = u[6].reshape(b, l, n_heads, 2 * HEAD_DIM)
    return o_c, o_d.reshape(m, gw), k_new, v_new, h_last, buf_new


def _advance(gen, value):
    try:
        return gen.send(value), None
    except StopIteration as done:
        return None, done.value


def _trunk(x, p, pos0, s_ret, s_rwkv, s_shift, s_lru_h, s_lru_conv, wts, is_prompt):
    b, l, d = x.shape
    m = b * l
    depth = wts["norm_g"].shape[0]
    h = x.reshape(m, d)
    ret_l, rwkv_l, shift_l, k_l, v_l, lh_l, lc_l = [], [], [], [], [], [], []
    for i in range(depth):
        j = i // 2
        g = wts["norm_g"][i]
        gn = lambda n: g[n:n + 1]
        h = ffn_block(h, gn(0), wts["ffn_in"], wts["ffn_out"], gn(1), i, 0)
        post = None
        if i % 2 == 0:
            o1, o2, post, sr, sw, sb = _mix_even(h.reshape(b, l, d), gn(2), pos0, s_ret[j], s_rwkv[j], s_shift[j],
                                                 wts["even"][j], is_prompt)
            ret_l.append(sr)
            rwkv_l.append(sw)
            shift_l.append(sb)
            w_out = wts["even"][j]["w_out"]
        else:
            o1, o2, kn, vn, lh, lc = yield from _mix_odd(h.reshape(b, l, d), gn(2), s_lru_h[j], s_lru_conv[j],
                                                         wts["odd"][j], is_prompt)
            k_l.append(kn)
            v_l.append(vn)
            lh_l.append(lh)
            lc_l.append(lc)
            w_out = wts["odd"][j]["w_out"]
        h = out_proj(o1, o2, w_out, h, gn(3), post)
        h = ffn_block(h, gn(4), wts["ffn_in"], wts["ffn_out"], gn(5), i, 1,
                      ple=(gn(6), wts["ple_gate"], p.reshape(depth, m, -1), wts["ple"], gn(7)))
    st = lambda lst: jnp.stack(lst, axis=0)
    return (h.reshape(b, l, d), st(k_l), st(v_l), st(ret_l), st(rwkv_l), st(shift_l), st(lh_l), st(lc_l))


def kernel(x_prompt, x_sample, cache_k, cache_v, state_ret, state_rwkv, state_rwkv_shift, state_lru_h, state_lru_conv, page_table, p_prompt, p_sample, norm_g, ffn_w_in, ffn_w_out, ple_w, ple_gate_w, ab_w_in, ab_w_out, rwkv_mu, rwkv_w0, rwkv_w1, rwkv_w2, rwkv_a0, rwkv_a1, rwkv_a2, rwkv_g1, rwkv_g2, rwkv_kk, rwkv_ka, rwkv_rk, rwkv_ln, cd_w_in, cd_w_out, diff_lam, diff_subln, lru_conv_w, lru_conv_b, lru_wa, lru_ba, lru_wi, lru_bi, lru_lambda):
    depth = norm_g.shape[0]
    n_a, n_c = state_ret.shape[0], state_lru_h.shape[0]
    bp = x_prompt.shape[0]
    gw = ab_w_out.shape[1] // 2
    bf = lambda t: t.astype(BF16)
    row = lambda t: t.reshape(1, -1)
    ones_bd = _block_diag(jnp.ones((gw // HEAD_DIM, HEAD_DIM, HEAD_DIM), BF16))
    perm = _key_group_perm(gw)
    kg = _to_key_group
    head_kg = perm // HEAD_DIM
    tile_head = np.tile(head_kg[:LANES], 2)
    ones_kk = jnp.asarray(tile_head[:, None] == head_kg[None, :LANES], BF16)
    ones_rk = jnp.asarray(tile_head[:, None] == (np.arange(gw) // HEAD_DIM)[None, :], BF16)

    def ab_in_kg(w):
        cols = [w[:, g * gw:(g + 1) * gw] for g in range(w.shape[1] // gw)]
        cols[4], cols[5] = kg(cols[4]), kg(cols[5])
        return bf(jnp.concatenate(cols, axis=1))

    wts = {
        "norm_g": norm_g,
        "ffn_in": bf(ffn_w_in), "ffn_out": bf(ffn_w_out), "ple": bf(ple_w), "ple_gate": bf(ple_gate_w),
        "even": [{
            "ab_w_in": ab_in_kg(ab_w_in[j]), "w_out": bf(ab_w_out[j]), "rwkv_ln": rwkv_ln[j],
            "ones_bd": ones_bd,
            "rwkv": {"mu": jnp.concatenate([kg(rwkv_mu[j][:2]), rwkv_mu[j][2:]], axis=0),
                     "w0": kg(row(rwkv_w0[j])), "w1": bf(rwkv_w1[j]), "w2": bf(kg(rwkv_w2[j])),
                     "a0": kg(row(rwkv_a0[j])), "a1": bf(rwkv_a1[j]), "a2": bf(kg(rwkv_a2[j])),
                     "g1": bf(rwkv_g1[j]), "g2": bf(rwkv_g2[j]), "kk": kg(row(rwkv_kk[j])),
                     "ka": kg(row(rwkv_ka[j])), "rk": kg(row(rwkv_rk[j])),
                     "ones_kk": ones_kk, "ones_rk": ones_rk},
        } for j in range(n_a)],
        "odd": [{
            "cd_w_in": bf(cd_w_in[j]), "w_out": bf(cd_w_out[j]), "diff_lam": diff_lam[j],
            "diff_subln": row(diff_subln[j]),
            "lru": {"conv_w": lru_conv_w[j], "conv_b": row(lru_conv_b[j]), "wa": bf(_block_diag(lru_wa[j])),
                    "ba": row(lru_ba[j]), "wi": bf(_block_diag(lru_wi[j])), "bi": row(lru_bi[j]),
                    "lam": row(lru_lambda[j])},
        } for j in range(n_c)],
    }
    zeros = lambda *shape: jnp.zeros(shape, F32)
    past_len = page_table.shape[1] * cache_k.shape[2]
    n_pool, page = cache_k.shape[1], cache_k.shape[2]
    as_rows = lambda c: c.reshape(n_c, n_pool, page * c.shape[3], c.shape[4])
    pages_k, pages_v = as_rows(cache_k), as_rows(cache_v)
    prompt = _trunk(x_prompt, p_prompt, 0.0, [None] * n_a, [None] * n_a,
                    zeros(n_a, bp, 1, 4 * gw), zeros(n_c, bp, gw), zeros(n_c, bp, CONV_W - 1, gw), wts, True)
    sample = _trunk(x_sample, p_sample, float(past_len), state_ret, state_rwkv, state_rwkv_shift,
                    state_lru_h, state_lru_conv, wts, False)
    (qkv_p, out_p), (qkv_s, out_s) = _advance(prompt, None), _advance(sample, None)
    j = 0
    while out_p is None:
        odd = wts["odd"][j]
        lam_init = 0.8 - 0.6 * math.exp(-0.3 * (2 * j + 1))
        o_p, o_s = diff_attn(*qkv_p, *qkv_s, pages_k[j], pages_v[j], page_table, odd["diff_lam"],
                             odd["diff_subln"], lam_init)
        (qkv_p, out_p), (qkv_s, out_s) = _advance(prompt, o_p), _advance(sample, o_s)
        j += 1
    yp, kp, vp, rp, wp, sp, hp, cp = out_p
    ys, ks_, vs, rs, ws, ss, hs, cs = out_s
    return (yp, ys, kp, vp, rp, wp, sp, hp, cp, ks_, vs, rs, ws, ss, hs, cs)
```

```python
import functools
import math

import jax
import jax.numpy as jnp
import numpy as np
from jax import lax
from jax.experimental import pallas as pl
from jax.experimental.pallas import tpu as pltpu

F32 = jnp.float32
BF16 = jnp.bfloat16

HEAD_DIM = 64
CONV_W = 4
LRU_C = 8.0
ROPE_BASE = 10000.0
EPS = 1e-6
RWKV_GN_EPS = 64e-5
RET_CHUNK = 256
ATTN_BLOCK = 512
SCAN_CHUNK = 128
LRU_GROUP = 2
SCAN_GROUP = 8
LANES = 128
SUBLANES = 8
VMEM_LIMIT = 48 * 1024 * 1024


def _cparams(sem):
    return pltpu.CompilerParams(dimension_semantics=sem, vmem_limit_bytes=VMEM_LIMIT)


def _tile(n, pref):
    t = min(n, pref)
    while n % t:
        t //= 2
    return t


def _rms(x, g):
    return x * lax.rsqrt(jnp.mean(x * x, axis=-1, keepdims=True) + EPS) * g


def _dot(a, b):
    return jnp.dot(a.astype(BF16), b.astype(BF16), preferred_element_type=F32)


def _hi_lo(x):
    hi = x.astype(BF16)
    lo = (x - hi.astype(F32)).astype(BF16)
    return jnp.concatenate([hi, lo], axis=1)


def _seg_sum(x, ones_pair):
    return jnp.concatenate([jnp.dot(_hi_lo(x[:, t * LANES:(t + 1) * LANES]), ones_pair, preferred_element_type=F32)
                            for t in range(x.shape[1] // LANES)], axis=1)


def _ffn_kernel(h_ref, gpre_ref, wg_ref, wu_ref, wo_ref, gpost_ref, *rest):
    o_ref, xn_ref, acc_ref = rest[-3:]
    ple = rest[:-3]
    j = pl.program_id(1)

    @pl.when(j == 0)
    def _():
        xn_ref[...] = _rms(h_ref[...], gpre_ref[...]).astype(BF16)
        acc_ref[...] = jnp.zeros_like(acc_ref)

    xn = xn_ref[...]
    gate = jnp.dot(xn, wg_ref[...], preferred_element_type=F32)
    up = jnp.dot(xn, wu_ref[...], preferred_element_type=F32)
    act = (gate * jax.nn.sigmoid(gate) * up).astype(BF16)
    acc_ref[...] += jnp.dot(act, wo_ref[...], preferred_element_type=F32)

    @pl.when(j == pl.num_programs(1) - 1)
    def _():
        h = h_ref[...] + 0.5 * _rms(acc_ref[...], gpost_ref[...])
        if ple:
            g6_ref, wgate_ref, p_ref, wp_ref, g7_ref = ple
            gate_p = jax.nn.sigmoid(_dot(_rms(h, g6_ref[...]), wgate_ref[...]))
            h = h + _rms(gate_p * _dot(p_ref[...], wp_ref[...]), g7_ref[...])
        o_ref[...] = h


def ffn_block(h, g_pre, w_in, w_out, g_post, layer, half, ple=None):
    m, d = h.shape
    f = w_out.shape[2]
    tm = _tile(m, 1024)
    tf = _tile(f, 512 if tm >= 512 else 2048)
    nf = f // tf
    vec = pl.BlockSpec((1, d), lambda i, j: (0, 0))
    extra, extra_specs = [], []
    if ple is not None:
        g6, gate_w, p, emb_w, g7 = ple
        pd = p.shape[2]
        extra = [g6, gate_w, p, emb_w, g7]
        extra_specs = [vec, pl.BlockSpec((None, d, d), lambda i, j: (layer, 0, 0)),
                       pl.BlockSpec((None, tm, pd), lambda i, j: (layer, i, 0)),
                       pl.BlockSpec((None, pd, d), lambda i, j: (layer, 0, 0)), vec]
    return pl.pallas_call(
        _ffn_kernel,
        grid=(m // tm, nf),
        in_specs=[
            pl.BlockSpec((tm, d), lambda i, j: (i, 0)),
            vec,
            pl.BlockSpec((None, None, d, tf), lambda i, j: (layer, half, 0, j)),
            pl.BlockSpec((None, None, d, tf), lambda i, j: (layer, half, 0, j + nf)),
            pl.BlockSpec((None, None, tf, d), lambda i, j: (layer, half, j, 0)),
            vec,
        ] + extra_specs,
        out_specs=pl.BlockSpec((tm, d), lambda i, j: (i, 0)),
        out_shape=jax.ShapeDtypeStruct((m, d), F32),
        scratch_shapes=[pltpu.VMEM((tm, d), BF16), pltpu.VMEM((tm, d), F32)],
        compiler_params=_cparams(("parallel", "arbitrary")),
        name="ffn_block",
    )(h, g_pre, w_in, w_in, w_out, g_post, *extra)


def _norm_matmul_kernel(h_ref, g_ref, w_ref, *o_refs, n_groups, head_major):
    xn = _rms(h_ref[...], g_ref[...]).astype(BF16)
    tm, tn = o_refs[0].shape
    heads = tn // LANES
    for gi in range(n_groups):
        res = jnp.dot(xn, w_ref[:, gi * tn:(gi + 1) * tn], preferred_element_type=F32)
        o_refs[gi][...] = res
        if gi in head_major:
            hm_ref = o_refs[n_groups + head_major.index(gi)]
            for hh in range(heads):
                hm_ref[pl.ds(hh, tm, stride=heads), :] = res[:, hh * LANES:(hh + 1) * LANES]


def norm_matmul(h, g, w, tn, head_major=()):
    m, d = h.shape
    n = w.shape[1]
    tm = _tile(m, 512)
    heads = tn // LANES
    rows = pl.BlockSpec((tm, tn), lambda i: (i, 0))
    return pl.pallas_call(
        functools.partial(_norm_matmul_kernel, n_groups=n // tn, head_major=tuple(head_major)),
        grid=(m // tm,),
        in_specs=[
            pl.BlockSpec((tm, d), lambda i: (i, 0)),
            pl.BlockSpec((1, d), lambda i: (0, 0)),
            pl.BlockSpec((d, n), lambda i: (0, 0)),
        ],
        out_specs=[rows] * (n // tn) + [pl.BlockSpec((tm * heads, LANES), lambda i: (i, 0))] * len(head_major),
        out_shape=([jax.ShapeDtypeStruct((m, tn), F32)] * (n // tn)
                   + [jax.ShapeDtypeStruct((m * heads, LANES), F32)] * len(head_major)),
        compiler_params=_cparams(("parallel",)),
        name="norm_matmul",
    )(h, g, w)


def _out_proj_kernel(oa_ref, ob_ref, wa_ref, wb_ref, h_ref, g_ref, *rest):
    ob = ob_ref[...]
    if len(rest) > 1:
        bonus_ref, gate_ref, ln_ref, ones_ref = rest[:4]
        ones_bd = ones_ref[...]
        inv = 1.0 / HEAD_DIM
        oc = ob - _seg_sum(ob, ones_bd) * inv
        on = oc * lax.rsqrt(_seg_sum(oc * oc, ones_bd) * inv + RWKV_GN_EPS)
        ln = ln_ref[...]
        ob = (on * ln[0:1] + ln[1:2] + bonus_ref[...]) * gate_ref[...]
    o_ref = rest[-1]
    y = _dot(oa_ref[...], wa_ref[...]) + _dot(ob, wb_ref[...])
    o_ref[...] = h_ref[...] + _rms(y, g_ref[...])


def out_proj(oa, ob, w, h, g, rwkv_post=None):
    m, d = h.shape
    gw = oa.shape[1]
    tm = _tile(m, 512)
    rows = pl.BlockSpec((tm, gw), lambda i: (i, 0))
    extra, extra_specs = [], []
    if rwkv_post is not None:
        bonus, gate, ln, ones_bd = rwkv_post
        extra = [bonus, gate, ln, ones_bd]
        extra_specs = [rows, rows, pl.BlockSpec(ln.shape, lambda i: (0, 0)),
                       pl.BlockSpec(ones_bd.shape, lambda i: (0, 0))]
    return pl.pallas_call(
        _out_proj_kernel,
        grid=(m // tm,),
        in_specs=[
            rows, rows,
            pl.BlockSpec((gw, d), lambda i: (0, 0)),
            pl.BlockSpec((gw, d), lambda i: (1, 0)),
            pl.BlockSpec((tm, d), lambda i: (i, 0)),
            pl.BlockSpec((1, d), lambda i: (0, 0)),
        ] + extra_specs,
        out_specs=pl.BlockSpec((tm, d), lambda i: (i, 0)),
        out_shape=jax.ShapeDtypeStruct((m, d), F32),
        compiler_params=_cparams(("parallel",)),
        name="out_proj",
    )(oa, ob, w, w, h, g, *extra)


def _retention_tables(n_heads, c, pos):
    lg = jnp.log1p(-jnp.exp2(-5.0 - jnp.arange(n_heads, dtype=F32)))
    idx = jnp.arange(c, dtype=F32)
    rel = idx[:, None] - idx[None, :]
    dmask = jnp.where(rel[None] >= 0, jnp.exp(jnp.maximum(rel, 0.0)[None] * lg[:, None, None]), 0.0)
    rep = lambda t: jnp.repeat(t, HEAD_DIM, axis=-1)
    q_dec = rep(jnp.exp((idx[:, None] + 1.0) * lg[None, :]))
    k_dec = rep(jnp.exp((c - 1.0 - idx[:, None]) * lg[None, :]))
    c_dec = rep(jnp.exp(c * lg)[None, :])
    half = HEAD_DIM // 2
    freq = 1.0 / (ROPE_BASE ** jnp.linspace(0.0, 1.0, half, dtype=F32))
    ang = pos[:, None] * freq[None, :]
    cos, sin = jnp.cos(ang), jnp.sin(ang)
    cos_t = jnp.tile(jnp.concatenate([cos, cos], axis=-1), (1, n_heads))
    sin_t = jnp.tile(jnp.concatenate([-sin, sin], axis=-1), (1, n_heads))
    return dmask, q_dec, k_dec, c_dec, cos_t, sin_t


def _retention_kernel(q_ref, k_ref, v_ref, g_ref, cos_ref, sin_ref, dmask_ref, qdec_ref, kdec_ref,
                      cdec_ref, ones_ref, o_ref, s_ref, s_scr, *, n_heads):
    c = pl.program_id(1)

    @pl.when(c == 0)
    def _():
        s_scr[...] = jnp.zeros_like(s_scr)

    q, k, v, g = q_ref[0], k_ref[0], v_ref[0], g_ref[0]
    cos, sin = cos_ref[...], sin_ref[...]
    rows, width = q.shape
    lane = lax.broadcasted_iota(jnp.int32, q.shape, 1)
    first_half = (lane % HEAD_DIM) < (HEAD_DIM // 2)

    def rot(x):
        swapped = jnp.where(first_half, pltpu.roll(x, width - HEAD_DIM // 2, 1),
                            pltpu.roll(x, HEAD_DIM // 2, 1))
        return x * cos + swapped * sin

    qr = rot(q)
    kr = rot(k) * (HEAD_DIM ** -0.5)
    kd = kr * kdec_ref[...]
    qdec = qdec_ref[...]
    cdec = cdec_ref[...]
    head_a = lax.broadcasted_iota(jnp.int32, (rows, LANES), 1) < HEAD_DIM
    same_head = (lax.broadcasted_iota(jnp.int32, (LANES, LANES), 0) // HEAD_DIM
                 == lax.broadcasted_iota(jnp.int32, (LANES, LANES), 1) // HEAD_DIM)
    nt = (((1,), (1,)), ((), ()))
    outs = []
    for p in range(n_heads // 2):
        sl = slice(p * LANES, (p + 1) * LANES)
        qp = qr[:, sl].astype(BF16)
        kp = kr[:, sl].astype(BF16)
        vp = v[:, sl].astype(BF16)
        zero = jnp.zeros_like(qp)
        att_a = lax.dot_general(jnp.where(head_a, qp, zero), kp, nt, preferred_element_type=F32) * dmask_ref[2 * p]
        att_b = lax.dot_general(jnp.where(head_a, zero, qp), kp, nt,
                                preferred_element_type=F32) * dmask_ref[2 * p + 1]
        s_old = s_scr[p]
        inner = jnp.where(head_a, jnp.dot(att_a.astype(BF16), vp, preferred_element_type=F32),
                          jnp.dot(att_b.astype(BF16), vp, preferred_element_type=F32))
        outs.append(inner + jnp.dot(qp, s_old.astype(BF16), preferred_element_type=F32) * qdec[:, sl])
        update = lax.dot_general(kd[:, sl].astype(BF16), vp, (((0,), (0,)), ((), ())),
                                 preferred_element_type=F32)
        s_scr[p] = jnp.where(same_head, s_old * cdec[:, sl] + update, 0.0)
    o = jnp.concatenate(outs, axis=1)
    ones_bd = ones_ref[...]
    inv = 1.0 / HEAD_DIM
    oc = o - _seg_sum(o, ones_bd) * inv
    on = oc * lax.rsqrt(_seg_sum(oc * oc, ones_bd) * inv + EPS)
    o_ref[0] = on * (g * jax.nn.sigmoid(g))

    @pl.when(c == pl.num_programs(1) - 1)
    def _():
        for p in range(n_heads // 2):
            s_pair = s_scr[p]
            s_ref[0, 2 * p] = s_pair[:HEAD_DIM, :HEAD_DIM]
            s_ref[0, 2 * p + 1] = s_pair[HEAD_DIM:, HEAD_DIM:]


def retention_prompt(q, k, v, g, pos, ones_bd):
    b, l, width = q.shape
    n_heads = width // HEAD_DIM
    c = _tile(l, RET_CHUNK)
    dmask, q_dec, k_dec, c_dec, cos_t, sin_t = _retention_tables(n_heads, c, pos)
    seq = pl.BlockSpec((1, c, width), lambda i, j: (i, j, 0))
    tab = pl.BlockSpec((c, width), lambda i, j: (j, 0))
    fixed = lambda shape: pl.BlockSpec(shape, lambda i, j: (0,) * len(shape))
    return pl.pallas_call(
        functools.partial(_retention_kernel, n_heads=n_heads),
        grid=(b, l // c),
        in_specs=[seq, seq, seq, seq, tab, tab, fixed((n_heads, c, c)), fixed((c, width)),
                  fixed((c, width)), fixed((1, width)), fixed(ones_bd.shape)],
        out_specs=[seq, pl.BlockSpec((1, n_heads, HEAD_DIM, HEAD_DIM), lambda i, j: (i, 0, 0, 0))],
        out_shape=[jax.ShapeDtypeStruct((b, l, width), F32),
                   jax.ShapeDtypeStruct((b, n_heads, HEAD_DIM, HEAD_DIM), F32)],
        scratch_shapes=[pltpu.VMEM((n_heads // 2, LANES, LANES), F32)],
        compiler_params=_cparams(("parallel", "arbitrary")),
        name="retention_prompt",
    )(q, k, v, g, cos_t, sin_t, dmask, q_dec, k_dec, c_dec, ones_bd)


def _retention_step_kernel(q_ref, k_ref, v_ref, g_ref, cos_ref, sin_ref, gam_ref, s0_ref, o_ref, s_ref):
    half = HEAD_DIM // 2
    cos, sin = cos_ref[...], sin_ref[...]
    rot = lambda x: x * cos + jnp.concatenate([x[half:], x[:half]], axis=0) * sin
    q = rot(q_ref[...])
    k = rot(k_ref[...]) * (HEAD_DIM ** -0.5)
    v, g = v_ref[...], g_ref[...]
    gam = gam_ref[0]
    att = jnp.sum(q * k, axis=0, keepdims=True)
    cross = jnp.zeros_like(v)
    for ki in range(HEAD_DIM):
        s_row = s0_ref[0, ki]
        cross = cross + q[ki:ki + 1, :] * s_row
        s_ref[0, ki] = s_row * gam + k[ki:ki + 1, :] * v
    o = att * v + cross * gam
    oc = o - jnp.mean(o, axis=0, keepdims=True)
    on = oc * lax.rsqrt(jnp.mean(oc * oc, axis=0, keepdims=True) + EPS)
    o_ref[...] = on * (g * jax.nn.sigmoid(g))


def _batch_last(s):
    return jnp.transpose(s, (1, 2, 3, 0))


def _batch_first(s):
    return jnp.transpose(s, (3, 0, 1, 2))


def retention_step(q, k, v, g, s0, pos):
    b, width = q.shape
    n_heads = width // HEAD_DIM
    half = HEAD_DIM // 2
    freq = 1.0 / (ROPE_BASE ** jnp.linspace(0.0, 1.0, half, dtype=F32))
    ang = pos * freq
    lanes = lambda t: jnp.broadcast_to(t[:, None], (HEAD_DIM, b))
    cos_c = lanes(jnp.concatenate([jnp.cos(ang), jnp.cos(ang)]))
    sin_c = lanes(jnp.concatenate([-jnp.sin(ang), jnp.sin(ang)]))
    gam = jnp.exp(jnp.log1p(-jnp.exp2(-5.0 - jnp.arange(n_heads, dtype=F32))))
    gam = jnp.broadcast_to(gam[:, None, None], (n_heads, 1, b))
    vec = pl.BlockSpec((HEAD_DIM, b), lambda h: (h, 0))
    table = pl.BlockSpec((HEAD_DIM, b), lambda h: (0, 0))
    sspec = pl.BlockSpec((1, HEAD_DIM, HEAD_DIM, b), lambda h: (h, 0, 0, 0))
    o, s = pl.pallas_call(
        _retention_step_kernel,
        grid=(n_heads,),
        in_specs=[vec, vec, vec, vec, table, table, pl.BlockSpec((1, 1, b), lambda h: (h, 0, 0)), sspec],
        out_specs=[vec, sspec],
        out_shape=[jax.ShapeDtypeStruct((width, b), F32),
                   jax.ShapeDtypeStruct((n_heads, HEAD_DIM, HEAD_DIM, b), F32)],
        compiler_params=_cparams(("parallel",)),
        name="retention_step",
    )(q.T, k.T, v.T, g.T, cos_c, sin_c, gam, _batch_last(s0))
    return o.T, _batch_first(s)


def _softplus(x):
    return jnp.maximum(x, 0.0) + jnp.log1p(jnp.exp(-jnp.abs(x)))


def _shifted(cur, carry_row):
    if cur.shape[0] == 1:
        return carry_row
    first = lax.broadcasted_iota(jnp.int32, cur.shape, 0) == 0
    return jnp.where(first, carry_row, pltpu.roll(cur, 1, 0))


PREP_PARAMS = ("mu", "w0", "w1", "w2", "a0", "a1", "a2", "g1", "g2", "kk", "ka", "rk", "ones_kk", "ones_rk")
N_PREP_PARAMS = len(PREP_PARAMS)
N_PREP_OUT = 8


def _rwkv_prep_kernel(*refs, project):
    n_in = 4 if project else 8
    (mu_ref, w0_ref, w1_ref, w2_ref, a0_ref, a1_ref, a2_ref, g1_ref, g2_ref, kkp_ref, kap_ref, rk_ref,
     ones_kk_ref, ones_rk_ref) = refs[n_in:n_in + N_PREP_PARAMS]
    n_out = n_in + N_PREP_PARAMS
    r_out, w_out, k_out, v_out, kk_out, kka_out, g_out, bonus_out = refs[n_out:n_out + N_PREP_OUT]
    if project:
        h_ref, gnorm_ref, wproj_ref, buf_ref = refs[:4]
        last_ref, carry = refs[n_out + N_PREP_OUT:]
        width = wproj_ref.shape[1] // 4
        xn = _rms(h_ref[0], gnorm_ref[...]).astype(BF16)
        cur = [jnp.dot(xn, wproj_ref[:, gi * width:(gi + 1) * width], preferred_element_type=F32)
               for gi in range(4)]

        @pl.when(pl.program_id(1) == 0)
        def _():
            for gi in range(4):
                carry[gi] = buf_ref[0, :, gi * width:(gi + 1) * width]

        prev = [_shifted(x, carry[gi]) for gi, x in enumerate(cur)]
        for gi, x in enumerate(cur):
            carry[gi] = x[x.shape[0] - 1:, :]
            last_ref[0, :, gi * width:(gi + 1) * width] = x[x.shape[0] - 1:, :]
    else:
        cur = [ref[0] for ref in refs[:4]]
        prev = [ref[0] for ref in refs[4:8]]
    mu = mu_ref[...]
    lerp = lambda x, xp, i: x + (xp - x) * mu[i:i + 1]
    zr, pz = cur[3], prev[3]
    r = lerp(cur[0], prev[0], 0)
    kx = lerp(cur[1], prev[1], 1)
    vx = lerp(cur[2], prev[2], 2)
    zw, za, zg = lerp(zr, pz, 3), lerp(zr, pz, 4), lerp(zr, pz, 5)
    wpre = w0_ref[...] + _dot(jnp.tanh(_dot(zw, w1_ref[...])), w2_ref[...])
    decay = jnp.exp(-jnp.exp(-_softplus(-wpre) - 0.5))
    a = jax.nn.sigmoid(a0_ref[...] + _dot(_dot(za, a1_ref[...]), a2_ref[...]))
    g = _dot(jax.nn.sigmoid(_dot(zg, g1_ref[...])), g2_ref[...])
    kk = kx * kkp_ref[...]
    n_tiles = kk.shape[1] // LANES

    def head_sum(x, ones_ref):
        tile_sum = sum(x[:, t * LANES:(t + 1) * LANES] for t in range(n_tiles))
        return jnp.dot(_hi_lo(tile_sum), ones_ref[...], preferred_element_type=F32)

    kk_norm = head_sum(kk * kk, ones_kk_ref)
    kk = kk / jnp.maximum(jnp.sqrt(jnp.concatenate([kk_norm] * n_tiles, axis=1)), 1e-12)
    k32 = kx * (1.0 + (a - 1.0) * kap_ref[...])
    r_out[0] = r
    w_out[0] = decay
    k_out[0] = k32
    v_out[0] = vx
    kk_out[0] = kk
    kka_out[0] = kk * a
    g_out[0] = g
    bonus_out[0] = head_sum(r * k32 * rk_ref[...], ones_rk_ref) * vx


def rwkv_prep(cur, prev, prm):
    b, l, w = cur[0].shape
    tl = _tile(l, 256)
    seq = pl.BlockSpec((1, tl, w), lambda i, j: (i, j, 0))
    full = lambda a: pl.BlockSpec(a.shape, lambda i, j: (0,) * a.ndim)
    params = [prm[n] for n in PREP_PARAMS]
    return pl.pallas_call(
        functools.partial(_rwkv_prep_kernel, project=False),
        grid=(b, l // tl),
        in_specs=[seq] * 8 + [full(a) for a in params],
        out_specs=[seq] * N_PREP_OUT,
        out_shape=[jax.ShapeDtypeStruct((b, l, w), F32)] * N_PREP_OUT,
        compiler_params=_cparams(("parallel", "arbitrary")),
        name="rwkv_prep",
    )(*cur, *prev, *params)


def rwkv_project_prep(h, g_norm, w_proj, buf, prm):
    b, l, d = h.shape
    w = w_proj.shape[1] // 4
    tl = _tile(l, 512)
    seq = pl.BlockSpec((1, tl, w), lambda i, j: (i, j, 0))
    row = pl.BlockSpec((1, 1, 4 * w), lambda i, j: (i, 0, 0))
    full = lambda a: pl.BlockSpec(a.shape, lambda i, j: (0,) * a.ndim)
    params = [prm[n] for n in PREP_PARAMS]
    *outs, last = pl.pallas_call(
        functools.partial(_rwkv_prep_kernel, project=True),
        grid=(b, l // tl),
        in_specs=[pl.BlockSpec((1, tl, d), lambda i, j: (i, j, 0)), full(g_norm), full(w_proj), row]
        + [full(a) for a in params],
        out_specs=[seq] * N_PREP_OUT + [row],
        out_shape=[jax.ShapeDtypeStruct((b, l, w), F32)] * N_PREP_OUT + [jax.ShapeDtypeStruct((b, 1, 4 * w), F32)],
        scratch_shapes=[pltpu.VMEM((4, 1, w), F32)],
        compiler_params=_cparams(("parallel", "arbitrary")),
        name="rwkv_project_prep",
    )(h, g_norm, w_proj, buf, *params)
    return outs, last


def _value_columns(v8):
    hi = v8.astype(BF16).astype(F32)
    lo = (v8 - hi).astype(BF16).astype(F32)
    stacked = jnp.concatenate([part[:, p * LANES:(p + 1) * LANES]
                               for p in range(v8.shape[1] // LANES) for part in (hi, lo)], axis=0)
    cols = stacked.T
    return jnp.concatenate([cols[:HEAD_DIM], cols[HEAD_DIM:]], axis=1).astype(BF16)


def _rwkv_scan_kernel(r_ref, w_ref, k_ref, kk_ref, kka_ref, v_ref, sel_ref, ones_ref, o_ref, s_ref, s_scr,
                      ot_scr, *, n_heads):
    c = pl.program_id(1)
    n_grp, tc = r_ref.shape[0], r_ref.shape[1]
    head_lanes = LANES // n_heads
    n_kg = HEAD_DIM // head_lanes

    @pl.when(c == 0)
    def _():
        s_scr[...] = jnp.zeros_like(s_scr)

    ot_scr[...] = jnp.zeros_like(ot_scr)
    ones1 = ones_ref[...]
    t_lane = lax.broadcasted_iota(jnp.int32, (n_grp * HEAD_DIM, LANES), 1) % head_lanes
    rows_of = lambda x, g: x[g * HEAD_DIM:(g + 1) * HEAD_DIM]

    def block(tb, carry):
        base = pl.multiple_of(tb * SUBLANES, SUBLANES)
        refs = {"kk": kk_ref, "w": w_ref, "kka": kka_ref, "k": k_ref, "r": r_ref}
        vp = jnp.concatenate([_value_columns(v_ref[g, pl.ds(base, SUBLANES), :]) for g in range(n_grp)], axis=0)
        tile = base // head_lanes
        head_sum = lambda x: jnp.dot(x.astype(BF16), ones1, preferred_element_type=F32)
        for i in range(SUBLANES):
            row = lambda name, g, kg: jnp.broadcast_to(
                refs[name][g, pl.ds(base, SUBLANES), kg * LANES:(kg + 1) * LANES][i:i + 1, :], (HEAD_DIM, LANES))
            sa = head_sum(jnp.concatenate(
                [sum(s_scr[g, kg] * row("kk", g, kg) for kg in range(n_kg)) for g in range(n_grp)], axis=0))
            vcol = jnp.dot(vp, sel_ref[i], preferred_element_type=F32)
            reads = []
            for g in range(n_grp):
                sa_g, vcol_g = rows_of(sa, g), rows_of(vcol, g)
                read = None
                for kg in range(n_kg):
                    s_new = (s_scr[g, kg] * row("w", g, kg) - sa_g * row("kka", g, kg)
                             + vcol_g * row("k", g, kg))
                    s_scr[g, kg] = s_new
                    term = s_new * row("r", g, kg)
                    read = term if read is None else read + term
                reads.append(read)
            o = head_sum(jnp.concatenate(reads, axis=0))
            ot_scr[tile] = jnp.where(t_lane == (base + i) % head_lanes, o, ot_scr[tile])
        return carry

    lax.fori_loop(0, tc // SUBLANES, block, 0)

    for tile in range(tc // head_lanes):
        for g in range(n_grp):
            o_t = rows_of(ot_scr[tile], g).T
            for h in range(n_heads):
                o_ref[g, tile * head_lanes:(tile + 1) * head_lanes, h * HEAD_DIM:(h + 1) * HEAD_DIM] = (
                    o_t[h * head_lanes:(h + 1) * head_lanes, :])

    @pl.when(c == pl.num_programs(1) - 1)
    def _():
        s_ref[...] = s_scr[...]


def _key_group_perm(width):
    n_heads = width // HEAD_DIM
    head_lanes = LANES // n_heads
    n = np.arange(width)
    return (n % LANES) // head_lanes * HEAD_DIM + n // LANES * head_lanes + n % head_lanes


def _to_key_group(t):
    width = t.shape[-1]
    n_heads = width // HEAD_DIM
    head_lanes = LANES // n_heads
    split = t.reshape(*t.shape[:-1], n_heads, HEAD_DIM // head_lanes, head_lanes)
    return jnp.swapaxes(split, -3, -2).reshape(t.shape)


def _from_key_group(t):
    width = t.shape[-1]
    n_heads = width // HEAD_DIM
    head_lanes = LANES // n_heads
    split = t.reshape(*t.shape[:-1], HEAD_DIM // head_lanes, n_heads, head_lanes)
    return jnp.swapaxes(split, -3, -2).reshape(t.shape)


def rwkv_scan_prompt(r, w, k, kk, kka, v):
    b, l, width = r.shape
    n_heads = width // HEAD_DIM
    head_lanes = LANES // n_heads
    n_kg = HEAD_DIM // head_lanes
    tc = _tile(l, SCAN_CHUNK)
    grp = _tile(b, SCAN_GROUP)
    kl = np.arange(LANES)
    col_head = 2 * ((kl % HEAD_DIM) // (2 * SUBLANES)) + kl // HEAD_DIM
    sel = ((kl[None, :, None] % SUBLANES == np.arange(SUBLANES)[:, None, None])
           & (col_head[None, :, None] == kl[None, None, :] // head_lanes))
    ones1 = kl[:, None] // head_lanes == kl[None, :] // head_lanes
    sel, ones1 = jnp.asarray(sel, BF16), jnp.asarray(ones1, BF16)
    seq = pl.BlockSpec((grp, tc, width), lambda i, j: (i, j, 0))
    state = pl.BlockSpec((grp, n_kg, HEAD_DIM, LANES), lambda i, j: (i, 0, 0, 0))
    o, s = pl.pallas_call(
        functools.partial(_rwkv_scan_kernel, n_heads=n_heads),
        grid=(b // grp, l // tc),
        in_specs=[seq, seq, seq, seq, seq, seq,
                  pl.BlockSpec(sel.shape, lambda i, j: (0, 0, 0)),
                  pl.BlockSpec(ones1.shape, lambda i, j: (0, 0))],
        out_specs=[seq, state],
        out_shape=[jax.ShapeDtypeStruct((b, l, width), F32),
                   jax.ShapeDtypeStruct((b, n_kg, HEAD_DIM, LANES), F32)],
        scratch_shapes=[pltpu.VMEM((grp, n_kg, HEAD_DIM, LANES), F32),
                        pltpu.VMEM((tc // head_lanes, grp * HEAD_DIM, LANES), F32)],
        compiler_params=_cparams(("parallel", "arbitrary")),
        name="rwkv_scan_prompt",
    )(r, w, k, kk, kka, v, sel, ones1)
    s = s.reshape(b, n_kg, HEAD_DIM, n_heads, head_lanes).transpose(0, 3, 2, 1, 4)
    return o, s.reshape(b, n_heads, HEAD_DIM, HEAD_DIM)


def _rwkv_step_kernel(r_ref, w_ref, k_ref, kk_ref, kka_ref, v_ref, s0_ref, o_ref, s_ref):
    r, w, k, kk, kka, v = (ref[...] for ref in (r_ref, w_ref, k_ref, kk_ref, kka_ref, v_ref))
    for vi in range(HEAD_DIM):
        s_row = s0_ref[0, vi]
        sa = -jnp.sum(s_row * kk, axis=0, keepdims=True)
        s_new = s_row * w + sa * kka + v[vi:vi + 1, :] * k
        s_ref[0, vi] = s_new
        o_ref[vi:vi + 1, :] = jnp.sum(s_new * r, axis=0, keepdims=True)


def rwkv_step(r, w, k, kk, kka, v, s0):
    b, width = r.shape
    n_heads = width // HEAD_DIM
    vec = pl.BlockSpec((HEAD_DIM, b), lambda h: (h, 0))
    sspec = pl.BlockSpec((1, HEAD_DIM, HEAD_DIM, b), lambda h: (h, 0, 0, 0))
    o, s = pl.pallas_call(
        _rwkv_step_kernel,
        grid=(n_heads,),
        in_specs=[vec] * 6 + [sspec],
        out_specs=[vec, sspec],
        out_shape=[jax.ShapeDtypeStruct((width, b), F32),
                   jax.ShapeDtypeStruct((n_heads, HEAD_DIM, HEAD_DIM, b), F32)],
        compiler_params=_cparams(("parallel",)),
        name="rwkv_step",
    )(r.T, w.T, k.T, kk.T, kka.T, v.T, _batch_last(s0))
    return o.T, _batch_first(s)


def _diff_lambda(lp, lam_init):
    e1 = jnp.exp(jnp.sum(lp[0:1] * lp[1:2], axis=-1, keepdims=True))
    e2 = jnp.exp(jnp.sum(lp[2:3] * lp[3:4], axis=-1, keepdims=True))
    return e1 - e2 + lam_init


def _diff_attn_kernel(pt_ref, q_ref, k_ref, v_ref, qs_ref, kns_ref, vns_ref, *rest, lam_init, n_heads, n_pages):
    n_cache = (len(rest) - 7) // 2
    kc_refs, vc_refs = rest[:n_cache], rest[n_cache:2 * n_cache]
    lam_ref, subln_ref, o_ref, os_ref, m_scr, l_scr, acc_scr = rest[2 * n_cache:]
    del pt_ref
    lam = _diff_lambda(lam_ref[...], lam_init)
    _prompt_attention(q_ref, k_ref, v_ref, lam, subln_ref, o_ref, m_scr, l_scr, acc_scr, lam_init)
    for r in range(n_cache // n_pages):
        _decode_attention(r, qs_ref, kns_ref, vns_ref, kc_refs[r * n_pages:(r + 1) * n_pages],
                          vc_refs[r * n_pages:(r + 1) * n_pages], lam, subln_ref, os_ref, lam_init, n_heads)


def _prompt_attention(q_ref, k_ref, v_ref, lam, subln_ref, o_ref, m_scr, l_scr, acc_scr, lam_init):
    i = pl.program_id(2)
    tq = q_ref.shape[1]
    tk = tq
    scale = HEAD_DIM ** -0.5
    m_scr[...] = jnp.full_like(m_scr, -jnp.inf)
    l_scr[...] = jnp.zeros_like(l_scr)
    acc_scr[...] = jnp.zeros_like(acc_scr)
    q = (q_ref[0] * scale).astype(BF16)

    def update(j, on_diagonal):
        rows = pl.ds(pl.multiple_of(j * tk, tk), tk)
        k, v = k_ref[0, rows, :].astype(BF16), v_ref[0, rows, :].astype(BF16)
        if on_diagonal:
            visible = (lax.broadcasted_iota(jnp.int32, (tq, tk), 1)
                       <= lax.broadcasted_iota(jnp.int32, (tq, tk), 0))
        for mi in range(2):
            sl = slice(mi * HEAD_DIM, (mi + 1) * HEAD_DIM)
            s = lax.dot_general(q[:, sl], k[:, sl], (((1,), (1,)), ((), ())), preferred_element_type=F32)
            if on_diagonal:
                s = jnp.where(visible, s, -jnp.inf)
            m_old = m_scr[mi]
            m_new = jnp.maximum(m_old, jnp.max(s, axis=-1, keepdims=True))
            alpha = jnp.exp(m_old - m_new)
            p = jnp.exp(s - jnp.concatenate([m_new] * (tk // LANES), axis=1))
            l_scr[mi] = alpha * l_scr[mi] + jnp.sum(p, axis=-1, keepdims=True)
            acc_scr[mi] = alpha * acc_scr[mi] + jnp.dot(p.astype(BF16), v, preferred_element_type=F32)
            m_scr[mi] = m_new

    def below_diagonal(j, carry):
        update(j, False)
        return carry

    lax.fori_loop(0, i, below_diagonal, 0)
    update(i, True)
    o = acc_scr[0] / l_scr[0] - lam * (acc_scr[1] / l_scr[1])
    o_ref[0] = _rms(o, subln_ref[...]) * (1.0 - lam_init)


def _decode_attention(r, q_ref, kn_ref, vn_ref, kc_refs, vc_refs, lam, subln_ref, o_ref, lam_init, n_heads):
    n_rows = 2 * n_heads
    dv = 2 * HEAD_DIM
    scale = HEAD_DIM ** -0.5
    row = lax.broadcasted_iota(jnp.int32, (n_rows, dv), 0)
    lane = lax.broadcasted_iota(jnp.int32, (n_rows, dv), 1)
    qmat = jnp.where(lane // HEAD_DIM == row % 2, q_ref[r], 0.0)
    rows_per = kc_refs[0].shape[1]
    cols = len(kc_refs) * rows_per
    own = (lax.broadcasted_iota(jnp.int32, (n_rows, cols), 1) % n_heads
           == lax.broadcasted_iota(jnp.int32, (n_rows, cols), 0) // 2)
    s = jnp.concatenate(
        [lax.dot_general(qmat.astype(BF16), kc_ref[0].astype(BF16), (((1,), (1,)), ((), ())),
                         preferred_element_type=F32) for kc_ref in kc_refs], axis=1) * scale
    s = jnp.where(own, s, -jnp.inf)
    s_new = jnp.sum(qmat * kn_ref[r], axis=-1, keepdims=True) * scale
    m = jnp.maximum(jnp.max(s, axis=-1, keepdims=True), s_new)
    pr = jnp.exp(s - m)
    p_new = jnp.exp(s_new - m)
    prb = pr.astype(BF16)
    pv = sum(jnp.dot(prb[:, i * rows_per:(i + 1) * rows_per], vc_ref[0].astype(BF16), preferred_element_type=F32)
             for i, vc_ref in enumerate(vc_refs))
    acc = (pv + p_new * vn_ref[r]) / (jnp.sum(pr, axis=-1, keepdims=True) + p_new)
    acc = acc * jnp.where(row % 2 == 0, 1.0, -lam)
    subln = subln_ref[...]
    for h in range(n_heads):
        o = acc[2 * h:2 * h + 1] + acc[2 * h + 1:2 * h + 2]
        o_ref[r, h:h + 1, :] = _rms(o, subln) * (1.0 - lam_init)


def diff_attn(q, k, v, qs, ks, vs, cache_k, cache_v, page_table, lam_p, subln, lam_init):
    b, l, width = q.shape
    bs, n_pages = page_table.shape
    dv = 2 * HEAD_DIM
    n_heads = width // dv
    t = _tile(l, ATTN_BLOCK)
    n = l // t
    n_steps = b * n_heads * n
    per_step = bs // n_steps
    assert per_step * n_steps == bs, "sample rows must divide evenly over the prompt attention grid"
    rows = cache_k.shape[1]
    step = lambda bi, h, i: (bi * n_heads + h) * n + i
    per_map = lambda x: jnp.repeat(x.reshape(bs, n_heads, dv), 2, axis=1)
    qspec = pl.BlockSpec((1, t, dv), lambda bi, h, i, pt: (bi, i, h))
    kspec = pl.BlockSpec((1, l, dv), lambda bi, h, i, pt: (bi, 0, h))
    vec = pl.BlockSpec((per_step, 2 * n_heads, dv), lambda bi, h, i, pt: (step(bi, h, i), 0, 0))
    cache = [pl.BlockSpec((1, rows, dv), functools.partial(
        lambda bi, h, i, pt, r, slot: (pt[(step(bi, h, i) * per_step + r) * n_pages + slot], 0, 0), r=r, slot=slot))
        for r in range(per_step) for slot in range(n_pages)]
    const = lambda x: pl.BlockSpec(x.shape, lambda bi, h, i, pt: (0, 0))
    o, o_s = pl.pallas_call(
        functools.partial(_diff_attn_kernel, lam_init=lam_init, n_heads=n_heads, n_pages=n_pages),
        grid_spec=pltpu.PrefetchScalarGridSpec(
            num_scalar_prefetch=1,
            grid=(b, n_heads, n),
            in_specs=[qspec, kspec, kspec, vec, vec, vec] + cache + cache + [const(lam_p), const(subln)],
            out_specs=[qspec, pl.BlockSpec((per_step, n_heads, dv), lambda bi, h, i, pt: (step(bi, h, i), 0, 0))],
            scratch_shapes=[pltpu.VMEM((2, t, LANES), F32), pltpu.VMEM((2, t, LANES), F32),
                            pltpu.VMEM((2, t, dv), F32)],
        ),
        out_shape=[jax.ShapeDtypeStruct((b, l, width), F32), jax.ShapeDtypeStruct((bs, n_heads, dv), F32)],
        compiler_params=_cparams(("parallel", "parallel", "arbitrary")),
        name="diff_attn",
    )(page_table.reshape(-1), q, k, v, per_map(qs), per_map(ks), per_map(vs),
      *([cache_k] * (per_step * n_pages)), *([cache_v] * (per_step * n_pages)), lam_p, subln)
    return o, o_s.reshape(bs, width)


def _lru_coeffs(x, x1, x2, x3, cw, cb, wa, ba, wi, bi, lam):
    xc = x3 * cw[0:1] + x2 * cw[1:2] + x1 * cw[2:3] + x * cw[3:4]
    xc = xc + cb
    r = jax.nn.sigmoid(_dot(xc, wa) + ba)
    ig = jax.nn.sigmoid(_dot(xc, wi) + bi)
    log_a = -LRU_C * r * _softplus(-lam)
    a = jnp.exp(log_a)
    return a, jnp.sqrt(-jnp.tanh(log_a) * (a * a + 1.0)) * (ig * xc)


def _lru_seq_kernel(x_ref, gr_ref, buf_ref, h0_ref, cw_ref, cb_ref, wa_ref, ba_ref, wi_ref, bi_ref, lam_ref,
                    o_ref, hl_ref, carry, h_scr, a_scr, b_scr):
    n_grp, tl = x_ref.shape[0], x_ref.shape[1]
    n_carry = carry.shape[1]

    @pl.when(pl.program_id(1) == 0)
    def _():
        for g in range(n_grp):
            for d in range(n_carry):
                carry[g, d] = buf_ref[g, d:d + 1, :]
        h_scr[...] = h0_ref[...]

    for g in range(n_grp):
        x = x_ref[g]
        x1 = _shifted(x, carry[g, n_carry - 1])
        x2 = _shifted(x1, carry[g, n_carry - 2])
        x3 = _shifted(x2, carry[g, n_carry - 3])
        for d in range(n_carry):
            carry[g, d] = x[tl - n_carry + d:tl - n_carry + d + 1, :]
        a_scr[g], b_scr[g] = _lru_coeffs(x, x1, x2, x3, cw_ref[...], cb_ref[...], wa_ref[...], ba_ref[...],
                                         wi_ref[...], bi_ref[...], lam_ref[...])
    row_id = lax.broadcasted_iota(jnp.int32, (SUBLANES, x_ref.shape[2]), 0)

    def block(tb, hs):
        base = pl.multiple_of(tb * SUBLANES, SUBLANES)
        rows = pl.ds(base, SUBLANES)
        a8 = [a_scr[g, rows, :] for g in range(n_grp)]
        b8 = [b_scr[g, rows, :] for g in range(n_grp)]
        hs = list(hs)
        out = [jnp.zeros_like(a8[0]) for _ in range(n_grp)]
        for i in range(SUBLANES):
            for g in range(n_grp):
                hs[g] = a8[g][i:i + 1, :] * hs[g] + b8[g][i:i + 1, :]
                out[g] = jnp.where(row_id == i, hs[g], out[g])
        for g in range(n_grp):
            o_ref[g, rows, :] = out[g] * jax.nn.gelu(gr_ref[g, rows, :])
        return tuple(hs)

    hs = lax.fori_loop(0, tl // SUBLANES, block, tuple(h_scr[g] for g in range(n_grp)))
    for g in range(n_grp):
        h_scr[g] = hs[g]
        hl_ref[g] = hs[g]


def lru_prompt(x, gr, buf, h0, prm):
    bsz, l, w = x.shape
    tl = _tile(l, 512)
    grp = _tile(bsz, LRU_GROUP)
    seq = pl.BlockSpec((grp, tl, w), lambda i, j: (i, j, 0))
    vec = pl.BlockSpec((grp, 1, w), lambda i, j: (i, 0, 0))
    full = lambda a: pl.BlockSpec(a.shape, lambda i, j: (0,) * a.ndim)
    params = [prm[n] for n in ("conv_w", "conv_b", "wa", "ba", "wi", "bi", "lam")]
    nb = buf.shape[1]
    o, hl = pl.pallas_call(
        _lru_seq_kernel,
        grid=(bsz // grp, l // tl),
        in_specs=[seq, seq, pl.BlockSpec((grp, nb, w), lambda i, j: (i, 0, 0)), vec] + [full(a) for a in params],
        out_specs=[seq, vec],
        out_shape=[jax.ShapeDtypeStruct((bsz, l, w), F32), jax.ShapeDtypeStruct((bsz, 1, w), F32)],
        scratch_shapes=[pltpu.VMEM((grp, nb, 1, w), F32), pltpu.VMEM((grp, 1, w), F32),
                        pltpu.VMEM((grp, tl, w), F32), pltpu.VMEM((grp, tl, w), F32)],
        compiler_params=_cparams(("parallel", "arbitrary")),
        name="lru_prompt",
    )(x, gr, buf, h0.reshape(bsz, 1, w), *params)
    return o, hl.reshape(bsz, w)


def _lru_step_kernel(x_ref, x1_ref, x2_ref, x3_ref, gr_ref, h0_ref, cw_ref, cb_ref, wa_ref, ba_ref, wi_ref,
                     bi_ref, lam_ref, o_ref, h_ref):
    a, b = _lru_coeffs(x_ref[...], x1_ref[...], x2_ref[...], x3_ref[...], cw_ref[...], cb_ref[...], wa_ref[...],
                       ba_ref[...], wi_ref[...], bi_ref[...], lam_ref[...])
    h = a * h0_ref[...] + b
    h_ref[...] = h
    o_ref[...] = h * jax.nn.gelu(gr_ref[...])


def lru_step(x, buf, gr, h0, prm):
    m, w = x.shape
    tm = _tile(m, 512)
    rows = pl.BlockSpec((tm, w), lambda i: (i, 0))
    full = lambda a: pl.BlockSpec(a.shape, lambda i: (0,) * a.ndim)
    params = [prm[n] for n in ("conv_w", "conv_b", "wa", "ba", "wi", "bi", "lam")]
    nb = buf.shape[1]
    return pl.pallas_call(
        _lru_step_kernel,
        grid=(m // tm,),
        in_specs=[rows] * 6 + [full(a) for a in params],
        out_specs=[rows] * 2,
        out_shape=[jax.ShapeDtypeStruct((m, w), F32)] * 2,
        compiler_params=_cparams(("parallel",)),
        name="lru_step",
    )(x, buf[:, nb - 1], buf[:, nb - 2], buf[:, nb - 3], gr, h0, *params)


def _block_diag(w):
    n, d, e = w.shape
    eye = jnp.eye(n, dtype=w.dtype)
    return (eye[:, None, :, None] * w[:, :, None, :]).reshape(n * d, n * e)


def _mix_even(h, g_norm, pos0, s_ret, s_rwkv, buf, wts, is_prompt):
    b, l, d = h.shape
    m = b * l
    gw = d // 2
    seq = lambda t: t.reshape(b, l, gw)
    flat = lambda t: t.reshape(m, gw)
    to_kg, to_nat = _to_key_group, _from_key_group
    groups = lambda t: [t[..., i * gw:(i + 1) * gw] for i in range(4)]
    br, bk, bv, bz = groups(buf)
    buf_kg = [to_kg(br), to_kg(bk), bv, bz]
    w_in = wts["ab_w_in"]
    if is_prompt:
        qa, ka, va, ga = norm_matmul(h.reshape(m, d), g_norm, w_in[:, :4 * gw], gw)
        (r, w, k, v, kk, kka, g, bonus), last = rwkv_project_prep(
            h, g_norm, w_in[:, 4 * gw:], jnp.concatenate(buf_kg, axis=-1), wts["rwkv"])
        last = groups(last)
        pos = pos0 + jnp.arange(l, dtype=F32)
        o_a, s_ret_new = retention_prompt(seq(qa), seq(ka), seq(va), seq(ga), pos, wts["ones_bd"])
        o_a = flat(o_a)
        o_b, s_rwkv_new = rwkv_scan_prompt(r, w, k, kk, kka, v)
    else:
        u = norm_matmul(h.reshape(m, d), g_norm, w_in, gw)
        qa, ka, va, ga = u[:4]
        last = [seq(t) for t in u[4:8]]
        rows = lambda t: t.reshape(1, m, gw)
        r, w, k, v, kk, kka, g, bonus = rwkv_prep([rows(t) for t in last], [rows(t) for t in buf_kg], wts["rwkv"])
        o_a, s_ret_new = retention_step(qa, ka, va, ga, s_ret, jnp.float32(pos0))
        o_b, s_rwkv_new = rwkv_step(*(flat(to_nat(t)) for t in (r, w, k, kk, kka)), flat(v), s_rwkv)
    buf_new = jnp.concatenate([to_nat(last[0]), to_nat(last[1]), last[2], last[3]], axis=-1)
    post = (flat(bonus), flat(g), wts["rwkv_ln"], wts["ones_bd"])
    return o_a, flat(o_b), post, s_ret_new, s_rwkv_new, buf_new


def _mix_odd(h, g_norm, lru_h, lru_buf, wts, is_prompt):
    b, l, d = h.shape
    m = b * l
    gw = d // 2
    u = norm_matmul(h.reshape(m, d), g_norm, wts["cd_w_in"], gw, head_major=(1, 2))
    seq = lambda t: t.reshape(b, l, gw)
    xr = seq(u[3])
    if is_prompt:
        o_c = yield seq(u[0]), seq(u[1]), seq(u[2])
        o_c = o_c.reshape(m, gw)
        o_d, h_last = lru_prompt(xr, seq(u[4]), lru_buf, lru_h, wts["lru"])
    else:
        o_c = yield u[0], u[1], u[2]
        o_d, h_last = lru_step(u[3], lru_buf, u[4], lru_h, wts["lru"])
    buf_new = jnp.concatenate([lru_buf, xr], axis=1)[:, l:]
    n_heads = gw // (2 * HEAD_DIM)
    k_new = u[5].reshape(b, l, n_heads, 2 * HEAD_DIM)
    v_new = u[6].reshape(b, l, n_heads, 2 * HEAD_DIM)
    return o_c, o_d.reshape(m, gw), k_new, v_new, h_last, buf_new


def _advance(gen, value):
    try:
        return gen.send(value), None
    except StopIteration as done:
        return None, done.value


def _trunk(x, p, pos0, s_ret, s_rwkv, s_shift, s_lru_h, s_lru_conv, wts, is_prompt):
    b, l, d = x.shape
    m = b * l
    depth = wts["norm_g"].shape[0]
    h = x.reshape(m, d)
    ret_l, rwkv_l, shift_l, k_l, v_l, lh_l, lc_l = [], [], [], [], [], [], []
    for i in range(depth):
        j = i // 2
        g = wts["norm_g"][i]
        gn = lambda n: g[n:n + 1]
        h = ffn_block(h, gn(0), wts["ffn_in"], wts["ffn_out"], gn(1), i, 0)
        post = None
        if i % 2 == 0:
            o1, o2, post, sr, sw, sb = _mix_even(h.reshape(b, l, d), gn(2), pos0, s_ret[j], s_rwkv[j], s_shift[j],
                                                 wts["even"][j], is_prompt)
            ret_l.append(sr)
            rwkv_l.append(sw)
            shift_l.append(sb)
            w_out = wts["even"][j]["w_out"]
        else:
            o1, o2, kn, vn, lh, lc = yield from _mix_odd(h.reshape(b, l, d), gn(2), s_lru_h[j], s_lru_conv[j],
                                                         wts["odd"][j], is_prompt)
            k_l.append(kn)
            v_l.append(vn)
            lh_l.append(lh)
            lc_l.append(lc)
            w_out = wts["odd"][j]["w_out"]
        h = out_proj(o1, o2, w_out, h, gn(3), post)
        h = ffn_block(h, gn(4), wts["ffn_in"], wts["ffn_out"], gn(5), i, 1,
                      ple=(gn(6), wts["ple_gate"], p.reshape(depth, m, -1), wts["ple"], gn(7)))
    st = lambda lst: jnp.stack(lst, axis=0)
    return (h.reshape(b, l, d), st(k_l), st(v_l), st(ret_l), st(rwkv_l), st(shift_l), st(lh_l), st(lc_l))


def kernel(x_prompt, x_sample, cache_k, cache_v, state_ret, state_rwkv, state_rwkv_shift, state_lru_h, state_lru_conv, page_table, p_prompt, p_sample, norm_g, ffn_w_in, ffn_w_out, ple_w, ple_gate_w, ab_w_in, ab_w_out, rwkv_mu, rwkv_w0, rwkv_w1, rwkv_w2, rwkv_a0, rwkv_a1, rwkv_a2, rwkv_g1, rwkv_g2, rwkv_kk, rwkv_ka, rwkv_rk, rwkv_ln, cd_w_in, cd_w_out, diff_lam, diff_subln, lru_conv_w, lru_conv_b, lru_wa, lru_ba, lru_wi, lru_bi, lru_lambda):
    depth = norm_g.shape[0]
    n_a, n_c = state_ret.shape[0], state_lru_h.shape[0]
    bp = x_prompt.shape[0]
    gw = ab_w_out.shape[1] // 2
    bf = lambda t: t.astype(BF16)
    row = lambda t: t.reshape(1, -1)
    lane2 = np.arange(2 * LANES) % LANES
    ones_bd = jnp.asarray(lane2[:, None] // HEAD_DIM == np.arange(LANES)[None, :] // HEAD_DIM, BF16)
    perm = _key_group_perm(gw)
    kg = _to_key_group
    head_kg = perm // HEAD_DIM
    tile_head = np.tile(head_kg[:LANES], 2)
    ones_kk = jnp.asarray(tile_head[:, None] == head_kg[None, :LANES], BF16)
    ones_rk = jnp.asarray(tile_head[:, None] == (np.arange(gw) // HEAD_DIM)[None, :], BF16)

    def ab_in_kg(w):
        cols = [w[:, g * gw:(g + 1) * gw] for g in range(w.shape[1] // gw)]
        cols[4], cols[5] = kg(cols[4]), kg(cols[5])
        return bf(jnp.concatenate(cols, axis=1))

    wts = {
        "norm_g": norm_g,
        "ffn_in": bf(ffn_w_in), "ffn_out": bf(ffn_w_out), "ple": bf(ple_w), "ple_gate": bf(ple_gate_w),
        "even": [{
            "ab_w_in": ab_in_kg(ab_w_in[j]), "w_out": bf(ab_w_out[j]), "rwkv_ln": rwkv_ln[j],
            "ones_bd": ones_bd,
            "rwkv": {"mu": jnp.concatenate([kg(rwkv_mu[j][:2]), rwkv_mu[j][2:]], axis=0),
                     "w0": kg(row(rwkv_w0[j])), "w1": bf(rwkv_w1[j]), "w2": bf(kg(rwkv_w2[j])),
                     "a0": kg(row(rwkv_a0[j])), "a1": bf(rwkv_a1[j]), "a2": bf(kg(rwkv_a2[j])),
                     "g1": bf(rwkv_g1[j]), "g2": bf(rwkv_g2[j]), "kk": kg(row(rwkv_kk[j])),
                     "ka": kg(row(rwkv_ka[j])), "rk": kg(row(rwkv_rk[j])),
                     "ones_kk": ones_kk, "ones_rk": ones_rk},
        } for j in range(n_a)],
        "odd": [{
            "cd_w_in": bf(cd_w_in[j]), "w_out": bf(cd_w_out[j]), "diff_lam": diff_lam[j],
            "diff_subln": row(diff_subln[j]),
            "lru": {"conv_w": lru_conv_w[j], "conv_b": row(lru_conv_b[j]), "wa": bf(_block_diag(lru_wa[j])),
                    "ba": row(lru_ba[j]), "wi": bf(_block_diag(lru_wi[j])), "bi": row(lru_bi[j]),
                    "lam": row(lru_lambda[j])},
        } for j in range(n_c)],
    }
    zeros = lambda *shape: jnp.zeros(shape, F32)
    past_len = page_table.shape[1] * cache_k.shape[2]
    n_pool, page = cache_k.shape[1], cache_k.shape[2]
    as_rows = lambda c: c.reshape(n_c, n_pool, page * c.shape[3], c.shape[4])
    pages_k, pages_v = as_rows(cache_k), as_rows(cache_v)
    prompt = _trunk(x_prompt, p_prompt, 0.0, [None] * n_a, [None] * n_a,
                    zeros(n_a, bp, 1, 4 * gw), zeros(n_c, bp, gw), zeros(n_c, bp, CONV_W - 1, gw), wts, True)
    sample = _trunk(x_sample, p_sample, float(past_len), state_ret, state_rwkv, state_rwkv_shift,
                    state_lru_h, state_lru_conv, wts, False)
    (qkv_p, out_p), (qkv_s, out_s) = _advance(prompt, None), _advance(sample, None)
    j = 0
    while out_p is None:
        odd = wts["odd"][j]
        lam_init = 0.8 - 0.6 * math.exp(-0.3 * (2 * j + 1))
        o_p, o_s = diff_attn(*qkv_p, *qkv_s, pages_k[j], pages_v[j], page_table, odd["diff_lam"],
                             odd["diff_subln"], lam_init)
        (qkv_p, out_p), (qkv_s, out_s) = _advance(prompt, o_p), _advance(sample, o_s)
        j += 1
    yp, kp, vp, rp, wp, sp, hp, cp = out_p
    ys, ks_, vs, rs, ws, ss, hs, cs = out_s
    return (yp, ys, kp, vp, rp, wp, sp, hp, cp, ks_, vs, rs, ws, ss, hs, cs)
```

```python
import functools
import math

import jax
import jax.numpy as jnp
import numpy as np
from jax import lax
from jax.experimental import pallas as pl
from jax.experimental.pallas import tpu as pltpu

F32 = jnp.float32
BF16 = jnp.bfloat16

HEAD_DIM = 64
CONV_W = 4
LRU_C = 8.0
ROPE_BASE = 10000.0
EPS = 1e-6
RWKV_GN_EPS = 64e-5
RET_CHUNK = 256
ATTN_BLOCK = 512
SCAN_CHUNK = 128
LRU_GROUP = 2
SCAN_GROUP = 8
LANES = 128
SUBLANES = 8
VMEM_LIMIT = 48 * 1024 * 1024


def _cparams(sem):
    return pltpu.CompilerParams(dimension_semantics=sem, vmem_limit_bytes=VMEM_LIMIT)


def _tile(n, pref):
    t = min(n, pref)
    while n % t:
        t //= 2
    return t


def _rms(x, g):
    return x * lax.rsqrt(jnp.mean(x * x, axis=-1, keepdims=True) + EPS) * g


def _dot(a, b):
    return jnp.dot(a.astype(BF16), b.astype(BF16), preferred_element_type=F32)


def _hi_lo(x):
    hi = x.astype(BF16)
    lo = (x - hi.astype(F32)).astype(BF16)
    return jnp.concatenate([hi, lo], axis=1)


def _seg_sum(x, ones_pair):
    return jnp.concatenate([jnp.dot(_hi_lo(x[:, t * LANES:(t + 1) * LANES]), ones_pair, preferred_element_type=F32)
                            for t in range(x.shape[1] // LANES)], axis=1)


def _ffn_kernel(h_ref, gpre_ref, wg_ref, wu_ref, wo_ref, gpost_ref, *rest):
    o_ref, xn_ref, acc_ref = rest[-3:]
    ple = rest[:-3]
    j = pl.program_id(1)

    @pl.when(j == 0)
    def _():
        xn_ref[...] = _rms(h_ref[...], gpre_ref[...]).astype(BF16)
        acc_ref[...] = jnp.zeros_like(acc_ref)

    xn = xn_ref[...]
    gate = jnp.dot(xn, wg_ref[...], preferred_element_type=F32)
    up = jnp.dot(xn, wu_ref[...], preferred_element_type=F32)
    act = (gate * jax.nn.sigmoid(gate) * up).astype(BF16)
    acc_ref[...] += jnp.dot(act, wo_ref[...], preferred_element_type=F32)

    @pl.when(j == pl.num_programs(1) - 1)
    def _():
        h = h_ref[...] + 0.5 * _rms(acc_ref[...], gpost_ref[...])
        if ple:
            g6_ref, wgate_ref, p_ref, wp_ref, g7_ref = ple
            gate_p = jax.nn.sigmoid(_dot(_rms(h, g6_ref[...]), wgate_ref[...]))
            h = h + _rms(gate_p * _dot(p_ref[...], wp_ref[...]), g7_ref[...])
        o_ref[...] = h


def ffn_block(h, g_pre, w_in, w_out, g_post, layer, half, ple=None):
    m, d = h.shape
    f = w_out.shape[2]
    tm = _tile(m, 1024)
    tf = _tile(f, 512 if tm >= 512 else 2048)
    nf = f // tf
    vec = pl.BlockSpec((1, d), lambda i, j: (0, 0))
    extra, extra_specs = [], []
    if ple is not None:
        g6, gate_w, p, emb_w, g7 = ple
        pd = p.shape[2]
        extra = [g6, gate_w, p, emb_w, g7]
        extra_specs = [vec, pl.BlockSpec((None, d, d), lambda i, j: (layer, 0, 0)),
                       pl.BlockSpec((None, tm, pd), lambda i, j: (layer, i, 0)),
                       pl.BlockSpec((None, pd, d), lambda i, j: (layer, 0, 0)), vec]
    return pl.pallas_call(
        _ffn_kernel,
        grid=(m // tm, nf),
        in_specs=[
            pl.BlockSpec((tm, d), lambda i, j: (i, 0)),
            vec,
            pl.BlockSpec((None, None, d, tf), lambda i, j: (layer, half, 0, j)),
            pl.BlockSpec((None, None, d, tf), lambda i, j: (layer, half, 0, j + nf)),
            pl.BlockSpec((None, None, tf, d), lambda i, j: (layer, half, j, 0)),
            vec,
        ] + extra_specs,
        out_specs=pl.BlockSpec((tm, d), lambda i, j: (i, 0)),
        out_shape=jax.ShapeDtypeStruct((m, d), F32),
        scratch_shapes=[pltpu.VMEM((tm, d), BF16), pltpu.VMEM((tm, d), F32)],
        compiler_params=_cparams(("parallel", "arbitrary")),
        name="ffn_block",
    )(h, g_pre, w_in, w_in, w_out, g_post, *extra)


def _norm_matmul_kernel(h_ref, g_ref, w_ref, *o_refs, n_groups, head_major):
    xn = _rms(h_ref[...], g_ref[...]).astype(BF16)
    tm, tn = o_refs[0].shape
    heads = tn // LANES
    for gi in range(n_groups):
        res = jnp.dot(xn, w_ref[:, gi * tn:(gi + 1) * tn], preferred_element_type=F32)
        o_refs[gi][...] = res
        if gi in head_major:
            hm_ref = o_refs[n_groups + head_major.index(gi)]
            for hh in range(heads):
                hm_ref[pl.ds(hh, tm, stride=heads), :] = res[:, hh * LANES:(hh + 1) * LANES]


def norm_matmul(h, g, w, tn, head_major=()):
    m, d = h.shape
    n = w.shape[1]
    tm = _tile(m, 512)
    heads = tn // LANES
    rows = pl.BlockSpec((tm, tn), lambda i: (i, 0))
    return pl.pallas_call(
        functools.partial(_norm_matmul_kernel, n_groups=n // tn, head_major=tuple(head_major)),
        grid=(m // tm,),
        in_specs=[
            pl.BlockSpec((tm, d), lambda i: (i, 0)),
            pl.BlockSpec((1, d), lambda i: (0, 0)),
            pl.BlockSpec((d, n), lambda i: (0, 0)),
        ],
        out_specs=[rows] * (n // tn) + [pl.BlockSpec((tm * heads, LANES), lambda i: (i, 0))] * len(head_major),
        out_shape=([jax.ShapeDtypeStruct((m, tn), F32)] * (n // tn)
                   + [jax.ShapeDtypeStruct((m * heads, LANES), F32)] * len(head_major)),
        compiler_params=_cparams(("parallel",)),
        name="norm_matmul",
    )(h, g, w)


def _out_proj_kernel(oa_ref, ob_ref, wa_ref, wb_ref, h_ref, g_ref, *rest):
    ob = ob_ref[...]
    if len(rest) > 1:
        bonus_ref, gate_ref, ln_ref, ones_ref = rest[:4]
        ones_bd = ones_ref[...]
        inv = 1.0 / HEAD_DIM
        oc = ob - _seg_sum(ob, ones_bd) * inv
        on = oc * lax.rsqrt(_seg_sum(oc * oc, ones_bd) * inv + RWKV_GN_EPS)
        ln = ln_ref[...]
        ob = (on * ln[0:1] + ln[1:2] + bonus_ref[...]) * gate_ref[...]
    o_ref = rest[-1]
    y = _dot(oa_ref[...], wa_ref[...]) + _dot(ob, wb_ref[...])
    o_ref[...] = h_ref[...] + _rms(y, g_ref[...])


def out_proj(oa, ob, w, h, g, rwkv_post=None):
    m, d = h.shape
    gw = oa.shape[1]
    tm = _tile(m, 512)
    rows = pl.BlockSpec((tm, gw), lambda i: (i, 0))
    extra, extra_specs = [], []
    if rwkv_post is not None:
        bonus, gate, ln, ones_bd = rwkv_post
        extra = [bonus, gate, ln, ones_bd]
        extra_specs = [rows, rows, pl.BlockSpec(ln.shape, lambda i: (0, 0)),
                       pl.BlockSpec(ones_bd.shape, lambda i: (0, 0))]
    return pl.pallas_call(
        _out_proj_kernel,
        grid=(m // tm,),
        in_specs=[
            rows, rows,
            pl.BlockSpec((gw, d), lambda i: (0, 0)),
            pl.BlockSpec((gw, d), lambda i: (1, 0)),
            pl.BlockSpec((tm, d), lambda i: (i, 0)),
            pl.BlockSpec((1, d), lambda i: (0, 0)),
        ] + extra_specs,
        out_specs=pl.BlockSpec((tm, d), lambda i: (i, 0)),
        out_shape=jax.ShapeDtypeStruct((m, d), F32),
        compiler_params=_cparams(("parallel",)),
        name="out_proj",
    )(oa, ob, w, w, h, g, *extra)


def _retention_tables(n_heads, c, pos):
    lg = jnp.log1p(-jnp.exp2(-5.0 - jnp.arange(n_heads, dtype=F32)))
    idx = jnp.arange(c, dtype=F32)
    rel = idx[:, None] - idx[None, :]
    dmask = jnp.where(rel[None] >= 0, jnp.exp(jnp.maximum(rel, 0.0)[None] * lg[:, None, None]), 0.0)
    rep = lambda t: jnp.repeat(t, HEAD_DIM, axis=-1)
    q_dec = rep(jnp.exp((idx[:, None] + 1.0) * lg[None, :]))
    k_dec = rep(jnp.exp((c - 1.0 - idx[:, None]) * lg[None, :]))
    c_dec = rep(jnp.exp(c * lg)[None, :])
    half = HEAD_DIM // 2
    freq = 1.0 / (ROPE_BASE ** jnp.linspace(0.0, 1.0, half, dtype=F32))
    ang = pos[:, None] * freq[None, :]
    cos, sin = jnp.cos(ang), jnp.sin(ang)
    cos_t = jnp.tile(jnp.concatenate([cos, cos], axis=-1), (1, n_heads))
    sin_t = jnp.tile(jnp.concatenate([-sin, sin], axis=-1), (1, n_heads))
    return dmask, q_dec, k_dec, c_dec, cos_t, sin_t


def _retention_kernel(q_ref, k_ref, v_ref, g_ref, cos_ref, sin_ref, dmask_ref, qdec_ref, kdec_ref,
                      cdec_ref, ones_ref, o_ref, s_ref, s_scr, *, n_heads):
    c = pl.program_id(1)

    @pl.when(c == 0)
    def _():
        s_scr[...] = jnp.zeros_like(s_scr)

    q, k, v, g = q_ref[0], k_ref[0], v_ref[0], g_ref[0]
    cos, sin = cos_ref[...], sin_ref[...]
    rows, width = q.shape
    lane = lax.broadcasted_iota(jnp.int32, q.shape, 1)
    first_half = (lane % HEAD_DIM) < (HEAD_DIM // 2)

    def rot(x):
        swapped = jnp.where(first_half, pltpu.roll(x, width - HEAD_DIM // 2, 1),
                            pltpu.roll(x, HEAD_DIM // 2, 1))
        return x * cos + swapped * sin

    qr = rot(q)
    kr = rot(k) * (HEAD_DIM ** -0.5)
    kd = kr * kdec_ref[...]
    qdec = qdec_ref[...]
    cdec = cdec_ref[...]
    head_a = lax.broadcasted_iota(jnp.int32, (rows, LANES), 1) < HEAD_DIM
    same_head = (lax.broadcasted_iota(jnp.int32, (LANES, LANES), 0) // HEAD_DIM
                 == lax.broadcasted_iota(jnp.int32, (LANES, LANES), 1) // HEAD_DIM)
    nt = (((1,), (1,)), ((), ()))
    outs = []
    for p in range(n_heads // 2):
        sl = slice(p * LANES, (p + 1) * LANES)
        qp = qr[:, sl].astype(BF16)
        kp = kr[:, sl].astype(BF16)
        vp = v[:, sl].astype(BF16)
        zero = jnp.zeros_like(qp)
        att_a = lax.dot_general(jnp.where(head_a, qp, zero), kp, nt, preferred_element_type=F32) * dmask_ref[2 * p]
        att_b = lax.dot_general(jnp.where(head_a, zero, qp), kp, nt,
                                preferred_element_type=F32) * dmask_ref[2 * p + 1]
        s_old = s_scr[p]
        inner = jnp.where(head_a, jnp.dot(att_a.astype(BF16), vp, preferred_element_type=F32),
                          jnp.dot(att_b.astype(BF16), vp, preferred_element_type=F32))
        outs.append(inner + jnp.dot(qp, s_old.astype(BF16), preferred_element_type=F32) * qdec[:, sl])
        update = lax.dot_general(kd[:, sl].astype(BF16), vp, (((0,), (0,)), ((), ())),
                                 preferred_element_type=F32)
        s_scr[p] = jnp.where(same_head, s_old * cdec[:, sl] + update, 0.0)
    o = jnp.concatenate(outs, axis=1)
    ones_bd = ones_ref[...]
    inv = 1.0 / HEAD_DIM
    oc = o - _seg_sum(o, ones_bd) * inv
    on = oc * lax.rsqrt(_seg_sum(oc * oc, ones_bd) * inv + EPS)
    o_ref[0] = on * (g * jax.nn.sigmoid(g))

    @pl.when(c == pl.num_programs(1) - 1)
    def _():
        for p in range(n_heads // 2):
            s_pair = s_scr[p]
            s_ref[0, 2 * p] = s_pair[:HEAD_DIM, :HEAD_DIM]
            s_ref[0, 2 * p + 1] = s_pair[HEAD_DIM:, HEAD_DIM:]


def retention_prompt(q, k, v, g, pos, ones_bd):
    b, l, width = q.shape
    n_heads = width // HEAD_DIM
    c = _tile(l, RET_CHUNK)
    dmask, q_dec, k_dec, c_dec, cos_t, sin_t = _retention_tables(n_heads, c, pos)
    seq = pl.BlockSpec((1, c, width), lambda i, j: (i, j, 0))
    tab = pl.BlockSpec((c, width), lambda i, j: (j, 0))
    fixed = lambda shape: pl.BlockSpec(shape, lambda i, j: (0,) * len(shape))
    return pl.pallas_call(
        functools.partial(_retention_kernel, n_heads=n_heads),
        grid=(b, l // c),
        in_specs=[seq, seq, seq, seq, tab, tab, fixed((n_heads, c, c)), fixed((c, width)),
                  fixed((c, width)), fixed((1, width)), fixed(ones_bd.shape)],
        out_specs=[seq, pl.BlockSpec((1, n_heads, HEAD_DIM, HEAD_DIM), lambda i, j: (i, 0, 0, 0))],
        out_shape=[jax.ShapeDtypeStruct((b, l, width), F32),
                   jax.ShapeDtypeStruct((b, n_heads, HEAD_DIM, HEAD_DIM), F32)],
        scratch_shapes=[pltpu.VMEM((n_heads // 2, LANES, LANES), F32)],
        compiler_params=_cparams(("parallel", "arbitrary")),
        name="retention_prompt",
    )(q, k, v, g, cos_t, sin_t, dmask, q_dec, k_dec, c_dec, ones_bd)


def _retention_step_kernel(q_ref, k_ref, v_ref, g_ref, cos_ref, sin_ref, gam_ref, s0_ref, o_ref, s_ref):
    half = HEAD_DIM // 2
    cos, sin = cos_ref[...], sin_ref[...]
    rot = lambda x: x * cos + jnp.concatenate([x[half:], x[:half]], axis=0) * sin
    q = rot(q_ref[...])
    k = rot(k_ref[...]) * (HEAD_DIM ** -0.5)
    v, g = v_ref[...], g_ref[...]
    gam = gam_ref[0]
    att = jnp.sum(q * k, axis=0, keepdims=True)
    cross = jnp.zeros_like(v)
    for ki in range(HEAD_DIM):
        s_row = s0_ref[0, ki]
        cross = cross + q[ki:ki + 1, :] * s_row
        s_ref[0, ki] = s_row * gam + k[ki:ki + 1, :] * v
    o = att * v + cross * gam
    oc = o - jnp.mean(o, axis=0, keepdims=True)
    on = oc * lax.rsqrt(jnp.mean(oc * oc, axis=0, keepdims=True) + EPS)
    o_ref[...] = on * (g * jax.nn.sigmoid(g))


def _batch_last(s):
    return jnp.transpose(s, (1, 2, 3, 0))


def _batch_first(s):
    return jnp.transpose(s, (3, 0, 1, 2))


def retention_step(q, k, v, g, s0, pos):
    b, width = q.shape
    n_heads = width // HEAD_DIM
    half = HEAD_DIM // 2
    freq = 1.0 / (ROPE_BASE ** jnp.linspace(0.0, 1.0, half, dtype=F32))
    ang = pos * freq
    lanes = lambda t: jnp.broadcast_to(t[:, None], (HEAD_DIM, b))
    cos_c = lanes(jnp.concatenate([jnp.cos(ang), jnp.cos(ang)]))
    sin_c = lanes(jnp.concatenate([-jnp.sin(ang), jnp.sin(ang)]))
    gam = jnp.exp(jnp.log1p(-jnp.exp2(-5.0 - jnp.arange(n_heads, dtype=F32))))
    gam = jnp.broadcast_to(gam[:, None, None], (n_heads, 1, b))
    vec = pl.BlockSpec((HEAD_DIM, b), lambda h: (h, 0))
    table = pl.BlockSpec((HEAD_DIM, b), lambda h: (0, 0))
    sspec = pl.BlockSpec((1, HEAD_DIM, HEAD_DIM, b), lambda h: (h, 0, 0, 0))
    o, s = pl.pallas_call(
        _retention_step_kernel,
        grid=(n_heads,),
        in_specs=[vec, vec, vec, vec, table, table, pl.BlockSpec((1, 1, b), lambda h: (h, 0, 0)), sspec],
        out_specs=[vec, sspec],
        out_shape=[jax.ShapeDtypeStruct((width, b), F32),
                   jax.ShapeDtypeStruct((n_heads, HEAD_DIM, HEAD_DIM, b), F32)],
        compiler_params=_cparams(("parallel",)),
        name="retention_step",
    )(q.T, k.T, v.T, g.T, cos_c, sin_c, gam, _batch_last(s0))
    return o.T, _batch_first(s)


def _softplus(x):
    return jnp.maximum(x, 0.0) + jnp.log1p(jnp.exp(-jnp.abs(x)))


def _shifted(cur, carry_row):
    if cur.shape[0] == 1:
        return carry_row
    first = lax.broadcasted_iota(jnp.int32, cur.shape, 0) == 0
    return jnp.where(first, carry_row, pltpu.roll(cur, 1, 0))


PREP_PARAMS = ("mu", "w0", "w1", "w2", "a0", "a1", "a2", "g1", "g2", "kk", "ka", "rk", "ones_kk", "ones_rk")
N_PREP_PARAMS = len(PREP_PARAMS)
N_PREP_OUT = 8


def _rwkv_prep_kernel(*refs, project):
    n_in = 4 if project else 8
    (mu_ref, w0_ref, w1_ref, w2_ref, a0_ref, a1_ref, a2_ref, g1_ref, g2_ref, kkp_ref, kap_ref, rk_ref,
     ones_kk_ref, ones_rk_ref) = refs[n_in:n_in + N_PREP_PARAMS]
    n_out = n_in + N_PREP_PARAMS
    r_out, w_out, k_out, v_out, kk_out, kka_out, g_out, bonus_out = refs[n_out:n_out + N_PREP_OUT]
    if project:
        h_ref, gnorm_ref, wproj_ref, buf_ref = refs[:4]
        last_ref, carry = refs[n_out + N_PREP_OUT:]
        width = wproj_ref.shape[1] // 4
        xn = _rms(h_ref[0], gnorm_ref[...]).astype(BF16)
        cur = [jnp.dot(xn, wproj_ref[:, gi * width:(gi + 1) * width], preferred_element_type=F32)
               for gi in range(4)]

        @pl.when(pl.program_id(1) == 0)
        def _():
            for gi in range(4):
                carry[gi] = buf_ref[0, :, gi * width:(gi + 1) * width]

        prev = [_shifted(x, carry[gi]) for gi, x in enumerate(cur)]
        for gi, x in enumerate(cur):
            carry[gi] = x[x.shape[0] - 1:, :]
            last_ref[0, :, gi * width:(gi + 1) * width] = x[x.shape[0] - 1:, :]
    else:
        cur = [ref[0] for ref in refs[:4]]
        prev = [ref[0] for ref in refs[4:8]]
    mu = mu_ref[...]
    lerp = lambda x, xp, i: x + (xp - x) * mu[i:i + 1]
    zr, pz = cur[3], prev[3]
    r = lerp(cur[0], prev[0], 0)
    kx = lerp(cur[1], prev[1], 1)
    vx = lerp(cur[2], prev[2], 2)
    zw, za, zg = lerp(zr, pz, 3), lerp(zr, pz, 4), lerp(zr, pz, 5)
    wpre = w0_ref[...] + _dot(jnp.tanh(_dot(zw, w1_ref[...])), w2_ref[...])
    decay = jnp.exp(-jnp.exp(-_softplus(-wpre) - 0.5))
    a = jax.nn.sigmoid(a0_ref[...] + _dot(_dot(za, a1_ref[...]), a2_ref[...]))
    g = _dot(jax.nn.sigmoid(_dot(zg, g1_ref[...])), g2_ref[...])
    kk = kx * kkp_ref[...]
    n_tiles = kk.shape[1] // LANES

    def head_sum(x, ones_ref):
        tile_sum = sum(x[:, t * LANES:(t + 1) * LANES] for t in range(n_tiles))
        return jnp.dot(_hi_lo(tile_sum), ones_ref[...], preferred_element_type=F32)

    kk_norm = head_sum(kk * kk, ones_kk_ref)
    kk = kk / jnp.maximum(jnp.sqrt(jnp.concatenate([kk_norm] * n_tiles, axis=1)), 1e-12)
    k32 = kx * (1.0 + (a - 1.0) * kap_ref[...])
    r_out[0] = r
    w_out[0] = decay
    k_out[0] = k32
    v_out[0] = vx
    kk_out[0] = kk
    kka_out[0] = kk * a
    g_out[0] = g
    bonus_out[0] = head_sum(r * k32 * rk_ref[...], ones_rk_ref) * vx


def rwkv_prep(cur, prev, prm):
    b, l, w = cur[0].shape
    tl = _tile(l, 256)
    seq = pl.BlockSpec((1, tl, w), lambda i, j: (i, j, 0))
    full = lambda a: pl.BlockSpec(a.shape, lambda i, j: (0,) * a.ndim)
    params = [prm[n] for n in PREP_PARAMS]
    return pl.pallas_call(
        functools.partial(_rwkv_prep_kernel, project=False),
        grid=(b, l // tl),
        in_specs=[seq] * 8 + [full(a) for a in params],
        out_specs=[seq] * N_PREP_OUT,
        out_shape=[jax.ShapeDtypeStruct((b, l, w), F32)] * N_PREP_OUT,
        compiler_params=_cparams(("parallel", "arbitrary")),
        name="rwkv_prep",
    )(*cur, *prev, *params)


def rwkv_project_prep(h, g_norm, w_proj, buf, prm):
    b, l, d = h.shape
    w = w_proj.shape[1] // 4
    tl = _tile(l, 512)
    seq = pl.BlockSpec((1, tl, w), lambda i, j: (i, j, 0))
    row = pl.BlockSpec((1, 1, 4 * w), lambda i, j: (i, 0, 0))
    full = lambda a: pl.BlockSpec(a.shape, lambda i, j: (0,) * a.ndim)
    params = [prm[n] for n in PREP_PARAMS]
    *outs, last = pl.pallas_call(
        functools.partial(_rwkv_prep_kernel, project=True),
        grid=(b, l // tl),
        in_specs=[pl.BlockSpec((1, tl, d), lambda i, j: (i, j, 0)), full(g_norm), full(w_proj), row]
        + [full(a) for a in params],
        out_specs=[seq] * N_PREP_OUT + [row],
        out_shape=[jax.ShapeDtypeStruct((b, l, w), F32)] * N_PREP_OUT + [jax.ShapeDtypeStruct((b, 1, 4 * w), F32)],
        scratch_shapes=[pltpu.VMEM((4, 1, w), F32)],
        compiler_params=_cparams(("parallel", "arbitrary")),
        name="rwkv_project_prep",
    )(h, g_norm, w_proj, buf, *params)
    return outs, last


def _value_columns(v8):
    hi = v8.astype(BF16).astype(F32)
    lo = (v8 - hi).astype(BF16).astype(F32)
    stacked = jnp.concatenate([part[:, p * LANES:(p + 1) * LANES]
                               for p in range(v8.shape[1] // LANES) for part in (hi, lo)], axis=0)
    cols = stacked.T
    return jnp.concatenate([cols[:HEAD_DIM], cols[HEAD_DIM:]], axis=1).astype(BF16)


def _rwkv_scan_kernel(r_ref, w_ref, k_ref, kk_ref, kka_ref, v_ref, sel_ref, ones_ref, o_ref, s_ref, s_scr,
                      ot_scr, *, n_heads):
    c = pl.program_id(1)
    n_grp, tc = r_ref.shape[0], r_ref.shape[1]
    head_lanes = LANES // n_heads
    n_kg = HEAD_DIM // head_lanes

    @pl.when(c == 0)
    def _():
        s_scr[...] = jnp.zeros_like(s_scr)

    ot_scr[...] = jnp.zeros_like(ot_scr)
    ones1 = ones_ref[...]
    t_lane = lax.broadcasted_iota(jnp.int32, (n_grp * HEAD_DIM, LANES), 1) % head_lanes
    rows_of = lambda x, g: x[g * HEAD_DIM:(g + 1) * HEAD_DIM]

    def block(tb, carry):
        base = pl.multiple_of(tb * SUBLANES, SUBLANES)
        refs = {"kk": kk_ref, "w": w_ref, "kka": kka_ref, "k": k_ref, "r": r_ref}
        vp = jnp.concatenate([_value_columns(v_ref[g, pl.ds(base, SUBLANES), :]) for g in range(n_grp)], axis=0)
        tile = base // head_lanes
        head_sum = lambda x: jnp.dot(x.astype(BF16), ones1, preferred_element_type=F32)
        for i in range(SUBLANES):
            row = lambda name, g, kg: jnp.broadcast_to(
                refs[name][g, pl.ds(base, SUBLANES), kg * LANES:(kg + 1) * LANES][i:i + 1, :], (HEAD_DIM, LANES))
            sa = head_sum(jnp.concatenate(
                [sum(s_scr[g, kg] * row("kk", g, kg) for kg in range(n_kg)) for g in range(n_grp)], axis=0))
            vcol = jnp.dot(vp, sel_ref[i], preferred_element_type=F32)
            reads = []
            for g in range(n_grp):
                sa_g, vcol_g = rows_of(sa, g), rows_of(vcol, g)
                read = None
                for kg in range(n_kg):
                    s_new = (s_scr[g, kg] * row("w", g, kg) - sa_g * row("kka", g, kg)
                             + vcol_g * row("k", g, kg))
                    s_scr[g, kg] = s_new
                    term = s_new * row("r", g, kg)
                    read = term if read is None else read + term
                reads.append(read)
            o = head_sum(jnp.concatenate(reads, axis=0))
            ot_scr[tile] = jnp.where(t_lane == (base + i) % head_lanes, o, ot_scr[tile])
        return carry

    lax.fori_loop(0, tc // SUBLANES, block, 0)

    for tile in range(tc // head_lanes):
        for g in range(n_grp):
            o_t = rows_of(ot_scr[tile], g).T
            for h in range(n_heads):
                o_ref[g, tile * head_lanes:(tile + 1) * head_lanes, h * HEAD_DIM:(h + 1) * HEAD_DIM] = (
                    o_t[h * head_lanes:(h + 1) * head_lanes, :])

    @pl.when(c == pl.num_programs(1) - 1)
    def _():
        s_ref[...] = s_scr[...]


def _key_group_perm(width):
    n_heads = width // HEAD_DIM
    head_lanes = LANES // n_heads
    n = np.arange(width)
    return (n % LANES) // head_lanes * HEAD_DIM + n // LANES * head_lanes + n % head_lanes


def _to_key_group(t):
    width = t.shape[-1]
    n_heads = width // HEAD_DIM
    head_lanes = LANES // n_heads
    split = t.reshape(*t.shape[:-1], n_heads, HEAD_DIM // head_lanes, head_lanes)
    return jnp.swapaxes(split, -3, -2).reshape(t.shape)


def _from_key_group(t):
    width = t.shape[-1]
    n_heads = width // HEAD_DIM
    head_lanes = LANES // n_heads
    split = t.reshape(*t.shape[:-1], HEAD_DIM // head_lanes, n_heads, head_lanes)
    return jnp.swapaxes(split, -3, -2).reshape(t.shape)


def rwkv_scan_prompt(r, w, k, kk, kka, v):
    b, l, width = r.shape
    n_heads = width // HEAD_DIM
    head_lanes = LANES // n_heads
    n_kg = HEAD_DIM // head_lanes
    tc = _tile(l, SCAN_CHUNK)
    grp = _tile(b, SCAN_GROUP)
    kl = np.arange(LANES)
    col_head = 2 * ((kl % HEAD_DIM) // (2 * SUBLANES)) + kl // HEAD_DIM
    sel = ((kl[None, :, None] % SUBLANES == np.arange(SUBLANES)[:, None, None])
           & (col_head[None, :, None] == kl[None, None, :] // head_lanes))
    ones1 = kl[:, None] // head_lanes == kl[None, :] // head_lanes
    sel, ones1 = jnp.asarray(sel, BF16), jnp.asarray(ones1, BF16)
    seq = pl.BlockSpec((grp, tc, width), lambda i, j: (i, j, 0))
    state = pl.BlockSpec((grp, n_kg, HEAD_DIM, LANES), lambda i, j: (i, 0, 0, 0))
    o, s = pl.pallas_call(
        functools.partial(_rwkv_scan_kernel, n_heads=n_heads),
        grid=(b // grp, l // tc),
        in_specs=[seq, seq, seq, seq, seq, seq,
                  pl.BlockSpec(sel.shape, lambda i, j: (0, 0, 0)),
                  pl.BlockSpec(ones1.shape, lambda i, j: (0, 0))],
        out_specs=[seq, state],
        out_shape=[jax.ShapeDtypeStruct((b, l, width), F32),
                   jax.ShapeDtypeStruct((b, n_kg, HEAD_DIM, LANES), F32)],
        scratch_shapes=[pltpu.VMEM((grp, n_kg, HEAD_DIM, LANES), F32),
                        pltpu.VMEM((tc // head_lanes, grp * HEAD_DIM, LANES), F32)],
        compiler_params=_cparams(("parallel", "arbitrary")),
        name="rwkv_scan_prompt",
    )(r, w, k, kk, kka, v, sel, ones1)
    s = s.reshape(b, n_kg, HEAD_DIM, n_heads, head_lanes).transpose(0, 3, 2, 1, 4)
    return o, s.reshape(b, n_heads, HEAD_DIM, HEAD_DIM)


def _rwkv_step_kernel(r_ref, w_ref, k_ref, kk_ref, kka_ref, v_ref, s0_ref, o_ref, s_ref):
    r, w, k, kk, kka, v = (ref[...] for ref in (r_ref, w_ref, k_ref, kk_ref, kka_ref, v_ref))
    for vi in range(HEAD_DIM):
        s_row = s0_ref[0, vi]
        sa = -jnp.sum(s_row * kk, axis=0, keepdims=True)
        s_new = s_row * w + sa * kka + v[vi:vi + 1, :] * k
        s_ref[0, vi] = s_new
        o_ref[vi:vi + 1, :] = jnp.sum(s_new * r, axis=0, keepdims=True)


def rwkv_step(r, w, k, kk, kka, v, s0):
    b, width = r.shape
    n_heads = width // HEAD_DIM
    vec = pl.BlockSpec((HEAD_DIM, b), lambda h: (h, 0))
    sspec = pl.BlockSpec((1, HEAD_DIM, HEAD_DIM, b), lambda h: (h, 0, 0, 0))
    o, s = pl.pallas_call(
        _rwkv_step_kernel,
        grid=(n_heads,),
        in_specs=[vec] * 6 + [sspec],
        out_specs=[vec, sspec],
        out_shape=[jax.ShapeDtypeStruct((width, b), F32),
                   jax.ShapeDtypeStruct((n_heads, HEAD_DIM, HEAD_DIM, b), F32)],
        compiler_params=_cparams(("parallel",)),
        name="rwkv_step",
    )(r.T, w.T, k.T, kk.T, kka.T, v.T, _batch_last(s0))
    return o.T, _batch_first(s)


def _diff_lambda(lp, lam_init):
    e1 = jnp.exp(jnp.sum(lp[0:1] * lp[1:2], axis=-1, keepdims=True))
    e2 = jnp.exp(jnp.sum(lp[2:3] * lp[3:4], axis=-1, keepdims=True))
    return e1 - e2 + lam_init


def _page_copies(pt_ref, cache_k, cache_v, kbuf, vbuf, sem, step, slot, n_pages, per_step):
    copies = []
    for idx in range(per_step * n_pages):
        page = pt_ref[step * (per_step * n_pages) + idx]
        copies.append(pltpu.make_async_copy(cache_k.at[page], kbuf.at[slot, idx], sem.at[slot, 0]))
        copies.append(pltpu.make_async_copy(cache_v.at[page], vbuf.at[slot, idx], sem.at[slot, 1]))
    return copies


def _diff_attn_kernel(pt_ref, q_ref, k_ref, v_ref, qs_ref, kns_ref, vns_ref, cache_k, cache_v, lam_ref, subln_ref,
                      o_ref, os_ref, m_scr, l_scr, acc_scr, kbuf, vbuf, sem, *, lam_init, n_heads, n_pages):
    per_step = qs_ref.shape[0]
    step = (pl.program_id(0) * pl.num_programs(1) + pl.program_id(1)) * pl.num_programs(2) + pl.program_id(2)
    n_steps = pl.num_programs(0) * pl.num_programs(1) * pl.num_programs(2)
    slot = step % 2
    copies = functools.partial(_page_copies, pt_ref, cache_k, cache_v, kbuf, vbuf, sem,
                               n_pages=n_pages, per_step=per_step)

    @pl.when(step == 0)
    def _():
        for i, c in enumerate(copies(step, slot)):
            c.start(priority=i % 2)

    @pl.when(step + 1 < n_steps)
    def _():
        for i, c in enumerate(copies(step + 1, 1 - slot)):
            c.start(priority=i % 2)

    lam = _diff_lambda(lam_ref[...], lam_init)
    _prompt_attention(q_ref, k_ref, v_ref, lam, subln_ref, o_ref, m_scr, l_scr, acc_scr, lam_init)
    for c in copies(step, slot):
        c.wait()
    for r in range(per_step):
        pages = range(r * n_pages, (r + 1) * n_pages)
        _decode_attention(r, qs_ref, kns_ref, vns_ref, [kbuf.at[slot, i] for i in pages],
                          [vbuf.at[slot, i] for i in pages], lam, subln_ref, os_ref, lam_init, n_heads)


def _prompt_attention(q_ref, k_ref, v_ref, lam, subln_ref, o_ref, m_scr, l_scr, acc_scr, lam_init):
    i = pl.program_id(2)
    tq = q_ref.shape[1]
    tk = tq
    scale = HEAD_DIM ** -0.5
    m_scr[...] = jnp.full_like(m_scr, -jnp.inf)
    l_scr[...] = jnp.zeros_like(l_scr)
    acc_scr[...] = jnp.zeros_like(acc_scr)
    q = (q_ref[0] * scale).astype(BF16)

    def update(j, on_diagonal):
        rows = pl.ds(pl.multiple_of(j * tk, tk), tk)
        k, v = k_ref[0, rows, :].astype(BF16), v_ref[0, rows, :].astype(BF16)
        if on_diagonal:
            visible = (lax.broadcasted_iota(jnp.int32, (tq, tk), 1)
                       <= lax.broadcasted_iota(jnp.int32, (tq, tk), 0))
        for mi in range(2):
            sl = slice(mi * HEAD_DIM, (mi + 1) * HEAD_DIM)
            s = lax.dot_general(q[:, sl], k[:, sl], (((1,), (1,)), ((), ())), preferred_element_type=F32)
            if on_diagonal:
                s = jnp.where(visible, s, -jnp.inf)
            m_old = m_scr[mi]
            m_new = jnp.maximum(m_old, jnp.max(s, axis=-1, keepdims=True))
            alpha = jnp.exp(m_old - m_new)
            p = jnp.exp(s - jnp.concatenate([m_new] * (tk // LANES), axis=1))
            l_scr[mi] = alpha * l_scr[mi] + jnp.sum(p, axis=-1, keepdims=True)
            acc_scr[mi] = alpha * acc_scr[mi] + jnp.dot(p.astype(BF16), v, preferred_element_type=F32)
            m_scr[mi] = m_new

    def below_diagonal(j, carry):
        update(j, False)
        return carry

    lax.fori_loop(0, i, below_diagonal, 0)
    update(i, True)
    o = acc_scr[0] / l_scr[0] - lam * (acc_scr[1] / l_scr[1])
    o_ref[0] = _rms(o, subln_ref[...]) * (1.0 - lam_init)


def _decode_attention(r, q_ref, kn_ref, vn_ref, kc_refs, vc_refs, lam, subln_ref, o_ref, lam_init, n_heads):
    n_rows = 2 * n_heads
    dv = 2 * HEAD_DIM
    scale = HEAD_DIM ** -0.5
    row = lax.broadcasted_iota(jnp.int32, (n_rows, dv), 0)
    lane = lax.broadcasted_iota(jnp.int32, (n_rows, dv), 1)
    qmat = jnp.where(lane // HEAD_DIM == row % 2, q_ref[r], 0.0)
    rows_per = kc_refs[0].shape[0]
    cols = len(kc_refs) * rows_per
    own = (lax.broadcasted_iota(jnp.int32, (n_rows, cols), 1) % n_heads
           == lax.broadcasted_iota(jnp.int32, (n_rows, cols), 0) // 2)
    s = jnp.concatenate(
        [lax.dot_general(qmat.astype(BF16), kc_ref[...].astype(BF16), (((1,), (1,)), ((), ())),
                         preferred_element_type=F32) for kc_ref in kc_refs], axis=1) * scale
    s = jnp.where(own, s, -jnp.inf)
    s_new = jnp.sum(qmat * kn_ref[r], axis=-1, keepdims=True) * scale
    m = jnp.maximum(jnp.max(s, axis=-1, keepdims=True), s_new)
    pr = jnp.exp(s - m)
    p_new = jnp.exp(s_new - m)
    prb = pr.astype(BF16)
    pv = sum(jnp.dot(prb[:, i * rows_per:(i + 1) * rows_per], vc_ref[...].astype(BF16), preferred_element_type=F32)
             for i, vc_ref in enumerate(vc_refs))
    acc = (pv + p_new * vn_ref[r]) / (jnp.sum(pr, axis=-1, keepdims=True) + p_new)
    acc = acc * jnp.where(row % 2 == 0, 1.0, -lam)
    subln = subln_ref[...]
    for h in range(n_heads):
        o = acc[2 * h:2 * h + 1] + acc[2 * h + 1:2 * h + 2]
        o_ref[r, h:h + 1, :] = _rms(o, subln) * (1.0 - lam_init)


def diff_attn(q, k, v, qs, ks, vs, cache_k, cache_v, page_table, lam_p, subln, lam_init):
    b, l, width = q.shape
    bs, n_pages = page_table.shape
    dv = 2 * HEAD_DIM
    n_heads = width // dv
    t = _tile(l, ATTN_BLOCK)
    n = l // t
    n_steps = b * n_heads * n
    per_step = bs // n_steps
    assert per_step * n_steps == bs, "sample rows must divide evenly over the prompt attention grid"
    rows = cache_k.shape[1]
    step = lambda bi, h, i: (bi * n_heads + h) * n + i
    per_map = lambda x: jnp.repeat(x.reshape(bs, n_heads, dv), 2, axis=1)
    qspec = pl.BlockSpec((1, t, dv), lambda bi, h, i, pt: (bi, i, h))
    kspec = pl.BlockSpec((1, l, dv), lambda bi, h, i, pt: (bi, 0, h))
    vec = pl.BlockSpec((per_step, 2 * n_heads, dv), lambda bi, h, i, pt: (step(bi, h, i), 0, 0))
    cache = pl.BlockSpec(memory_space=pl.ANY)
    const = lambda x: pl.BlockSpec(x.shape, lambda bi, h, i, pt: (0, 0))
    page_buf = pltpu.VMEM((2, per_step * n_pages, rows, dv), F32)
    o, o_s = pl.pallas_call(
        functools.partial(_diff_attn_kernel, lam_init=lam_init, n_heads=n_heads, n_pages=n_pages),
        grid_spec=pltpu.PrefetchScalarGridSpec(
            num_scalar_prefetch=1,
            grid=(b, n_heads, n),
            in_specs=[qspec, kspec, kspec, vec, vec, vec, cache, cache, const(lam_p), const(subln)],
            out_specs=[qspec, pl.BlockSpec((per_step, n_heads, dv), lambda bi, h, i, pt: (step(bi, h, i), 0, 0))],
            scratch_shapes=[pltpu.VMEM((2, t, LANES), F32), pltpu.VMEM((2, t, LANES), F32),
                            pltpu.VMEM((2, t, dv), F32), page_buf, page_buf, pltpu.SemaphoreType.DMA((2, 2))],
        ),
        out_shape=[jax.ShapeDtypeStruct((b, l, width), F32), jax.ShapeDtypeStruct((bs, n_heads, dv), F32)],
        compiler_params=_cparams(("arbitrary", "arbitrary", "arbitrary")),
        name="diff_attn",
    )(page_table.reshape(-1), q, k, v, per_map(qs), per_map(ks), per_map(vs), cache_k, cache_v, lam_p, subln)
    return o, o_s.reshape(bs, width)


def _lru_coeffs(x, x1, x2, x3, cw, cb, wa, ba, wi, bi, lam):
    xc = x3 * cw[0:1] + x2 * cw[1:2] + x1 * cw[2:3] + x * cw[3:4]
    xc = xc + cb
    r = jax.nn.sigmoid(_dot(xc, wa) + ba)
    ig = jax.nn.sigmoid(_dot(xc, wi) + bi)
    log_a = -LRU_C * r * _softplus(-lam)
    a = jnp.exp(log_a)
    return a, jnp.sqrt(-jnp.tanh(log_a) * (a * a + 1.0)) * (ig * xc)


def _lru_seq_kernel(x_ref, gr_ref, buf_ref, h0_ref, cw_ref, cb_ref, wa_ref, ba_ref, wi_ref, bi_ref, lam_ref,
                    o_ref, hl_ref, carry, h_scr, a_scr, b_scr):
    n_grp, tl = x_ref.shape[0], x_ref.shape[1]
    n_carry = carry.shape[1]

    @pl.when(pl.program_id(1) == 0)
    def _():
        for g in range(n_grp):
            for d in range(n_carry):
                carry[g, d] = buf_ref[g, d:d + 1, :]
        h_scr[...] = h0_ref[...]

    for g in range(n_grp):
        x = x_ref[g]
        x1 = _shifted(x, carry[g, n_carry - 1])
        x2 = _shifted(x1, carry[g, n_carry - 2])
        x3 = _shifted(x2, carry[g, n_carry - 3])
        for d in range(n_carry):
            carry[g, d] = x[tl - n_carry + d:tl - n_carry + d + 1, :]
        a_scr[g], b_scr[g] = _lru_coeffs(x, x1, x2, x3, cw_ref[...], cb_ref[...], wa_ref[...], ba_ref[...],
                                         wi_ref[...], bi_ref[...], lam_ref[...])
    row_id = lax.broadcasted_iota(jnp.int32, (SUBLANES, x_ref.shape[2]), 0)

    def block(tb, hs):
        base = pl.multiple_of(tb * SUBLANES, SUBLANES)
        rows = pl.ds(base, SUBLANES)
        a8 = [a_scr[g, rows, :] for g in range(n_grp)]
        b8 = [b_scr[g, rows, :] for g in range(n_grp)]
        hs = list(hs)
        out = [jnp.zeros_like(a8[0]) for _ in range(n_grp)]
        for i in range(SUBLANES):
            for g in range(n_grp):
                hs[g] = a8[g][i:i + 1, :] * hs[g] + b8[g][i:i + 1, :]
                out[g] = jnp.where(row_id == i, hs[g], out[g])
        for g in range(n_grp):
            o_ref[g, rows, :] = out[g] * jax.nn.gelu(gr_ref[g, rows, :])
        return tuple(hs)

    hs = lax.fori_loop(0, tl // SUBLANES, block, tuple(h_scr[g] for g in range(n_grp)))
    for g in range(n_grp):
        h_scr[g] = hs[g]
        hl_ref[g] = hs[g]


def lru_prompt(x, gr, buf, h0, prm):
    bsz, l, w = x.shape
    tl = _tile(l, 512)
    grp = _tile(bsz, LRU_GROUP)
    seq = pl.BlockSpec((grp, tl, w), lambda i, j: (i, j, 0))
    vec = pl.BlockSpec((grp, 1, w), lambda i, j: (i, 0, 0))
    full = lambda a: pl.BlockSpec(a.shape, lambda i, j: (0,) * a.ndim)
    params = [prm[n] for n in ("conv_w", "conv_b", "wa", "ba", "wi", "bi", "lam")]
    nb = buf.shape[1]
    o, hl = pl.pallas_call(
        _lru_seq_kernel,
        grid=(bsz // grp, l // tl),
        in_specs=[seq, seq, pl.BlockSpec((grp, nb, w), lambda i, j: (i, 0, 0)), vec] + [full(a) for a in params],
        out_specs=[seq, vec],
        out_shape=[jax.ShapeDtypeStruct((bsz, l, w), F32), jax.ShapeDtypeStruct((bsz, 1, w), F32)],
        scratch_shapes=[pltpu.VMEM((grp, nb, 1, w), F32), pltpu.VMEM((grp, 1, w), F32),
                        pltpu.VMEM((grp, tl, w), F32), pltpu.VMEM((grp, tl, w), F32)],
        compiler_params=_cparams(("parallel", "arbitrary")),
        name="lru_prompt",
    )(x, gr, buf, h0.reshape(bsz, 1, w), *params)
    return o, hl.reshape(bsz, w)


def _lru_step_kernel(x_ref, x1_ref, x2_ref, x3_ref, gr_ref, h0_ref, cw_ref, cb_ref, wa_ref, ba_ref, wi_ref,
                     bi_ref, lam_ref, o_ref, h_ref):
    a, b = _lru_coeffs(x_ref[...], x1_ref[...], x2_ref[...], x3_ref[...], cw_ref[...], cb_ref[...], wa_ref[...],
                       ba_ref[...], wi_ref[...], bi_ref[...], lam_ref[...])
    h = a * h0_ref[...] + b
    h_ref[...] = h
    o_ref[...] = h * jax.nn.gelu(gr_ref[...])


def lru_step(x, buf, gr, h0, prm):
    m, w = x.shape
    tm = _tile(m, 512)
    rows = pl.BlockSpec((tm, w), lambda i: (i, 0))
    full = lambda a: pl.BlockSpec(a.shape, lambda i: (0,) * a.ndim)
    params = [prm[n] for n in ("conv_w", "conv_b", "wa", "ba", "wi", "bi", "lam")]
    nb = buf.shape[1]
    return pl.pallas_call(
        _lru_step_kernel,
        grid=(m // tm,),
        in_specs=[rows] * 6 + [full(a) for a in params],
        out_specs=[rows] * 2,
        out_shape=[jax.ShapeDtypeStruct((m, w), F32)] * 2,
        compiler_params=_cparams(("parallel",)),
        name="lru_step",
    )(x, buf[:, nb - 1], buf[:, nb - 2], buf[:, nb - 3], gr, h0, *params)


def _block_diag(w):
    n, d, e = w.shape
    eye = jnp.eye(n, dtype=w.dtype)
    return (eye[:, None, :, None] * w[:, :, None, :]).reshape(n * d, n * e)


def _mix_even(h, g_norm, pos0, s_ret, s_rwkv, buf, wts, is_prompt):
    b, l, d = h.shape
    m = b * l
    gw = d // 2
    seq = lambda t: t.reshape(b, l, gw)
    flat = lambda t: t.reshape(m, gw)
    to_kg, to_nat = _to_key_group, _from_key_group
    groups = lambda t: [t[..., i * gw:(i + 1) * gw] for i in range(4)]
    br, bk, bv, bz = groups(buf)
    buf_kg = [to_kg(br), to_kg(bk), bv, bz]
    w_in = wts["ab_w_in"]
    if is_prompt:
        qa, ka, va, ga = norm_matmul(h.reshape(m, d), g_norm, w_in[:, :4 * gw], gw)
        (r, w, k, v, kk, kka, g, bonus), last = rwkv_project_prep(
            h, g_norm, w_in[:, 4 * gw:], jnp.concatenate(buf_kg, axis=-1), wts["rwkv"])
        last = groups(last)
        pos = pos0 + jnp.arange(l, dtype=F32)
        o_a, s_ret_new = retention_prompt(seq(qa), seq(ka), seq(va), seq(ga), pos, wts["ones_bd"])
        o_a = flat(o_a)
        o_b, s_rwkv_new = rwkv_scan_prompt(r, w, k, kk, kka, v)
    else:
        u = norm_matmul(h.reshape(m, d), g_norm, w_in, gw)
        qa, ka, va, ga = u[:4]
        last = [seq(t) for t in u[4:8]]
        rows = lambda t: t.reshape(1, m, gw)
        r, w, k, v, kk, kka, g, bonus = rwkv_prep([rows(t) for t in last], [rows(t) for t in buf_kg], wts["rwkv"])
        o_a, s_ret_new = retention_step(qa, ka, va, ga, s_ret, jnp.float32(pos0))
        o_b, s_rwkv_new = rwkv_step(*(flat(to_nat(t)) for t in (r, w, k, kk, kka)), flat(v), s_rwkv)
    buf_new = jnp.concatenate([to_nat(last[0]), to_nat(last[1]), last[2], last[3]], axis=-1)
    post = (flat(bonus), flat(g), wts["rwkv_ln"], wts["ones_bd"])
    return o_a, flat(o_b), post, s_ret_new, s_rwkv_new, buf_new


def _mix_odd(h, g_norm, lru_h, lru_buf, wts, is_prompt):
    b, l, d = h.shape
    m = b * l
    gw = d // 2
    u = norm_matmul(h.reshape(m, d), g_norm, wts["cd_w_in"], gw, head_major=(1, 2))
    seq = lambda t: t.reshape(b, l, gw)
    xr = seq(u[3])
    if is_prompt:
        o_c = yield seq(u[0]), seq(u[1]), seq(u[2])
        o_c = o_c.reshape(m, gw)
        o_d, h_last = lru_prompt(xr, seq(u[4]), lru_buf, lru_h, wts["lru"])
    else:
        o_c = yield u[0], u[1], u[2]
        o_d, h_last = lru_step(u[3], lru_buf, u[4], lru_h, wts["lru"])
    buf_new = jnp.concatenate([lru_buf, xr], axis=1)[:, l:]
    n_heads = gw // (2 * HEAD_DIM)
    k_new = u[5].reshape(b, l, n_heads, 2 * HEAD_DIM)
    v_new = u[6].reshape(b, l, n_heads, 2 * HEAD_DIM)
    return o_c, o_d.reshape(m, gw), k_new, v_new, h_last, buf_new


def _advance(gen, value):
    try:
        return gen.send(value), None
    except StopIteration as done:
        return None, done.value


def _trunk(x, p, pos0, s_ret, s_rwkv, s_shift, s_lru_h, s_lru_conv, wts, is_prompt):
    b, l, d = x.shape
    m = b * l
    depth = wts["norm_g"].shape[0]
    h = x.reshape(m, d)
    ret_l, rwkv_l, shift_l, k_l, v_l, lh_l, lc_l = [], [], [], [], [], [], []
    for i in range(depth):
        j = i // 2
        g = wts["norm_g"][i]
        gn = lambda n: g[n:n + 1]
        h = ffn_block(h, gn(0), wts["ffn_in"], wts["ffn_out"], gn(1), i, 0)
        post = None
        if i % 2 == 0:
            o1, o2, post, sr, sw, sb = _mix_even(h.reshape(b, l, d), gn(2), pos0, s_ret[j], s_rwkv[j], s_shift[j],
                                                 wts["even"][j], is_prompt)
            ret_l.append(sr)
            rwkv_l.append(sw)
            shift_l.append(sb)
            w_out = wts["even"][j]["w_out"]
        else:
            o1, o2, kn, vn, lh, lc = yield from _mix_odd(h.reshape(b, l, d), gn(2), s_lru_h[j], s_lru_conv[j],
                                                         wts["odd"][j], is_prompt)
            k_l.append(kn)
            v_l.append(vn)
            lh_l.append(lh)
            lc_l.append(lc)
            w_out = wts["odd"][j]["w_out"]
        h = out_proj(o1, o2, w_out, h, gn(3), post)
        h = ffn_block(h, gn(4), wts["ffn_in"], wts["ffn_out"], gn(5), i, 1,
                      ple=(gn(6), wts["ple_gate"], p.reshape(depth, m, -1), wts["ple"], gn(7)))
    st = lambda lst: jnp.stack(lst, axis=0)
    return (h.reshape(b, l, d), st(k_l), st(v_l), st(ret_l), st(rwkv_l), st(shift_l), st(lh_l), st(lc_l))


def kernel(x_prompt, x_sample, cache_k, cache_v, state_ret, state_rwkv, state_rwkv_shift, state_lru_h, state_lru_conv, page_table, p_prompt, p_sample, norm_g, ffn_w_in, ffn_w_out, ple_w, ple_gate_w, ab_w_in, ab_w_out, rwkv_mu, rwkv_w0, rwkv_w1, rwkv_w2, rwkv_a0, rwkv_a1, rwkv_a2, rwkv_g1, rwkv_g2, rwkv_kk, rwkv_ka, rwkv_rk, rwkv_ln, cd_w_in, cd_w_out, diff_lam, diff_subln, lru_conv_w, lru_conv_b, lru_wa, lru_ba, lru_wi, lru_bi, lru_lambda):
    depth = norm_g.shape[0]
    n_a, n_c = state_ret.shape[0], state_lru_h.shape[0]
    bp = x_prompt.shape[0]
    gw = ab_w_out.shape[1] // 2
    bf = lambda t: t.astype(BF16)
    row = lambda t: t.reshape(1, -1)
    lane2 = np.arange(2 * LANES) % LANES
    ones_bd = jnp.asarray(lane2[:, None] // HEAD_DIM == np.arange(LANES)[None, :] // HEAD_DIM, BF16)
    perm = _key_group_perm(gw)
    kg = _to_key_group
    head_kg = perm // HEAD_DIM
    tile_head = np.tile(head_kg[:LANES], 2)
    ones_kk = jnp.asarray(tile_head[:, None] == head_kg[None, :LANES], BF16)
    ones_rk = jnp.asarray(tile_head[:, None] == (np.arange(gw) // HEAD_DIM)[None, :], BF16)

    def ab_in_kg(w):
        cols = [w[:, g * gw:(g + 1) * gw] for g in range(w.shape[1] // gw)]
        cols[4], cols[5] = kg(cols[4]), kg(cols[5])
        return bf(jnp.concatenate(cols, axis=1))

    wts = {
        "norm_g": norm_g,
        "ffn_in": bf(ffn_w_in), "ffn_out": bf(ffn_w_out), "ple": bf(ple_w), "ple_gate": bf(ple_gate_w),
        "even": [{
            "ab_w_in": ab_in_kg(ab_w_in[j]), "w_out": bf(ab_w_out[j]), "rwkv_ln": rwkv_ln[j],
            "ones_bd": ones_bd,
            "rwkv": {"mu": jnp.concatenate([kg(rwkv_mu[j][:2]), rwkv_mu[j][2:]], axis=0),
                     "w0": kg(row(rwkv_w0[j])), "w1": bf(rwkv_w1[j]), "w2": bf(kg(rwkv_w2[j])),
                     "a0": kg(row(rwkv_a0[j])), "a1": bf(rwkv_a1[j]), "a2": bf(kg(rwkv_a2[j])),
                     "g1": bf(rwkv_g1[j]), "g2": bf(rwkv_g2[j]), "kk": kg(row(rwkv_kk[j])),
                     "ka": kg(row(rwkv_ka[j])), "rk": kg(row(rwkv_rk[j])),
                     "ones_kk": ones_kk, "ones_rk": ones_rk},
        } for j in range(n_a)],
        "odd": [{
            "cd_w_in": bf(cd_w_in[j]), "w_out": bf(cd_w_out[j]), "diff_lam": diff_lam[j],
            "diff_subln": row(diff_subln[j]),
            "lru": {"conv_w": lru_conv_w[j], "conv_b": row(lru_conv_b[j]), "wa": bf(_block_diag(lru_wa[j])),
                    "ba": row(lru_ba[j]), "wi": bf(_block_diag(lru_wi[j])), "bi": row(lru_bi[j]),
                    "lam": row(lru_lambda[j])},
        } for j in range(n_c)],
    }
    zeros = lambda *shape: jnp.zeros(shape, F32)
    past_len = page_table.shape[1] * cache_k.shape[2]
    n_pool, page = cache_k.shape[1], cache_k.shape[2]
    as_rows = lambda c: c.reshape(n_c, n_pool, page * c.shape[3], c.shape[4])
    pages_k, pages_v = as_rows(cache_k), as_rows(cache_v)
    prompt = _trunk(x_prompt, p_prompt, 0.0, [None] * n_a, [None] * n_a,
                    zeros(n_a, bp, 1, 4 * gw), zeros(n_c, bp, gw), zeros(n_c, bp, CONV_W - 1, gw), wts, True)
    sample = _trunk(x_sample, p_sample, float(past_len), state_ret, state_rwkv, state_rwkv_shift,
                    state_lru_h, state_lru_conv, wts, False)
    (qkv_p, out_p), (qkv_s, out_s) = _advance(prompt, None), _advance(sample, None)
    j = 0
    while out_p is None:
        odd = wts["odd"][j]
        lam_init = 0.8 - 0.6 * math.exp(-0.3 * (2 * j + 1))
        o_p, o_s = diff_attn(*qkv_p, *qkv_s, pages_k[j], pages_v[j], page_table, odd["diff_lam"],
                             odd["diff_subln"], lam_init)
        (qkv_p, out_p), (qkv_s, out_s) = _advance(prompt, o_p), _advance(sample, o_s)
        j += 1
    yp, kp, vp, rp, wp, sp, hp, cp = out_p
    ys, ks_, vs, rs, ws, ss, hs, cs = out_s
    return (yp, ys, kp, vp, rp, wp, sp, hp, cp, ks_, vs, rs, ws, ss, hs, cs)
```

```python
import functools
import math

import jax
import jax.numpy as jnp
import numpy as np
from jax import lax
from jax.experimental import pallas as pl
from jax.experimental.pallas import tpu as pltpu

F32 = jnp.float32
BF16 = jnp.bfloat16

HEAD_DIM = 64
CONV_W = 4
LRU_C = 8.0
ROPE_BASE = 10000.0
EPS = 1e-6
RWKV_GN_EPS = 64e-5
RET_CHUNK = 256
ATTN_BLOCK = 512
SCAN_CHUNK = 128
LRU_GROUP = 2
EPILOGUE_ROWS = 256
SCAN_GROUP = 8
LANES = 128
SUBLANES = 8
VMEM_LIMIT = 48 * 1024 * 1024


def _cparams(sem):
    return pltpu.CompilerParams(dimension_semantics=sem, vmem_limit_bytes=VMEM_LIMIT)


def _tile(n, pref):
    t = min(n, pref)
    while n % t:
        t //= 2
    return t


def _rms(x, g):
    return x * lax.rsqrt(jnp.mean(x * x, axis=-1, keepdims=True) + EPS) * g


def _dot(a, b):
    return jnp.dot(a.astype(BF16), b.astype(BF16), preferred_element_type=F32)


def _hi_lo(x):
    hi = x.astype(BF16)
    lo = (x - hi.astype(F32)).astype(BF16)
    return jnp.concatenate([hi, lo], axis=1)


def _seg_sum(x, ones_pair):
    return jnp.concatenate([jnp.dot(_hi_lo(x[:, t * LANES:(t + 1) * LANES]), ones_pair, preferred_element_type=F32)
                            for t in range(x.shape[1] // LANES)], axis=1)


def _ffn_kernel(h_ref, gpre_ref, wg_ref, wu_ref, wo_ref, gpost_ref, *rest):
    o_ref, xn_ref, acc_ref = rest[-3:]
    ple = rest[:-3]
    j = pl.program_id(1)

    @pl.when(j == 0)
    def _():
        xn_ref[...] = _rms(h_ref[...], gpre_ref[...]).astype(BF16)
        acc_ref[...] = jnp.zeros_like(acc_ref)

    xn = xn_ref[...]
    gate = jnp.dot(xn, wg_ref[...], preferred_element_type=F32)
    up = jnp.dot(xn, wu_ref[...], preferred_element_type=F32)
    act = (gate * jax.nn.sigmoid(gate) * up).astype(BF16)
    acc_ref[...] += jnp.dot(act, wo_ref[...], preferred_element_type=F32)

    @pl.when(j == pl.num_programs(1) - 1)
    def _():
        rows_total = h_ref.shape[0]
        chunk = min(rows_total, EPILOGUE_ROWS)
        for r0 in range(0, rows_total, chunk):
            rows = slice(r0, r0 + chunk)
            h = h_ref[rows, :] + 0.5 * _rms(acc_ref[rows, :], gpost_ref[...])
            if ple:
                g6_ref, wgate_ref, p_ref, wp_ref, g7_ref = ple
                gate_p = jax.nn.sigmoid(_dot(_rms(h, g6_ref[...]), wgate_ref[...]))
                h = h + _rms(gate_p * _dot(p_ref[rows, :], wp_ref[...]), g7_ref[...])
            o_ref[rows, :] = h


def ffn_block(h, g_pre, w_in, w_out, g_post, layer, half, ple=None):
    m, d = h.shape
    f = w_out.shape[2]
    tm = _tile(m, 1024)
    tf = _tile(f, 512 if tm >= 512 else 2048)
    nf = f // tf
    vec = pl.BlockSpec((1, d), lambda i, j: (0, 0))
    extra, extra_specs = [], []
    if ple is not None:
        g6, gate_w, p, emb_w, g7 = ple
        pd = p.shape[2]
        extra = [g6, gate_w, p, emb_w, g7]
        extra_specs = [vec, pl.BlockSpec((None, d, d), lambda i, j: (layer, 0, 0)),
                       pl.BlockSpec((None, tm, pd), lambda i, j: (layer, i, 0)),
                       pl.BlockSpec((None, pd, d), lambda i, j: (layer, 0, 0)), vec]
    return pl.pallas_call(
        _ffn_kernel,
        grid=(m // tm, nf),
        in_specs=[
            pl.BlockSpec((tm, d), lambda i, j: (i, 0)),
            vec,
            pl.BlockSpec((None, None, d, tf), lambda i, j: (layer, half, 0, j)),
            pl.BlockSpec((None, None, d, tf), lambda i, j: (layer, half, 0, j + nf)),
            pl.BlockSpec((None, None, tf, d), lambda i, j: (layer, half, j, 0)),
            vec,
        ] + extra_specs,
        out_specs=pl.BlockSpec((tm, d), lambda i, j: (i, 0)),
        out_shape=jax.ShapeDtypeStruct((m, d), F32),
        scratch_shapes=[pltpu.VMEM((tm, d), BF16), pltpu.VMEM((tm, d), F32)],
        compiler_params=_cparams(("parallel", "arbitrary")),
        name="ffn_block",
    )(h, g_pre, w_in, w_in, w_out, g_post, *extra)


def _norm_matmul_kernel(h_ref, g_ref, w_ref, *o_refs, n_groups, head_major):
    xn = _rms(h_ref[...], g_ref[...]).astype(BF16)
    tm, tn = o_refs[0].shape
    heads = tn // LANES
    for gi in range(n_groups):
        res = jnp.dot(xn, w_ref[:, gi * tn:(gi + 1) * tn], preferred_element_type=F32)
        o_refs[gi][...] = res
        if gi in head_major:
            hm_ref = o_refs[n_groups + head_major.index(gi)]
            for hh in range(heads):
                hm_ref[pl.ds(hh, tm, stride=heads), :] = res[:, hh * LANES:(hh + 1) * LANES]


def norm_matmul(h, g, w, tn, head_major=()):
    m, d = h.shape
    n = w.shape[1]
    tm = _tile(m, 512)
    heads = tn // LANES
    rows = pl.BlockSpec((tm, tn), lambda i: (i, 0))
    return pl.pallas_call(
        functools.partial(_norm_matmul_kernel, n_groups=n // tn, head_major=tuple(head_major)),
        grid=(m // tm,),
        in_specs=[
            pl.BlockSpec((tm, d), lambda i: (i, 0)),
            pl.BlockSpec((1, d), lambda i: (0, 0)),
            pl.BlockSpec((d, n), lambda i: (0, 0)),
        ],
        out_specs=[rows] * (n // tn) + [pl.BlockSpec((tm * heads, LANES), lambda i: (i, 0))] * len(head_major),
        out_shape=([jax.ShapeDtypeStruct((m, tn), F32)] * (n // tn)
                   + [jax.ShapeDtypeStruct((m * heads, LANES), F32)] * len(head_major)),
        compiler_params=_cparams(("parallel",)),
        name="norm_matmul",
    )(h, g, w)


def _out_proj_kernel(oa_ref, ob_ref, wa_ref, wb_ref, h_ref, g_ref, *rest):
    ob = ob_ref[...]
    if len(rest) > 1:
        bonus_ref, gate_ref, ln_ref, ones_ref = rest[:4]
        ones_bd = ones_ref[...]
        inv = 1.0 / HEAD_DIM
        oc = ob - _seg_sum(ob, ones_bd) * inv
        on = oc * lax.rsqrt(_seg_sum(oc * oc, ones_bd) * inv + RWKV_GN_EPS)
        ln = ln_ref[...]
        ob = (on * ln[0:1] + ln[1:2] + bonus_ref[...]) * gate_ref[...]
    o_ref = rest[-1]
    y = _dot(oa_ref[...], wa_ref[...]) + _dot(ob, wb_ref[...])
    o_ref[...] = h_ref[...] + _rms(y, g_ref[...])


def out_proj(oa, ob, w, h, g, rwkv_post=None):
    m, d = h.shape
    gw = oa.shape[1]
    tm = _tile(m, 512)
    rows = pl.BlockSpec((tm, gw), lambda i: (i, 0))
    extra, extra_specs = [], []
    if rwkv_post is not None:
        bonus, gate, ln, ones_bd = rwkv_post
        extra = [bonus, gate, ln, ones_bd]
        extra_specs = [rows, rows, pl.BlockSpec(ln.shape, lambda i: (0, 0)),
                       pl.BlockSpec(ones_bd.shape, lambda i: (0, 0))]
    return pl.pallas_call(
        _out_proj_kernel,
        grid=(m // tm,),
        in_specs=[
            rows, rows,
            pl.BlockSpec((gw, d), lambda i: (0, 0)),
            pl.BlockSpec((gw, d), lambda i: (1, 0)),
            pl.BlockSpec((tm, d), lambda i: (i, 0)),
            pl.BlockSpec((1, d), lambda i: (0, 0)),
        ] + extra_specs,
        out_specs=pl.BlockSpec((tm, d), lambda i: (i, 0)),
        out_shape=jax.ShapeDtypeStruct((m, d), F32),
        compiler_params=_cparams(("parallel",)),
        name="out_proj",
    )(oa, ob, w, w, h, g, *extra)


def _retention_tables(n_heads, c, pos):
    lg = jnp.log1p(-jnp.exp2(-5.0 - jnp.arange(n_heads, dtype=F32)))
    idx = jnp.arange(c, dtype=F32)
    rel = idx[:, None] - idx[None, :]
    dmask = jnp.where(rel[None] >= 0, jnp.exp(jnp.maximum(rel, 0.0)[None] * lg[:, None, None]), 0.0)
    rep = lambda t: jnp.repeat(t, HEAD_DIM, axis=-1)
    q_dec = rep(jnp.exp((idx[:, None] + 1.0) * lg[None, :]))
    k_dec = rep(jnp.exp((c - 1.0 - idx[:, None]) * lg[None, :]))
    c_dec = rep(jnp.exp(c * lg)[None, :])
    half = HEAD_DIM // 2
    freq = 1.0 / (ROPE_BASE ** jnp.linspace(0.0, 1.0, half, dtype=F32))
    ang = pos[:, None] * freq[None, :]
    cos, sin = jnp.cos(ang), jnp.sin(ang)
    cos_t = jnp.tile(jnp.concatenate([cos, cos], axis=-1), (1, n_heads))
    sin_t = jnp.tile(jnp.concatenate([-sin, sin], axis=-1), (1, n_heads))
    return dmask, q_dec, k_dec, c_dec, cos_t, sin_t


def _retention_kernel(q_ref, k_ref, v_ref, g_ref, cos_ref, sin_ref, dmask_ref, qdec_ref, kdec_ref,
                      cdec_ref, ones_ref, o_ref, s_ref, s_scr, *, n_heads):
    c = pl.program_id(1)

    @pl.when(c == 0)
    def _():
        s_scr[...] = jnp.zeros_like(s_scr)

    q, k, v, g = q_ref[0], k_ref[0], v_ref[0], g_ref[0]
    cos, sin = cos_ref[...], sin_ref[...]
    rows, width = q.shape
    lane = lax.broadcasted_iota(jnp.int32, q.shape, 1)
    first_half = (lane % HEAD_DIM) < (HEAD_DIM // 2)

    def rot(x):
        swapped = jnp.where(first_half, pltpu.roll(x, width - HEAD_DIM // 2, 1),
                            pltpu.roll(x, HEAD_DIM // 2, 1))
        return x * cos + swapped * sin

    qr = rot(q)
    kr = rot(k) * (HEAD_DIM ** -0.5)
    kd = kr * kdec_ref[...]
    qdec = qdec_ref[...]
    cdec = cdec_ref[...]
    head_a = lax.broadcasted_iota(jnp.int32, (rows, LANES), 1) < HEAD_DIM
    same_head = (lax.broadcasted_iota(jnp.int32, (LANES, LANES), 0) // HEAD_DIM
                 == lax.broadcasted_iota(jnp.int32, (LANES, LANES), 1) // HEAD_DIM)
    nt = (((1,), (1,)), ((), ()))
    outs = []
    for p in range(n_heads // 2):
        sl = slice(p * LANES, (p + 1) * LANES)
        qp = qr[:, sl].astype(BF16)
        kp = kr[:, sl].astype(BF16)
        vp = v[:, sl].astype(BF16)
        zero = jnp.zeros_like(qp)
        att_a = lax.dot_general(jnp.where(head_a, qp, zero), kp, nt, preferred_element_type=F32) * dmask_ref[2 * p]
        att_b = lax.dot_general(jnp.where(head_a, zero, qp), kp, nt,
                                preferred_element_type=F32) * dmask_ref[2 * p + 1]
        s_old = s_scr[p]
        inner = jnp.where(head_a, jnp.dot(att_a.astype(BF16), vp, preferred_element_type=F32),
                          jnp.dot(att_b.astype(BF16), vp, preferred_element_type=F32))
        outs.append(inner + jnp.dot(qp, s_old.astype(BF16), preferred_element_type=F32) * qdec[:, sl])
        update = lax.dot_general(kd[:, sl].astype(BF16), vp, (((0,), (0,)), ((), ())),
                                 preferred_element_type=F32)
        s_scr[p] = jnp.where(same_head, s_old * cdec[:, sl] + update, 0.0)
    o = jnp.concatenate(outs, axis=1)
    ones_bd = ones_ref[...]
    inv = 1.0 / HEAD_DIM
    oc = o - _seg_sum(o, ones_bd) * inv
    on = oc * lax.rsqrt(_seg_sum(oc * oc, ones_bd) * inv + EPS)
    o_ref[0] = on * (g * jax.nn.sigmoid(g))

    @pl.when(c == pl.num_programs(1) - 1)
    def _():
        for p in range(n_heads // 2):
            s_pair = s_scr[p]
            s_ref[0, 2 * p] = s_pair[:HEAD_DIM, :HEAD_DIM]
            s_ref[0, 2 * p + 1] = s_pair[HEAD_DIM:, HEAD_DIM:]


def retention_prompt(q, k, v, g, pos, ones_bd):
    b, l, width = q.shape
    n_heads = width // HEAD_DIM
    c = _tile(l, RET_CHUNK)
    dmask, q_dec, k_dec, c_dec, cos_t, sin_t = _retention_tables(n_heads, c, pos)
    seq = pl.BlockSpec((1, c, width), lambda i, j: (i, j, 0))
    tab = pl.BlockSpec((c, width), lambda i, j: (j, 0))
    fixed = lambda shape: pl.BlockSpec(shape, lambda i, j: (0,) * len(shape))
    return pl.pallas_call(
        functools.partial(_retention_kernel, n_heads=n_heads),
        grid=(b, l // c),
        in_specs=[seq, seq, seq, seq, tab, tab, fixed((n_heads, c, c)), fixed((c, width)),
                  fixed((c, width)), fixed((1, width)), fixed(ones_bd.shape)],
        out_specs=[seq, pl.BlockSpec((1, n_heads, HEAD_DIM, HEAD_DIM), lambda i, j: (i, 0, 0, 0))],
        out_shape=[jax.ShapeDtypeStruct((b, l, width), F32),
                   jax.ShapeDtypeStruct((b, n_heads, HEAD_DIM, HEAD_DIM), F32)],
        scratch_shapes=[pltpu.VMEM((n_heads // 2, LANES, LANES), F32)],
        compiler_params=_cparams(("parallel", "arbitrary")),
        name="retention_prompt",
    )(q, k, v, g, cos_t, sin_t, dmask, q_dec, k_dec, c_dec, ones_bd)


def _retention_step_kernel(q_ref, k_ref, v_ref, g_ref, cos_ref, sin_ref, gam_ref, s0_ref, o_ref, s_ref):
    half = HEAD_DIM // 2
    cos, sin = cos_ref[...], sin_ref[...]
    rot = lambda x: x * cos + jnp.concatenate([x[half:], x[:half]], axis=0) * sin
    q = rot(q_ref[...])
    k = rot(k_ref[...]) * (HEAD_DIM ** -0.5)
    v, g = v_ref[...], g_ref[...]
    gam = gam_ref[0]
    att = jnp.sum(q * k, axis=0, keepdims=True)
    cross = jnp.zeros_like(v)
    for ki in range(HEAD_DIM):
        s_row = s0_ref[0, ki]
        cross = cross + q[ki:ki + 1, :] * s_row
        s_ref[0, ki] = s_row * gam + k[ki:ki + 1, :] * v
    o = att * v + cross * gam
    oc = o - jnp.mean(o, axis=0, keepdims=True)
    on = oc * lax.rsqrt(jnp.mean(oc * oc, axis=0, keepdims=True) + EPS)
    o_ref[...] = on * (g * jax.nn.sigmoid(g))


def _batch_last(s):
    return jnp.transpose(s, (1, 2, 3, 0))


def _batch_first(s):
    return jnp.transpose(s, (3, 0, 1, 2))


def retention_step(q, k, v, g, s0, pos):
    b, width = q.shape
    n_heads = width // HEAD_DIM
    half = HEAD_DIM // 2
    freq = 1.0 / (ROPE_BASE ** jnp.linspace(0.0, 1.0, half, dtype=F32))
    ang = pos * freq
    lanes = lambda t: jnp.broadcast_to(t[:, None], (HEAD_DIM, b))
    cos_c = lanes(jnp.concatenate([jnp.cos(ang), jnp.cos(ang)]))
    sin_c = lanes(jnp.concatenate([-jnp.sin(ang), jnp.sin(ang)]))
    gam = jnp.exp(jnp.log1p(-jnp.exp2(-5.0 - jnp.arange(n_heads, dtype=F32))))
    gam = jnp.broadcast_to(gam[:, None, None], (n_heads, 1, b))
    vec = pl.BlockSpec((HEAD_DIM, b), lambda h: (h, 0))
    table = pl.BlockSpec((HEAD_DIM, b), lambda h: (0, 0))
    sspec = pl.BlockSpec((1, HEAD_DIM, HEAD_DIM, b), lambda h: (h, 0, 0, 0))
    o, s = pl.pallas_call(
        _retention_step_kernel,
        grid=(n_heads,),
        in_specs=[vec, vec, vec, vec, table, table, pl.BlockSpec((1, 1, b), lambda h: (h, 0, 0)), sspec],
        out_specs=[vec, sspec],
        out_shape=[jax.ShapeDtypeStruct((width, b), F32),
                   jax.ShapeDtypeStruct((n_heads, HEAD_DIM, HEAD_DIM, b), F32)],
        compiler_params=_cparams(("parallel",)),
        name="retention_step",
    )(q.T, k.T, v.T, g.T, cos_c, sin_c, gam, _batch_last(s0))
    return o.T, _batch_first(s)


def _softplus(x):
    return jnp.maximum(x, 0.0) + jnp.log1p(jnp.exp(-jnp.abs(x)))


def _shifted(cur, carry_row):
    if cur.shape[0] == 1:
        return carry_row
    first = lax.broadcasted_iota(jnp.int32, cur.shape, 0) == 0
    return jnp.where(first, carry_row, pltpu.roll(cur, 1, 0))


PREP_PARAMS = ("mu", "w0", "w1", "w2", "a0", "a1", "a2", "g1", "g2", "kk", "ka", "rk", "ones_kk", "ones_rk")
N_PREP_PARAMS = len(PREP_PARAMS)
N_PREP_OUT = 8


def _rwkv_prep_kernel(*refs, project):
    n_in = 4 if project else 8
    (mu_ref, w0_ref, w1_ref, w2_ref, a0_ref, a1_ref, a2_ref, g1_ref, g2_ref, kkp_ref, kap_ref, rk_ref,
     ones_kk_ref, ones_rk_ref) = refs[n_in:n_in + N_PREP_PARAMS]
    n_out = n_in + N_PREP_PARAMS
    r_out, w_out, k_out, v_out, kk_out, kka_out, g_out, bonus_out = refs[n_out:n_out + N_PREP_OUT]
    if project:
        h_ref, gnorm_ref, wproj_ref, buf_ref = refs[:4]
        last_ref, carry = refs[n_out + N_PREP_OUT:]
        width = wproj_ref.shape[1] // 4
        xn = _rms(h_ref[0], gnorm_ref[...]).astype(BF16)
        cur = [jnp.dot(xn, wproj_ref[:, gi * width:(gi + 1) * width], preferred_element_type=F32)
               for gi in range(4)]

        @pl.when(pl.program_id(1) == 0)
        def _():
            for gi in range(4):
                carry[gi] = buf_ref[0, :, gi * width:(gi + 1) * width]

        prev = [_shifted(x, carry[gi]) for gi, x in enumerate(cur)]
        for gi, x in enumerate(cur):
            carry[gi] = x[x.shape[0] - 1:, :]
            last_ref[0, :, gi * width:(gi + 1) * width] = x[x.shape[0] - 1:, :]
    else:
        cur = [ref[0] for ref in refs[:4]]
        prev = [ref[0] for ref in refs[4:8]]
    mu = mu_ref[...]
    lerp = lambda x, xp, i: x + (xp - x) * mu[i:i + 1]
    zr, pz = cur[3], prev[3]
    r = lerp(cur[0], prev[0], 0)
    kx = lerp(cur[1], prev[1], 1)
    vx = lerp(cur[2], prev[2], 2)
    zw, za, zg = lerp(zr, pz, 3), lerp(zr, pz, 4), lerp(zr, pz, 5)
    wpre = w0_ref[...] + _dot(jnp.tanh(_dot(zw, w1_ref[...])), w2_ref[...])
    decay = jnp.exp(-jnp.exp(-_softplus(-wpre) - 0.5))
    a = jax.nn.sigmoid(a0_ref[...] + _dot(_dot(za, a1_ref[...]), a2_ref[...]))
    g = _dot(jax.nn.sigmoid(_dot(zg, g1_ref[...])), g2_ref[...])
    kk = kx * kkp_ref[...]
    n_tiles = kk.shape[1] // LANES

    def head_sum(x, ones_ref):
        tile_sum = sum(x[:, t * LANES:(t + 1) * LANES] for t in range(n_tiles))
        return jnp.dot(_hi_lo(tile_sum), ones_ref[...], preferred_element_type=F32)

    kk_norm = head_sum(kk * kk, ones_kk_ref)
    kk = kk / jnp.maximum(jnp.sqrt(jnp.concatenate([kk_norm] * n_tiles, axis=1)), 1e-12)
    k32 = kx * (1.0 + (a - 1.0) * kap_ref[...])
    r_out[0] = r
    w_out[0] = decay
    k_out[0] = k32
    v_out[0] = vx
    kk_out[0] = kk
    kka_out[0] = kk * a
    g_out[0] = g
    bonus_out[0] = head_sum(r * k32 * rk_ref[...], ones_rk_ref) * vx


def rwkv_prep(cur, prev, prm):
    b, l, w = cur[0].shape
    tl = _tile(l, 256)
    seq = pl.BlockSpec((1, tl, w), lambda i, j: (i, j, 0))
    full = lambda a: pl.BlockSpec(a.shape, lambda i, j: (0,) * a.ndim)
    params = [prm[n] for n in PREP_PARAMS]
    return pl.pallas_call(
        functools.partial(_rwkv_prep_kernel, project=False),
        grid=(b, l // tl),
        in_specs=[seq] * 8 + [full(a) for a in params],
        out_specs=[seq] * N_PREP_OUT,
        out_shape=[jax.ShapeDtypeStruct((b, l, w), F32)] * N_PREP_OUT,
        compiler_params=_cparams(("parallel", "arbitrary")),
        name="rwkv_prep",
    )(*cur, *prev, *params)


def rwkv_project_prep(h, g_norm, w_proj, buf, prm):
    b, l, d = h.shape
    w = w_proj.shape[1] // 4
    tl = _tile(l, 512)
    seq = pl.BlockSpec((1, tl, w), lambda i, j: (i, j, 0))
    row = pl.BlockSpec((1, 1, 4 * w), lambda i, j: (i, 0, 0))
    full = lambda a: pl.BlockSpec(a.shape, lambda i, j: (0,) * a.ndim)
    params = [prm[n] for n in PREP_PARAMS]
    *outs, last = pl.pallas_call(
        functools.partial(_rwkv_prep_kernel, project=True),
        grid=(b, l // tl),
        in_specs=[pl.BlockSpec((1, tl, d), lambda i, j: (i, j, 0)), full(g_norm), full(w_proj), row]
        + [full(a) for a in params],
        out_specs=[seq] * N_PREP_OUT + [row],
        out_shape=[jax.ShapeDtypeStruct((b, l, w), F32)] * N_PREP_OUT + [jax.ShapeDtypeStruct((b, 1, 4 * w), F32)],
        scratch_shapes=[pltpu.VMEM((4, 1, w), F32)],
        compiler_params=_cparams(("parallel", "arbitrary")),
        name="rwkv_project_prep",
    )(h, g_norm, w_proj, buf, *params)
    return outs, last


def _value_columns(v8):
    hi = v8.astype(BF16).astype(F32)
    lo = (v8 - hi).astype(BF16).astype(F32)
    stacked = jnp.concatenate([part[:, p * LANES:(p + 1) * LANES]
                               for p in range(v8.shape[1] // LANES) for part in (hi, lo)], axis=0)
    cols = stacked.T
    return jnp.concatenate([cols[:HEAD_DIM], cols[HEAD_DIM:]], axis=1).astype(BF16)


def _rwkv_scan_kernel(r_ref, w_ref, k_ref, kk_ref, kka_ref, v_ref, sel_ref, ones_ref, o_ref, s_ref, s_scr,
                      ot_scr, *, n_heads):
    c = pl.program_id(1)
    n_grp, tc = r_ref.shape[0], r_ref.shape[1]
    head_lanes = LANES // n_heads
    n_kg = HEAD_DIM // head_lanes

    @pl.when(c == 0)
    def _():
        s_scr[...] = jnp.zeros_like(s_scr)

    ot_scr[...] = jnp.zeros_like(ot_scr)
    ones1 = ones_ref[...]
    t_lane = lax.broadcasted_iota(jnp.int32, (n_grp * HEAD_DIM, LANES), 1) % head_lanes
    rows_of = lambda x, g: x[g * HEAD_DIM:(g + 1) * HEAD_DIM]

    def block(tb, carry):
        base = pl.multiple_of(tb * SUBLANES, SUBLANES)
        refs = {"kk": kk_ref, "w": w_ref, "kka": kka_ref, "k": k_ref, "r": r_ref}
        vp = jnp.concatenate([_value_columns(v_ref[g, pl.ds(base, SUBLANES), :]) for g in range(n_grp)], axis=0)
        tile = base // head_lanes
        head_sum = lambda x: jnp.dot(x.astype(BF16), ones1, preferred_element_type=F32)
        for i in range(SUBLANES):
            row = lambda name, g, kg: jnp.broadcast_to(
                refs[name][g, pl.ds(base, SUBLANES), kg * LANES:(kg + 1) * LANES][i:i + 1, :], (HEAD_DIM, LANES))
            sa = head_sum(jnp.concatenate(
                [sum(s_scr[g, kg] * row("kk", g, kg) for kg in range(n_kg)) for g in range(n_grp)], axis=0))
            vcol = jnp.dot(vp, sel_ref[i], preferred_element_type=F32)
            reads = []
            for g in range(n_grp):
                sa_g, vcol_g = rows_of(sa, g), rows_of(vcol, g)
                read = None
                for kg in range(n_kg):
                    s_new = (s_scr[g, kg] * row("w", g, kg) - sa_g * row("kka", g, kg)
                             + vcol_g * row("k", g, kg))
                    s_scr[g, kg] = s_new
                    term = s_new * row("r", g, kg)
                    read = term if read is None else read + term
                reads.append(read)
            o = head_sum(jnp.concatenate(reads, axis=0))
            ot_scr[tile] = jnp.where(t_lane == (base + i) % head_lanes, o, ot_scr[tile])
        return carry

    lax.fori_loop(0, tc // SUBLANES, block, 0)

    for tile in range(tc // head_lanes):
        for g in range(n_grp):
            o_t = rows_of(ot_scr[tile], g).T
            for h in range(n_heads):
                o_ref[g, tile * head_lanes:(tile + 1) * head_lanes, h * HEAD_DIM:(h + 1) * HEAD_DIM] = (
                    o_t[h * head_lanes:(h + 1) * head_lanes, :])

    @pl.when(c == pl.num_programs(1) - 1)
    def _():
        s_ref[...] = s_scr[...]


def _key_group_perm(width):
    n_heads = width // HEAD_DIM
    head_lanes = LANES // n_heads
    n = np.arange(width)
    return (n % LANES) // head_lanes * HEAD_DIM + n // LANES * head_lanes + n % head_lanes


def _to_key_group(t):
    width = t.shape[-1]
    n_heads = width // HEAD_DIM
    head_lanes = LANES // n_heads
    split = t.reshape(*t.shape[:-1], n_heads, HEAD_DIM // head_lanes, head_lanes)
    return jnp.swapaxes(split, -3, -2).reshape(t.shape)


def _from_key_group(t):
    width = t.shape[-1]
    n_heads = width // HEAD_DIM
    head_lanes = LANES // n_heads
    split = t.reshape(*t.shape[:-1], HEAD_DIM // head_lanes, n_heads, head_lanes)
    return jnp.swapaxes(split, -3, -2).reshape(t.shape)


def rwkv_scan_prompt(r, w, k, kk, kka, v):
    b, l, width = r.shape
    n_heads = width // HEAD_DIM
    head_lanes = LANES // n_heads
    n_kg = HEAD_DIM // head_lanes
    tc = _tile(l, SCAN_CHUNK)
    grp = _tile(b, SCAN_GROUP)
    kl = np.arange(LANES)
    col_head = 2 * ((kl % HEAD_DIM) // (2 * SUBLANES)) + kl // HEAD_DIM
    sel = ((kl[None, :, None] % SUBLANES == np.arange(SUBLANES)[:, None, None])
           & (col_head[None, :, None] == kl[None, None, :] // head_lanes))
    ones1 = kl[:, None] // head_lanes == kl[None, :] // head_lanes
    sel, ones1 = jnp.asarray(sel, BF16), jnp.asarray(ones1, BF16)
    seq = pl.BlockSpec((grp, tc, width), lambda i, j: (i, j, 0))
    state = pl.BlockSpec((grp, n_kg, HEAD_DIM, LANES), lambda i, j: (i, 0, 0, 0))
    o, s = pl.pallas_call(
        functools.partial(_rwkv_scan_kernel, n_heads=n_heads),
        grid=(b // grp, l // tc),
        in_specs=[seq, seq, seq, seq, seq, seq,
                  pl.BlockSpec(sel.shape, lambda i, j: (0, 0, 0)),
                  pl.BlockSpec(ones1.shape, lambda i, j: (0, 0))],
        out_specs=[seq, state],
        out_shape=[jax.ShapeDtypeStruct((b, l, width), F32),
                   jax.ShapeDtypeStruct((b, n_kg, HEAD_DIM, LANES), F32)],
        scratch_shapes=[pltpu.VMEM((grp, n_kg, HEAD_DIM, LANES), F32),
                        pltpu.VMEM((tc // head_lanes, grp * HEAD_DIM, LANES), F32)],
        compiler_params=_cparams(("parallel", "arbitrary")),
        name="rwkv_scan_prompt",
    )(r, w, k, kk, kka, v, sel, ones1)
    s = s.reshape(b, n_kg, HEAD_DIM, n_heads, head_lanes).transpose(0, 3, 2, 1, 4)
    return o, s.reshape(b, n_heads, HEAD_DIM, HEAD_DIM)


def _rwkv_step_kernel(r_ref, w_ref, k_ref, kk_ref, kka_ref, v_ref, s0_ref, o_ref, s_ref):
    r, w, k, kk, kka, v = (ref[...] for ref in (r_ref, w_ref, k_ref, kk_ref, kka_ref, v_ref))
    for vi in range(HEAD_DIM):
        s_row = s0_ref[0, vi]
        sa = -jnp.sum(s_row * kk, axis=0, keepdims=True)
        s_new = s_row * w + sa * kka + v[vi:vi + 1, :] * k
        s_ref[0, vi] = s_new
        o_ref[vi:vi + 1, :] = jnp.sum(s_new * r, axis=0, keepdims=True)


def rwkv_step(r, w, k, kk, kka, v, s0):
    b, width = r.shape
    n_heads = width // HEAD_DIM
    vec = pl.BlockSpec((HEAD_DIM, b), lambda h: (h, 0))
    sspec = pl.BlockSpec((1, HEAD_DIM, HEAD_DIM, b), lambda h: (h, 0, 0, 0))
    o, s = pl.pallas_call(
        _rwkv_step_kernel,
        grid=(n_heads,),
        in_specs=[vec] * 6 + [sspec],
        out_specs=[vec, sspec],
        out_shape=[jax.ShapeDtypeStruct((width, b), F32),
                   jax.ShapeDtypeStruct((n_heads, HEAD_DIM, HEAD_DIM, b), F32)],
        compiler_params=_cparams(("parallel",)),
        name="rwkv_step",
    )(r.T, w.T, k.T, kk.T, kka.T, v.T, _batch_last(s0))
    return o.T, _batch_first(s)


def _diff_lambda(lp, lam_init):
    e1 = jnp.exp(jnp.sum(lp[0:1] * lp[1:2], axis=-1, keepdims=True))
    e2 = jnp.exp(jnp.sum(lp[2:3] * lp[3:4], axis=-1, keepdims=True))
    return e1 - e2 + lam_init


def _page_copies(pt_ref, cache_k, cache_v, kbuf, vbuf, sem, step, slot, n_pages, per_step):
    copies = []
    for idx in range(per_step * n_pages):
        page = pt_ref[step * (per_step * n_pages) + idx]
        copies.append(pltpu.make_async_copy(cache_k.at[page], kbuf.at[slot, idx], sem.at[slot, 0]))
        copies.append(pltpu.make_async_copy(cache_v.at[page], vbuf.at[slot, idx], sem.at[slot, 1]))
    return copies


def _diff_attn_kernel(pt_ref, q_ref, k_ref, v_ref, qs_ref, kns_ref, vns_ref, cache_k, cache_v, lam_ref, subln_ref,
                      o_ref, os_ref, m_scr, l_scr, acc_scr, kbuf, vbuf, sem, *, lam_init, n_heads, n_pages):
    per_step = qs_ref.shape[0]
    step = (pl.program_id(0) * pl.num_programs(1) + pl.program_id(1)) * pl.num_programs(2) + pl.program_id(2)
    n_steps = pl.num_programs(0) * pl.num_programs(1) * pl.num_programs(2)
    slot = step % 2
    copies = functools.partial(_page_copies, pt_ref, cache_k, cache_v, kbuf, vbuf, sem,
                               n_pages=n_pages, per_step=per_step)

    @pl.when(step == 0)
    def _():
        for i, c in enumerate(copies(step, slot)):
            c.start(priority=i % 2)

    @pl.when(step + 1 < n_steps)
    def _():
        for i, c in enumerate(copies(step + 1, 1 - slot)):
            c.start(priority=i % 2)

    lam = _diff_lambda(lam_ref[...], lam_init)
    _prompt_attention(q_ref, k_ref, v_ref, lam, subln_ref, o_ref, m_scr, l_scr, acc_scr, lam_init)
    for c in copies(step, slot):
        c.wait()
    for r in range(per_step):
        pages = range(r * n_pages, (r + 1) * n_pages)
        _decode_attention(r, qs_ref, kns_ref, vns_ref, [kbuf.at[slot, i] for i in pages],
                          [vbuf.at[slot, i] for i in pages], lam, subln_ref, os_ref, lam_init, n_heads)


def _prompt_attention(q_ref, k_ref, v_ref, lam, subln_ref, o_ref, m_scr, l_scr, acc_scr, lam_init):
    i = pl.program_id(2)
    tq = q_ref.shape[1]
    tk = tq
    scale = HEAD_DIM ** -0.5
    m_scr[...] = jnp.full_like(m_scr, -jnp.inf)
    l_scr[...] = jnp.zeros_like(l_scr)
    acc_scr[...] = jnp.zeros_like(acc_scr)
    q = (q_ref[0] * scale).astype(BF16)

    def update(j, on_diagonal):
        rows = pl.ds(pl.multiple_of(j * tk, tk), tk)
        k, v = k_ref[0, rows, :].astype(BF16), v_ref[0, rows, :].astype(BF16)
        if on_diagonal:
            visible = (lax.broadcasted_iota(jnp.int32, (tq, tk), 1)
                       <= lax.broadcasted_iota(jnp.int32, (tq, tk), 0))
        for mi in range(2):
            sl = slice(mi * HEAD_DIM, (mi + 1) * HEAD_DIM)
            s = lax.dot_general(q[:, sl], k[:, sl], (((1,), (1,)), ((), ())), preferred_element_type=F32)
            if on_diagonal:
                s = jnp.where(visible, s, -jnp.inf)
            m_old = m_scr[mi]
            m_new = jnp.maximum(m_old, jnp.max(s, axis=-1, keepdims=True))
            alpha = jnp.exp(m_old - m_new)
            p = jnp.exp(s - jnp.concatenate([m_new] * (tk // LANES), axis=1))
            l_scr[mi] = alpha * l_scr[mi] + jnp.sum(p, axis=-1, keepdims=True)
            acc_scr[mi] = alpha * acc_scr[mi] + jnp.dot(p.astype(BF16), v, preferred_element_type=F32)
            m_scr[mi] = m_new

    def below_diagonal(j, carry):
        update(j, False)
        return carry

    lax.fori_loop(0, i, below_diagonal, 0)
    update(i, True)
    o = acc_scr[0] / l_scr[0] - lam * (acc_scr[1] / l_scr[1])
    o_ref[0] = _rms(o, subln_ref[...]) * (1.0 - lam_init)


def _decode_attention(r, q_ref, kn_ref, vn_ref, kc_refs, vc_refs, lam, subln_ref, o_ref, lam_init, n_heads):
    n_rows = 2 * n_heads
    dv = 2 * HEAD_DIM
    scale = HEAD_DIM ** -0.5
    row = lax.broadcasted_iota(jnp.int32, (n_rows, dv), 0)
    lane = lax.broadcasted_iota(jnp.int32, (n_rows, dv), 1)
    qmat = jnp.where(lane // HEAD_DIM == row % 2, q_ref[r], 0.0)
    rows_per = kc_refs[0].shape[0]
    cols = len(kc_refs) * rows_per
    own = (lax.broadcasted_iota(jnp.int32, (n_rows, cols), 1) % n_heads
           == lax.broadcasted_iota(jnp.int32, (n_rows, cols), 0) // 2)
    s = jnp.concatenate(
        [lax.dot_general(qmat.astype(BF16), kc_ref[...].astype(BF16), (((1,), (1,)), ((), ())),
                         preferred_element_type=F32) for kc_ref in kc_refs], axis=1) * scale
    s = jnp.where(own, s, -jnp.inf)
    s_new = jnp.sum(qmat * kn_ref[r], axis=-1, keepdims=True) * scale
    m = jnp.maximum(jnp.max(s, axis=-1, keepdims=True), s_new)
    pr = jnp.exp(s - m)
    p_new = jnp.exp(s_new - m)
    prb = pr.astype(BF16)
    pv = sum(jnp.dot(prb[:, i * rows_per:(i + 1) * rows_per], vc_ref[...].astype(BF16), preferred_element_type=F32)
             for i, vc_ref in enumerate(vc_refs))
    acc = (pv + p_new * vn_ref[r]) / (jnp.sum(pr, axis=-1, keepdims=True) + p_new)
    acc = acc * jnp.where(row % 2 == 0, 1.0, -lam)
    subln = subln_ref[...]
    for h in range(n_heads):
        o = acc[2 * h:2 * h + 1] + acc[2 * h + 1:2 * h + 2]
        o_ref[r, h:h + 1, :] = _rms(o, subln) * (1.0 - lam_init)


def diff_attn(q, k, v, qs, ks, vs, cache_k, cache_v, page_table, lam_p, subln, lam_init):
    b, l, width = q.shape
    bs, n_pages = page_table.shape
    dv = 2 * HEAD_DIM
    n_heads = width // dv
    t = _tile(l, ATTN_BLOCK)
    n = l // t
    n_steps = b * n_heads * n
    per_step = bs // n_steps
    assert per_step * n_steps == bs, "sample rows must divide evenly over the prompt attention grid"
    rows = cache_k.shape[1]
    step = lambda bi, h, i: (bi * n_heads + h) * n + i
    per_map = lambda x: jnp.repeat(x.reshape(bs, n_heads, dv), 2, axis=1)
    qspec = pl.BlockSpec((1, t, dv), lambda bi, h, i, pt: (bi, i, h))
    kspec = pl.BlockSpec((1, l, dv), lambda bi, h, i, pt: (bi, 0, h))
    vec = pl.BlockSpec((per_step, 2 * n_heads, dv), lambda bi, h, i, pt: (step(bi, h, i), 0, 0))
    cache = pl.BlockSpec(memory_space=pl.ANY)
    const = lambda x: pl.BlockSpec(x.shape, lambda bi, h, i, pt: (0, 0))
    page_buf = pltpu.VMEM((2, per_step * n_pages, rows, dv), F32)
    o, o_s = pl.pallas_call(
        functools.partial(_diff_attn_kernel, lam_init=lam_init, n_heads=n_heads, n_pages=n_pages),
        grid_spec=pltpu.PrefetchScalarGridSpec(
            num_scalar_prefetch=1,
            grid=(b, n_heads, n),
            in_specs=[qspec, kspec, kspec, vec, vec, vec, cache, cache, const(lam_p), const(subln)],
            out_specs=[qspec, pl.BlockSpec((per_step, n_heads, dv), lambda bi, h, i, pt: (step(bi, h, i), 0, 0))],
            scratch_shapes=[pltpu.VMEM((2, t, LANES), F32), pltpu.VMEM((2, t, LANES), F32),
                            pltpu.VMEM((2, t, dv), F32), page_buf, page_buf, pltpu.SemaphoreType.DMA((2, 2))],
        ),
        out_shape=[jax.ShapeDtypeStruct((b, l, width), F32), jax.ShapeDtypeStruct((bs, n_heads, dv), F32)],
        compiler_params=_cparams(("arbitrary", "arbitrary", "arbitrary")),
        name="diff_attn",
    )(page_table.reshape(-1), q, k, v, per_map(qs), per_map(ks), per_map(vs), cache_k, cache_v, lam_p, subln)
    return o, o_s.reshape(bs, width)


def _lru_coeffs(x, x1, x2, x3, cw, cb, wa, ba, wi, bi, lam):
    xc = x3 * cw[0:1] + x2 * cw[1:2] + x1 * cw[2:3] + x * cw[3:4]
    xc = xc + cb
    r = jax.nn.sigmoid(_dot(xc, wa) + ba)
    ig = jax.nn.sigmoid(_dot(xc, wi) + bi)
    log_a = -LRU_C * r * _softplus(-lam)
    a = jnp.exp(log_a)
    return a, jnp.sqrt(-jnp.tanh(log_a) * (a * a + 1.0)) * (ig * xc)


def _lru_seq_kernel(x_ref, gr_ref, buf_ref, h0_ref, cw_ref, cb_ref, wa_ref, ba_ref, wi_ref, bi_ref, lam_ref,
                    o_ref, hl_ref, carry, h_scr, a_scr, b_scr):
    n_grp, tl = x_ref.shape[0], x_ref.shape[1]
    n_carry = carry.shape[1]

    @pl.when(pl.program_id(1) == 0)
    def _():
        for g in range(n_grp):
            for d in range(n_carry):
                carry[g, d] = buf_ref[g, d:d + 1, :]
        h_scr[...] = h0_ref[...]

    for g in range(n_grp):
        x = x_ref[g]
        x1 = _shifted(x, carry[g, n_carry - 1])
        x2 = _shifted(x1, carry[g, n_carry - 2])
        x3 = _shifted(x2, carry[g, n_carry - 3])
        for d in range(n_carry):
            carry[g, d] = x[tl - n_carry + d:tl - n_carry + d + 1, :]
        a_scr[g], b_scr[g] = _lru_coeffs(x, x1, x2, x3, cw_ref[...], cb_ref[...], wa_ref[...], ba_ref[...],
                                         wi_ref[...], bi_ref[...], lam_ref[...])
    row_id = lax.broadcasted_iota(jnp.int32, (SUBLANES, x_ref.shape[2]), 0)

    def block(tb, hs):
        base = pl.multiple_of(tb * SUBLANES, SUBLANES)
        rows = pl.ds(base, SUBLANES)
        a8 = [a_scr[g, rows, :] for g in range(n_grp)]
        b8 = [b_scr[g, rows, :] for g in range(n_grp)]
        hs = list(hs)
        out = [jnp.zeros_like(a8[0]) for _ in range(n_grp)]
        for i in range(SUBLANES):
            for g in range(n_grp):
                hs[g] = a8[g][i:i + 1, :] * hs[g] + b8[g][i:i + 1, :]
                out[g] = jnp.where(row_id == i, hs[g], out[g])
        for g in range(n_grp):
            o_ref[g, rows, :] = out[g] * jax.nn.gelu(gr_ref[g, rows, :])
        return tuple(hs)

    hs = lax.fori_loop(0, tl // SUBLANES, block, tuple(h_scr[g] for g in range(n_grp)))
    for g in range(n_grp):
        h_scr[g] = hs[g]
        hl_ref[g] = hs[g]


def lru_prompt(x, gr, buf, h0, prm):
    bsz, l, w = x.shape
    tl = _tile(l, 512)
    grp = _tile(bsz, LRU_GROUP)
    seq = pl.BlockSpec((grp, tl, w), lambda i, j: (i, j, 0))
    vec = pl.BlockSpec((grp, 1, w), lambda i, j: (i, 0, 0))
    full = lambda a: pl.BlockSpec(a.shape, lambda i, j: (0,) * a.ndim)
    params = [prm[n] for n in ("conv_w", "conv_b", "wa", "ba", "wi", "bi", "lam")]
    nb = buf.shape[1]
    o, hl = pl.pallas_call(
        _lru_seq_kernel,
        grid=(bsz // grp, l // tl),
        in_specs=[seq, seq, pl.BlockSpec((grp, nb, w), lambda i, j: (i, 0, 0)), vec] + [full(a) for a in params],
        out_specs=[seq, vec],
        out_shape=[jax.ShapeDtypeStruct((bsz, l, w), F32), jax.ShapeDtypeStruct((bsz, 1, w), F32)],
        scratch_shapes=[pltpu.VMEM((grp, nb, 1, w), F32), pltpu.VMEM((grp, 1, w), F32),
                        pltpu.VMEM((grp, tl, w), F32), pltpu.VMEM((grp, tl, w), F32)],
        compiler_params=_cparams(("parallel", "arbitrary")),
        name="lru_prompt",
    )(x, gr, buf, h0.reshape(bsz, 1, w), *params)
    return o, hl.reshape(bsz, w)


def _lru_step_kernel(x_ref, x1_ref, x2_ref, x3_ref, gr_ref, h0_ref, cw_ref, cb_ref, wa_ref, ba_ref, wi_ref,
                     bi_ref, lam_ref, o_ref, h_ref):
    a, b = _lru_coeffs(x_ref[...], x1_ref[...], x2_ref[...], x3_ref[...], cw_ref[...], cb_ref[...], wa_ref[...],
                       ba_ref[...], wi_ref[...], bi_ref[...], lam_ref[...])
    h = a * h0_ref[...] + b
    h_ref[...] = h
    o_ref[...] = h * jax.nn.gelu(gr_ref[...])


def lru_step(x, buf, gr, h0, prm):
    m, w = x.shape
    tm = _tile(m, 512)
    rows = pl.BlockSpec((tm, w), lambda i: (i, 0))
    full = lambda a: pl.BlockSpec(a.shape, lambda i: (0,) * a.ndim)
    params = [prm[n] for n in ("conv_w", "conv_b", "wa", "ba", "wi", "bi", "lam")]
    nb = buf.shape[1]
    return pl.pallas_call(
        _lru_step_kernel,
        grid=(m // tm,),
        in_specs=[rows] * 6 + [full(a) for a in params],
        out_specs=[rows] * 2,
        out_shape=[jax.ShapeDtypeStruct((m, w), F32)] * 2,
        compiler_params=_cparams(("parallel",)),
        name="lru_step",
    )(x, buf[:, nb - 1], buf[:, nb - 2], buf[:, nb - 3], gr, h0, *params)


def _block_diag(w):
    n, d, e = w.shape
    eye = jnp.eye(n, dtype=w.dtype)
    return (eye[:, None, :, None] * w[:, :, None, :]).reshape(n * d, n * e)


def _mix_even(h, g_norm, pos0, s_ret, s_rwkv, buf, wts, is_prompt):
    b, l, d = h.shape
    m = b * l
    gw = d // 2
    seq = lambda t: t.reshape(b, l, gw)
    flat = lambda t: t.reshape(m, gw)
    to_kg, to_nat = _to_key_group, _from_key_group
    groups = lambda t: [t[..., i * gw:(i + 1) * gw] for i in range(4)]
    br, bk, bv, bz = groups(buf)
    buf_kg = [to_kg(br), to_kg(bk), bv, bz]
    w_in = wts["ab_w_in"]
    if is_prompt:
        qa, ka, va, ga = norm_matmul(h.reshape(m, d), g_norm, w_in[:, :4 * gw], gw)
        (r, w, k, v, kk, kka, g, bonus), last = rwkv_project_prep(
            h, g_norm, w_in[:, 4 * gw:], jnp.concatenate(buf_kg, axis=-1), wts["rwkv"])
        last = groups(last)
        pos = pos0 + jnp.arange(l, dtype=F32)
        o_a, s_ret_new = retention_prompt(seq(qa), seq(ka), seq(va), seq(ga), pos, wts["ones_bd"])
        o_a = flat(o_a)
        o_b, s_rwkv_new = rwkv_scan_prompt(r, w, k, kk, kka, v)
    else:
        u = norm_matmul(h.reshape(m, d), g_norm, w_in, gw)
        qa, ka, va, ga = u[:4]
        last = [seq(t) for t in u[4:8]]
        rows = lambda t: t.reshape(1, m, gw)
        r, w, k, v, kk, kka, g, bonus = rwkv_prep([rows(t) for t in last], [rows(t) for t in buf_kg], wts["rwkv"])
        o_a, s_ret_new = retention_step(qa, ka, va, ga, s_ret, jnp.float32(pos0))
        o_b, s_rwkv_new = rwkv_step(*(flat(to_nat(t)) for t in (r, w, k, kk, kka)), flat(v), s_rwkv)
    buf_new = jnp.concatenate([to_nat(last[0]), to_nat(last[1]), last[2], last[3]], axis=-1)
    post = (flat(bonus), flat(g), wts["rwkv_ln"], wts["ones_bd"])
    return o_a, flat(o_b), post, s_ret_new, s_rwkv_new, buf_new


def _mix_odd(h, g_norm, lru_h, lru_buf, wts, is_prompt):
    b, l, d = h.shape
    m = b * l
    gw = d // 2
    u = norm_matmul(h.reshape(m, d), g_norm, wts["cd_w_in"], gw, head_major=(1, 2))
    seq = lambda t: t.reshape(b, l, gw)
    xr = seq(u[3])
    if is_prompt:
        o_c = yield seq(u[0]), seq(u[1]), seq(u[2])
        o_c = o_c.reshape(m, gw)
        o_d, h_last = lru_prompt(xr, seq(u[4]), lru_buf, lru_h, wts["lru"])
    else:
        o_c = yield u[0], u[1], u[2]
        o_d, h_last = lru_step(u[3], lru_buf, u[4], lru_h, wts["lru"])
    buf_new = jnp.concatenate([lru_buf, xr], axis=1)[:, l:]
    n_heads = gw // (2 * HEAD_DIM)
    k_new = u[5].reshape(b, l, n_heads, 2 * HEAD_DIM)
    v_new = u[6].reshape(b, l, n_heads, 2 * HEAD_DIM)
    return o_c, o_d.reshape(m, gw), k_new, v_new, h_last, buf_new


def _advance(gen, value):
    try:
        return gen.send(value), None
    except StopIteration as done:
        return None, done.value


def _trunk(x, p, pos0, s_ret, s_rwkv, s_shift, s_lru_h, s_lru_conv, wts, is_prompt):
    b, l, d = x.shape
    m = b * l
    depth = wts["norm_g"].shape[0]
    h = x.reshape(m, d)
    ret_l, rwkv_l, shift_l, k_l, v_l, lh_l, lc_l = [], [], [], [], [], [], []
    for i in range(depth):
        j = i // 2
        g = wts["norm_g"][i]
        gn = lambda n: g[n:n + 1]
        h = ffn_block(h, gn(0), wts["ffn_in"], wts["ffn_out"], gn(1), i, 0)
        post = None
        if i % 2 == 0:
            o1, o2, post, sr, sw, sb = _mix_even(h.reshape(b, l, d), gn(2), pos0, s_ret[j], s_rwkv[j], s_shift[j],
                                                 wts["even"][j], is_prompt)
            ret_l.append(sr)
            rwkv_l.append(sw)
            shift_l.append(sb)
            w_out = wts["even"][j]["w_out"]
        else:
            o1, o2, kn, vn, lh, lc = yield from _mix_odd(h.reshape(b, l, d), gn(2), s_lru_h[j], s_lru_conv[j],
                                                         wts["odd"][j], is_prompt)
            k_l.append(kn)
            v_l.append(vn)
            lh_l.append(lh)
            lc_l.append(lc)
            w_out = wts["odd"][j]["w_out"]
        h = out_proj(o1, o2, w_out, h, gn(3), post)
        h = ffn_block(h, gn(4), wts["ffn_in"], wts["ffn_out"], gn(5), i, 1,
                      ple=(gn(6), wts["ple_gate"], p.reshape(depth, m, -1), wts["ple"], gn(7)))
    st = lambda lst: jnp.stack(lst, axis=0)
    return (h.reshape(b, l, d), st(k_l), st(v_l), st(ret_l), st(rwkv_l), st(shift_l), st(lh_l), st(lc_l))


def kernel(x_prompt, x_sample, cache_k, cache_v, state_ret, state_rwkv, state_rwkv_shift, state_lru_h, state_lru_conv, page_table, p_prompt, p_sample, norm_g, ffn_w_in, ffn_w_out, ple_w, ple_gate_w, ab_w_in, ab_w_out, rwkv_mu, rwkv_w0, rwkv_w1, rwkv_w2, rwkv_a0, rwkv_a1, rwkv_a2, rwkv_g1, rwkv_g2, rwkv_kk, rwkv_ka, rwkv_rk, rwkv_ln, cd_w_in, cd_w_out, diff_lam, diff_subln, lru_conv_w, lru_conv_b, lru_wa, lru_ba, lru_wi, lru_bi, lru_lambda):
    depth = norm_g.shape[0]
    n_a, n_c = state_ret.shape[0], state_lru_h.shape[0]
    bp = x_prompt.shape[0]
    gw = ab_w_out.shape[1] // 2
    bf = lambda t: t.astype(BF16)
    row = lambda t: t.reshape(1, -1)
    lane2 = np.arange(2 * LANES) % LANES
    ones_bd = jnp.asarray(lane2[:, None] // HEAD_DIM == np.arange(LANES)[None, :] // HEAD_DIM, BF16)
    perm = _key_group_perm(gw)
    kg = _to_key_group
    head_kg = perm // HEAD_DIM
    tile_head = np.tile(head_kg[:LANES], 2)
    ones_kk = jnp.asarray(tile_head[:, None] == head_kg[None, :LANES], BF16)
    ones_rk = jnp.asarray(tile_head[:, None] == (np.arange(gw) // HEAD_DIM)[None, :], BF16)

    def ab_in_kg(w):
        cols = [w[:, g * gw:(g + 1) * gw] for g in range(w.shape[1] // gw)]
        cols[4], cols[5] = kg(cols[4]), kg(cols[5])
        return bf(jnp.concatenate(cols, axis=1))

    wts = {
        "norm_g": norm_g,
        "ffn_in": bf(ffn_w_in), "ffn_out": bf(ffn_w_out), "ple": bf(ple_w), "ple_gate": bf(ple_gate_w),
        "even": [{
            "ab_w_in": ab_in_kg(ab_w_in[j]), "w_out": bf(ab_w_out[j]), "rwkv_ln": rwkv_ln[j],
            "ones_bd": ones_bd,
            "rwkv": {"mu": jnp.concatenate([kg(rwkv_mu[j][:2]), rwkv_mu[j][2:]], axis=0),
                     "w0": kg(row(rwkv_w0[j])), "w1": bf(rwkv_w1[j]), "w2": bf(kg(rwkv_w2[j])),
                     "a0": kg(row(rwkv_a0[j])), "a1": bf(rwkv_a1[j]), "a2": bf(kg(rwkv_a2[j])),
                     "g1": bf(rwkv_g1[j]), "g2": bf(rwkv_g2[j]), "kk": kg(row(rwkv_kk[j])),
                     "ka": kg(row(rwkv_ka[j])), "rk": kg(row(rwkv_rk[j])),
                     "ones_kk": ones_kk, "ones_rk": ones_rk},
        } for j in range(n_a)],
        "odd": [{
            "cd_w_in": bf(cd_w_in[j]), "w_out": bf(cd_w_out[j]), "diff_lam": diff_lam[j],
            "diff_subln": row(diff_subln[j]),
            "lru": {"conv_w": lru_conv_w[j], "conv_b": row(lru_conv_b[j]), "wa": bf(_block_diag(lru_wa[j])),
                    "ba": row(lru_ba[j]), "wi": bf(_block_diag(lru_wi[j])), "bi": row(lru_bi[j]),
                    "lam": row(lru_lambda[j])},
        } for j in range(n_c)],
    }
    zeros = lambda *shape: jnp.zeros(shape, F32)
    past_len = page_table.shape[1] * cache_k.shape[2]
    n_pool, page = cache_k.shape[1], cache_k.shape[2]
    as_rows = lambda c: c.reshape(n_c, n_pool, page * c.shape[3], c.shape[4])
    pages_k, pages_v = as_rows(cache_k), as_rows(cache_v)
    prompt = _trunk(x_prompt, p_prompt, 0.0, [None] * n_a, [None] * n_a,
                    zeros(n_a, bp, 1, 4 * gw), zeros(n_c, bp, gw), zeros(n_c, bp, CONV_W - 1, gw), wts, True)
    sample = _trunk(x_sample, p_sample, float(past_len), state_ret, state_rwkv, state_rwkv_shift,
                    state_lru_h, state_lru_conv, wts, False)
    (qkv_p, out_p), (qkv_s, out_s) = _advance(prompt, None), _advance(sample, None)
    j = 0
    while out_p is None:
        odd = wts["odd"][j]
        lam_init = 0.8 - 0.6 * math.exp(-0.3 * (2 * j + 1))
        o_p, o_s = diff_attn(*qkv_p, *qkv_s, pages_k[j], pages_v[j], page_table, odd["diff_lam"],
                             odd["diff_subln"], lam_init)
        (qkv_p, out_p), (qkv_s, out_s) = _advance(prompt, o_p), _advance(sample, o_s)
        j += 1
    yp, kp, vp, rp, wp, sp, hp, cp = out_p
    ys, ks_, vs, rs, ws, ss, hs, cs = out_s
    return (yp, ys, kp, vp, rp, wp, sp, hp, cp, ks_, vs, rs, ws, ss, hs, cs)
```
